```python
import jax, jax.numpy as jnp
from jax import lax
import numpy as np

D_MODEL = 1024
BATCH = 4
SEQ = 8192
DEPTH = 2

HEAD_DIM = 64
N_HEADS_A = 16
N_HEADS_B = 16
N_KV_GROUPS_B = 4
HEADS_PER_GROUP = N_HEADS_B // N_KV_GROUPS_B
Q_BLOCK = 128
ROPE_THETA = 500000.0
ROPE_DIM = HEAD_DIM // 4
CMP_BLOCK = 64
SEL_BLOCK = CMP_BLOCK
N_SELECT = 16
WINDOW = 512
SEL_Q_CHUNK = 64
CMP_HIDDEN = 256
N_EXPERTS = 64
TOP_K = 8
D_EXPERT = 256
D_SHARED = 256
ROUTED_SCALE = 2.5
MOE_BLOCK = 256
N_A_LAYERS = DEPTH // 2
N_B_LAYERS = DEPTH - N_A_LAYERS
EPS = 1e-6
FORCE_SCORE = 1e4

kernel_name = 'fox_nsa_yoco_moe_adaln_trunk'


def rmsnorm(x, gain):
    xf = x.astype(jnp.float32)
    xf = xf * lax.rsqrt(jnp.mean(xf * xf, axis=-1, keepdims=True) + EPS)
    return (xf * gain.astype(jnp.float32)).astype(x.dtype)


def modulate(xn, shift, scale):
    return xn * (1 + scale[:, None, :]) + shift[:, None, :]


def rope_tables(positions, dtype):
    inv_freq = ROPE_THETA ** (-jnp.arange(0, ROPE_DIM, 2, dtype=jnp.float32) / ROPE_DIM)
    ang = positions.astype(jnp.float32)[..., None] * inv_freq
    return jnp.cos(ang)[:, :, None, :].astype(dtype), jnp.sin(ang)[:, :, None, :].astype(dtype)


def apply_rope(x, cos, sin):
    half = ROPE_DIM // 2
    x1 = x[..., :half]
    x2 = x[..., half:ROPE_DIM]
    return jnp.concatenate([x1 * cos - x2 * sin, x2 * cos + x1 * sin, x[..., ROPE_DIM:]], axis=-1)


def forgetting_attention(u, w_in, b_f, q_gain, k_gain, w_o):
    B, S, _ = u.shape
    hd = N_HEADS_A * HEAD_DIM
    proj = u @ w_in
    q = rmsnorm(proj[..., :hd].reshape(B, S, N_HEADS_A, HEAD_DIM), q_gain)
    k = rmsnorm(proj[..., hd:2 * hd].reshape(B, S, N_HEADS_A, HEAD_DIM), k_gain)
    v = proj[..., 2 * hd:3 * hd].reshape(B, S, N_HEADS_A, HEAD_DIM)
    log_f = jax.nn.log_sigmoid((proj[..., 3 * hd:] + b_f).astype(jnp.float32))
    cum = jnp.cumsum(log_f, axis=1)
    cum_k = cum.transpose(0, 2, 1)
    nqb = S // Q_BLOCK
    q_blocks = q.reshape(B, nqb, Q_BLOCK, N_HEADS_A, HEAD_DIM).transpose(1, 0, 2, 3, 4)
    cum_blocks = cum.reshape(B, nqb, Q_BLOCK, N_HEADS_A).transpose(1, 0, 2, 3)
    key_pos = jnp.arange(S)
    scale = HEAD_DIM ** -0.5

    def block(args):
        qb, cq, bi = args
        logits = jnp.einsum('bqhd,bkhd->bhqk', qb, k, preferred_element_type=jnp.float32) * scale
        logits = logits + (cq.transpose(0, 2, 1)[..., None] - cum_k[:, :, None, :])
        q_pos = bi * Q_BLOCK + jnp.arange(Q_BLOCK)
        mask = key_pos[None, :] <= q_pos[:, None]
        p = jax.nn.softmax(jnp.where(mask[None, None], logits, -jnp.inf), axis=-1)
        return jnp.einsum('bhqk,bkhd->bqhd', p.astype(v.dtype), v)

    o = lax.map(block, (q_blocks, cum_blocks, jnp.arange(nqb)))
    o = o.transpose(1, 0, 2, 3, 4).reshape(B, S, hd)
    return o @ w_o


def shared_kv(h, cond, norm_kv, w_ada_kv, b_ada_kv, w_kv, pe_k, pe_v, w_ck1, w_ck2, w_cv1, w_cv2,
              k_gains, cos, sin):
    B, S, _ = h.shape
    shift, scale = jnp.split(cond @ w_ada_kv + b_ada_kv, 2, axis=-1)
    s = modulate(rmsnorm(h, norm_kv), shift, scale)
    kv = (s @ w_kv).reshape(B, S, 6, N_KV_GROUPS_B, HEAD_DIM)
    nb = S // CMP_BLOCK

    def compress(t, pe, w1, w2):
        blocks = t.reshape(B, nb, CMP_BLOCK, N_KV_GROUPS_B, HEAD_DIM) + pe[None, None, :, None, :]
        flat = blocks.transpose(0, 1, 3, 2, 4).reshape(B, nb, N_KV_GROUPS_B, CMP_BLOCK * HEAD_DIM)
        return jax.nn.silu(flat @ w1) @ w2

    k_cmp = rmsnorm(compress(kv[:, :, 0], pe_k, w_ck1, w_ck2), k_gains[0])
    v_cmp = compress(kv[:, :, 1], pe_v, w_cv1, w_cv2)
    k_sel = apply_rope(rmsnorm(kv[:, :, 2], k_gains[1]), cos, sin)
    v_sel = kv[:, :, 3]
    k_win = apply_rope(rmsnorm(kv[:, :, 4], k_gains[2]), cos, sin)
    v_win = kv[:, :, 5]
    return (k_cmp, v_cmp, k_sel, v_sel, k_win, v_win)


def selected_branch(q_rot, k_sel, v_sel, sel_idx):
    B, S, G, HPG, Dh = q_rot.shape
    nb = S // SEL_BLOCK
    nc = S // SEL_Q_CHUNK
    n_sel = sel_idx.shape[-1]
    k_blk = k_sel.reshape(B, nb, SEL_BLOCK, G, Dh).transpose(0, 3, 1, 2, 4)
    v_blk = v_sel.reshape(B, nb, SEL_BLOCK, G, Dh).transpose(0, 3, 1, 2, 4)
    qc = q_rot.reshape(B, nc, SEL_Q_CHUNK, G, HPG, Dh).transpose(1, 0, 2, 3, 4, 5)
    ic = sel_idx.reshape(B, nc, SEL_Q_CHUNK, G, n_sel).transpose(1, 0, 2, 3, 4)
    b_ix = jnp.arange(B)[:, None, None, None]
    g_ix = jnp.arange(G)[None, None, :, None]
    scale = Dh ** -0.5

    def chunk(args):
        qb, idx, ci = args
        kg = k_blk[b_ix, g_ix, idx]
        vg = v_blk[b_ix, g_ix, idx]
        logits = jnp.einsum('bqghd,bqgnkd->bqghnk', qb, kg, preferred_element_type=jnp.float32) * scale
        key_pos = idx[..., None] * SEL_BLOCK + jnp.arange(SEL_BLOCK)
        q_pos = ci * SEL_Q_CHUNK + jnp.arange(SEL_Q_CHUNK)
        mask = key_pos <= q_pos[None, :, None, None, None]
        logits = jnp.where(mask[:, :, :, None], logits, -jnp.inf)
        p = jax.nn.softmax(logits.reshape(B, SEL_Q_CHUNK, G, HPG, n_sel * SEL_BLOCK), axis=-1)
        return jnp.einsum('bqghm,bqgmd->bqghd', p.astype(vg.dtype),
                          vg.reshape(B, SEL_Q_CHUNK, G, n_sel * SEL_BLOCK, Dh))

    o = lax.map(chunk, (qc, ic, jnp.arange(nc)))
    return o.transpose(1, 0, 2, 3, 4, 5).reshape(B, S, G * HPG, Dh)


def window_branch(q_rot, k_win, v_win):
    B, S, G, HPG, Dh = q_rot.shape
    nq = S // Q_BLOCK
    span = WINDOW + Q_BLOCK
    kp = jnp.pad(k_win, ((0, 0), (WINDOW, 0), (0, 0), (0, 0)))
    vp = jnp.pad(v_win, ((0, 0), (WINDOW, 0), (0, 0), (0, 0)))
    qb_all = q_rot.reshape(B, nq, Q_BLOCK, G, HPG, Dh).transpose(1, 0, 2, 3, 4, 5)
    scale = Dh ** -0.5

    def block(args):
        qb, bi = args
        start = bi * Q_BLOCK
        kb = lax.dynamic_slice_in_dim(kp, start, span, axis=1)
        vb = lax.dynamic_slice_in_dim(vp, start, span, axis=1)
        logits = jnp.einsum('bqghd,bkgd->bqghk', qb, kb, preferred_element_type=jnp.float32) * scale
        q_pos = start + jnp.arange(Q_BLOCK)
        k_pos = start - WINDOW + jnp.arange(span)
        diff = q_pos[:, None] - k_pos[None, :]
        mask = (diff >= 0) & (diff < WINDOW) & (k_pos >= 0)[None, :]
        p = jax.nn.softmax(jnp.where(mask[None, :, None, None, :], logits, -jnp.inf), axis=-1)
        return jnp.einsum('bqghk,bkgd->bqghd', p.astype(vb.dtype), vb)

    o = lax.map(block, (qb_all, jnp.arange(nq)))
    return o.transpose(1, 0, 2, 3, 4, 5).reshape(B, S, G * HPG, Dh)


def nsa_attention(u, kv, cos, sin, w_in, b_gate, q_gain, w_o):
    k_cmp, v_cmp, k_sel, v_sel, k_win, v_win = kv
    B, S, _ = u.shape
    hd = N_HEADS_B * HEAD_DIM
    proj = u @ w_in
    q = rmsnorm(proj[..., :hd].reshape(B, S, N_HEADS_B, HEAD_DIM), q_gain)
    gates = jax.nn.sigmoid(proj[..., hd:] + b_gate).reshape(B, S, N_HEADS_B, 3)
    q_rot = apply_rope(q, cos, sin).reshape(B, S, N_KV_GROUPS_B, HEADS_PER_GROUP, HEAD_DIM)
    q_plain = q.reshape(B, S, N_KV_GROUPS_B, HEADS_PER_GROUP, HEAD_DIM)
    t_pos = jnp.arange(S)
    nb = S // CMP_BLOCK
    blk = jnp.arange(nb)
    logits = jnp.einsum('bsghd,bngd->bsghn', q_plain, k_cmp, preferred_element_type=jnp.float32) * HEAD_DIM ** -0.5
    vis = ((blk[None, :] + 1) * CMP_BLOCK - 1 <= t_pos[:, None])[None, :, None, None, :]
    p_cmp = jnp.where(vis, jax.nn.softmax(jnp.where(vis, logits, -1e30), axis=-1), 0.0)
    o_cmp = jnp.einsum('bsghn,bngd->bsghd', p_cmp.astype(v_cmp.dtype), v_cmp).reshape(B, S, N_HEADS_B, HEAD_DIM)
    importance = p_cmp.sum(axis=3)
    cur = t_pos // SEL_BLOCK
    forced = (blk[None, :] == 0) | (blk[None, :] == cur[:, None]) | (blk[None, :] == cur[:, None] - 1)
    future = blk[None, :] > cur[:, None]
    importance = jnp.where(forced[None, :, None, :], FORCE_SCORE,
                           jnp.where(future[None, :, None, :], -FORCE_SCORE, importance))
    n_sel = min(N_SELECT, nb)
    _, sel_idx = lax.top_k(importance, n_sel)
    o_sel = selected_branch(q_rot, k_sel, v_sel, sel_idx)
    o_win = window_branch(q_rot, k_win, v_win)
    o = gates[..., 0:1] * o_cmp + gates[..., 1:2] * o_sel + gates[..., 2:3] * o_win
    return o.reshape(B, S, hd) @ w_o


def moe_ffn(u, w_router, b_router, w_eg, w_eu, w_ed, w_sg, w_su, w_sd):
    B, S, D = u.shape
    T = B * S
    h = u.reshape(T, D)
    scores = jax.nn.sigmoid(jnp.matmul(h, w_router, preferred_element_type=jnp.float32))
    _, top_e = lax.top_k(scores + b_router.astype(jnp.float32), TOP_K)
    top_s = jnp.take_along_axis(scores, top_e, axis=-1)
    top_w = top_s / jnp.sum(top_s, axis=-1, keepdims=True) * ROUTED_SCALE
    n_assign = T * TOP_K
    flat_e = top_e.reshape(-1)
    flat_t = jnp.repeat(jnp.arange(T, dtype=jnp.int32), TOP_K)
    flat_w = top_w.reshape(-1)
    order = jnp.argsort(flat_e)
    se, st, sw = flat_e[order], flat_t[order], flat_w[order]
    counts = jnp.bincount(flat_e, length=N_EXPERTS)
    padded = (counts + MOE_BLOCK - 1) // MOE_BLOCK * MOE_BLOCK
    pad_end = jnp.cumsum(padded)
    pad_start = pad_end - padded
    grp_start = jnp.cumsum(counts) - counts
    dest = pad_start[se] + (jnp.arange(n_assign) - grp_start[se])
    n_blocks = (n_assign + MOE_BLOCK - 1) // MOE_BLOCK + N_EXPERTS
    n_rows = n_blocks * MOE_BLOCK
    row_tok = jnp.zeros((n_rows,), jnp.int32).at[dest].set(st)
    row_w = jnp.zeros((n_rows,), h.dtype).at[dest].set(sw.astype(h.dtype))
    block_e = jnp.minimum(jnp.searchsorted(pad_end, jnp.arange(n_blocks) * MOE_BLOCK, side='right'),
                          N_EXPERTS - 1)

    def expert_block(args):
        tok, e = args
        xb = h[tok]
        return (jax.nn.silu(xb @ w_eg[e]) * (xb @ w_eu[e])) @ w_ed[e]

    y_rows = lax.map(expert_block, (row_tok.reshape(n_blocks, MOE_BLOCK), block_e))
    routed = jax.ops.segment_sum(y_rows.reshape(n_rows, D) * row_w[:, None], row_tok, num_segments=T)
    shared = (jax.nn.silu(h @ w_sg) * (h @ w_su)) @ w_sd
    return (routed + shared).reshape(B, S, D)


def setup_inputs(seed: int = 0) -> dict:
    key = jax.random.key(seed)
    ks = jax.random.split(key, 36)
    D = D_MODEL
    HDA = N_HEADS_A * HEAD_DIM
    HDB = N_HEADS_B * HEAD_DIM
    G = N_KV_GROUPS_B

    def nrm(k, shape, s):
        return jax.random.normal(k, shape, jnp.float32) * s

    positions = (jax.random.randint(ks[2], (BATCH, 1), 0, 4096, dtype=jnp.int32)
                 + jnp.arange(SEQ, dtype=jnp.int32)[None, :])
    return {
        'x': nrm(ks[0], (BATCH, SEQ, D), 1.0),
        'c': nrm(ks[1], (BATCH, D), 1.0),
        'positions': positions,
        'w_ada': nrm(ks[3], (DEPTH, D, 6 * D), 0.5 * D ** -0.5),
        'b_ada': nrm(ks[4], (DEPTH, 6 * D), 0.02),
        'norm_mix': 1.0 + nrm(ks[5], (DEPTH, D), 0.05),
        'norm_ffn': 1.0 + nrm(ks[6], (DEPTH, D), 0.05),
        'w_in_a': nrm(ks[7], (N_A_LAYERS, D, 3 * HDA + N_HEADS_A), D ** -0.5),
        'b_f_a': 3.0 + nrm(ks[8], (N_A_LAYERS, N_HEADS_A), 0.1),
        'qnorm_a': 1.0 + nrm(ks[9], (N_A_LAYERS, HEAD_DIM), 0.05),
        'knorm_a': 1.0 + nrm(ks[10], (N_A_LAYERS, HEAD_DIM), 0.05),
        'w_o_a': nrm(ks[11], (N_A_LAYERS, HDA, D), HDA ** -0.5),
        'norm_kv': 1.0 + nrm(ks[12], (D,), 0.05),
        'w_ada_kv': nrm(ks[13], (D, 2 * D), 0.5 * D ** -0.5),
        'b_ada_kv': nrm(ks[14], (2 * D,), 0.02),
        'w_kv_b': nrm(ks[15], (D, 6 * G * HEAD_DIM), D ** -0.5),
        'pe_cmp_k': nrm(ks[16], (CMP_BLOCK, HEAD_DIM), 0.1),
        'pe_cmp_v': nrm(ks[17], (CMP_BLOCK, HEAD_DIM), 0.1),
        'w_cmp_k1': nrm(ks[18], (CMP_BLOCK * HEAD_DIM, CMP_HIDDEN), (CMP_BLOCK * HEAD_DIM) ** -0.5),
        'w_cmp_k2': nrm(ks[19], (CMP_HIDDEN, HEAD_DIM), CMP_HIDDEN ** -0.5),
        'w_cmp_v1': nrm(ks[20], (CMP_BLOCK * HEAD_DIM, CMP_HIDDEN), (CMP_BLOCK * HEAD_DIM) ** -0.5),
        'w_cmp_v2': nrm(ks[21], (CMP_HIDDEN, HEAD_DIM), CMP_HIDDEN ** -0.5),
        'knorm_b': 1.0 + nrm(ks[22], (3, HEAD_DIM), 0.05),
        'w_in_b': nrm(ks[23], (N_B_LAYERS, D, HDB + 3 * N_HEADS_B), D ** -0.5),
        'b_gate_b': nrm(ks[24], (N_B_LAYERS, 3 * N_HEADS_B), 0.1),
        'qnorm_b': 1.0 + nrm(ks[25], (N_B_LAYERS, HEAD_DIM), 0.05),
        'w_o_b': nrm(ks[26], (N_B_LAYERS, HDB, D), HDB ** -0.5),
        'w_router': nrm(ks[27], (DEPTH, D, N_EXPERTS), D ** -0.5),
        'b_router': nrm(ks[28], (DEPTH, N_EXPERTS), 0.01),
        'w_exp_gate': nrm(ks[29], (DEPTH, N_EXPERTS, D, D_EXPERT), D ** -0.5),
        'w_exp_up': nrm(ks[30], (DEPTH, N_EXPERTS, D, D_EXPERT), D ** -0.5),
        'w_exp_down': nrm(ks[31], (DEPTH, N_EXPERTS, D_EXPERT, D), D_EXPERT ** -0.5),
        'w_sh_gate': nrm(ks[32], (DEPTH, D, D_SHARED), D ** -0.5),
        'w_sh_up': nrm(ks[33], (DEPTH, D, D_SHARED), D ** -0.5),
        'w_sh_down': nrm(ks[34], (DEPTH, D_SHARED, D), D_SHARED ** -0.5),
    }


def reference(x, c, positions, w_ada, b_ada, norm_mix, norm_ffn, w_in_a, b_f_a, qnorm_a, knorm_a,
              w_o_a, norm_kv, w_ada_kv, b_ada_kv, w_kv_b, pe_cmp_k, pe_cmp_v, w_cmp_k1, w_cmp_k2,
              w_cmp_v1, w_cmp_v2, knorm_b, w_in_b, b_gate_b, qnorm_b, w_o_b, w_router, b_router,
              w_exp_gate, w_exp_up, w_exp_down, w_sh_gate, w_sh_up, w_sh_down):
    cos, sin = rope_tables(positions, x.dtype)
    cond = jax.nn.silu(c)
    h = x
    kv = None
    for layer in range(DEPTH):
        ada = cond @ w_ada[layer] + b_ada[layer]
        sh_m, sc_m, g_m, sh_f, sc_f, g_f = jnp.split(ada, 6, axis=-1)
        u = modulate(rmsnorm(h, norm_mix[layer]), sh_m, sc_m)
        if layer < N_A_LAYERS:
            mix = forgetting_attention(u, w_in_a[layer], b_f_a[layer], qnorm_a[layer], knorm_a[layer],
                                       w_o_a[layer])
        else:
            if layer == N_A_LAYERS:
                kv = shared_kv(h, cond, norm_kv, w_ada_kv, b_ada_kv, w_kv_b, pe_cmp_k, pe_cmp_v,
                               w_cmp_k1, w_cmp_k2, w_cmp_v1, w_cmp_v2, knorm_b, cos, sin)
            j = layer - N_A_LAYERS
            mix = nsa_attention(u, kv, cos, sin, w_in_b[j], b_gate_b[j], qnorm_b[j], w_o_b[j])
        h = h + g_m[:, None, :] * mix
        u = modulate(rmsnorm(h, norm_ffn[layer]), sh_f, sc_f)
        h = h + g_f[:, None, :] * moe_ffn(u, w_router[layer], b_router[layer], w_exp_gate[layer],
                                          w_exp_up[layer], w_exp_down[layer], w_sh_gate[layer],
                                          w_sh_up[layer], w_sh_down[layer])
    return h
```

```python
import functools

import jax
import jax.numpy as jnp
from jax import lax
from jax.experimental import pallas as pl
from jax.experimental.pallas import tpu as pltpu

F32 = jnp.float32
BF16 = jnp.bfloat16
I32 = jnp.int32

HEAD_DIM = 64
N_HEADS_A = 16
N_HEADS_B = 16
N_KV_GROUPS_B = 4
HEADS_PER_GROUP = N_HEADS_B // N_KV_GROUPS_B
ROPE_THETA = 500000.0
ROPE_DIM = HEAD_DIM // 4
CMP_BLOCK = 64
N_SELECT = 16
WINDOW = 512
N_EXPERTS = 64
TOP_K = 8
ROUTED_SCALE = 2.5
EPS = 1e-6
FORCE_SCORE = 1e4
NEG = -1e30

LANES = 128
SUBLANES = 8
BF16_ROWS = 16

ROW_TILE = 512
FOX_TQ = 512
FOX_TK = 512
NSA_TQ = 256
NSA_TK = 512
WIN_TK = 256
MOE_TOK_TILE = 256
MOE_UNIT = BF16_ROWS
MOE_BLOCK = 256
VMEM_LIMIT = 56 * 1024 * 1024


def _cp(*sem):
    return pltpu.CompilerParams(dimension_semantics=sem, vmem_limit_bytes=VMEM_LIMIT)


def _dot(a, b):
    return jnp.dot(a, b, preferred_element_type=F32)


def _dot_nt(a, b):
    return lax.dot_general(a, b, (((1,), (1,)), ((), ())), preferred_element_type=F32)


def _dot_tn(a, b):
    return lax.dot_general(a, b, (((0,), (0,)), ((), ())), preferred_element_type=F32)


def _split2(x):
    hi = x.astype(BF16)
    lo = (x - hi.astype(F32)).astype(BF16)
    return hi, lo


def _split3(x):
    hi = x.astype(BF16)
    r = x - hi.astype(F32)
    mid = r.astype(BF16)
    lo = (r - mid.astype(F32)).astype(BF16)
    return hi, mid, lo


def _dot_f32(a, b):
    a_hi, a_lo = _split2(a)
    b_hi, b_lo = _split2(b)
    return _dot(a_hi, b_hi) + _dot(a_lo, b_hi) + _dot(a_hi, b_lo)


def _sigmoid(x):
    return 1.0 / (1.0 + jnp.exp(-x))


def _silu(x):
    return x * _sigmoid(x)


def _ada_kernel(c_ref, w_ref, b_ref, o_ref):
    o_ref[...] = _dot_f32(_silu(c_ref[...]), w_ref[...]) + b_ref[...]


def _ada(c_pad, w, b):
    d, n = w.shape
    tn = 1024
    return pl.pallas_call(
        _ada_kernel,
        grid=(n // tn,),
        in_specs=[pl.BlockSpec((SUBLANES, d), lambda j: (0, 0)),
                  pl.BlockSpec((d, tn), lambda j: (0, j)),
                  pl.BlockSpec((1, tn), lambda j: (0, j))],
        out_specs=pl.BlockSpec((SUBLANES, tn), lambda j: (0, j)),
        out_shape=jax.ShapeDtypeStruct((SUBLANES, n), F32),
        compiler_params=_cp("parallel"),
        name="ada",
    )(c_pad, w, b.reshape(1, n))


def _norm_mod(x, gain, shift, scale):
    xn = x * lax.rsqrt(jnp.mean(x * x, axis=-1, keepdims=True) + EPS) * gain
    return xn * (1.0 + scale) + shift


def _nm_kernel(h_ref, gain_ref, shift_ref, scale_ref, w_ref, o_ref, u_scr, *, tiles_per_batch):
    i = pl.program_id(0)

    @pl.when(pl.program_id(1) == 0)
    def _():
        b = i // tiles_per_batch
        u = _norm_mod(h_ref[...], gain_ref[...], shift_ref[pl.ds(b, 1), :], scale_ref[pl.ds(b, 1), :])
        u_scr[...] = u.astype(BF16)

    o_ref[...] = _dot(u_scr[...], w_ref[...]).astype(o_ref.dtype)


def _norm_mod_matmul(h, gain, ada, shift_blk, scale_blk, w, seq, tn, name):
    t, d = h.shape
    n = w.shape[1]
    tm = ROW_TILE
    return pl.pallas_call(
        functools.partial(_nm_kernel, tiles_per_batch=seq // tm),
        grid=(t // tm, n // tn),
        in_specs=[pl.BlockSpec((tm, d), lambda i, j: (i, 0)),
                  pl.BlockSpec((1, d), lambda i, j: (0, 0)),
                  pl.BlockSpec((SUBLANES, d), lambda i, j: (0, shift_blk)),
                  pl.BlockSpec((SUBLANES, d), lambda i, j: (0, scale_blk)),
                  pl.BlockSpec((d, tn), lambda i, j: (0, j))],
        out_specs=pl.BlockSpec((tm, tn), lambda i, j: (i, j)),
        out_shape=jax.ShapeDtypeStruct((t, n), F32),
        scratch_shapes=[pltpu.VMEM((tm, d), BF16)],
        compiler_params=_cp("parallel", "arbitrary"),
        name=name,
    )(h, gain.reshape(1, d), ada, ada, w)


def _logsig_cumsum_kernel(z_ref, b_ref, tri_ref, o_ref, carry):
    @pl.when(pl.program_id(0) == 0)
    def _():
        carry[...] = jnp.zeros_like(carry)

    z = z_ref[...] + b_ref[...]
    lf = jnp.minimum(z, 0.0) - jnp.log1p(jnp.exp(-jnp.abs(z)))
    hi, mid, lo = _split3(lf)
    tri = tri_ref[...]
    c = _dot(tri, hi) + _dot(tri, mid) + _dot(tri, lo) + carry[...]
    o_ref[...] = c
    ts = c.shape[0]
    carry[...] = c[ts - 1:ts, :]


def _logsig_cumsum(z, bias):
    s, c = z.shape
    ts = 256
    tri = (jnp.arange(ts)[:, None] >= jnp.arange(ts)[None, :]).astype(BF16)
    return pl.pallas_call(
        _logsig_cumsum_kernel,
        grid=(s // ts,),
        in_specs=[pl.BlockSpec((ts, c), lambda i: (i, 0)),
                  pl.BlockSpec((1, c), lambda i: (0, 0)),
                  pl.BlockSpec((ts, ts), lambda i: (0, 0))],
        out_specs=pl.BlockSpec((ts, c), lambda i: (i, 0)),
        out_shape=jax.ShapeDtypeStruct((s, c), F32),
        scratch_shapes=[pltpu.VMEM((1, c), F32)],
        compiler_params=_cp("arbitrary"),
        name="fox_logsig_cumsum",
    )(z, bias, tri)


def _head_rms(x, gain):
    return x * lax.rsqrt(jnp.mean(x * x, axis=-1, keepdims=True) + EPS) * gain


def _fox_prep_kernel(q_ref, k_ref, v_ref, cum_ref, qg_ref, kg_ref, qo_ref, ko_ref, vo_ref):
    bh0 = (pl.program_id(0) * (N_HEADS_A // 2) + pl.program_id(1)) * 2
    ts = q_ref.shape[0]
    cum = cum_ref[...]
    lane_c = lax.broadcasted_iota(I32, cum.shape, 1)
    lane = lax.broadcasted_iota(I32, (ts, HEAD_DIM), 1)
    for h2 in range(2):
        sl = slice(h2 * HEAD_DIM, (h2 + 1) * HEAD_DIM)
        c = jnp.sum(jnp.where(lane_c == bh0 + h2, cum, 0.0), axis=-1, keepdims=True)
        hi, mid, lo = (p.astype(F32) for p in _split3(c))
        qn = _head_rms(q_ref[:, sl], qg_ref[...]) * (HEAD_DIM ** -0.5)
        kn = _head_rms(k_ref[:, sl], kg_ref[...])
        ext_q = jnp.where(lane == 0, hi, jnp.where(lane == 1, mid, jnp.where(lane == 2, lo,
                          jnp.where(lane < 6, 1.0, 0.0))))
        ext_k = jnp.where(lane < 3, 1.0, jnp.where(lane == 3, -hi, jnp.where(lane == 4, -mid,
                          jnp.where(lane == 5, -lo, 0.0))))
        ext_v = jnp.where(lane == 0, 1.0, 0.0)
        qo_ref[0, h2] = jnp.concatenate([qn, ext_q], axis=1).astype(BF16)
        ko_ref[0, h2] = jnp.concatenate([kn, ext_k], axis=1).astype(BF16)
        vo_ref[0, h2] = jnp.concatenate([v_ref[:, sl], ext_v], axis=1).astype(BF16)


def _fox_prep(proj, cum, q_gain, k_gain, batch, seq):
    ts = ROW_TILE
    nsb = seq // ts
    hp = N_HEADS_A // 2
    out = jax.ShapeDtypeStruct((batch, N_HEADS_A, seq, LANES), BF16)
    ospec = pl.BlockSpec((1, 2, ts, LANES), lambda b, p, i: (b, p, i, 0))
    return pl.pallas_call(
        _fox_prep_kernel,
        grid=(batch, hp, nsb),
        in_specs=[pl.BlockSpec((ts, LANES), lambda b, p, i: (b * nsb + i, p)),
                  pl.BlockSpec((ts, LANES), lambda b, p, i: (b * nsb + i, hp + p)),
                  pl.BlockSpec((ts, LANES), lambda b, p, i: (b * nsb + i, 2 * hp + p)),
                  pl.BlockSpec((ts, cum.shape[1]), lambda b, p, i: (i, 0)),
                  pl.BlockSpec((1, HEAD_DIM), lambda b, p, i: (0, 0)),
                  pl.BlockSpec((1, HEAD_DIM), lambda b, p, i: (0, 0))],
        out_specs=[ospec, ospec, ospec],
        out_shape=[out, out, out],
        compiler_params=_cp("parallel", "parallel", "parallel"),
        name="fox_prep",
    )(proj, proj, proj, cum, q_gain.reshape(1, HEAD_DIM), k_gain.reshape(1, HEAD_DIM))


def _flash_kernel(q_ref, k_ref, v_ref, o_ref, m_scr, acc_scr, *, tq, tk, window):
    i = pl.program_id(1)
    q = q_ref[0, 0]
    rows = q.shape[0]
    m_scr[...] = jnp.full(m_scr.shape, NEG, F32)
    acc_scr[...] = jnp.zeros(acc_scr.shape, F32)
    q_pos = i * tq + jnp.bitwise_and(lax.broadcasted_iota(I32, (rows, 1), 0), tq - 1)

    def step(j, masked):
        ks = pl.multiple_of(j * tk, tk)
        s = _dot_nt(q, k_ref[0, pl.ds(ks, tk), :])
        if masked:
            k_pos = ks + lax.broadcasted_iota(I32, (1, tk), 1)
            ok = k_pos <= q_pos
            if window is not None:
                ok = jnp.logical_and(ok, q_pos - k_pos < window)
            s = jnp.where(ok, s, NEG)
        m_old = m_scr[...]
        m_new = jnp.maximum(m_old, jnp.max(s, axis=-1, keepdims=True))
        p = jnp.exp(s - m_new).astype(BF16)
        acc_scr[...] = jnp.exp(m_old - m_new) * acc_scr[...] + _dot(p, v_ref[0, pl.ds(ks, tk), :])
        m_scr[...] = m_new

    q_lo = i * tq
    q_hi = q_lo + tq - 1
    j_hi = q_hi // tk
    if window is None:
        j_clear = (q_lo + 1) // tk
        lax.fori_loop(0, j_clear, lambda j, c: (step(j, False), c)[1], 0)
        lax.fori_loop(j_clear, j_hi + 1, lambda j, c: (step(j, True), c)[1], 0)
    else:
        j_lo = jnp.maximum(q_lo - window + 1, 0) // tk
        lax.fori_loop(j_lo, j_hi + 1, lambda j, c: (step(j, True), c)[1], 0)

    acc = acc_scr[...]
    o_ref[0, 0] = (acc[:, :HEAD_DIM] / acc[:, HEAD_DIM:HEAD_DIM + 1]).astype(o_ref.dtype)


def _flash(q, k, v, tq, tk, window, name):
    bh, nq, rows, kd = q.shape
    s = k.shape[1]
    return pl.pallas_call(
        functools.partial(_flash_kernel, tq=tq, tk=tk, window=window),
        grid=(bh, nq),
        in_specs=[pl.BlockSpec((1, 1, rows, kd), lambda b, i: (b, i, 0, 0)),
                  pl.BlockSpec((1, s, kd), lambda b, i: (b, 0, 0)),
                  pl.BlockSpec((1, s, LANES), lambda b, i: (b, 0, 0))],
        out_specs=pl.BlockSpec((1, 1, rows, HEAD_DIM), lambda b, i: (b, i, 0, 0)),
        out_shape=jax.ShapeDtypeStruct((bh, nq, rows, HEAD_DIM), BF16),
        scratch_shapes=[pltpu.VMEM((rows, 1), F32), pltpu.VMEM((rows, LANES), F32)],
        compiler_params=_cp("parallel", "arbitrary"),
        name=name,
    )(q, k, v)


def _oproj_kernel(o_ref, w_ref, h_ref, g_ref, out_ref, *, tiles_per_batch):
    b = pl.program_id(0) // tiles_per_batch
    out_ref[...] = h_ref[...] + g_ref[pl.ds(b, 1), :] * _dot(o_ref[...], w_ref[...])


def _oproj_residual(o, w, h, ada, gate_blk, seq):
    t, d = h.shape
    tm = ROW_TILE
    return pl.pallas_call(
        functools.partial(_oproj_kernel, tiles_per_batch=seq // tm),
        grid=(t // tm,),
        in_specs=[pl.BlockSpec((tm, o.shape[1]), lambda i: (i, 0)),
                  pl.BlockSpec(w.shape, lambda i: (0, 0)),
                  pl.BlockSpec((tm, d), lambda i: (i, 0)),
                  pl.BlockSpec((SUBLANES, d), lambda i: (0, gate_blk))],
        out_specs=pl.BlockSpec((tm, d), lambda i: (i, 0)),
        out_shape=jax.ShapeDtypeStruct((t, d), F32),
        compiler_params=_cp("parallel"),
        name="oproj_residual",
    )(o, w, h, ada)


def _seg_rms(x, seg_ones, gain_tiled):
    ss = _dot((x * x).astype(BF16), seg_ones)
    return x * lax.rsqrt(ss * (1.0 / HEAD_DIM) + EPS) * gain_tiled


def _rope(x, cos_t, sin_t):
    width = x.shape[1]
    reps = width // LANES
    lane = jnp.bitwise_and(lax.broadcasted_iota(I32, x.shape, 1), HEAD_DIM - 1)
    half = ROPE_DIM // 2
    partner = jnp.where(lane < half, pltpu.roll(x, width - half, 1), pltpu.roll(x, half, 1))
    return x * jnp.tile(cos_t, (1, reps)) + partner * jnp.tile(sin_t, (1, reps))


def _nsa_q_kernel(p_ref, z_ref, cos_ref, sin_ref, gain_ref, seg_ref, bg_ref, qp_ref, qr_ref, g_ref):
    q = _seg_rms(p_ref[...], seg_ref[...], gain_ref[...]) * (HEAD_DIM ** -0.5)
    qp_ref[...] = q.astype(BF16)
    qr_ref[...] = _rope(q, cos_ref[...], sin_ref[...]).astype(BF16)
    g_ref[...] = _sigmoid(z_ref[...] + bg_ref[...])


def _nsa_q_prep(proj, cos_t, sin_t, q_gain, b_gate_pad):
    t = proj.shape[0]
    hd = N_HEADS_B * HEAD_DIM
    tm = ROW_TILE
    seg = (jnp.arange(hd)[:, None] // HEAD_DIM == jnp.arange(hd)[None, :] // HEAD_DIM).astype(BF16)
    row = lambda i: (i, 0)
    fix = lambda i: (0, 0)
    return pl.pallas_call(
        _nsa_q_kernel,
        grid=(t // tm,),
        in_specs=[pl.BlockSpec((tm, hd), row),
                  pl.BlockSpec((tm, LANES), lambda i: (i, hd // LANES)),
                  pl.BlockSpec((tm, LANES), row), pl.BlockSpec((tm, LANES), row),
                  pl.BlockSpec((1, hd), fix), pl.BlockSpec((hd, hd), fix), pl.BlockSpec((1, LANES), fix)],
        out_specs=[pl.BlockSpec((tm, hd), row), pl.BlockSpec((tm, hd), row), pl.BlockSpec((tm, LANES), row)],
        out_shape=[jax.ShapeDtypeStruct((t, hd), BF16), jax.ShapeDtypeStruct((t, hd), BF16),
                   jax.ShapeDtypeStruct((t, LANES), F32)],
        compiler_params=_cp("parallel"),
        name="nsa_q_prep",
    )(proj, proj, cos_t, sin_t, jnp.tile(q_gain, N_HEADS_B).reshape(1, hd), seg, b_gate_pad)


def _nsa_kv_kernel(kv_ref, cos_ref, sin_ref, g1_ref, g2_ref, seg_ref, ks_ref, vs_ref, kw_ref, vw_ref):
    gw = N_KV_GROUPS_B * HEAD_DIM
    ks = _seg_rms(kv_ref[:, 2 * gw:3 * gw], seg_ref[...], g1_ref[...])
    kw = _seg_rms(kv_ref[:, 4 * gw:5 * gw], seg_ref[...], g2_ref[...])
    ks_ref[...] = _rope(ks, cos_ref[...], sin_ref[...]).astype(BF16)
    kw_ref[...] = _rope(kw, cos_ref[...], sin_ref[...]).astype(BF16)
    vs_ref[...] = kv_ref[:, 3 * gw:4 * gw].astype(BF16)
    vw_ref[...] = kv_ref[:, 5 * gw:6 * gw].astype(BF16)


def _nsa_kv_prep(kv, cos_t, sin_t, k_gains):
    t, n = kv.shape
    gw = N_KV_GROUPS_B * HEAD_DIM
    tm = ROW_TILE
    seg = (jnp.arange(gw)[:, None] // HEAD_DIM == jnp.arange(gw)[None, :] // HEAD_DIM).astype(BF16)
    row = lambda i: (i, 0)
    fix = lambda i: (0, 0)
    out = jax.ShapeDtypeStruct((t, gw), BF16)
    return pl.pallas_call(
        _nsa_kv_kernel,
        grid=(t // tm,),
        in_specs=[pl.BlockSpec((tm, n), row), pl.BlockSpec((tm, LANES), row), pl.BlockSpec((tm, LANES), row),
                  pl.BlockSpec((1, gw), fix), pl.BlockSpec((1, gw), fix), pl.BlockSpec((gw, gw), fix)],
        out_specs=[pl.BlockSpec((tm, gw), row)] * 4,
        out_shape=[out] * 4,
        compiler_params=_cp("parallel"),
        name="nsa_kv_prep",
    )(kv, cos_t, sin_t, jnp.tile(k_gains[1], N_KV_GROUPS_B).reshape(1, gw),
      jnp.tile(k_gains[2], N_KV_GROUPS_B).reshape(1, gw), seg)


def _compress_kernel(x_ref, pe_ref, w1_ref, w2_ref, g_ref, o_ref, *, norm):
    x = (x_ref[...] + pe_ref[...]).astype(BF16)
    y = _dot(_silu(_dot(x, w1_ref[...])).astype(BF16), w2_ref[...])
    if norm:
        y = _head_rms(y, g_ref[...])
    o_ref[...] = y.astype(o_ref.dtype)


def _compress(x, pe_flat, w1, w2, gain, norm, name):
    r, kdim = x.shape
    tm = min(ROW_TILE, r)
    fix = lambda i: (0, 0)
    return pl.pallas_call(
        functools.partial(_compress_kernel, norm=norm),
        grid=(r // tm,),
        in_specs=[pl.BlockSpec((tm, kdim), lambda i: (i, 0)), pl.BlockSpec((1, kdim), fix),
                  pl.BlockSpec(w1.shape, fix), pl.BlockSpec(w2.shape, fix), pl.BlockSpec((1, HEAD_DIM), fix)],
        out_specs=pl.BlockSpec((tm, HEAD_DIM), lambda i: (i, 0)),
        out_shape=jax.ShapeDtypeStruct((r, HEAD_DIM), BF16),
        compiler_params=_cp("parallel"),
        name=name,
    )(x, pe_flat, w1, w2, gain.reshape(1, HEAD_DIM))


def _cmp_select_kernel(q_ref, k_ref, v_ref, o_ref, nb_ref, *, tq, n_sel):
    i = pl.program_id(1)
    q = q_ref[0, 0]
    rows = q.shape[0]
    nblk = k_ref.shape[1]
    logits = _dot_nt(q, k_ref[0])
    t_pos = i * tq + jnp.bitwise_and(lax.broadcasted_iota(I32, (rows, 1), 0), tq - 1)
    blk = lax.broadcasted_iota(I32, (1, nblk), 1)
    vis = (blk + 1) * CMP_BLOCK - 1 <= t_pos
    lm = jnp.where(vis, logits, NEG)
    e = jnp.exp(lm - jnp.max(lm, axis=-1, keepdims=True))
    p = jnp.where(vis, e / jnp.sum(e, axis=-1, keepdims=True), 0.0)
    o_ref[0, 0] = _dot(p.astype(BF16), v_ref[0])[:, :HEAD_DIM].astype(o_ref.dtype)

    imp = p[0:tq]
    for hh in range(1, rows // tq):
        imp = imp + p[hh * tq:(hh + 1) * tq]
    cur = lax.shift_right_logical(t_pos[0:tq], CMP_BLOCK.bit_length() - 1)
    forced = jnp.logical_or(blk == 0, jnp.logical_or(blk == cur, blk == cur - 1))
    imp = jnp.where(forced, FORCE_SCORE, jnp.where(blk > cur, -FORCE_SCORE, imp))
    blk_f = blk.astype(F32)
    sel = jnp.zeros(imp.shape, F32)
    for _ in range(n_sel):
        best = jnp.max(imp, axis=-1, keepdims=True)
        first = jnp.min(jnp.where(imp == best, blk_f, float(nblk)), axis=-1, keepdims=True)
        hit = blk_f == first
        sel = jnp.where(hit, 1.0, sel)
        imp = jnp.where(hit, -jnp.inf, imp)
    nb_ref[0] = jnp.where(sel > 0.5, 0.0, NEG).astype(nb_ref.dtype)


def _cmp_select(q, k_cmp, v_cmp, seq, tq):
    bg, nq, rows, _ = q.shape
    nblk = k_cmp.shape[1]
    return pl.pallas_call(
        functools.partial(_cmp_select_kernel, tq=tq, n_sel=min(N_SELECT, nblk)),
        grid=(bg, nq),
        in_specs=[pl.BlockSpec((1, 1, rows, HEAD_DIM), lambda b, i: (b, i, 0, 0)),
                  pl.BlockSpec((1, nblk, HEAD_DIM), lambda b, i: (b, 0, 0)),
                  pl.BlockSpec((1, nblk, LANES), lambda b, i: (b, 0, 0))],
        out_specs=[pl.BlockSpec((1, 1, rows, HEAD_DIM), lambda b, i: (b, i, 0, 0)),
                   pl.BlockSpec((1, tq, nblk), lambda b, i: (b, i, 0))],
        out_shape=[jax.ShapeDtypeStruct((bg, nq, rows, HEAD_DIM), BF16),
                   jax.ShapeDtypeStruct((bg, seq, nblk), BF16)],
        compiler_params=_cp("parallel", "parallel"),
        name="nsa_cmp_select",
    )(q, k_cmp, v_cmp)


def _nsa_out_kernel(oc_ref, os_ref, ow_ref, g_ref, e_ref, w_ref, h_ref, ada_ref, out_ref, *, tiles_per_batch):
    b = pl.program_id(0) // tiles_per_batch
    g_hi, g_lo = _split2(g_ref[...])
    o = None
    for j, o_ref in enumerate((oc_ref, os_ref, ow_ref)):
        gate = _dot(g_hi, e_ref[j]) + _dot(g_lo, e_ref[j])
        term = gate * o_ref[...].astype(F32)
        o = term if o is None else o + term
    out_ref[...] = h_ref[...] + ada_ref[pl.ds(b, 1), :] * _dot(o.astype(BF16), w_ref[...])


def _nsa_out(o_cmp, o_sel, o_win, gates, w_o, h, ada, gate_blk, seq):
    t, d = h.shape
    hd = o_cmp.shape[1]
    tm = ROW_TILE
    col = jnp.arange(LANES)[:, None]
    lane = jnp.arange(hd)[None, :]
    expand = jnp.stack([(col == (lane // HEAD_DIM) * 3 + j) for j in range(3)]).astype(BF16)
    row = lambda i: (i, 0)
    return pl.pallas_call(
        functools.partial(_nsa_out_kernel, tiles_per_batch=seq // tm),
        grid=(t // tm,),
        in_specs=[pl.BlockSpec((tm, hd), row), pl.BlockSpec((tm, hd), row), pl.BlockSpec((tm, hd), row),
                  pl.BlockSpec((tm, LANES), row),
                  pl.BlockSpec((3, LANES, hd), lambda i: (0, 0, 0)),
                  pl.BlockSpec(w_o.shape, lambda i: (0, 0)),
                  pl.BlockSpec((tm, d), row),
                  pl.BlockSpec((SUBLANES, d), lambda i: (0, gate_blk))],
        out_specs=pl.BlockSpec((tm, d), row),
        out_shape=jax.ShapeDtypeStruct((t, d), F32),
        compiler_params=_cp("parallel"),
        name="nsa_out",
    )(o_cmp, o_sel, o_win, gates, expand, w_o, h, ada)


def _route_kernel(h_ref, gain_ref, shift_ref, scale_ref, wr_ref, br_ref, u_ref, e_ref, w_ref, *, tiles_per_batch):
    b = pl.program_id(0) // tiles_per_batch
    u = _norm_mod(h_ref[...], gain_ref[...], shift_ref[pl.ds(b, 1), :], scale_ref[pl.ds(b, 1), :])
    u_ref[...] = u.astype(BF16)
    scores = _sigmoid(_dot_f32(u, wr_ref[...]))
    lane = lax.broadcasted_iota(I32, scores.shape, 1)
    lane_f = lane.astype(F32)
    biased = jnp.where(lane < N_EXPERTS, scores + br_ref[...], -jnp.inf)
    idx, val = [], []
    for _ in range(TOP_K):
        best = jnp.max(biased, axis=-1, keepdims=True)
        first = jnp.min(jnp.where(biased == best, lane_f, float(LANES)), axis=-1, keepdims=True)
        hit = lane_f == first
        idx.append(first)
        val.append(jnp.sum(jnp.where(hit, scores, 0.0), axis=-1, keepdims=True))
        biased = jnp.where(hit, -jnp.inf, biased)
    top_s = jnp.concatenate(val, axis=1)
    e_ref[...] = jnp.concatenate(idx, axis=1).astype(I32)
    w_ref[...] = top_s / jnp.sum(top_s, axis=-1, keepdims=True) * ROUTED_SCALE


def _route(h, gain, ada, shift_blk, scale_blk, w_router_pad, b_router_pad, seq):
    t, d = h.shape
    tm = ROW_TILE
    row = lambda i: (i, 0)
    fix = lambda i: (0, 0)
    return pl.pallas_call(
        functools.partial(_route_kernel, tiles_per_batch=seq // tm),
        grid=(t // tm,),
        in_specs=[pl.BlockSpec((tm, d), row), pl.BlockSpec((1, d), fix),
                  pl.BlockSpec((SUBLANES, d), lambda i: (0, shift_blk)),
                  pl.BlockSpec((SUBLANES, d), lambda i: (0, scale_blk)),
                  pl.BlockSpec((d, LANES), fix), pl.BlockSpec((1, LANES), fix)],
        out_specs=[pl.BlockSpec((tm, d), row), pl.BlockSpec((tm, TOP_K), row), pl.BlockSpec((tm, TOP_K), row)],
        out_shape=[jax.ShapeDtypeStruct((t, d), BF16), jax.ShapeDtypeStruct((t, TOP_K), I32),
                   jax.ShapeDtypeStruct((t, TOP_K), F32)],
        compiler_params=_cp("parallel"),
        name="moe_route",
    )(h, gain.reshape(1, d), ada, ada, w_router_pad, b_router_pad)


def _one_hot_t(pos_ref, weight_ref, n_rows):
    tt = pos_ref.shape[0]
    col = lax.broadcasted_iota(I32, (tt, n_rows), 1)
    acc = jnp.zeros((tt, n_rows), F32)
    for k in range(TOP_K):
        hit = col == pos_ref[:, k:k + 1]
        acc = acc + jnp.where(hit, 1.0 if weight_ref is None else weight_ref[:, k:k + 1], 0.0)
    return acc.astype(BF16)


def _dispatch_kernel(u_ref, pos_ref, x_ref):
    x_ref[0] = _dot_tn(_one_hot_t(pos_ref, None, x_ref.shape[1]), u_ref[...]).astype(x_ref.dtype)


def _dispatch(u, lpos, n_tiles_total, r_loc):
    t, d = u.shape
    tt = MOE_TOK_TILE
    n_real = t // tt
    return pl.pallas_call(
        _dispatch_kernel,
        grid=(n_tiles_total,),
        in_specs=[pl.BlockSpec((tt, d), lambda i: (jnp.minimum(i, n_real - 1), 0)),
                  pl.BlockSpec((tt, TOP_K), lambda i: (i, 0))],
        out_specs=pl.BlockSpec((1, r_loc, d), lambda i: (i, 0, 0)),
        out_shape=jax.ShapeDtypeStruct((n_tiles_total, r_loc, d), BF16),
        compiler_params=_cp("parallel"),
        name="moe_dispatch",
    )(u, lpos)


def _expert_kernel(tbl_ref, be_ref, nblk_ref, x_hbm, wg_ref, wu_ref, wd_ref, y_hbm, xbuf, ybuf, gsem, ssem, *, upb):
    del be_ref
    b = pl.program_id(0)
    nblk = nblk_ref[0]
    slot = lax.rem(b, 2)

    def gather(blk, sl):
        for m in range(upb):
            row = pl.multiple_of(tbl_ref[blk * upb + m], MOE_UNIT)
            yield pltpu.make_async_copy(x_hbm.at[pl.ds(row, MOE_UNIT), :],
                                        xbuf.at[sl, pl.ds(m * MOE_UNIT, MOE_UNIT), :], gsem.at[sl, m])

    def scatter(blk, sl):
        for m in range(upb):
            row = pl.multiple_of(tbl_ref[blk * upb + m], MOE_UNIT)
            yield pltpu.make_async_copy(ybuf.at[sl, pl.ds(m * MOE_UNIT, MOE_UNIT), :],
                                        y_hbm.at[pl.ds(row, MOE_UNIT), :], ssem.at[sl, m])

    @pl.when(jnp.logical_and(b == 0, nblk > 0))
    def _():
        for cp in gather(0, 0):
            cp.start()

    @pl.when(b + 1 < nblk)
    def _():
        for cp in gather(b + 1, 1 - slot):
            cp.start()

    @pl.when(jnp.logical_and(b >= 2, b < nblk + 2))
    def _():
        for cp in scatter(b - 2, slot):
            cp.wait()

    @pl.when(b < nblk)
    def _():
        for cp in gather(b, slot):
            cp.wait()
        x = xbuf[slot]
        hid = _silu(_dot(x, wg_ref[0])) * _dot(x, wu_ref[0])
        ybuf[slot] = _dot(hid.astype(BF16), wd_ref[0]).astype(ybuf.dtype)
        for cp in scatter(b, slot):
            cp.start()


def _experts(x_loc, unit_rows, block_e, n_blocks_used, w_gate, w_up, w_down, n_blocks):
    rows, d = x_loc.shape
    de = w_gate.shape[2]
    upb = MOE_BLOCK // MOE_UNIT
    grid_spec = pltpu.PrefetchScalarGridSpec(
        num_scalar_prefetch=3,
        grid=(n_blocks + 2,),
        in_specs=[pl.BlockSpec(memory_space=pl.ANY),
                  pl.BlockSpec((1, d, de), lambda b, tbl, be, nb: (be[b], 0, 0)),
                  pl.BlockSpec((1, d, de), lambda b, tbl, be, nb: (be[b], 0, 0)),
                  pl.BlockSpec((1, de, d), lambda b, tbl, be, nb: (be[b], 0, 0))],
        out_specs=pl.BlockSpec(memory_space=pl.ANY),
        scratch_shapes=[pltpu.VMEM((2, MOE_BLOCK, d), BF16), pltpu.VMEM((2, MOE_BLOCK, d), BF16),
                        pltpu.SemaphoreType.DMA((2, upb)), pltpu.SemaphoreType.DMA((2, upb))],
    )
    return pl.pallas_call(
        functools.partial(_expert_kernel, upb=upb),
        grid_spec=grid_spec,
        out_shape=jax.ShapeDtypeStruct((rows, d), BF16),
        input_output_aliases={3: 0},
        compiler_params=_cp("arbitrary"),
        name="moe_experts",
    )(unit_rows, block_e, n_blocks_used, x_loc, w_gate, w_up, w_down)


def _combine_kernel(y_ref, pos_ref, tw_ref, u_ref, wsg_ref, wsu_ref, wsd_ref, h_ref, g_ref, out_ref, *, tiles_per_batch):
    b = pl.program_id(0) // tiles_per_batch
    routed = _dot(_one_hot_t(pos_ref, tw_ref, y_ref.shape[1]), y_ref[0])
    u = u_ref[...]
    hid = _silu(_dot(u, wsg_ref[...])) * _dot(u, wsu_ref[...])
    shared = _dot(hid.astype(BF16), wsd_ref[...])
    out_ref[...] = h_ref[...] + g_ref[pl.ds(b, 1), :] * (routed + shared)


def _combine(y_loc, lpos, top_w, u, w_sg, w_su, w_sd, h, ada, gate_blk, seq):
    t, d = h.shape
    tt = MOE_TOK_TILE
    r_loc = y_loc.shape[1]
    row = lambda i: (i, 0)
    fix = lambda i: (0, 0)
    return pl.pallas_call(
        functools.partial(_combine_kernel, tiles_per_batch=seq // tt),
        grid=(t // tt,),
        in_specs=[pl.BlockSpec((1, r_loc, d), lambda i: (i, 0, 0)),
                  pl.BlockSpec((tt, TOP_K), row), pl.BlockSpec((tt, TOP_K), row),
                  pl.BlockSpec((tt, d), row),
                  pl.BlockSpec(w_sg.shape, fix), pl.BlockSpec(w_su.shape, fix), pl.BlockSpec(w_sd.shape, fix),
                  pl.BlockSpec((tt, d), row),
                  pl.BlockSpec((SUBLANES, d), lambda i: (0, gate_blk))],
        out_specs=pl.BlockSpec((tt, d), row),
        out_shape=jax.ShapeDtypeStruct((t, d), F32),
        compiler_params=_cp("parallel"),
        name="moe_combine",
    )(y_loc, lpos, top_w, u, w_sg, w_su, w_sd, h, ada)


def _moe_tables(top_e, n_tok_tiles, n_dummy_tiles, r_loc, n_blocks):
    tt, unit, upb = MOE_TOK_TILE, MOE_UNIT, MOE_BLOCK // MOE_UNIT
    e3 = top_e.reshape(n_tok_tiles, tt * TOP_K)
    onehot = (e3[:, :, None] == jnp.arange(N_EXPERTS, dtype=I32)).astype(I32)
    rank = jnp.sum((jnp.cumsum(onehot, axis=1) - onehot) * onehot, axis=-1)
    cnt = jnp.sum(onehot, axis=1)
    padded = (cnt + unit - 1) // unit * unit
    loff = jnp.cumsum(padded, axis=1) - padded
    lpos = jnp.take_along_axis(loff, e3, axis=1) + rank
    lpos = jnp.concatenate([lpos.reshape(n_tok_tiles * tt, TOP_K),
                            jnp.full((n_dummy_tiles * tt, TOP_K), -1, I32)], axis=0)

    nu = padded // unit
    nu_e = jnp.sum(nu, axis=0)
    pnu_e = (nu_e + upb - 1) // upb * upb
    end_e = jnp.cumsum(pnu_e)
    start_e = end_e - pnu_e
    seg_start = (start_e[:, None] + (jnp.cumsum(nu, axis=0) - nu).T).reshape(-1)
    seg_nu = nu.T.reshape(-1)
    seg_row = (jnp.arange(n_tok_tiles, dtype=I32)[None, :] * r_loc + loff.T).reshape(-1)
    g = jnp.arange(n_blocks * upb, dtype=I32)
    seg = jnp.clip(jnp.searchsorted(seg_start, g, side="right") - 1, 0, seg_start.shape[0] - 1)
    m = g - seg_start[seg]
    valid = m < seg_nu[seg]
    pad_idx = jnp.cumsum(jnp.logical_not(valid).astype(I32)) - 1
    n_dump_units = n_dummy_tiles * r_loc // unit
    dump_row = n_tok_tiles * r_loc + jnp.minimum(pad_idx, n_dump_units - 1) * unit
    unit_rows = jnp.where(valid, seg_row[seg] + m * unit, dump_row).astype(I32)
    block_e = jnp.minimum(jnp.searchsorted(end_e // upb, jnp.arange(n_blocks + 2, dtype=I32), side="right"),
                          N_EXPERTS - 1).astype(I32)
    n_used = (end_e[-1] // upb).astype(I32).reshape(1)
    return lpos, unit_rows, block_e, n_used


def _moe_ffn(h, gain, ada, blk0, w_router, b_router, w_eg, w_eu, w_ed, w_sg, w_su, w_sd, seq):
    t, d = h.shape
    tt, unit, upb = MOE_TOK_TILE, MOE_UNIT, MOE_BLOCK // MOE_UNIT
    wr = jnp.pad(w_router, ((0, 0), (0, LANES - N_EXPERTS)))
    br = jnp.pad(b_router.astype(F32), (0, LANES - N_EXPERTS)).reshape(1, LANES)
    u, top_e, top_w = _route(h, gain, ada, blk0, blk0 + 1, wr, br, seq)

    n_tok_tiles = t // tt
    r_loc = -(-(tt * TOP_K + N_EXPERTS * (unit - 1)) // LANES) * LANES
    n_dummy_tiles = -(-(N_EXPERTS * (upb - 1) * unit) // r_loc)
    max_units = n_tok_tiles * (tt * TOP_K // unit + N_EXPERTS)
    n_blocks = -(-max_units // upb) + N_EXPERTS
    lpos, unit_rows, block_e, n_used = _moe_tables(top_e, n_tok_tiles, n_dummy_tiles, r_loc, n_blocks)

    x_loc = _dispatch(u, lpos, n_tok_tiles + n_dummy_tiles, r_loc)
    y_loc = _experts(x_loc.reshape(-1, d), unit_rows, block_e, n_used,
                     w_eg.astype(BF16), w_eu.astype(BF16), w_ed.astype(BF16), n_blocks)
    y_loc = y_loc.reshape(n_tok_tiles + n_dummy_tiles, r_loc, d)
    return _combine(y_loc, lpos, top_w, u, w_sg.astype(BF16), w_su.astype(BF16), w_sd.astype(BF16),
                    h, ada, blk0 + 2, seq)


def _fox_layer(h, ada, norm_gain, w_in, b_f, q_gain, k_gain, w_o, batch, seq):
    t, d = h.shape
    hd = N_HEADS_A * HEAD_DIM
    w = jnp.pad(w_in, ((0, 0), (0, LANES - N_HEADS_A))).astype(BF16)
    proj = _norm_mod_matmul(h, norm_gain, ada, 0, 1, w, seq, 640, "fox_in_proj")
    z = proj[:, 3 * hd:3 * hd + N_HEADS_A].reshape(batch, seq, N_HEADS_A)
    z = z.transpose(1, 0, 2).reshape(seq, batch * N_HEADS_A)
    cum = _logsig_cumsum(z, jnp.tile(b_f.astype(F32), batch).reshape(1, -1))
    q, k, v = _fox_prep(proj, cum, q_gain, k_gain, batch, seq)
    bh = batch * N_HEADS_A
    o = _flash(q.reshape(bh, seq // FOX_TQ, FOX_TQ, LANES), k.reshape(bh, seq, LANES),
               v.reshape(bh, seq, LANES), FOX_TQ, FOX_TK, None, "fox_attention")
    o = o.reshape(batch, N_HEADS_A, seq, HEAD_DIM).transpose(0, 2, 1, 3).reshape(t, hd)
    return _oproj_residual(o, w_o.astype(BF16), h, ada, 2, seq)


def _rope_tables(positions):
    inv_freq = ROPE_THETA ** (-jnp.arange(0, ROPE_DIM, 2, dtype=F32) / ROPE_DIM)
    ang = positions.astype(F32).reshape(-1, 1) * inv_freq
    half = ROPE_DIM // 2
    cos, sin = jnp.cos(ang), jnp.sin(ang)
    rest = HEAD_DIM - ROPE_DIM
    cos_h = jnp.concatenate([cos, cos, jnp.ones((ang.shape[0], rest), F32)], axis=1)
    sin_h = jnp.concatenate([-sin, sin, jnp.zeros((ang.shape[0], rest), F32)], axis=1)
    del half
    return jnp.tile(cos_h, (1, LANES // HEAD_DIM)), jnp.tile(sin_h, (1, LANES // HEAD_DIM))


def _group_rows(x, batch, seq, tq):
    g, hpg = N_KV_GROUPS_B, HEADS_PER_GROUP
    x = x.reshape(batch, seq // tq, tq, g, hpg, HEAD_DIM).transpose(0, 3, 1, 4, 2, 5)
    return x.reshape(batch * g, seq // tq, hpg * tq, HEAD_DIM)


def _ungroup_rows(x, batch, seq, tq):
    g, hpg = N_KV_GROUPS_B, HEADS_PER_GROUP
    x = x.reshape(batch, g, seq // tq, hpg, tq, HEAD_DIM).transpose(0, 2, 4, 1, 3, 5)
    return x.reshape(batch * seq, g * hpg * HEAD_DIM)


def _per_group(x, batch, seq):
    g = N_KV_GROUPS_B
    return x.reshape(batch, seq, g, HEAD_DIM).transpose(0, 2, 1, 3).reshape(batch * g, seq, HEAD_DIM)


def _with_ones(v):
    pad = jnp.zeros(v.shape[:-1] + (LANES - HEAD_DIM,), v.dtype).at[..., 0].set(1)
    return jnp.concatenate([v, pad], axis=-1)


def _shared_kv(h, ada_kv, norm_kv, w_kv, pe_k, pe_v, w_ck1, w_ck2, w_cv1, w_cv2, k_gains, cos_t, sin_t, batch, seq):
    g = N_KV_GROUPS_B
    gw = g * HEAD_DIM
    nb = seq // CMP_BLOCK
    kv = _norm_mod_matmul(h, norm_kv, ada_kv, 0, 1, w_kv.astype(BF16), seq, w_kv.shape[1], "nsa_kv_proj")
    k_sel, v_sel, k_win, v_win = _nsa_kv_prep(kv, cos_t, sin_t, k_gains)

    def blocks(x):
        x = x.reshape(batch, nb, CMP_BLOCK, g, HEAD_DIM).transpose(0, 1, 3, 2, 4)
        return x.reshape(batch * nb * g, CMP_BLOCK * HEAD_DIM)

    k_cmp = _compress(blocks(kv[:, 0:gw]), pe_k.reshape(1, -1), w_ck1.astype(BF16), w_ck2.astype(BF16),
                      k_gains[0], True, "nsa_compress_k")
    v_cmp = _compress(blocks(kv[:, gw:2 * gw]), pe_v.reshape(1, -1), w_cv1.astype(BF16), w_cv2.astype(BF16),
                      k_gains[0], False, "nsa_compress_v")
    to_bg = lambda x: x.reshape(batch, nb, g, HEAD_DIM).transpose(0, 2, 1, 3).reshape(batch * g, nb, HEAD_DIM)
    k_cmp = to_bg(k_cmp)
    v_cmp = jnp.pad(to_bg(v_cmp), ((0, 0), (0, 0), (0, LANES - HEAD_DIM)))

    blk_onehot = (jnp.arange(seq)[:, None] // CMP_BLOCK == jnp.arange(nb)[None, :]).astype(BF16)
    k_sel = jnp.concatenate([_per_group(k_sel, batch, seq),
                             jnp.broadcast_to(blk_onehot, (batch * g, seq, nb))], axis=-1)
    return (k_cmp, v_cmp, k_sel, _with_ones(_per_group(v_sel, batch, seq)),
            _per_group(k_win, batch, seq), _with_ones(_per_group(v_win, batch, seq)))


def _nsa_layer(h, ada, norm_gain, kv, cos_t, sin_t, w_in, b_gate, q_gain, w_o, batch, seq):
    hd = N_HEADS_B * HEAD_DIM
    k_cmp, v_cmp, k_sel, v_sel, k_win, v_win = kv
    n_gate = 3 * N_HEADS_B
    w = jnp.pad(w_in, ((0, 0), (0, LANES - n_gate))).astype(BF16)
    proj = _norm_mod_matmul(h, norm_gain, ada, 0, 1, w, seq, hd + LANES, "nsa_in_proj")
    bg = jnp.pad(b_gate.astype(F32), (0, LANES - n_gate)).reshape(1, LANES)
    q_plain, q_rot, gates = _nsa_q_prep(proj, cos_t, sin_t, q_gain, bg)

    tq = NSA_TQ
    qp = _group_rows(q_plain, batch, seq, tq)
    qr = _group_rows(q_rot, batch, seq, tq)
    o_cmp, negbias = _cmp_select(qp, k_cmp, v_cmp, seq, tq)
    nbias = jnp.broadcast_to(negbias.reshape(qr.shape[0], seq // tq, 1, tq, -1),
                             (qr.shape[0], seq // tq, HEADS_PER_GROUP, tq, negbias.shape[-1]))
    q_sel = jnp.concatenate([qr, nbias.reshape(qr.shape[0], seq // tq, HEADS_PER_GROUP * tq, -1)], axis=-1)
    o_sel = _flash(q_sel, k_sel, v_sel, tq, NSA_TK, None, "nsa_selected")
    o_win = _flash(qr, k_win, v_win, tq, WIN_TK, WINDOW, "nsa_window")
    o_cmp, o_sel, o_win = (_ungroup_rows(o, batch, seq, tq) for o in (o_cmp, o_sel, o_win))
    return _nsa_out(o_cmp, o_sel, o_win, gates, w_o.astype(BF16), h, ada, 2, seq)


def kernel(x, c, positions, w_ada, b_ada, norm_mix, norm_ffn, w_in_a, b_f_a, qnorm_a, knorm_a, w_o_a, norm_kv, w_ada_kv, b_ada_kv, w_kv_b, pe_cmp_k, pe_cmp_v, w_cmp_k1, w_cmp_k2, w_cmp_v1, w_cmp_v2, knorm_b, w_in_b, b_gate_b, qnorm_b, w_o_b, w_router, b_router, w_exp_gate, w_exp_up, w_exp_down, w_sh_gate, w_sh_up, w_sh_down):
    batch, seq, d = x.shape
    depth = w_ada.shape[0]
    n_a = w_in_a.shape[0]
    c_pad = jnp.pad(c, ((0, SUBLANES - batch), (0, 0)))
    cos_t, sin_t = _rope_tables(positions)
    h = x.reshape(batch * seq, d)
    kv = None
    for layer in range(depth):
        ada = _ada(c_pad, w_ada[layer], b_ada[layer])
        if layer < n_a:
            h = _fox_layer(h, ada, norm_mix[layer], w_in_a[layer], b_f_a[layer], qnorm_a[layer],
                           knorm_a[layer], w_o_a[layer], batch, seq)
        else:
            if layer == n_a:
                ada_kv = _ada(c_pad, w_ada_kv, b_ada_kv)
                kv = _shared_kv(h, ada_kv, norm_kv, w_kv_b, pe_cmp_k, pe_cmp_v, w_cmp_k1, w_cmp_k2,
                                w_cmp_v1, w_cmp_v2, knorm_b, cos_t, sin_t, batch, seq)
            j = layer - n_a
            h = _nsa_layer(h, ada, norm_mix[layer], kv, cos_t, sin_t, w_in_b[j], b_gate_b[j], qnorm_b[j],
                           w_o_b[j], batch, seq)
        h = _moe_ffn(h, norm_ffn[layer], ada, 3, w_router[layer], b_router[layer], w_exp_gate[layer],
                     w_exp_up[layer], w_exp_down[layer], w_sh_gate[layer], w_sh_up[layer], w_sh_down[layer], seq)
    return h.reshape(batch, seq, d)
```

```python
import functools

import jax
import jax.numpy as jnp
from jax import lax
from jax.experimental import pallas as pl
from jax.experimental.pallas import tpu as pltpu

F32 = jnp.float32
BF16 = jnp.bfloat16
I32 = jnp.int32

HEAD_DIM = 64
N_HEADS_A = 16
N_HEADS_B = 16
N_KV_GROUPS_B = 4
HEADS_PER_GROUP = N_HEADS_B // N_KV_GROUPS_B
ROPE_THETA = 500000.0
ROPE_DIM = HEAD_DIM // 4
CMP_BLOCK = 64
N_SELECT = 16
WINDOW = 512
N_EXPERTS = 64
TOP_K = 8
ROUTED_SCALE = 2.5
EPS = 1e-6
FORCE_SCORE = 1e4
NEG = -1e30
LOG2E = 1.4426950408889634
QSCALE = HEAD_DIM ** -0.5 * LOG2E

LANES = 128
SUBLANES = 8
BF16_ROWS = 16

ROW_TILE = 512
FOX_TQ = 512
FOX_TK = 512
NSA_TQ = 256
NSA_TK = 512
WIN_TK = 256
MOE_TOK_TILE = 256
MOE_UNIT = BF16_ROWS
MOE_BLOCK = 256
VMEM_LIMIT = 56 * 1024 * 1024


def _cp(*sem):
    return pltpu.CompilerParams(dimension_semantics=sem, vmem_limit_bytes=VMEM_LIMIT)


def _dot(a, b):
    return jnp.dot(a, b, preferred_element_type=F32)


def _dot_nt(a, b):
    return lax.dot_general(a, b, (((1,), (1,)), ((), ())), preferred_element_type=F32)


def _dot_tn(a, b):
    return lax.dot_general(a, b, (((0,), (0,)), ((), ())), preferred_element_type=F32)


def _split2(x):
    hi = x.astype(BF16)
    lo = (x - hi.astype(F32)).astype(BF16)
    return hi, lo


def _split3(x):
    hi = x.astype(BF16)
    r = x - hi.astype(F32)
    mid = r.astype(BF16)
    lo = (r - mid.astype(F32)).astype(BF16)
    return hi, mid, lo


def _dot_f32(a, b):
    a_hi, a_lo = _split2(a)
    b_hi, b_lo = _split2(b)
    return _dot(a_hi, b_hi) + _dot(a_lo, b_hi) + _dot(a_hi, b_lo)


def _sigmoid(x):
    return 1.0 / (1.0 + jnp.exp(-x))


def _silu(x):
    return x * _sigmoid(x)


def _ada_kernel(c_ref, w_ref, b_ref, o_ref):
    o_ref[...] = _dot_f32(_silu(c_ref[...]), w_ref[...]) + b_ref[...]


def _ada(c_pad, w, b):
    d, n = w.shape
    tn = 1024
    return pl.pallas_call(
        _ada_kernel,
        grid=(n // tn,),
        in_specs=[pl.BlockSpec((SUBLANES, d), lambda j: (0, 0)),
                  pl.BlockSpec((d, tn), lambda j: (0, j)),
                  pl.BlockSpec((1, tn), lambda j: (0, j))],
        out_specs=pl.BlockSpec((SUBLANES, tn), lambda j: (0, j)),
        out_shape=jax.ShapeDtypeStruct((SUBLANES, n), F32),
        compiler_params=_cp("parallel"),
        name="ada",
    )(c_pad, w, b.reshape(1, n))


def _norm_mod(x, gain, shift, scale):
    xn = x * lax.rsqrt(jnp.mean(x * x, axis=-1, keepdims=True) + EPS) * gain
    return xn * (1.0 + scale) + shift


def _nm_kernel(h_ref, gain_ref, shift_ref, scale_ref, w_ref, o_ref, u_scr, *, tiles_per_batch):
    i = pl.program_id(0)

    @pl.when(pl.program_id(1) == 0)
    def _():
        b = i // tiles_per_batch
        u = _norm_mod(h_ref[...], gain_ref[...], shift_ref[pl.ds(b, 1), :], scale_ref[pl.ds(b, 1), :])
        u_scr[...] = u.astype(BF16)

    o_ref[...] = _dot(u_scr[...], w_ref[...]).astype(o_ref.dtype)


def _norm_mod_matmul(h, gain, ada, shift_blk, scale_blk, w, seq, tn, name):
    t, d = h.shape
    n = w.shape[1]
    tm = ROW_TILE
    return pl.pallas_call(
        functools.partial(_nm_kernel, tiles_per_batch=seq // tm),
        grid=(t // tm, n // tn),
        in_specs=[pl.BlockSpec((tm, d), lambda i, j: (i, 0)),
                  pl.BlockSpec((1, d), lambda i, j: (0, 0)),
                  pl.BlockSpec((SUBLANES, d), lambda i, j: (0, shift_blk)),
                  pl.BlockSpec((SUBLANES, d), lambda i, j: (0, scale_blk)),
                  pl.BlockSpec((d, tn), lambda i, j: (0, j))],
        out_specs=pl.BlockSpec((tm, tn), lambda i, j: (i, j)),
        out_shape=jax.ShapeDtypeStruct((t, n), F32),
        scratch_shapes=[pltpu.VMEM((tm, d), BF16)],
        compiler_params=_cp("parallel", "arbitrary"),
        name=name,
    )(h, gain.reshape(1, d), ada, ada, w)


def _logsig_cumsum_kernel(z_ref, b_ref, tri_ref, o_ref, carry):
    @pl.when(pl.program_id(0) == 0)
    def _():
        carry[...] = jnp.zeros_like(carry)

    z = z_ref[...] + b_ref[...]
    lf = jnp.minimum(z, 0.0) - jnp.log1p(jnp.exp(-jnp.abs(z)))
    hi, mid, lo = _split3(lf)
    tri = tri_ref[...]
    c = _dot(tri, hi) + _dot(tri, mid) + _dot(tri, lo) + carry[...]
    o_ref[...] = c
    ts = c.shape[0]
    carry[...] = c[ts - 1:ts, :]


def _logsig_cumsum(z, bias):
    s, c = z.shape
    ts = 256
    tri = (jnp.arange(ts)[:, None] >= jnp.arange(ts)[None, :]).astype(BF16)
    return pl.pallas_call(
        _logsig_cumsum_kernel,
        grid=(s // ts,),
        in_specs=[pl.BlockSpec((ts, c), lambda i: (i, 0)),
                  pl.BlockSpec((1, c), lambda i: (0, 0)),
                  pl.BlockSpec((ts, ts), lambda i: (0, 0))],
        out_specs=pl.BlockSpec((ts, c), lambda i: (i, 0)),
        out_shape=jax.ShapeDtypeStruct((s, c), F32),
        scratch_shapes=[pltpu.VMEM((1, c), F32)],
        compiler_params=_cp("arbitrary"),
        name="fox_logsig_cumsum",
    )(z, bias, tri)


def _head_rms(x, gain):
    return x * lax.rsqrt(jnp.mean(x * x, axis=-1, keepdims=True) + EPS) * gain


def _fox_prep_kernel(q_ref, k_ref, v_ref, cum_ref, qg_ref, kg_ref, qo_ref, ko_ref, vo_ref):
    bh0 = (pl.program_id(0) * (N_HEADS_A // 2) + pl.program_id(1)) * 2
    ts = q_ref.shape[0]
    cum = cum_ref[...]
    lane_c = lax.broadcasted_iota(I32, cum.shape, 1)
    lane = lax.broadcasted_iota(I32, (ts, HEAD_DIM), 1)
    for h2 in range(2):
        sl = slice(h2 * HEAD_DIM, (h2 + 1) * HEAD_DIM)
        c = jnp.sum(jnp.where(lane_c == bh0 + h2, cum, 0.0), axis=-1, keepdims=True) * LOG2E
        hi, mid, lo = (p.astype(F32) for p in _split3(c))
        qn = _head_rms(q_ref[:, sl], qg_ref[...]) * QSCALE
        kn = _head_rms(k_ref[:, sl], kg_ref[...])
        ext_q = jnp.where(lane == 0, hi, jnp.where(lane == 1, mid, jnp.where(lane == 2, lo,
                          jnp.where(lane < 6, 1.0, 0.0))))
        ext_k = jnp.where(lane < 3, 1.0, jnp.where(lane == 3, -hi, jnp.where(lane == 4, -mid,
                          jnp.where(lane == 5, -lo, 0.0))))
        ext_v = jnp.where(lane == 0, 1.0, 0.0)
        qo_ref[0, h2] = jnp.concatenate([qn, ext_q], axis=1).astype(BF16)
        ko_ref[0, h2] = jnp.concatenate([kn, ext_k], axis=1).T.astype(BF16)
        vo_ref[0, h2] = jnp.concatenate([v_ref[:, sl], ext_v], axis=1).astype(BF16)


def _fox_prep(proj, cum, q_gain, k_gain, batch, seq):
    ts = ROW_TILE
    nsb = seq // ts
    hp = N_HEADS_A // 2
    out = jax.ShapeDtypeStruct((batch, N_HEADS_A, seq, LANES), BF16)
    out_t = jax.ShapeDtypeStruct((batch, N_HEADS_A, LANES, seq), BF16)
    ospec = pl.BlockSpec((1, 2, ts, LANES), lambda b, p, i: (b, p, i, 0))
    ospec_t = pl.BlockSpec((1, 2, LANES, ts), lambda b, p, i: (b, p, 0, i))
    return pl.pallas_call(
        _fox_prep_kernel,
        grid=(batch, hp, nsb),
        in_specs=[pl.BlockSpec((ts, LANES), lambda b, p, i: (b * nsb + i, p)),
                  pl.BlockSpec((ts, LANES), lambda b, p, i: (b * nsb + i, hp + p)),
                  pl.BlockSpec((ts, LANES), lambda b, p, i: (b * nsb + i, 2 * hp + p)),
                  pl.BlockSpec((ts, cum.shape[1]), lambda b, p, i: (i, 0)),
                  pl.BlockSpec((1, HEAD_DIM), lambda b, p, i: (0, 0)),
                  pl.BlockSpec((1, HEAD_DIM), lambda b, p, i: (0, 0))],
        out_specs=[ospec, ospec_t, ospec],
        out_shape=[out, out_t, out],
        compiler_params=_cp("parallel", "parallel", "parallel"),
        name="fox_prep",
    )(proj, proj, proj, cum, q_gain.reshape(1, HEAD_DIM), k_gain.reshape(1, HEAD_DIM))


def _flash_kernel(q_ref, k_ref, v_ref, o_ref, m_scr, acc_scr, *, tq, tk, window):
    i = pl.program_id(1)
    q = q_ref[0, 0]
    rows = q.shape[0]
    m_scr[...] = jnp.full(m_scr.shape, NEG, F32)
    acc_scr[...] = jnp.zeros(acc_scr.shape, F32)
    q_pos = i * tq + jnp.bitwise_and(lax.broadcasted_iota(I32, (rows, 1), 0), tq - 1)

    def step(j, masked):
        ks = pl.multiple_of(j * tk, tk)
        s = _dot(q, k_ref[0, :, pl.ds(ks, tk)])
        if masked:
            k_pos = ks + lax.broadcasted_iota(I32, (1, tk), 1)
            ok = k_pos <= q_pos
            if window is not None:
                ok = jnp.logical_and(ok, q_pos - k_pos < window)
            s = jnp.where(ok, s, NEG)
        m_old = m_scr[...]
        m_new = jnp.maximum(m_old, jnp.max(s, axis=-1, keepdims=True))
        p = jnp.exp2(s - jnp.tile(m_new, (1, tk // LANES))).astype(BF16)
        acc_scr[...] = jnp.exp2(m_old - m_new) * acc_scr[...] + _dot(p, v_ref[0, pl.ds(ks, tk), :])
        m_scr[...] = m_new

    q_lo = i * tq
    q_hi = q_lo + tq - 1
    j_hi = q_hi // tk
    if window is None:
        j_clear = (q_lo + 1) // tk
        lax.fori_loop(0, j_clear, lambda j, c: (step(j, False), c)[1], 0)
        lax.fori_loop(j_clear, j_hi + 1, lambda j, c: (step(j, True), c)[1], 0)
    else:
        j_lo = jnp.maximum(q_lo - window + 1, 0) // tk
        lax.fori_loop(j_lo, j_hi + 1, lambda j, c: (step(j, True), c)[1], 0)

    acc = acc_scr[...]
    o_ref[0, 0] = (acc[:, :HEAD_DIM] / acc[:, HEAD_DIM:HEAD_DIM + 1]).astype(o_ref.dtype)


def _flash(q, k, v, tq, tk, window, name):
    bh, nq, rows, kd = q.shape
    s = k.shape[2]
    return pl.pallas_call(
        functools.partial(_flash_kernel, tq=tq, tk=tk, window=window),
        grid=(bh, nq),
        in_specs=[pl.BlockSpec((1, 1, rows, kd), lambda b, i: (b, i, 0, 0)),
                  pl.BlockSpec((1, kd, s), lambda b, i: (b, 0, 0)),
                  pl.BlockSpec((1, s, LANES), lambda b, i: (b, 0, 0))],
        out_specs=pl.BlockSpec((1, 1, rows, HEAD_DIM), lambda b, i: (b, i, 0, 0)),
        out_shape=jax.ShapeDtypeStruct((bh, nq, rows, HEAD_DIM), BF16),
        scratch_shapes=[pltpu.VMEM((rows, LANES), F32), pltpu.VMEM((rows, LANES), F32)],
        compiler_params=_cp("parallel", "arbitrary"),
        name=name,
    )(q, k, v)


def _oproj_kernel(o_ref, w_ref, h_ref, g_ref, out_ref, *, tiles_per_batch):
    b = pl.program_id(0) // tiles_per_batch
    out_ref[...] = h_ref[...] + g_ref[pl.ds(b, 1), :] * _dot(o_ref[...], w_ref[...])


def _oproj_residual(o, w, h, ada, gate_blk, seq):
    t, d = h.shape
    tm = ROW_TILE
    return pl.pallas_call(
        functools.partial(_oproj_kernel, tiles_per_batch=seq // tm),
        grid=(t // tm,),
        in_specs=[pl.BlockSpec((tm, o.shape[1]), lambda i: (i, 0)),
                  pl.BlockSpec(w.shape, lambda i: (0, 0)),
                  pl.BlockSpec((tm, d), lambda i: (i, 0)),
                  pl.BlockSpec((SUBLANES, d), lambda i: (0, gate_blk))],
        out_specs=pl.BlockSpec((tm, d), lambda i: (i, 0)),
        out_shape=jax.ShapeDtypeStruct((t, d), F32),
        compiler_params=_cp("parallel"),
        name="oproj_residual",
    )(o, w, h, ada)


def _seg_rms(x, seg_ones, gain_tiled):
    ss = _dot((x * x).astype(BF16), seg_ones)
    return x * lax.rsqrt(ss * (1.0 / HEAD_DIM) + EPS) * gain_tiled


def _rope(x, cos_t, sin_t):
    width = x.shape[1]
    reps = width // LANES
    lane = jnp.bitwise_and(lax.broadcasted_iota(I32, x.shape, 1), HEAD_DIM - 1)
    half = ROPE_DIM // 2
    partner = jnp.where(lane < half, pltpu.roll(x, width - half, 1), pltpu.roll(x, half, 1))
    return x * jnp.tile(cos_t, (1, reps)) + partner * jnp.tile(sin_t, (1, reps))


def _nsa_q_kernel(p_ref, z_ref, cos_ref, sin_ref, gain_ref, seg_ref, bg_ref, qp_ref, qr_ref, g_ref):
    q = _seg_rms(p_ref[...], seg_ref[...], gain_ref[...]) * QSCALE
    qp_ref[...] = q.astype(BF16)
    qr_ref[...] = _rope(q, cos_ref[...], sin_ref[...]).astype(BF16)
    g_ref[...] = _sigmoid(z_ref[...] + bg_ref[...])


def _nsa_q_prep(proj, cos_t, sin_t, q_gain, b_gate_pad):
    t = proj.shape[0]
    hd = N_HEADS_B * HEAD_DIM
    tm = ROW_TILE
    seg = (jnp.arange(hd)[:, None] // HEAD_DIM == jnp.arange(hd)[None, :] // HEAD_DIM).astype(BF16)
    row = lambda i: (i, 0)
    fix = lambda i: (0, 0)
    return pl.pallas_call(
        _nsa_q_kernel,
        grid=(t // tm,),
        in_specs=[pl.BlockSpec((tm, hd), row),
                  pl.BlockSpec((tm, LANES), lambda i: (i, hd // LANES)),
                  pl.BlockSpec((tm, LANES), row), pl.BlockSpec((tm, LANES), row),
                  pl.BlockSpec((1, hd), fix), pl.BlockSpec((hd, hd), fix), pl.BlockSpec((1, LANES), fix)],
        out_specs=[pl.BlockSpec((tm, hd), row), pl.BlockSpec((tm, hd), row), pl.BlockSpec((tm, LANES), row)],
        out_shape=[jax.ShapeDtypeStruct((t, hd), BF16), jax.ShapeDtypeStruct((t, hd), BF16),
                   jax.ShapeDtypeStruct((t, LANES), F32)],
        compiler_params=_cp("parallel"),
        name="nsa_q_prep",
    )(proj, proj, cos_t, sin_t, jnp.tile(q_gain, N_HEADS_B).reshape(1, hd), seg, b_gate_pad)


def _nsa_kv_kernel(kv_ref, cos_ref, sin_ref, g1_ref, g2_ref, seg_ref, ks_ref, vs_ref, kw_ref, vw_ref):
    gw = N_KV_GROUPS_B * HEAD_DIM
    ks = _seg_rms(kv_ref[:, 2 * gw:3 * gw], seg_ref[...], g1_ref[...])
    kw = _seg_rms(kv_ref[:, 4 * gw:5 * gw], seg_ref[...], g2_ref[...])
    ks_ref[...] = _rope(ks, cos_ref[...], sin_ref[...]).astype(BF16)
    kw_ref[...] = _rope(kw, cos_ref[...], sin_ref[...]).astype(BF16)
    vs_ref[...] = kv_ref[:, 3 * gw:4 * gw].astype(BF16)
    vw_ref[...] = kv_ref[:, 5 * gw:6 * gw].astype(BF16)


def _nsa_kv_prep(kv, cos_t, sin_t, k_gains):
    t, n = kv.shape
    gw = N_KV_GROUPS_B * HEAD_DIM
    tm = ROW_TILE
    seg = (jnp.arange(gw)[:, None] // HEAD_DIM == jnp.arange(gw)[None, :] // HEAD_DIM).astype(BF16)
    row = lambda i: (i, 0)
    fix = lambda i: (0, 0)
    out = jax.ShapeDtypeStruct((t, gw), BF16)
    return pl.pallas_call(
        _nsa_kv_kernel,
        grid=(t // tm,),
        in_specs=[pl.BlockSpec((tm, n), row), pl.BlockSpec((tm, LANES), row), pl.BlockSpec((tm, LANES), row),
                  pl.BlockSpec((1, gw), fix), pl.BlockSpec((1, gw), fix), pl.BlockSpec((gw, gw), fix)],
        out_specs=[pl.BlockSpec((tm, gw), row)] * 4,
        out_shape=[out] * 4,
        compiler_params=_cp("parallel"),
        name="nsa_kv_prep",
    )(kv, cos_t, sin_t, jnp.tile(k_gains[1], N_KV_GROUPS_B).reshape(1, gw),
      jnp.tile(k_gains[2], N_KV_GROUPS_B).reshape(1, gw), seg)


def _compress_kernel(x_ref, pe_ref, w1_ref, w2_ref, g_ref, o_ref, *, norm):
    x = (x_ref[...] + pe_ref[...]).astype(BF16)
    y = _dot(_silu(_dot(x, w1_ref[...])).astype(BF16), w2_ref[...])
    if norm:
        y = _head_rms(y, g_ref[...])
    o_ref[...] = y.astype(o_ref.dtype)


def _compress(x, pe_flat, w1, w2, gain, norm, name):
    r, kdim = x.shape
    tm = min(ROW_TILE, r)
    fix = lambda i: (0, 0)
    return pl.pallas_call(
        functools.partial(_compress_kernel, norm=norm),
        grid=(r // tm,),
        in_specs=[pl.BlockSpec((tm, kdim), lambda i: (i, 0)), pl.BlockSpec((1, kdim), fix),
                  pl.BlockSpec(w1.shape, fix), pl.BlockSpec(w2.shape, fix), pl.BlockSpec((1, HEAD_DIM), fix)],
        out_specs=pl.BlockSpec((tm, HEAD_DIM), lambda i: (i, 0)),
        out_shape=jax.ShapeDtypeStruct((r, HEAD_DIM), BF16),
        compiler_params=_cp("parallel"),
        name=name,
    )(x, pe_flat, w1, w2, gain.reshape(1, HEAD_DIM))


def _cmp_select_kernel(q_ref, k_ref, v_ref, o_ref, nb_ref, *, tq, n_sel):
    i = pl.program_id(1)
    q = q_ref[0, 0]
    rows = q.shape[0]
    nblk = k_ref.shape[1]
    logits = _dot_nt(q, k_ref[0])
    t_pos = i * tq + jnp.bitwise_and(lax.broadcasted_iota(I32, (rows, 1), 0), tq - 1)
    blk = lax.broadcasted_iota(I32, (1, nblk), 1)
    vis = (blk + 1) * CMP_BLOCK - 1 <= t_pos
    lm = jnp.where(vis, logits, NEG)
    e = jnp.exp2(lm - jnp.max(lm, axis=-1, keepdims=True))
    p = jnp.where(vis, e / jnp.sum(e, axis=-1, keepdims=True), 0.0)
    o_ref[0, 0] = _dot(p.astype(BF16), v_ref[0])[:, :HEAD_DIM].astype(o_ref.dtype)

    imp = p[0:tq]
    for hh in range(1, rows // tq):
        imp = imp + p[hh * tq:(hh + 1) * tq]
    cur = lax.shift_right_logical(t_pos[0:tq], CMP_BLOCK.bit_length() - 1)
    forced = jnp.logical_or(blk == 0, jnp.logical_or(blk == cur, blk == cur - 1))
    imp = jnp.where(forced, FORCE_SCORE, jnp.where(blk > cur, -FORCE_SCORE, imp))
    blk_f = blk.astype(F32)
    sel = jnp.zeros(imp.shape, F32)
    for _ in range(n_sel):
        best = jnp.max(imp, axis=-1, keepdims=True)
        first = jnp.min(jnp.where(imp == best, blk_f, float(nblk)), axis=-1, keepdims=True)
        hit = blk_f == first
        sel = jnp.where(hit, 1.0, sel)
        imp = jnp.where(hit, -jnp.inf, imp)
    nb_ref[0] = jnp.where(sel > 0.5, 0.0, NEG).astype(nb_ref.dtype)


def _cmp_select(q, k_cmp, v_cmp, seq, tq):
    bg, nq, rows, _ = q.shape
    nblk = k_cmp.shape[1]
    return pl.pallas_call(
        functools.partial(_cmp_select_kernel, tq=tq, n_sel=min(N_SELECT, nblk)),
        grid=(bg, nq),
        in_specs=[pl.BlockSpec((1, 1, rows, HEAD_DIM), lambda b, i: (b, i, 0, 0)),
                  pl.BlockSpec((1, nblk, HEAD_DIM), lambda b, i: (b, 0, 0)),
                  pl.BlockSpec((1, nblk, LANES), lambda b, i: (b, 0, 0))],
        out_specs=[pl.BlockSpec((1, 1, rows, HEAD_DIM), lambda b, i: (b, i, 0, 0)),
                   pl.BlockSpec((1, tq, nblk), lambda b, i: (b, i, 0))],
        out_shape=[jax.ShapeDtypeStruct((bg, nq, rows, HEAD_DIM), BF16),
                   jax.ShapeDtypeStruct((bg, seq, nblk), BF16)],
        compiler_params=_cp("parallel", "parallel"),
        name="nsa_cmp_select",
    )(q, k_cmp, v_cmp)


def _nsa_out_kernel(oc_ref, os_ref, ow_ref, g_ref, e_ref, w_ref, h_ref, ada_ref, out_ref, *, tiles_per_batch):
    b = pl.program_id(0) // tiles_per_batch
    g_hi, g_lo = _split2(g_ref[...])
    o = None
    for j, o_ref in enumerate((oc_ref, os_ref, ow_ref)):
        gate = _dot(g_hi, e_ref[j]) + _dot(g_lo, e_ref[j])
        term = gate * o_ref[...].astype(F32)
        o = term if o is None else o + term
    out_ref[...] = h_ref[...] + ada_ref[pl.ds(b, 1), :] * _dot(o.astype(BF16), w_ref[...])


def _nsa_out(o_cmp, o_sel, o_win, gates, w_o, h, ada, gate_blk, seq):
    t, d = h.shape
    hd = o_cmp.shape[1]
    tm = ROW_TILE
    col = jnp.arange(LANES)[:, None]
    lane = jnp.arange(hd)[None, :]
    expand = jnp.stack([(col == (lane // HEAD_DIM) * 3 + j) for j in range(3)]).astype(BF16)
    row = lambda i: (i, 0)
    return pl.pallas_call(
        functools.partial(_nsa_out_kernel, tiles_per_batch=seq // tm),
        grid=(t // tm,),
        in_specs=[pl.BlockSpec((tm, hd), row), pl.BlockSpec((tm, hd), row), pl.BlockSpec((tm, hd), row),
                  pl.BlockSpec((tm, LANES), row),
                  pl.BlockSpec((3, LANES, hd), lambda i: (0, 0, 0)),
                  pl.BlockSpec(w_o.shape, lambda i: (0, 0)),
                  pl.BlockSpec((tm, d), row),
                  pl.BlockSpec((SUBLANES, d), lambda i: (0, gate_blk))],
        out_specs=pl.BlockSpec((tm, d), row),
        out_shape=jax.ShapeDtypeStruct((t, d), F32),
        compiler_params=_cp("parallel"),
        name="nsa_out",
    )(o_cmp, o_sel, o_win, gates, expand, w_o, h, ada)


def _route_kernel(h_ref, gain_ref, shift_ref, scale_ref, wr_ref, br_ref, tri_ref, upper_ref,
                  u_ref, pos_ref, w_ref, nu_ref, off_ref, *, tiles_per_batch):
    b = pl.program_id(0) // tiles_per_batch
    u = _norm_mod(h_ref[...], gain_ref[...], shift_ref[pl.ds(b, 1), :], scale_ref[pl.ds(b, 1), :])
    u_ref[...] = u.astype(BF16)
    scores = _sigmoid(_dot_f32(u, wr_ref[...]))
    lane = lax.broadcasted_iota(I32, scores.shape, 1)
    lane_f = lane.astype(F32)
    biased = jnp.where(lane < N_EXPERTS, scores + br_ref[...], -jnp.inf)
    chosen = jnp.zeros(scores.shape, F32)
    hits, val = [], []
    for _ in range(TOP_K):
        best = jnp.max(biased, axis=-1, keepdims=True)
        first = jnp.min(jnp.where(biased == best, lane_f, float(LANES)), axis=-1, keepdims=True)
        hit = lane_f == first
        hits.append(hit)
        val.append(jnp.sum(jnp.where(hit, scores, 0.0), axis=-1, keepdims=True))
        biased = jnp.where(hit, -jnp.inf, biased)
        chosen = jnp.where(hit, 1.0, chosen)
    top_s = jnp.concatenate(val, axis=1)
    w_ref[...] = top_s / jnp.sum(top_s, axis=-1, keepdims=True) * ROUTED_SCALE

    before = _dot(tri_ref[...], chosen.astype(BF16))
    count = jnp.sum(chosen, axis=0, keepdims=True)
    units = jnp.floor((count + (MOE_UNIT - 1)) * (1.0 / MOE_UNIT))
    padded = jnp.broadcast_to(units * MOE_UNIT, (SUBLANES, LANES))
    offset = _dot(padded.astype(BF16), upper_ref[...])[0:1]
    row_of = before + offset
    pos = [jnp.sum(jnp.where(hit, row_of, 0.0), axis=-1, keepdims=True) for hit in hits]
    pos_ref[...] = jnp.concatenate(pos, axis=1).astype(I32)
    nu_ref[0] = units.astype(I32)
    off_ref[0] = offset.astype(I32)


def _route(h, gain, ada, shift_blk, scale_blk, w_router_pad, b_router_pad, seq):
    t, d = h.shape
    tt = MOE_TOK_TILE
    n_tiles = t // tt
    row = lambda i: (i, 0)
    fix = lambda i: (0, 0)
    tri = (jnp.arange(tt)[:, None] > jnp.arange(tt)[None, :]).astype(BF16)
    upper = (jnp.arange(LANES)[:, None] < jnp.arange(LANES)[None, :]).astype(BF16)
    per_tile = jax.ShapeDtypeStruct((n_tiles, 1, LANES), I32)
    per_tile_spec = pl.BlockSpec((1, 1, LANES), lambda i: (i, 0, 0))
    return pl.pallas_call(
        functools.partial(_route_kernel, tiles_per_batch=seq // tt),
        grid=(n_tiles,),
        in_specs=[pl.BlockSpec((tt, d), row), pl.BlockSpec((1, d), fix),
                  pl.BlockSpec((SUBLANES, d), lambda i: (0, shift_blk)),
                  pl.BlockSpec((SUBLANES, d), lambda i: (0, scale_blk)),
                  pl.BlockSpec((d, LANES), fix), pl.BlockSpec((1, LANES), fix),
                  pl.BlockSpec((tt, tt), fix), pl.BlockSpec((LANES, LANES), fix)],
        out_specs=[pl.BlockSpec((tt, d), row), pl.BlockSpec((tt, TOP_K), row), pl.BlockSpec((tt, TOP_K), row),
                   per_tile_spec, per_tile_spec],
        out_shape=[jax.ShapeDtypeStruct((t, d), BF16), jax.ShapeDtypeStruct((t, TOP_K), I32),
                   jax.ShapeDtypeStruct((t, TOP_K), F32), per_tile, per_tile],
        compiler_params=_cp("parallel"),
        name="moe_route",
    )(h, gain.reshape(1, d), ada, ada, w_router_pad, b_router_pad, tri, upper)


def _plan_kernel(nu_ref, off_ref, tbl_ref, be_ref, nused_ref, *, n_tiles, r_loc, upb, n_blocks):
    dump_base = n_tiles * r_loc

    def per_expert(e, carry):
        g0, pad = carry

        def per_tile(i, g):
            n = nu_ref[i * N_EXPERTS + e]
            base = i * r_loc + off_ref[i * N_EXPERTS + e]

            def per_unit(m, c):
                tbl_ref[g + m] = base + m * MOE_UNIT
                return c

            lax.fori_loop(0, n, per_unit, 0)
            return g + n

        g = lax.fori_loop(0, n_tiles, per_tile, g0)
        fill = lax.rem(upb - lax.rem(g - g0, upb), upb)

        def per_pad(m, c):
            tbl_ref[g + m] = dump_base + (pad + m) * MOE_UNIT
            return c

        lax.fori_loop(0, fill, per_pad, 0)
        g_end = g + fill

        def per_block(blk, c):
            be_ref[blk] = e
            return c

        lax.fori_loop(g0 // upb, g_end // upb, per_block, 0)
        return g_end, pad + fill

    g_end, _ = lax.fori_loop(0, N_EXPERTS, per_expert, (0, 0))
    nused_ref[0] = g_end // upb

    def tail_unit(m, c):
        tbl_ref[m] = dump_base
        return c

    def tail_block(blk, c):
        be_ref[blk] = N_EXPERTS - 1
        return c

    lax.fori_loop(g_end, n_blocks * upb, tail_unit, 0)
    lax.fori_loop(g_end // upb, n_blocks + 2, tail_block, 0)


def _plan(nu, off, n_tiles, r_loc, n_blocks):
    upb = MOE_BLOCK // MOE_UNIT
    smem = pl.BlockSpec(memory_space=pltpu.SMEM)
    return pl.pallas_call(
        functools.partial(_plan_kernel, n_tiles=n_tiles, r_loc=r_loc, upb=upb, n_blocks=n_blocks),
        in_specs=[smem, smem],
        out_specs=[smem, smem, smem],
        out_shape=[jax.ShapeDtypeStruct((n_blocks * upb,), I32), jax.ShapeDtypeStruct((n_blocks + 2,), I32),
                   jax.ShapeDtypeStruct((1,), I32)],
        name="moe_plan",
    )(nu.reshape(-1), off.reshape(-1))


def _one_hot_t(pos_ref, weight_ref, n_rows):
    tt = pos_ref.shape[0]
    col = lax.broadcasted_iota(I32, (tt, n_rows), 1)
    acc = jnp.zeros((tt, n_rows), F32)
    for k in range(TOP_K):
        hit = col == pos_ref[:, k:k + 1]
        acc = acc + jnp.where(hit, 1.0 if weight_ref is None else weight_ref[:, k:k + 1], 0.0)
    return acc.astype(BF16)


def _dispatch_kernel(u_ref, pos_ref, x_ref):
    x_ref[0] = _dot_tn(_one_hot_t(pos_ref, None, x_ref.shape[1]), u_ref[...]).astype(x_ref.dtype)


def _dispatch(u, lpos, n_tiles_total, r_loc):
    t, d = u.shape
    tt = MOE_TOK_TILE
    n_real = t // tt
    return pl.pallas_call(
        _dispatch_kernel,
        grid=(n_tiles_total,),
        in_specs=[pl.BlockSpec((tt, d), lambda i: (jnp.minimum(i, n_real - 1), 0)),
                  pl.BlockSpec((tt, TOP_K), lambda i: (i, 0))],
        out_specs=pl.BlockSpec((1, r_loc, d), lambda i: (i, 0, 0)),
        out_shape=jax.ShapeDtypeStruct((n_tiles_total, r_loc, d), BF16),
        compiler_params=_cp("parallel"),
        name="moe_dispatch",
    )(u, lpos)


def _expert_kernel(tbl_ref, be_ref, nblk_ref, x_hbm, wg_ref, wu_ref, wd_ref, y_hbm, xbuf, ybuf, gsem, ssem, *, upb):
    del be_ref
    b = pl.program_id(0)
    nblk = nblk_ref[0]
    slot = lax.rem(b, 2)

    def gather(blk, sl):
        for m in range(upb):
            row = pl.multiple_of(tbl_ref[blk * upb + m], MOE_UNIT)
            yield pltpu.make_async_copy(x_hbm.at[pl.ds(row, MOE_UNIT), :],
                                        xbuf.at[sl, pl.ds(m * MOE_UNIT, MOE_UNIT), :], gsem.at[sl, m])

    def scatter(blk, sl):
        for m in range(upb):
            row = pl.multiple_of(tbl_ref[blk * upb + m], MOE_UNIT)
            yield pltpu.make_async_copy(ybuf.at[sl, pl.ds(m * MOE_UNIT, MOE_UNIT), :],
                                        y_hbm.at[pl.ds(row, MOE_UNIT), :], ssem.at[sl, m])

    @pl.when(jnp.logical_and(b == 0, nblk > 0))
    def _():
        for cp in gather(0, 0):
            cp.start()

    @pl.when(b + 1 < nblk)
    def _():
        for cp in gather(b + 1, 1 - slot):
            cp.start()

    @pl.when(jnp.logical_and(b >= 2, b < nblk + 2))
    def _():
        for cp in scatter(b - 2, slot):
            cp.wait()

    @pl.when(b < nblk)
    def _():
        for cp in gather(b, slot):
            cp.wait()
        x = xbuf[slot]
        hid = _silu(_dot(x, wg_ref[0])) * _dot(x, wu_ref[0])
        ybuf[slot] = _dot(hid.astype(BF16), wd_ref[0]).astype(ybuf.dtype)
        for cp in scatter(b, slot):
            cp.start()


def _experts(x_loc, unit_rows, block_e, n_blocks_used, w_gate, w_up, w_down, n_blocks):
    rows, d = x_loc.shape
    de = w_gate.shape[2]
    upb = MOE_BLOCK // MOE_UNIT
    grid_spec = pltpu.PrefetchScalarGridSpec(
        num_scalar_prefetch=3,
        grid=(n_blocks + 2,),
        in_specs=[pl.BlockSpec(memory_space=pl.ANY),
                  pl.BlockSpec((1, d, de), lambda b, tbl, be, nb: (be[b], 0, 0)),
                  pl.BlockSpec((1, d, de), lambda b, tbl, be, nb: (be[b], 0, 0)),
                  pl.BlockSpec((1, de, d), lambda b, tbl, be, nb: (be[b], 0, 0))],
        out_specs=pl.BlockSpec(memory_space=pl.ANY),
        scratch_shapes=[pltpu.VMEM((2, MOE_BLOCK, d), BF16), pltpu.VMEM((2, MOE_BLOCK, d), BF16),
                        pltpu.SemaphoreType.DMA((2, upb)), pltpu.SemaphoreType.DMA((2, upb))],
    )
    return pl.pallas_call(
        functools.partial(_expert_kernel, upb=upb),
        grid_spec=grid_spec,
        out_shape=jax.ShapeDtypeStruct((rows, d), BF16),
        input_output_aliases={3: 0},
        compiler_params=_cp("arbitrary"),
        name="moe_experts",
    )(unit_rows, block_e, n_blocks_used, x_loc, w_gate, w_up, w_down)


def _combine_kernel(y_ref, pos_ref, tw_ref, u_ref, wsg_ref, wsu_ref, wsd_ref, h_ref, g_ref, out_ref, *, tiles_per_batch):
    b = pl.program_id(0) // tiles_per_batch
    routed = _dot(_one_hot_t(pos_ref, tw_ref, y_ref.shape[1]), y_ref[0])
    u = u_ref[...]
    hid = _silu(_dot(u, wsg_ref[...])) * _dot(u, wsu_ref[...])
    shared = _dot(hid.astype(BF16), wsd_ref[...])
    out_ref[...] = h_ref[...] + g_ref[pl.ds(b, 1), :] * (routed + shared)


def _combine(y_loc, lpos, top_w, u, w_sg, w_su, w_sd, h, ada, gate_blk, seq):
    t, d = h.shape
    tt = MOE_TOK_TILE
    r_loc = y_loc.shape[1]
    row = lambda i: (i, 0)
    fix = lambda i: (0, 0)
    return pl.pallas_call(
        functools.partial(_combine_kernel, tiles_per_batch=seq // tt),
        grid=(t // tt,),
        in_specs=[pl.BlockSpec((1, r_loc, d), lambda i: (i, 0, 0)),
                  pl.BlockSpec((tt, TOP_K), row), pl.BlockSpec((tt, TOP_K), row),
                  pl.BlockSpec((tt, d), row),
                  pl.BlockSpec(w_sg.shape, fix), pl.BlockSpec(w_su.shape, fix), pl.BlockSpec(w_sd.shape, fix),
                  pl.BlockSpec((tt, d), row),
                  pl.BlockSpec((SUBLANES, d), lambda i: (0, gate_blk))],
        out_specs=pl.BlockSpec((tt, d), row),
        out_shape=jax.ShapeDtypeStruct((t, d), F32),
        compiler_params=_cp("parallel"),
        name="moe_combine",
    )(y_loc, lpos, top_w, u, w_sg, w_su, w_sd, h, ada)


def _moe_ffn(h, gain, ada, blk0, w_router, b_router, w_eg, w_eu, w_ed, w_sg, w_su, w_sd, seq):
    t, d = h.shape
    tt, unit, upb = MOE_TOK_TILE, MOE_UNIT, MOE_BLOCK // MOE_UNIT
    wr = jnp.pad(w_router, ((0, 0), (0, LANES - N_EXPERTS)))
    br = jnp.pad(b_router.astype(F32), (0, LANES - N_EXPERTS)).reshape(1, LANES)
    u, lpos, top_w, nu, off = _route(h, gain, ada, blk0, blk0 + 1, wr, br, seq)

    n_tok_tiles = t // tt
    r_loc = -(-(tt * TOP_K + N_EXPERTS * (unit - 1)) // LANES) * LANES
    n_dummy_tiles = -(-(N_EXPERTS * (upb - 1) * unit) // r_loc)
    max_units = n_tok_tiles * (tt * TOP_K // unit + N_EXPERTS)
    n_blocks = -(-max_units // upb) + N_EXPERTS
    unit_rows, block_e, n_used = _plan(nu[:, 0, :N_EXPERTS], off[:, 0, :N_EXPERTS], n_tok_tiles, r_loc, n_blocks)
    lpos = jnp.concatenate([lpos, jnp.full((n_dummy_tiles * tt, TOP_K), -1, I32)], axis=0)

    x_loc = _dispatch(u, lpos, n_tok_tiles + n_dummy_tiles, r_loc)
    y_loc = _experts(x_loc.reshape(-1, d), unit_rows, block_e, n_used,
                     w_eg.astype(BF16), w_eu.astype(BF16), w_ed.astype(BF16), n_blocks)
    y_loc = y_loc.reshape(n_tok_tiles + n_dummy_tiles, r_loc, d)
    return _combine(y_loc, lpos, top_w, u, w_sg.astype(BF16), w_su.astype(BF16), w_sd.astype(BF16),
                    h, ada, blk0 + 2, seq)


def _fox_layer(h, ada, norm_gain, w_in, b_f, q_gain, k_gain, w_o, batch, seq):
    t, d = h.shape
    hd = N_HEADS_A * HEAD_DIM
    w = jnp.pad(w_in, ((0, 0), (0, LANES - N_HEADS_A))).astype(BF16)
    proj = _norm_mod_matmul(h, norm_gain, ada, 0, 1, w, seq, 640, "fox_in_proj")
    z = proj[:, 3 * hd:3 * hd + N_HEADS_A].reshape(batch, seq, N_HEADS_A)
    z = z.transpose(1, 0, 2).reshape(seq, batch * N_HEADS_A)
    cum = _logsig_cumsum(z, jnp.tile(b_f.astype(F32), batch).reshape(1, -1))
    q, k, v = _fox_prep(proj, cum, q_gain, k_gain, batch, seq)
    bh = batch * N_HEADS_A
    o = _flash(q.reshape(bh, seq // FOX_TQ, FOX_TQ, LANES), k.reshape(bh, LANES, seq),
               v.reshape(bh, seq, LANES), FOX_TQ, FOX_TK, None, "fox_attention")
    o = o.reshape(batch, N_HEADS_A, seq, HEAD_DIM).transpose(0, 2, 1, 3).reshape(t, hd)
    return _oproj_residual(o, w_o.astype(BF16), h, ada, 2, seq)


def _rope_tables(positions):
    inv_freq = ROPE_THETA ** (-jnp.arange(0, ROPE_DIM, 2, dtype=F32) / ROPE_DIM)
    ang = positions.astype(F32).reshape(-1, 1) * inv_freq
    half = ROPE_DIM // 2
    cos, sin = jnp.cos(ang), jnp.sin(ang)
    rest = HEAD_DIM - ROPE_DIM
    cos_h = jnp.concatenate([cos, cos, jnp.ones((ang.shape[0], rest), F32)], axis=1)
    sin_h = jnp.concatenate([-sin, sin, jnp.zeros((ang.shape[0], rest), F32)], axis=1)
    del half
    return jnp.tile(cos_h, (1, LANES // HEAD_DIM)), jnp.tile(sin_h, (1, LANES // HEAD_DIM))


def _group_rows(x, batch, seq, tq):
    g, hpg = N_KV_GROUPS_B, HEADS_PER_GROUP
    x = x.reshape(batch, seq // tq, tq, g, hpg, HEAD_DIM).transpose(0, 3, 1, 4, 2, 5)
    return x.reshape(batch * g, seq // tq, hpg * tq, HEAD_DIM)


def _ungroup_rows(x, batch, seq, tq):
    g, hpg = N_KV_GROUPS_B, HEADS_PER_GROUP
    x = x.reshape(batch, g, seq // tq, hpg, tq, HEAD_DIM).transpose(0, 2, 4, 1, 3, 5)
    return x.reshape(batch * seq, g * hpg * HEAD_DIM)


def _per_group(x, batch, seq):
    g = N_KV_GROUPS_B
    return x.reshape(batch, seq, g, HEAD_DIM).transpose(0, 2, 1, 3).reshape(batch * g, seq, HEAD_DIM)


def _with_ones(v):
    pad = jnp.zeros(v.shape[:-1] + (LANES - HEAD_DIM,), v.dtype).at[..., 0].set(1)
    return jnp.concatenate([v, pad], axis=-1)


def _shared_kv(h, ada_kv, norm_kv, w_kv, pe_k, pe_v, w_ck1, w_ck2, w_cv1, w_cv2, k_gains, cos_t, sin_t, batch, seq):
    g = N_KV_GROUPS_B
    gw = g * HEAD_DIM
    nb = seq // CMP_BLOCK
    kv = _norm_mod_matmul(h, norm_kv, ada_kv, 0, 1, w_kv.astype(BF16), seq, w_kv.shape[1], "nsa_kv_proj")
    k_sel, v_sel, k_win, v_win = _nsa_kv_prep(kv, cos_t, sin_t, k_gains)

    def blocks(x):
        x = x.reshape(batch, nb, CMP_BLOCK, g, HEAD_DIM).transpose(0, 1, 3, 2, 4)
        return x.reshape(batch * nb * g, CMP_BLOCK * HEAD_DIM)

    k_cmp = _compress(blocks(kv[:, 0:gw]), pe_k.reshape(1, -1), w_ck1.astype(BF16), w_ck2.astype(BF16),
                      k_gains[0], True, "nsa_compress_k")
    v_cmp = _compress(blocks(kv[:, gw:2 * gw]), pe_v.reshape(1, -1), w_cv1.astype(BF16), w_cv2.astype(BF16),
                      k_gains[0], False, "nsa_compress_v")
    to_bg = lambda x: x.reshape(batch, nb, g, HEAD_DIM).transpose(0, 2, 1, 3).reshape(batch * g, nb, HEAD_DIM)
    k_cmp = to_bg(k_cmp)
    v_cmp = jnp.pad(to_bg(v_cmp), ((0, 0), (0, 0), (0, LANES - HEAD_DIM)))

    def per_group_t(x):
        return x.reshape(batch, seq, g, HEAD_DIM).transpose(0, 2, 3, 1).reshape(batch * g, HEAD_DIM, seq)

    blk_onehot = (jnp.arange(nb)[:, None] == jnp.arange(seq)[None, :] // CMP_BLOCK).astype(BF16)
    k_sel = jnp.concatenate([per_group_t(k_sel), jnp.broadcast_to(blk_onehot, (batch * g, nb, seq))], axis=1)
    return (k_cmp, v_cmp, k_sel, _with_ones(_per_group(v_sel, batch, seq)),
            per_group_t(k_win), _with_ones(_per_group(v_win, batch, seq)))


def _nsa_layer(h, ada, norm_gain, kv, cos_t, sin_t, w_in, b_gate, q_gain, w_o, batch, seq):
    hd = N_HEADS_B * HEAD_DIM
    k_cmp, v_cmp, k_sel, v_sel, k_win, v_win = kv
    n_gate = 3 * N_HEADS_B
    w = jnp.pad(w_in, ((0, 0), (0, LANES - n_gate))).astype(BF16)
    proj = _norm_mod_matmul(h, norm_gain, ada, 0, 1, w, seq, hd + LANES, "nsa_in_proj")
    bg = jnp.pad(b_gate.astype(F32), (0, LANES - n_gate)).reshape(1, LANES)
    q_plain, q_rot, gates = _nsa_q_prep(proj, cos_t, sin_t, q_gain, bg)

    tq = NSA_TQ
    qp = _group_rows(q_plain, batch, seq, tq)
    qr = _group_rows(q_rot, batch, seq, tq)
    o_cmp, negbias = _cmp_select(qp, k_cmp, v_cmp, seq, tq)
    nbias = jnp.broadcast_to(negbias.reshape(qr.shape[0], seq // tq, 1, tq, -1),
                             (qr.shape[0], seq // tq, HEADS_PER_GROUP, tq, negbias.shape[-1]))
    q_sel = jnp.concatenate([qr, nbias.reshape(qr.shape[0], seq // tq, HEADS_PER_GROUP * tq, -1)], axis=-1)
    o_sel = _flash(q_sel, k_sel, v_sel, tq, NSA_TK, None, "nsa_selected")
    o_win = _flash(qr, k_win, v_win, tq, WIN_TK, WINDOW, "nsa_window")
    o_cmp, o_sel, o_win = (_ungroup_rows(o, batch, seq, tq) for o in (o_cmp, o_sel, o_win))
    return _nsa_out(o_cmp, o_sel, o_win, gates, w_o.astype(BF16), h, ada, 2, seq)


def kernel(x, c, positions, w_ada, b_ada, norm_mix, norm_ffn, w_in_a, b_f_a, qnorm_a, knorm_a, w_o_a, norm_kv, w_ada_kv, b_ada_kv, w_kv_b, pe_cmp_k, pe_cmp_v, w_cmp_k1, w_cmp_k2, w_cmp_v1, w_cmp_v2, knorm_b, w_in_b, b_gate_b, qnorm_b, w_o_b, w_router, b_router, w_exp_gate, w_exp_up, w_exp_down, w_sh_gate, w_sh_up, w_sh_down):
    batch, seq, d = x.shape
    depth = w_ada.shape[0]
    n_a = w_in_a.shape[0]
    c_pad = jnp.pad(c, ((0, SUBLANES - batch), (0, 0)))
    cos_t, sin_t = _rope_tables(positions)
    h = x.reshape(batch * seq, d)
    kv = None
    for layer in range(depth):
        ada = _ada(c_pad, w_ada[layer], b_ada[layer])
        if layer < n_a:
            h = _fox_layer(h, ada, norm_mix[layer], w_in_a[layer], b_f_a[layer], qnorm_a[layer],
                           knorm_a[layer], w_o_a[layer], batch, seq)
        else:
            if layer == n_a:
                ada_kv = _ada(c_pad, w_ada_kv, b_ada_kv)
                kv = _shared_kv(h, ada_kv, norm_kv, w_kv_b, pe_cmp_k, pe_cmp_v, w_cmp_k1, w_cmp_k2,
                                w_cmp_v1, w_cmp_v2, knorm_b, cos_t, sin_t, batch, seq)
            j = layer - n_a
            h = _nsa_layer(h, ada, norm_mix[layer], kv, cos_t, sin_t, w_in_b[j], b_gate_b[j], qnorm_b[j],
                           w_o_b[j], batch, seq)
        h = _moe_ffn(h, norm_ffn[layer], ada, 3, w_router[layer], b_router[layer], w_exp_gate[layer],
                     w_exp_up[layer], w_exp_down[layer], w_sh_gate[layer], w_sh_up[layer], w_sh_down[layer], seq)
    return h.reshape(batch, seq, d)
```

```python
import functools

import jax
import jax.numpy as jnp
from jax import lax
from jax.experimental import pallas as pl
from jax.experimental.pallas import tpu as pltpu

F32 = jnp.float32
BF16 = jnp.bfloat16
I32 = jnp.int32

HEAD_DIM = 64
N_HEADS_A = 16
N_HEADS_B = 16
N_KV_GROUPS_B = 4
HEADS_PER_GROUP = N_HEADS_B // N_KV_GROUPS_B
ROPE_THETA = 500000.0
ROPE_DIM = HEAD_DIM // 4
CMP_BLOCK = 64
N_SELECT = 16
WINDOW = 512
N_EXPERTS = 64
TOP_K = 8
ROUTED_SCALE = 2.5
EPS = 1e-6
FORCE_SCORE = 1e4
NEG = -1e30
LOG2E = 1.4426950408889634
QSCALE = HEAD_DIM ** -0.5 * LOG2E

LANES = 128
SUBLANES = 8
BF16_ROWS = 16

ROW_TILE = 512
FOX_TQ = 1024
FOX_TK = 512
NSA_TQ = 256
NSA_TK = 512
WIN_TK = 256
MOE_TOK_TILE = 256
MOE_UNIT = BF16_ROWS
MOE_BLOCK = 256
VMEM_LIMIT = 56 * 1024 * 1024


def _cp(*sem):
    return pltpu.CompilerParams(dimension_semantics=sem, vmem_limit_bytes=VMEM_LIMIT)


def _dot(a, b):
    return jnp.dot(a, b, preferred_element_type=F32)


def _dot_nt(a, b):
    return lax.dot_general(a, b, (((1,), (1,)), ((), ())), preferred_element_type=F32)


def _dot_tn(a, b):
    return lax.dot_general(a, b, (((0,), (0,)), ((), ())), preferred_element_type=F32)


def _split2(x):
    hi = x.astype(BF16)
    lo = (x - hi.astype(F32)).astype(BF16)
    return hi, lo


def _split3(x):
    hi = x.astype(BF16)
    r = x - hi.astype(F32)
    mid = r.astype(BF16)
    lo = (r - mid.astype(F32)).astype(BF16)
    return hi, mid, lo


def _dot_f32(a, b):
    a_hi, a_lo = _split2(a)
    b_hi, b_lo = _split2(b)
    return _dot(a_hi, b_hi) + _dot(a_lo, b_hi) + _dot(a_hi, b_lo)


def _sigmoid(x):
    return 1.0 / (1.0 + jnp.exp(-x))


def _silu(x):
    return x * _sigmoid(x)


def _ada_kernel(c_ref, w_ref, b_ref, o_ref):
    o_ref[...] = _dot_f32(_silu(c_ref[...]), w_ref[...]) + b_ref[...]


def _ada(c_pad, w, b):
    d, n = w.shape
    tn = 1024
    return pl.pallas_call(
        _ada_kernel,
        grid=(n // tn,),
        in_specs=[pl.BlockSpec((SUBLANES, d), lambda j: (0, 0)),
                  pl.BlockSpec((d, tn), lambda j: (0, j)),
                  pl.BlockSpec((1, tn), lambda j: (0, j))],
        out_specs=pl.BlockSpec((SUBLANES, tn), lambda j: (0, j)),
        out_shape=jax.ShapeDtypeStruct((SUBLANES, n), F32),
        compiler_params=_cp("parallel"),
        name="ada",
    )(c_pad, w, b.reshape(1, n))


def _norm_mod(x, gain, shift, scale):
    xn = x * lax.rsqrt(jnp.mean(x * x, axis=-1, keepdims=True) + EPS) * gain
    return xn * (1.0 + scale) + shift


def _nm_kernel(h_ref, gain_ref, shift_ref, scale_ref, w_ref, o_ref, u_scr, *, tiles_per_batch):
    i = pl.program_id(0)

    @pl.when(pl.program_id(1) == 0)
    def _():
        b = i // tiles_per_batch
        u = _norm_mod(h_ref[...], gain_ref[...], shift_ref[pl.ds(b, 1), :], scale_ref[pl.ds(b, 1), :])
        u_scr[...] = u.astype(BF16)

    o_ref[...] = _dot(u_scr[...], w_ref[...]).astype(o_ref.dtype)


def _norm_mod_matmul(h, gain, ada, shift_blk, scale_blk, w, seq, tn, name):
    t, d = h.shape
    n = w.shape[1]
    tm = ROW_TILE
    return pl.pallas_call(
        functools.partial(_nm_kernel, tiles_per_batch=seq // tm),
        grid=(t // tm, n // tn),
        in_specs=[pl.BlockSpec((tm, d), lambda i, j: (i, 0)),
                  pl.BlockSpec((1, d), lambda i, j: (0, 0)),
                  pl.BlockSpec((SUBLANES, d), lambda i, j: (0, shift_blk)),
                  pl.BlockSpec((SUBLANES, d), lambda i, j: (0, scale_blk)),
                  pl.BlockSpec((d, tn), lambda i, j: (0, j))],
        out_specs=pl.BlockSpec((tm, tn), lambda i, j: (i, j)),
        out_shape=jax.ShapeDtypeStruct((t, n), F32),
        scratch_shapes=[pltpu.VMEM((tm, d), BF16)],
        compiler_params=_cp("parallel", "arbitrary"),
        name=name,
    )(h, gain.reshape(1, d), ada, ada, w)


def _logsig_cumsum_kernel(z_ref, b_ref, tri_ref, o_ref, carry):
    @pl.when(pl.program_id(0) == 0)
    def _():
        carry[...] = jnp.zeros_like(carry)

    z = z_ref[...] + b_ref[...]
    lf = jnp.minimum(z, 0.0) - jnp.log1p(jnp.exp(-jnp.abs(z)))
    hi, mid, lo = _split3(lf)
    tri = tri_ref[...]
    c = _dot(tri, hi) + _dot(tri, mid) + _dot(tri, lo) + carry[...]
    o_ref[...] = c
    ts = c.shape[0]
    carry[...] = c[ts - 1:ts, :]


def _logsig_cumsum(z, bias):
    s, c = z.shape
    ts = 256
    tri = (jnp.arange(ts)[:, None] >= jnp.arange(ts)[None, :]).astype(BF16)
    return pl.pallas_call(
        _logsig_cumsum_kernel,
        grid=(s // ts,),
        in_specs=[pl.BlockSpec((ts, c), lambda i: (i, 0)),
                  pl.BlockSpec((1, c), lambda i: (0, 0)),
                  pl.BlockSpec((ts, ts), lambda i: (0, 0))],
        out_specs=pl.BlockSpec((ts, c), lambda i: (i, 0)),
        out_shape=jax.ShapeDtypeStruct((s, c), F32),
        scratch_shapes=[pltpu.VMEM((1, c), F32)],
        compiler_params=_cp("arbitrary"),
        name="fox_logsig_cumsum",
    )(z, bias, tri)


def _head_rms(x, gain):
    return x * lax.rsqrt(jnp.mean(x * x, axis=-1, keepdims=True) + EPS) * gain


def _fox_prep_kernel(q_ref, k_ref, v_ref, cum_ref, qg_ref, kg_ref, qo_ref, ko_ref, vo_ref):
    bh0 = (pl.program_id(0) * (N_HEADS_A // 2) + pl.program_id(1)) * 2
    ts = q_ref.shape[0]
    cum = cum_ref[...]
    lane_c = lax.broadcasted_iota(I32, cum.shape, 1)
    lane = lax.broadcasted_iota(I32, (ts, HEAD_DIM), 1)
    for h2 in range(2):
        sl = slice(h2 * HEAD_DIM, (h2 + 1) * HEAD_DIM)
        c = jnp.sum(jnp.where(lane_c == bh0 + h2, cum, 0.0), axis=-1, keepdims=True) * LOG2E
        hi, mid, lo = (p.astype(F32) for p in _split3(c))
        qn = _head_rms(q_ref[:, sl], qg_ref[...]) * QSCALE
        kn = _head_rms(k_ref[:, sl], kg_ref[...])
        ext_q = jnp.where(lane == 0, hi, jnp.where(lane == 1, mid, jnp.where(lane == 2, lo,
                          jnp.where(lane < 6, 1.0, 0.0))))
        ext_k = jnp.where(lane < 3, 1.0, jnp.where(lane == 3, -hi, jnp.where(lane == 4, -mid,
                          jnp.where(lane == 5, -lo, 0.0))))
        ext_v = jnp.where(lane == 0, 1.0, 0.0)
        qo_ref[0, h2] = jnp.concatenate([qn, ext_q], axis=1).astype(BF16)
        ko_ref[0, h2] = jnp.concatenate([kn, ext_k], axis=1).T.astype(BF16)
        vo_ref[0, h2] = jnp.concatenate([v_ref[:, sl], ext_v], axis=1).astype(BF16)


def _fox_prep(proj, cum, q_gain, k_gain, batch, seq):
    ts = ROW_TILE
    nsb = seq // ts
    hp = N_HEADS_A // 2
    out = jax.ShapeDtypeStruct((batch, N_HEADS_A, seq, LANES), BF16)
    out_t = jax.ShapeDtypeStruct((batch, N_HEADS_A, LANES, seq), BF16)
    ospec = pl.BlockSpec((1, 2, ts, LANES), lambda b, p, i: (b, p, i, 0))
    ospec_t = pl.BlockSpec((1, 2, LANES, ts), lambda b, p, i: (b, p, 0, i))
    return pl.pallas_call(
        _fox_prep_kernel,
        grid=(batch, hp, nsb),
        in_specs=[pl.BlockSpec((ts, LANES), lambda b, p, i: (b * nsb + i, p)),
                  pl.BlockSpec((ts, LANES), lambda b, p, i: (b * nsb + i, hp + p)),
                  pl.BlockSpec((ts, LANES), lambda b, p, i: (b * nsb + i, 2 * hp + p)),
                  pl.BlockSpec((ts, cum.shape[1]), lambda b, p, i: (i, 0)),
                  pl.BlockSpec((1, HEAD_DIM), lambda b, p, i: (0, 0)),
                  pl.BlockSpec((1, HEAD_DIM), lambda b, p, i: (0, 0))],
        out_specs=[ospec, ospec_t, ospec],
        out_shape=[out, out_t, out],
        compiler_params=_cp("parallel", "parallel", "parallel"),
        name="fox_prep",
    )(proj, proj, proj, cum, q_gain.reshape(1, HEAD_DIM), k_gain.reshape(1, HEAD_DIM))


def _flash_kernel(q_ref, k_ref, v_ref, o_ref, m_scr, acc_scr, *, tq, tk, window):
    i = pl.program_id(1)
    q = q_ref[0, 0]
    rows = q.shape[0]
    m_scr[...] = jnp.full(m_scr.shape, NEG, F32)
    acc_scr[...] = jnp.zeros(acc_scr.shape, F32)
    q_pos = i * tq + jnp.bitwise_and(lax.broadcasted_iota(I32, (rows, 1), 0), tq - 1)

    def step(j, masked):
        ks = pl.multiple_of(j * tk, tk)
        s = _dot(q, k_ref[0, :, pl.ds(ks, tk)])
        if masked:
            k_pos = ks + lax.broadcasted_iota(I32, (1, tk), 1)
            ok = k_pos <= q_pos
            if window is not None:
                ok = jnp.logical_and(ok, q_pos - k_pos < window)
            s = jnp.where(ok, s, NEG)
        m_old = m_scr[...]
        m_new = jnp.maximum(m_old, jnp.max(s, axis=-1, keepdims=True))
        p = jnp.exp2(s - jnp.tile(m_new, (1, tk // LANES))).astype(BF16)
        acc_scr[...] = jnp.exp2(m_old - m_new) * acc_scr[...] + _dot(p, v_ref[0, pl.ds(ks, tk), :])
        m_scr[...] = m_new

    q_lo = i * tq
    q_hi = q_lo + tq - 1
    j_hi = q_hi // tk
    if window is None:
        j_clear = (q_lo + 1) // tk
        lax.fori_loop(0, j_clear, lambda j, c: (step(j, False), c)[1], 0)
        lax.fori_loop(j_clear, j_hi + 1, lambda j, c: (step(j, True), c)[1], 0)
    else:
        j_lo = jnp.maximum(q_lo - window + 1, 0) // tk

        @pl.when(j_lo < j_hi)
        def _():
            step(j_lo, True)

        lax.fori_loop(j_lo + 1, j_hi, lambda j, c: (step(j, False), c)[1], 0)
        step(j_hi, True)

    acc = acc_scr[...]
    o_ref[0, 0] = (acc[:, :HEAD_DIM] / acc[:, HEAD_DIM:HEAD_DIM + 1]).astype(o_ref.dtype)


def _flash(q, k, v, tq, tk, window, name):
    bh, nq, rows, kd = q.shape
    s = k.shape[2]
    assert window is None or (tq == tk and window % tk == 0)
    return pl.pallas_call(
        functools.partial(_flash_kernel, tq=tq, tk=tk, window=window),
        grid=(bh, nq),
        in_specs=[pl.BlockSpec((1, 1, rows, kd), lambda b, i: (b, i, 0, 0)),
                  pl.BlockSpec((1, kd, s), lambda b, i: (b, 0, 0)),
                  pl.BlockSpec((1, s, LANES), lambda b, i: (b, 0, 0))],
        out_specs=pl.BlockSpec((1, 1, rows, HEAD_DIM), lambda b, i: (b, i, 0, 0)),
        out_shape=jax.ShapeDtypeStruct((bh, nq, rows, HEAD_DIM), BF16),
        scratch_shapes=[pltpu.VMEM((rows, LANES), F32), pltpu.VMEM((rows, LANES), F32)],
        compiler_params=_cp("parallel", "arbitrary"),
        name=name,
    )(q, k, v)


def _oproj_kernel(o_ref, w_ref, h_ref, g_ref, out_ref, *, tiles_per_batch):
    b = pl.program_id(0) // tiles_per_batch
    out_ref[...] = h_ref[...] + g_ref[pl.ds(b, 1), :] * _dot(o_ref[...], w_ref[...])


def _oproj_residual(o, w, h, ada, gate_blk, seq):
    t, d = h.shape
    tm = ROW_TILE
    return pl.pallas_call(
        functools.partial(_oproj_kernel, tiles_per_batch=seq // tm),
        grid=(t // tm,),
        in_specs=[pl.BlockSpec((tm, o.shape[1]), lambda i: (i, 0)),
                  pl.BlockSpec(w.shape, lambda i: (0, 0)),
                  pl.BlockSpec((tm, d), lambda i: (i, 0)),
                  pl.BlockSpec((SUBLANES, d), lambda i: (0, gate_blk))],
        out_specs=pl.BlockSpec((tm, d), lambda i: (i, 0)),
        out_shape=jax.ShapeDtypeStruct((t, d), F32),
        compiler_params=_cp("parallel"),
        name="oproj_residual",
    )(o, w, h, ada)


def _seg_rms(x, seg_ones, gain_tiled):
    ss = _dot((x * x).astype(BF16), seg_ones)
    return x * lax.rsqrt(ss * (1.0 / HEAD_DIM) + EPS) * gain_tiled


def _rope(x, cos_t, sin_t):
    width = x.shape[1]
    reps = width // LANES
    lane = jnp.bitwise_and(lax.broadcasted_iota(I32, x.shape, 1), HEAD_DIM - 1)
    half = ROPE_DIM // 2
    partner = jnp.where(lane < half, pltpu.roll(x, width - half, 1), pltpu.roll(x, half, 1))
    return x * jnp.tile(cos_t, (1, reps)) + partner * jnp.tile(sin_t, (1, reps))


def _nsa_q_kernel(p_ref, z_ref, cos_ref, sin_ref, gain_ref, seg_ref, bg_ref, qp_ref, qr_ref, g_ref):
    q = _seg_rms(p_ref[...], seg_ref[...], gain_ref[...]) * QSCALE
    qp_ref[...] = q.astype(BF16)
    qr_ref[...] = _rope(q, cos_ref[...], sin_ref[...]).astype(BF16)
    g_ref[...] = _sigmoid(z_ref[...] + bg_ref[...])


def _nsa_q_prep(proj, cos_t, sin_t, q_gain, b_gate_pad):
    t = proj.shape[0]
    hd = N_HEADS_B * HEAD_DIM
    tm = ROW_TILE
    seg = (jnp.arange(hd)[:, None] // HEAD_DIM == jnp.arange(hd)[None, :] // HEAD_DIM).astype(BF16)
    row = lambda i: (i, 0)
    fix = lambda i: (0, 0)
    return pl.pallas_call(
        _nsa_q_kernel,
        grid=(t // tm,),
        in_specs=[pl.BlockSpec((tm, hd), row),
                  pl.BlockSpec((tm, LANES), lambda i: (i, hd // LANES)),
                  pl.BlockSpec((tm, LANES), row), pl.BlockSpec((tm, LANES), row),
                  pl.BlockSpec((1, hd), fix), pl.BlockSpec((hd, hd), fix), pl.BlockSpec((1, LANES), fix)],
        out_specs=[pl.BlockSpec((tm, hd), row), pl.BlockSpec((tm, hd), row), pl.BlockSpec((tm, LANES), row)],
        out_shape=[jax.ShapeDtypeStruct((t, hd), BF16), jax.ShapeDtypeStruct((t, hd), BF16),
                   jax.ShapeDtypeStruct((t, LANES), F32)],
        compiler_params=_cp("parallel"),
        name="nsa_q_prep",
    )(proj, proj, cos_t, sin_t, jnp.tile(q_gain, N_HEADS_B).reshape(1, hd), seg, b_gate_pad)


def _nsa_kv_kernel(kv_ref, cos_ref, sin_ref, g1_ref, g2_ref, seg_ref, ks_ref, vs_ref, kw_ref, vw_ref):
    gw = N_KV_GROUPS_B * HEAD_DIM
    ks = _seg_rms(kv_ref[:, 2 * gw:3 * gw], seg_ref[...], g1_ref[...])
    kw = _seg_rms(kv_ref[:, 4 * gw:5 * gw], seg_ref[...], g2_ref[...])
    ks_ref[...] = _rope(ks, cos_ref[...], sin_ref[...]).astype(BF16)
    kw_ref[...] = _rope(kw, cos_ref[...], sin_ref[...]).astype(BF16)
    vs_ref[...] = kv_ref[:, 3 * gw:4 * gw].astype(BF16)
    vw_ref[...] = kv_ref[:, 5 * gw:6 * gw].astype(BF16)


def _nsa_kv_prep(kv, cos_t, sin_t, k_gains):
    t, n = kv.shape
    gw = N_KV_GROUPS_B * HEAD_DIM
    tm = ROW_TILE
    seg = (jnp.arange(gw)[:, None] // HEAD_DIM == jnp.arange(gw)[None, :] // HEAD_DIM).astype(BF16)
    row = lambda i: (i, 0)
    fix = lambda i: (0, 0)
    out = jax.ShapeDtypeStruct((t, gw), BF16)
    return pl.pallas_call(
        _nsa_kv_kernel,
        grid=(t // tm,),
        in_specs=[pl.BlockSpec((tm, n), row), pl.BlockSpec((tm, LANES), row), pl.BlockSpec((tm, LANES), row),
                  pl.BlockSpec((1, gw), fix), pl.BlockSpec((1, gw), fix), pl.BlockSpec((gw, gw), fix)],
        out_specs=[pl.BlockSpec((tm, gw), row)] * 4,
        out_shape=[out] * 4,
        compiler_params=_cp("parallel"),
        name="nsa_kv_prep",
    )(kv, cos_t, sin_t, jnp.tile(k_gains[1], N_KV_GROUPS_B).reshape(1, gw),
      jnp.tile(k_gains[2], N_KV_GROUPS_B).reshape(1, gw), seg)


def _compress_kernel(x_ref, pe_ref, w1_ref, w2_ref, g_ref, o_ref, *, norm):
    x = (x_ref[...] + pe_ref[...]).astype(BF16)
    y = _dot(_silu(_dot(x, w1_ref[...])).astype(BF16), w2_ref[...])
    if norm:
        y = _head_rms(y, g_ref[...])
    o_ref[...] = y.astype(o_ref.dtype)


def _compress(x, pe_flat, w1, w2, gain, norm, name):
    r, kdim = x.shape
    tm = min(ROW_TILE, r)
    fix = lambda i: (0, 0)
    return pl.pallas_call(
        functools.partial(_compress_kernel, norm=norm),
        grid=(r // tm,),
        in_specs=[pl.BlockSpec((tm, kdim), lambda i: (i, 0)), pl.BlockSpec((1, kdim), fix),
                  pl.BlockSpec(w1.shape, fix), pl.BlockSpec(w2.shape, fix), pl.BlockSpec((1, HEAD_DIM), fix)],
        out_specs=pl.BlockSpec((tm, HEAD_DIM), lambda i: (i, 0)),
        out_shape=jax.ShapeDtypeStruct((r, HEAD_DIM), BF16),
        compiler_params=_cp("parallel"),
        name=name,
    )(x, pe_flat, w1, w2, gain.reshape(1, HEAD_DIM))


def _cmp_select_kernel(q_ref, k_ref, v_ref, o_ref, nb_ref, *, tq, n_sel):
    i = pl.program_id(1)
    q = q_ref[0, 0]
    rows = q.shape[0]
    nblk = k_ref.shape[1]
    logits = _dot_nt(q, k_ref[0])
    t_pos = i * tq + jnp.bitwise_and(lax.broadcasted_iota(I32, (rows, 1), 0), tq - 1)
    blk = lax.broadcasted_iota(I32, (1, nblk), 1)
    vis = (blk + 1) * CMP_BLOCK - 1 <= t_pos
    lm = jnp.where(vis, logits, NEG)
    e = jnp.exp2(lm - jnp.max(lm, axis=-1, keepdims=True))
    p = jnp.where(vis, e / jnp.sum(e, axis=-1, keepdims=True), 0.0)
    o_ref[0, 0] = _dot(p.astype(BF16), v_ref[0])[:, :HEAD_DIM].astype(o_ref.dtype)

    imp = p[0:tq]
    for hh in range(1, rows // tq):
        imp = imp + p[hh * tq:(hh + 1) * tq]
    cur = lax.shift_right_logical(t_pos[0:tq], CMP_BLOCK.bit_length() - 1)
    forced = jnp.logical_or(blk == 0, jnp.logical_or(blk == cur, blk == cur - 1))
    cand = jnp.where(jnp.logical_or(forced, blk > cur), -jnp.inf, imp)
    sel = jnp.where(forced, 1.0, 0.0)
    blk_f = blk.astype(F32)
    for _ in range(max(n_sel - 3, 0)):
        best = jnp.max(cand, axis=-1, keepdims=True)
        first = jnp.min(jnp.where(cand == best, blk_f, float(nblk)), axis=-1, keepdims=True)
        hit = blk_f == first
        sel = jnp.where(jnp.logical_and(hit, best > -jnp.inf), 1.0, sel)
        cand = jnp.where(hit, -jnp.inf, cand)
    nb_ref[0] = jnp.where(sel > 0.5, 0.0, NEG).astype(nb_ref.dtype)


def _cmp_select(q, k_cmp, v_cmp, seq, tq):
    bg, nq, rows, _ = q.shape
    nblk = k_cmp.shape[1]
    return pl.pallas_call(
        functools.partial(_cmp_select_kernel, tq=tq, n_sel=min(N_SELECT, nblk)),
        grid=(bg, nq),
        in_specs=[pl.BlockSpec((1, 1, rows, HEAD_DIM), lambda b, i: (b, i, 0, 0)),
                  pl.BlockSpec((1, nblk, HEAD_DIM), lambda b, i: (b, 0, 0)),
                  pl.BlockSpec((1, nblk, LANES), lambda b, i: (b, 0, 0))],
        out_specs=[pl.BlockSpec((1, 1, rows, HEAD_DIM), lambda b, i: (b, i, 0, 0)),
                   pl.BlockSpec((1, tq, nblk), lambda b, i: (b, i, 0))],
        out_shape=[jax.ShapeDtypeStruct((bg, nq, rows, HEAD_DIM), BF16),
                   jax.ShapeDtypeStruct((bg, seq, nblk), BF16)],
        compiler_params=_cp("parallel", "parallel"),
        name="nsa_cmp_select",
    )(q, k_cmp, v_cmp)


def _nsa_out_kernel(oc_ref, os_ref, ow_ref, g_ref, e_ref, w_ref, h_ref, ada_ref, out_ref, *, tiles_per_batch):
    b = pl.program_id(0) // tiles_per_batch
    g_hi, g_lo = _split2(g_ref[...])
    o = None
    for j, o_ref in enumerate((oc_ref, os_ref, ow_ref)):
        gate = _dot(g_hi, e_ref[j]) + _dot(g_lo, e_ref[j])
        term = gate * o_ref[...].astype(F32)
        o = term if o is None else o + term
    out_ref[...] = h_ref[...] + ada_ref[pl.ds(b, 1), :] * _dot(o.astype(BF16), w_ref[...])


def _nsa_out(o_cmp, o_sel, o_win, gates, w_o, h, ada, gate_blk, seq):
    t, d = h.shape
    hd = o_cmp.shape[1]
    tm = ROW_TILE
    col = jnp.arange(LANES)[:, None]
    lane = jnp.arange(hd)[None, :]
    expand = jnp.stack([(col == (lane // HEAD_DIM) * 3 + j) for j in range(3)]).astype(BF16)
    row = lambda i: (i, 0)
    return pl.pallas_call(
        functools.partial(_nsa_out_kernel, tiles_per_batch=seq // tm),
        grid=(t // tm,),
        in_specs=[pl.BlockSpec((tm, hd), row), pl.BlockSpec((tm, hd), row), pl.BlockSpec((tm, hd), row),
                  pl.BlockSpec((tm, LANES), row),
                  pl.BlockSpec((3, LANES, hd), lambda i: (0, 0, 0)),
                  pl.BlockSpec(w_o.shape, lambda i: (0, 0)),
                  pl.BlockSpec((tm, d), row),
                  pl.BlockSpec((SUBLANES, d), lambda i: (0, gate_blk))],
        out_specs=pl.BlockSpec((tm, d), row),
        out_shape=jax.ShapeDtypeStruct((t, d), F32),
        compiler_params=_cp("parallel"),
        name="nsa_out",
    )(o_cmp, o_sel, o_win, gates, expand, w_o, h, ada)


def _route_kernel(h_ref, gain_ref, shift_ref, scale_ref, wr_ref, br_ref, tri_ref, upper_ref,
                  u_ref, pos_ref, w_ref, nu_ref, off_ref, *, tiles_per_batch):
    b = pl.program_id(0) // tiles_per_batch
    u = _norm_mod(h_ref[...], gain_ref[...], shift_ref[pl.ds(b, 1), :], scale_ref[pl.ds(b, 1), :])
    u_ref[...] = u.astype(BF16)
    scores = _sigmoid(_dot_f32(u, wr_ref[...]))
    lane = lax.broadcasted_iota(I32, scores.shape, 1)
    lane_f = lane.astype(F32)
    biased = jnp.where(lane < N_EXPERTS, scores + br_ref[...], -jnp.inf)
    chosen = jnp.zeros(scores.shape, F32)
    hits, val = [], []
    for _ in range(TOP_K):
        best = jnp.max(biased, axis=-1, keepdims=True)
        first = jnp.min(jnp.where(biased == best, lane_f, float(LANES)), axis=-1, keepdims=True)
        hit = lane_f == first
        hits.append(hit)
        val.append(jnp.sum(jnp.where(hit, scores, 0.0), axis=-1, keepdims=True))
        biased = jnp.where(hit, -jnp.inf, biased)
        chosen = jnp.where(hit, 1.0, chosen)
    top_s = jnp.concatenate(val, axis=1)
    w_ref[...] = top_s / jnp.sum(top_s, axis=-1, keepdims=True) * ROUTED_SCALE

    before = _dot(tri_ref[...], chosen.astype(BF16))
    count = jnp.sum(chosen, axis=0, keepdims=True)
    units = jnp.floor((count + (MOE_UNIT - 1)) * (1.0 / MOE_UNIT))
    padded = jnp.broadcast_to(units * MOE_UNIT, (SUBLANES, LANES))
    offset = _dot(padded.astype(BF16), upper_ref[...])[0:1]
    row_of = before + offset
    pos = [jnp.sum(jnp.where(hit, row_of, 0.0), axis=-1, keepdims=True) for hit in hits]
    pos_ref[...] = jnp.concatenate(pos, axis=1).astype(I32)
    nu_ref[0] = units.astype(I32)
    off_ref[0] = offset.astype(I32)


def _route(h, gain, ada, shift_blk, scale_blk, w_router_pad, b_router_pad, seq):
    t, d = h.shape
    tt = MOE_TOK_TILE
    n_tiles = t // tt
    row = lambda i: (i, 0)
    fix = lambda i: (0, 0)
    tri = (jnp.arange(tt)[:, None] > jnp.arange(tt)[None, :]).astype(BF16)
    upper = (jnp.arange(LANES)[:, None] < jnp.arange(LANES)[None, :]).astype(BF16)
    per_tile = jax.ShapeDtypeStruct((n_tiles, 1, LANES), I32)
    per_tile_spec = pl.BlockSpec((1, 1, LANES), lambda i: (i, 0, 0))
    return pl.pallas_call(
        functools.partial(_route_kernel, tiles_per_batch=seq // tt),
        grid=(n_tiles,),
        in_specs=[pl.BlockSpec((tt, d), row), pl.BlockSpec((1, d), fix),
                  pl.BlockSpec((SUBLANES, d), lambda i: (0, shift_blk)),
                  pl.BlockSpec((SUBLANES, d), lambda i: (0, scale_blk)),
                  pl.BlockSpec((d, LANES), fix), pl.BlockSpec((1, LANES), fix),
                  pl.BlockSpec((tt, tt), fix), pl.BlockSpec((LANES, LANES), fix)],
        out_specs=[pl.BlockSpec((tt, d), row), pl.BlockSpec((tt, TOP_K), row), pl.BlockSpec((tt, TOP_K), row),
                   per_tile_spec, per_tile_spec],
        out_shape=[jax.ShapeDtypeStruct((t, d), BF16), jax.ShapeDtypeStruct((t, TOP_K), I32),
                   jax.ShapeDtypeStruct((t, TOP_K), F32), per_tile, per_tile],
        compiler_params=_cp("parallel"),
        name="moe_route",
    )(h, gain.reshape(1, d), ada, ada, w_router_pad, b_router_pad, tri, upper)


PLAN_SLOTS = 256
PLAN_TILES = LANES


def _plan_kernel(nu_ref, off_ref, upper_ref, lower_ref, row_ref, exp_ref, nused_ref,
                 *, n_tiles, r_loc, upb, n_unit_slots, n_dump_units):
    nu = nu_ref[...]
    nu_t = nu.T
    off_t = off_ref[...].T
    upper = upper_ref[...]
    lower = lower_ref[...]
    lane = lax.broadcasted_iota(I32, (PLAN_SLOTS, LANES), 1)
    lane_f = lane.astype(F32)

    def exact_dot(a, b_bf16, left):
        parts = _split3(a)
        return sum(_dot(b_bf16, p) if left else _dot(p, b_bf16) for p in parts)

    units_before_t = _dot(nu_t.astype(BF16), upper)
    count_col = jnp.sum(nu_t, axis=1, keepdims=True)
    count_row = jnp.sum(nu, axis=0, keepdims=True)
    padded_col = jnp.floor((count_col + (upb - 1)) * (1.0 / upb)) * upb
    padded_row = jnp.floor((count_row + (upb - 1)) * (1.0 / upb)) * upb
    start_col = exact_dot(jnp.broadcast_to(padded_col, (LANES, LANES)), lower, True)
    real_col = exact_dot(jnp.broadcast_to(count_col, (LANES, LANES)), lower, True)
    start_row = exact_dot(jnp.broadcast_to(padded_row, (SUBLANES, LANES)), upper, False)[0:1]
    seg_start_t = start_col + units_before_t
    seg_real_t = real_col + units_before_t
    tile_id = lax.broadcasted_iota(I32, (LANES, LANES), 1).astype(F32)
    seg_row_t = tile_id * r_loc + off_t

    idx = pl.program_id(0) * PLAN_SLOTS + lax.broadcasted_iota(I32, (PLAN_SLOTS, 1), 0)
    slot = jnp.where(idx < n_unit_slots, idx, (idx - n_unit_slots) * upb).astype(F32)

    in_e = jnp.logical_and(start_row <= slot, lane < N_EXPERTS)
    expert = jnp.sum(jnp.where(in_e, 1.0, 0.0), axis=-1, keepdims=True) - 1.0
    pick_e = (lane_f == expert).astype(BF16)
    look = lambda tbl: exact_dot(tbl, pick_e, True)
    starts = look(seg_start_t)
    in_t = jnp.logical_and(starts <= slot, lane < n_tiles)
    tile = jnp.sum(jnp.where(in_t, 1.0, 0.0), axis=-1, keepdims=True) - 1.0
    at = lambda x: jnp.sum(jnp.where(lane_f == tile, x, 0.0), axis=-1, keepdims=True)
    m = slot - at(starts)
    n = at(look(nu_t))
    row = at(look(seg_row_t)) + m * MOE_UNIT
    pad_idx = slot - (at(look(seg_real_t)) + jnp.minimum(m, n))
    dump = n_tiles * r_loc + jnp.minimum(pad_idx, n_dump_units - 1.0) * MOE_UNIT
    row_ref[...] = jnp.where(m < n, row, dump).astype(I32)
    exp_ref[...] = expert.astype(I32)
    total = jnp.sum(padded_row, axis=-1, keepdims=True) * (1.0 / upb)
    nused_ref[...] = jnp.broadcast_to(total, (SUBLANES, LANES)).astype(I32)


def _plan(nu, off, n_tiles, r_loc, n_blocks, n_dump_units):
    assert n_tiles <= PLAN_TILES
    upb = MOE_BLOCK // MOE_UNIT
    n_unit_slots = -(-(n_blocks * upb) // PLAN_SLOTS) * PLAN_SLOTS
    n_steps = n_unit_slots // PLAN_SLOTS + -(-(n_blocks + 2) // PLAN_SLOTS)
    pad = lambda x: jnp.pad(x.astype(F32), ((0, PLAN_TILES - n_tiles), (0, 0)))
    upper = (jnp.arange(LANES)[:, None] < jnp.arange(LANES)[None, :]).astype(BF16)
    fix = lambda s: (0, 0)
    col = pl.BlockSpec((PLAN_SLOTS, 1), lambda s: (s, 0))
    rows, experts, n_used = pl.pallas_call(
        functools.partial(_plan_kernel, n_tiles=n_tiles, r_loc=r_loc, upb=upb, n_unit_slots=n_unit_slots,
                          n_dump_units=n_dump_units),
        grid=(n_steps,),
        in_specs=[pl.BlockSpec((PLAN_TILES, LANES), fix), pl.BlockSpec((PLAN_TILES, LANES), fix),
                  pl.BlockSpec((LANES, LANES), fix), pl.BlockSpec((LANES, LANES), fix)],
        out_specs=[col, col, pl.BlockSpec((SUBLANES, LANES), fix)],
        out_shape=[jax.ShapeDtypeStruct((n_steps * PLAN_SLOTS, 1), I32),
                   jax.ShapeDtypeStruct((n_steps * PLAN_SLOTS, 1), I32),
                   jax.ShapeDtypeStruct((SUBLANES, LANES), I32)],
        compiler_params=_cp("arbitrary"),
        name="moe_plan",
    )(pad(nu), pad(off), upper, upper.T)
    return (rows[:n_blocks * upb, 0], experts[n_unit_slots:n_unit_slots + n_blocks + 2, 0], n_used[0, :1])


def _one_hot_t(pos_ref, weight_ref, n_rows):
    tt = pos_ref.shape[0]
    col = lax.broadcasted_iota(I32, (tt, n_rows), 1)
    acc = jnp.zeros((tt, n_rows), F32)
    for k in range(TOP_K):
        hit = col == pos_ref[:, k:k + 1]
        acc = acc + jnp.where(hit, 1.0 if weight_ref is None else weight_ref[:, k:k + 1], 0.0)
    return acc.astype(BF16)


def _dispatch_kernel(u_ref, pos_ref, x_ref):
    x_ref[0] = _dot_tn(_one_hot_t(pos_ref, None, x_ref.shape[1]), u_ref[...]).astype(x_ref.dtype)


def _dispatch(u, lpos, n_tiles_total, r_loc):
    t, d = u.shape
    tt = MOE_TOK_TILE
    n_real = t // tt
    return pl.pallas_call(
        _dispatch_kernel,
        grid=(n_tiles_total,),
        in_specs=[pl.BlockSpec((tt, d), lambda i: (jnp.minimum(i, n_real - 1), 0)),
                  pl.BlockSpec((tt, TOP_K), lambda i: (i, 0))],
        out_specs=pl.BlockSpec((1, r_loc, d), lambda i: (i, 0, 0)),
        out_shape=jax.ShapeDtypeStruct((n_tiles_total, r_loc, d), BF16),
        compiler_params=_cp("parallel"),
        name="moe_dispatch",
    )(u, lpos)


def _expert_kernel(tbl_ref, be_ref, nblk_ref, x_hbm, wg_ref, wu_ref, wd_ref, y_hbm, xbuf, ybuf, gsem, ssem, *, upb):
    del be_ref
    b = pl.program_id(0)
    nblk = nblk_ref[0]
    slot = lax.rem(b, 2)

    def gather(blk, sl):
        for m in range(upb):
            row = pl.multiple_of(tbl_ref[blk * upb + m], MOE_UNIT)
            yield pltpu.make_async_copy(x_hbm.at[pl.ds(row, MOE_UNIT), :],
                                        xbuf.at[sl, pl.ds(m * MOE_UNIT, MOE_UNIT), :], gsem.at[sl, m])

    def scatter(blk, sl):
        for m in range(upb):
            row = pl.multiple_of(tbl_ref[blk * upb + m], MOE_UNIT)
            yield pltpu.make_async_copy(ybuf.at[sl, pl.ds(m * MOE_UNIT, MOE_UNIT), :],
                                        y_hbm.at[pl.ds(row, MOE_UNIT), :], ssem.at[sl, m])

    @pl.when(jnp.logical_and(b == 0, nblk > 0))
    def _():
        for cp in gather(0, 0):
            cp.start()

    @pl.when(b + 1 < nblk)
    def _():
        for cp in gather(b + 1, 1 - slot):
            cp.start()

    @pl.when(jnp.logical_and(b >= 2, b < nblk + 2))
    def _():
        for cp in scatter(b - 2, slot):
            cp.wait()

    @pl.when(b < nblk)
    def _():
        for cp in gather(b, slot):
            cp.wait()
        x = xbuf[slot]
        hid = _silu(_dot(x, wg_ref[0])) * _dot(x, wu_ref[0])
        ybuf[slot] = _dot(hid.astype(BF16), wd_ref[0]).astype(ybuf.dtype)
        for cp in scatter(b, slot):
            cp.start()


def _experts(x_loc, unit_rows, block_e, n_blocks_used, w_gate, w_up, w_down, n_blocks):
    rows, d = x_loc.shape
    de = w_gate.shape[2]
    upb = MOE_BLOCK // MOE_UNIT
    grid_spec = pltpu.PrefetchScalarGridSpec(
        num_scalar_prefetch=3,
        grid=(n_blocks + 2,),
        in_specs=[pl.BlockSpec(memory_space=pl.ANY),
                  pl.BlockSpec((1, d, de), lambda b, tbl, be, nb: (be[b], 0, 0)),
                  pl.BlockSpec((1, d, de), lambda b, tbl, be, nb: (be[b], 0, 0)),
                  pl.BlockSpec((1, de, d), lambda b, tbl, be, nb: (be[b], 0, 0))],
        out_specs=pl.BlockSpec(memory_space=pl.ANY),
        scratch_shapes=[pltpu.VMEM((2, MOE_BLOCK, d), BF16), pltpu.VMEM((2, MOE_BLOCK, d), BF16),
                        pltpu.SemaphoreType.DMA((2, upb)), pltpu.SemaphoreType.DMA((2, upb))],
    )
    return pl.pallas_call(
        functools.partial(_expert_kernel, upb=upb),
        grid_spec=grid_spec,
        out_shape=jax.ShapeDtypeStruct((rows, d), BF16),
        input_output_aliases={3: 0},
        compiler_params=_cp("arbitrary"),
        name="moe_experts",
    )(unit_rows, block_e, n_blocks_used, x_loc, w_gate, w_up, w_down)


def _combine_kernel(y_ref, pos_ref, tw_ref, u_ref, wsg_ref, wsu_ref, wsd_ref, h_ref, g_ref, out_ref, *, tiles_per_batch):
    b = pl.program_id(0) // tiles_per_batch
    routed = _dot(_one_hot_t(pos_ref, tw_ref, y_ref.shape[1]), y_ref[0])
    u = u_ref[...]
    hid = _silu(_dot(u, wsg_ref[...])) * _dot(u, wsu_ref[...])
    shared = _dot(hid.astype(BF16), wsd_ref[...])
    out_ref[...] = h_ref[...] + g_ref[pl.ds(b, 1), :] * (routed + shared)


def _combine(y_loc, lpos, top_w, u, w_sg, w_su, w_sd, h, ada, gate_blk, seq):
    t, d = h.shape
    tt = MOE_TOK_TILE
    r_loc = y_loc.shape[1]
    row = lambda i: (i, 0)
    fix = lambda i: (0, 0)
    return pl.pallas_call(
        functools.partial(_combine_kernel, tiles_per_batch=seq // tt),
        grid=(t // tt,),
        in_specs=[pl.BlockSpec((1, r_loc, d), lambda i: (i, 0, 0)),
                  pl.BlockSpec((tt, TOP_K), row), pl.BlockSpec((tt, TOP_K), row),
                  pl.BlockSpec((tt, d), row),
                  pl.BlockSpec(w_sg.shape, fix), pl.BlockSpec(w_su.shape, fix), pl.BlockSpec(w_sd.shape, fix),
                  pl.BlockSpec((tt, d), row),
                  pl.BlockSpec((SUBLANES, d), lambda i: (0, gate_blk))],
        out_specs=pl.BlockSpec((tt, d), row),
        out_shape=jax.ShapeDtypeStruct((t, d), F32),
        compiler_params=_cp("parallel"),
        name="moe_combine",
    )(y_loc, lpos, top_w, u, w_sg, w_su, w_sd, h, ada)


def _moe_ffn(h, gain, ada, blk0, w_router, b_router, w_eg, w_eu, w_ed, w_sg, w_su, w_sd, seq):
    t, d = h.shape
    tt, unit, upb = MOE_TOK_TILE, MOE_UNIT, MOE_BLOCK // MOE_UNIT
    wr = jnp.pad(w_router, ((0, 0), (0, LANES - N_EXPERTS)))
    br = jnp.pad(b_router.astype(F32), (0, LANES - N_EXPERTS)).reshape(1, LANES)
    u, lpos, top_w, nu, off = _route(h, gain, ada, blk0, blk0 + 1, wr, br, seq)

    n_tok_tiles = t // tt
    r_loc = -(-(tt * TOP_K + N_EXPERTS * (unit - 1)) // LANES) * LANES
    n_dummy_tiles = -(-(N_EXPERTS * (upb - 1) * unit) // r_loc)
    max_units = n_tok_tiles * (tt * TOP_K // unit + N_EXPERTS)
    n_blocks = -(-max_units // upb) + N_EXPERTS
    unit_rows, block_e, n_used = _plan(nu[:, 0, :], off[:, 0, :], n_tok_tiles, r_loc, n_blocks,
                                       n_dummy_tiles * r_loc // unit)
    lpos = jnp.concatenate([lpos, jnp.full((n_dummy_tiles * tt, TOP_K), -1, I32)], axis=0)

    x_loc = _dispatch(u, lpos, n_tok_tiles + n_dummy_tiles, r_loc)
    y_loc = _experts(x_loc.reshape(-1, d), unit_rows, block_e, n_used,
                     w_eg.astype(BF16), w_eu.astype(BF16), w_ed.astype(BF16), n_blocks)
    y_loc = y_loc.reshape(n_tok_tiles + n_dummy_tiles, r_loc, d)
    return _combine(y_loc, lpos, top_w, u, w_sg.astype(BF16), w_su.astype(BF16), w_sd.astype(BF16),
                    h, ada, blk0 + 2, seq)


def _fox_layer(h, ada, norm_gain, w_in, b_f, q_gain, k_gain, w_o, batch, seq):
    t, d = h.shape
    hd = N_HEADS_A * HEAD_DIM
    w = jnp.pad(w_in, ((0, 0), (0, LANES - N_HEADS_A))).astype(BF16)
    proj = _norm_mod_matmul(h, norm_gain, ada, 0, 1, w, seq, 640, "fox_in_proj")
    z = proj[:, 3 * hd:3 * hd + N_HEADS_A].reshape(batch, seq, N_HEADS_A)
    z = z.transpose(1, 0, 2).reshape(seq, batch * N_HEADS_A)
    cum = _logsig_cumsum(z, jnp.tile(b_f.astype(F32), batch).reshape(1, -1))
    q, k, v = _fox_prep(proj, cum, q_gain, k_gain, batch, seq)
    bh = batch * N_HEADS_A
    o = _flash(q.reshape(bh, seq // FOX_TQ, FOX_TQ, LANES), k.reshape(bh, LANES, seq),
               v.reshape(bh, seq, LANES), FOX_TQ, FOX_TK, None, "fox_attention")
    o = o.reshape(batch, N_HEADS_A, seq, HEAD_DIM).transpose(0, 2, 1, 3).reshape(t, hd)
    return _oproj_residual(o, w_o.astype(BF16), h, ada, 2, seq)


def _rope_tables(positions):
    inv_freq = ROPE_THETA ** (-jnp.arange(0, ROPE_DIM, 2, dtype=F32) / ROPE_DIM)
    ang = positions.astype(F32).reshape(-1, 1) * inv_freq
    half = ROPE_DIM // 2
    cos, sin = jnp.cos(ang), jnp.sin(ang)
    rest = HEAD_DIM - ROPE_DIM
    cos_h = jnp.concatenate([cos, cos, jnp.ones((ang.shape[0], rest), F32)], axis=1)
    sin_h = jnp.concatenate([-sin, sin, jnp.zeros((ang.shape[0], rest), F32)], axis=1)
    del half
    return jnp.tile(cos_h, (1, LANES // HEAD_DIM)), jnp.tile(sin_h, (1, LANES // HEAD_DIM))


def _group_rows(x, batch, seq, tq):
    g, hpg = N_KV_GROUPS_B, HEADS_PER_GROUP
    x = x.reshape(batch, seq // tq, tq, g, hpg, HEAD_DIM).transpose(0, 3, 1, 4, 2, 5)
    return x.reshape(batch * g, seq // tq, hpg * tq, HEAD_DIM)


def _ungroup_rows(x, batch, seq, tq):
    g, hpg = N_KV_GROUPS_B, HEADS_PER_GROUP
    x = x.reshape(batch, g, seq // tq, hpg, tq, HEAD_DIM).transpose(0, 2, 4, 1, 3, 5)
    return x.reshape(batch * seq, g * hpg * HEAD_DIM)


def _per_group(x, batch, seq):
    g = N_KV_GROUPS_B
    return x.reshape(batch, seq, g, HEAD_DIM).transpose(0, 2, 1, 3).reshape(batch * g, seq, HEAD_DIM)


def _with_ones(v):
    pad = jnp.zeros(v.shape[:-1] + (LANES - HEAD_DIM,), v.dtype).at[..., 0].set(1)
    return jnp.concatenate([v, pad], axis=-1)


def _shared_kv(h, ada_kv, norm_kv, w_kv, pe_k, pe_v, w_ck1, w_ck2, w_cv1, w_cv2, k_gains, cos_t, sin_t, batch, seq):
    g = N_KV_GROUPS_B
    gw = g * HEAD_DIM
    nb = seq // CMP_BLOCK
    kv = _norm_mod_matmul(h, norm_kv, ada_kv, 0, 1, w_kv.astype(BF16), seq, w_kv.shape[1], "nsa_kv_proj")
    k_sel, v_sel, k_win, v_win = _nsa_kv_prep(kv, cos_t, sin_t, k_gains)

    def blocks(x):
        x = x.reshape(batch, nb, CMP_BLOCK, g, HEAD_DIM).transpose(0, 1, 3, 2, 4)
        return x.reshape(batch * nb * g, CMP_BLOCK * HEAD_DIM)

    k_cmp = _compress(blocks(kv[:, 0:gw]), pe_k.reshape(1, -1), w_ck1.astype(BF16), w_ck2.astype(BF16),
                      k_gains[0], True, "nsa_compress_k")
    v_cmp = _compress(blocks(kv[:, gw:2 * gw]), pe_v.reshape(1, -1), w_cv1.astype(BF16), w_cv2.astype(BF16),
                      k_gains[0], False, "nsa_compress_v")
    to_bg = lambda x: x.reshape(batch, nb, g, HEAD_DIM).transpose(0, 2, 1, 3).reshape(batch * g, nb, HEAD_DIM)
    k_cmp = to_bg(k_cmp)
    v_cmp = jnp.pad(to_bg(v_cmp), ((0, 0), (0, 0), (0, LANES - HEAD_DIM)))

    def per_group_t(x):
        return x.reshape(batch, seq, g, HEAD_DIM).transpose(0, 2, 3, 1).reshape(batch * g, HEAD_DIM, seq)

    blk_onehot = (jnp.arange(nb)[:, None] == jnp.arange(seq)[None, :] // CMP_BLOCK).astype(BF16)
    k_sel = jnp.concatenate([per_group_t(k_sel), jnp.broadcast_to(blk_onehot, (batch * g, nb, seq))], axis=1)
    return (k_cmp, v_cmp, k_sel, _with_ones(_per_group(v_sel, batch, seq)),
            per_group_t(k_win), _with_ones(_per_group(v_win, batch, seq)))


def _nsa_layer(h, ada, norm_gain, kv, cos_t, sin_t, w_in, b_gate, q_gain, w_o, batch, seq):
    hd = N_HEADS_B * HEAD_DIM
    k_cmp, v_cmp, k_sel, v_sel, k_win, v_win = kv
    n_gate = 3 * N_HEADS_B
    w = jnp.pad(w_in, ((0, 0), (0, LANES - n_gate))).astype(BF16)
    proj = _norm_mod_matmul(h, norm_gain, ada, 0, 1, w, seq, hd + LANES, "nsa_in_proj")
    bg = jnp.pad(b_gate.astype(F32), (0, LANES - n_gate)).reshape(1, LANES)
    q_plain, q_rot, gates = _nsa_q_prep(proj, cos_t, sin_t, q_gain, bg)

    tq = NSA_TQ
    qp = _group_rows(q_plain, batch, seq, tq)
    qr = _group_rows(q_rot, batch, seq, tq)
    o_cmp, negbias = _cmp_select(qp, k_cmp, v_cmp, seq, tq)
    nbias = jnp.broadcast_to(negbias.reshape(qr.shape[0], seq // tq, 1, tq, -1),
                             (qr.shape[0], seq // tq, HEADS_PER_GROUP, tq, negbias.shape[-1]))
    q_sel = jnp.concatenate([qr, nbias.reshape(qr.shape[0], seq // tq, HEADS_PER_GROUP * tq, -1)], axis=-1)
    o_sel = _flash(q_sel, k_sel, v_sel, tq, NSA_TK, None, "nsa_selected")
    o_win = _flash(qr, k_win, v_win, tq, WIN_TK, WINDOW, "nsa_window")
    o_cmp, o_sel, o_win = (_ungroup_rows(o, batch, seq, tq) for o in (o_cmp, o_sel, o_win))
    return _nsa_out(o_cmp, o_sel, o_win, gates, w_o.astype(BF16), h, ada, 2, seq)


def kernel(x, c, positions, w_ada, b_ada, norm_mix, norm_ffn, w_in_a, b_f_a, qnorm_a, knorm_a, w_o_a, norm_kv, w_ada_kv, b_ada_kv, w_kv_b, pe_cmp_k, pe_cmp_v, w_cmp_k1, w_cmp_k2, w_cmp_v1, w_cmp_v2, knorm_b, w_in_b, b_gate_b, qnorm_b, w_o_b, w_router, b_router, w_exp_gate, w_exp_up, w_exp_down, w_sh_gate, w_sh_up, w_sh_down):
    batch, seq, d = x.shape
    depth = w_ada.shape[0]
    n_a = w_in_a.shape[0]
    c_pad = jnp.pad(c, ((0, SUBLANES - batch), (0, 0)))
    cos_t, sin_t = _rope_tables(positions)
    h = x.reshape(batch * seq, d)
    kv = None
    for layer in range(depth):
        ada = _ada(c_pad, w_ada[layer], b_ada[layer])
        if layer < n_a:
            h = _fox_layer(h, ada, norm_mix[layer], w_in_a[layer], b_f_a[layer], qnorm_a[layer],
                           knorm_a[layer], w_o_a[layer], batch, seq)
        else:
            if layer == n_a:
                ada_kv = _ada(c_pad, w_ada_kv, b_ada_kv)
                kv = _shared_kv(h, ada_kv, norm_kv, w_kv_b, pe_cmp_k, pe_cmp_v, w_cmp_k1, w_cmp_k2,
                                w_cmp_v1, w_cmp_v2, knorm_b, cos_t, sin_t, batch, seq)
            j = layer - n_a
            h = _nsa_layer(h, ada, norm_mix[layer], kv, cos_t, sin_t, w_in_b[j], b_gate_b[j], qnorm_b[j],
                           w_o_b[j], batch, seq)
        h = _moe_ffn(h, norm_ffn[layer], ada, 3, w_router[layer], b_router[layer], w_exp_gate[layer],
                     w_exp_up[layer], w_exp_down[layer], w_sh_gate[layer], w_sh_up[layer], w_sh_down[layer], seq)
    return h.reshape(batch, seq, d)
```

```python
import functools

import jax
import jax.numpy as jnp
from jax import lax
from jax.experimental import pallas as pl
from jax.experimental.pallas import tpu as pltpu

F32 = jnp.float32
BF16 = jnp.bfloat16
I32 = jnp.int32

HEAD_DIM = 64
N_HEADS_A = 16
N_HEADS_B = 16
N_KV_GROUPS_B = 4
HEADS_PER_GROUP = N_HEADS_B // N_KV_GROUPS_B
ROPE_THETA = 500000.0
ROPE_DIM = HEAD_DIM // 4
CMP_BLOCK = 64
N_SELECT = 16
WINDOW = 512
N_EXPERTS = 64
TOP_K = 8
ROUTED_SCALE = 2.5
EPS = 1e-6
FORCE_SCORE = 1e4
NEG = -1e30
LOG2E = 1.4426950408889634
QSCALE = HEAD_DIM ** -0.5 * LOG2E

LANES = 128
SUBLANES = 8
BF16_ROWS = 16

ROW_TILE = 512
FOX_TQ = 1024
FOX_TK = 512
NSA_TQ = 256
NSA_TK = 512
MOE_TOK_TILE = 256
MOE_UNIT = BF16_ROWS
MOE_BLOCK = 256
VMEM_LIMIT = 56 * 1024 * 1024


def _cp(*sem):
    return pltpu.CompilerParams(dimension_semantics=sem, vmem_limit_bytes=VMEM_LIMIT)


def _dot(a, b):
    return jnp.dot(a, b, preferred_element_type=F32)


def _dot_nt(a, b):
    return lax.dot_general(a, b, (((1,), (1,)), ((), ())), preferred_element_type=F32)


def _dot_tn(a, b):
    return lax.dot_general(a, b, (((0,), (0,)), ((), ())), preferred_element_type=F32)


def _split2(x):
    hi = x.astype(BF16)
    lo = (x - hi.astype(F32)).astype(BF16)
    return hi, lo


def _split3(x):
    hi = x.astype(BF16)
    r = x - hi.astype(F32)
    mid = r.astype(BF16)
    lo = (r - mid.astype(F32)).astype(BF16)
    return hi, mid, lo


def _dot_f32(a, b):
    a_hi, a_lo = _split2(a)
    b_hi, b_lo = _split2(b)
    return _dot(a_hi, b_hi) + _dot(a_lo, b_hi) + _dot(a_hi, b_lo)


def _sigmoid(x):
    return 1.0 / (1.0 + jnp.exp(-x))


def _silu(x):
    return x * _sigmoid(x)


def _ada_kernel(c_ref, w_ref, b_ref, o_ref):
    o_ref[...] = _dot_f32(_silu(c_ref[...]), w_ref[...]) + b_ref[...]


def _ada(c_pad, w, b):
    d, n = w.shape
    tn = 1024
    return pl.pallas_call(
        _ada_kernel,
        grid=(n // tn,),
        in_specs=[pl.BlockSpec((SUBLANES, d), lambda j: (0, 0)),
                  pl.BlockSpec((d, tn), lambda j: (0, j)),
                  pl.BlockSpec((1, tn), lambda j: (0, j))],
        out_specs=pl.BlockSpec((SUBLANES, tn), lambda j: (0, j)),
        out_shape=jax.ShapeDtypeStruct((SUBLANES, n), F32),
        compiler_params=_cp("parallel"),
        name="ada",
    )(c_pad, w, b.reshape(1, n))


def _norm_mod(x, gain, shift, scale):
    xn = x * lax.rsqrt(jnp.mean(x * x, axis=-1, keepdims=True) + EPS) * gain
    return xn * (1.0 + scale) + shift


def _nm_kernel(h_ref, gain_ref, shift_ref, scale_ref, w_ref, o_ref, u_scr, *, tiles_per_batch):
    i = pl.program_id(0)

    @pl.when(pl.program_id(1) == 0)
    def _():
        b = i // tiles_per_batch
        u = _norm_mod(h_ref[...], gain_ref[...], shift_ref[pl.ds(b, 1), :], scale_ref[pl.ds(b, 1), :])
        u_scr[...] = u.astype(BF16)

    o_ref[...] = _dot(u_scr[...], w_ref[...]).astype(o_ref.dtype)


def _norm_mod_matmul(h, gain, ada, shift_blk, scale_blk, w, seq, tn, name):
    t, d = h.shape
    n = w.shape[1]
    tm = ROW_TILE
    return pl.pallas_call(
        functools.partial(_nm_kernel, tiles_per_batch=seq // tm),
        grid=(t // tm, n // tn),
        in_specs=[pl.BlockSpec((tm, d), lambda i, j: (i, 0)),
                  pl.BlockSpec((1, d), lambda i, j: (0, 0)),
                  pl.BlockSpec((SUBLANES, d), lambda i, j: (0, shift_blk)),
                  pl.BlockSpec((SUBLANES, d), lambda i, j: (0, scale_blk)),
                  pl.BlockSpec((d, tn), lambda i, j: (0, j))],
        out_specs=pl.BlockSpec((tm, tn), lambda i, j: (i, j)),
        out_shape=jax.ShapeDtypeStruct((t, n), F32),
        scratch_shapes=[pltpu.VMEM((tm, d), BF16)],
        compiler_params=_cp("parallel", "arbitrary"),
        name=name,
    )(h, gain.reshape(1, d), ada, ada, w)


def _logsig_cumsum_kernel(z_ref, b_ref, tri_ref, o_ref, carry):
    @pl.when(pl.program_id(0) == 0)
    def _():
        carry[...] = jnp.zeros_like(carry)

    z = z_ref[...] + b_ref[...]
    lf = jnp.minimum(z, 0.0) - jnp.log1p(jnp.exp(-jnp.abs(z)))
    hi, mid, lo = _split3(lf)
    tri = tri_ref[...]
    c = _dot(tri, hi) + _dot(tri, mid) + _dot(tri, lo) + carry[...]
    o_ref[...] = c
    ts = c.shape[0]
    carry[...] = c[ts - 1:ts, :]


def _logsig_cumsum(z, bias):
    s, c = z.shape
    ts = 256
    tri = (jnp.arange(ts)[:, None] >= jnp.arange(ts)[None, :]).astype(BF16)
    return pl.pallas_call(
        _logsig_cumsum_kernel,
        grid=(s // ts,),
        in_specs=[pl.BlockSpec((ts, c), lambda i: (i, 0)),
                  pl.BlockSpec((1, c), lambda i: (0, 0)),
                  pl.BlockSpec((ts, ts), lambda i: (0, 0))],
        out_specs=pl.BlockSpec((ts, c), lambda i: (i, 0)),
        out_shape=jax.ShapeDtypeStruct((s, c), F32),
        scratch_shapes=[pltpu.VMEM((1, c), F32)],
        compiler_params=_cp("arbitrary"),
        name="fox_logsig_cumsum",
    )(z, bias, tri)


def _head_rms(x, gain):
    return x * lax.rsqrt(jnp.mean(x * x, axis=-1, keepdims=True) + EPS) * gain


def _fox_prep_kernel(q_ref, k_ref, v_ref, cum_ref, qg_ref, kg_ref, qo_ref, ko_ref, vo_ref):
    bh0 = (pl.program_id(0) * (N_HEADS_A // 2) + pl.program_id(1)) * 2
    ts = q_ref.shape[0]
    cum = cum_ref[...]
    lane_c = lax.broadcasted_iota(I32, cum.shape, 1)
    lane = lax.broadcasted_iota(I32, (ts, HEAD_DIM), 1)
    for h2 in range(2):
        sl = slice(h2 * HEAD_DIM, (h2 + 1) * HEAD_DIM)
        c = jnp.sum(jnp.where(lane_c == bh0 + h2, cum, 0.0), axis=-1, keepdims=True) * LOG2E
        hi, mid, lo = (p.astype(F32) for p in _split3(c))
        qn = _head_rms(q_ref[:, sl], qg_ref[...]) * QSCALE
        kn = _head_rms(k_ref[:, sl], kg_ref[...])
        ext_q = jnp.where(lane == 0, hi, jnp.where(lane == 1, mid, jnp.where(lane == 2, lo,
                          jnp.where(lane < 6, 1.0, 0.0))))
        ext_k = jnp.where(lane < 3, 1.0, jnp.where(lane == 3, -hi, jnp.where(lane == 4, -mid,
                          jnp.where(lane == 5, -lo, 0.0))))
        ext_v = jnp.where(lane == 0, 1.0, 0.0)
        qo_ref[0, h2] = jnp.concatenate([qn, ext_q], axis=1).astype(BF16)
        ko_ref[0, h2] = jnp.concatenate([kn, ext_k], axis=1).T.astype(BF16)
        vo_ref[0, h2] = jnp.concatenate([v_ref[:, sl], ext_v], axis=1).astype(BF16)


def _fox_prep(proj, cum, q_gain, k_gain, batch, seq):
    ts = ROW_TILE
    nsb = seq // ts
    hp = N_HEADS_A // 2
    out = jax.ShapeDtypeStruct((batch, N_HEADS_A, seq, LANES), BF16)
    out_t = jax.ShapeDtypeStruct((batch, N_HEADS_A, LANES, seq), BF16)
    ospec = pl.BlockSpec((1, 2, ts, LANES), lambda b, p, i: (b, p, i, 0))
    ospec_t = pl.BlockSpec((1, 2, LANES, ts), lambda b, p, i: (b, p, 0, i))
    return pl.pallas_call(
        _fox_prep_kernel,
        grid=(batch, hp, nsb),
        in_specs=[pl.BlockSpec((ts, LANES), lambda b, p, i: (b * nsb + i, p)),
                  pl.BlockSpec((ts, LANES), lambda b, p, i: (b * nsb + i, hp + p)),
                  pl.BlockSpec((ts, LANES), lambda b, p, i: (b * nsb + i, 2 * hp + p)),
                  pl.BlockSpec((ts, cum.shape[1]), lambda b, p, i: (i, 0)),
                  pl.BlockSpec((1, HEAD_DIM), lambda b, p, i: (0, 0)),
                  pl.BlockSpec((1, HEAD_DIM), lambda b, p, i: (0, 0))],
        out_specs=[ospec, ospec_t, ospec],
        out_shape=[out, out_t, out],
        compiler_params=_cp("parallel", "parallel", "parallel"),
        name="fox_prep",
    )(proj, proj, proj, cum, q_gain.reshape(1, HEAD_DIM), k_gain.reshape(1, HEAD_DIM))


def _store_heads(o_ref, o, tq, grouped):
    if not grouped:
        o_ref[0, 0] = o.astype(o_ref.dtype)
        return
    for hh in range(o.shape[0] // tq):
        o_ref[:, hh * HEAD_DIM:(hh + 1) * HEAD_DIM] = o[hh * tq:(hh + 1) * tq].astype(o_ref.dtype)


def _flash_kernel(*refs, tq, tk, with_bias, grouped):
    if with_bias:
        q_ref, nb_ref, k_ref, v_ref, o_ref, m_scr, acc_scr = refs
    else:
        q_ref, k_ref, v_ref, o_ref, m_scr, acc_scr = refs
    i = pl.program_id(1)
    q = q_ref[0, 0]
    rows = q.shape[0]
    if with_bias:
        q = jnp.concatenate([jnp.tile(nb_ref[0], (rows // tq, 1)), q], axis=1)
    m_scr[...] = jnp.full(m_scr.shape, NEG, F32)
    acc_scr[...] = jnp.zeros(acc_scr.shape, F32)
    q_pos = i * tq + jnp.bitwise_and(lax.broadcasted_iota(I32, (rows, 1), 0), tq - 1)

    def step(j, masked):
        ks = pl.multiple_of(j * tk, tk)
        s = _dot(q, k_ref[0, :, pl.ds(ks, tk)])
        if masked:
            k_pos = ks + lax.broadcasted_iota(I32, (1, tk), 1)
            s = jnp.where(k_pos <= q_pos, s, NEG)
        m_old = m_scr[...]
        m_new = jnp.maximum(m_old, jnp.max(s, axis=-1, keepdims=True))
        p = jnp.exp2(s - jnp.tile(m_new, (1, tk // LANES))).astype(BF16)
        acc_scr[...] = jnp.exp2(m_old - m_new) * acc_scr[...] + _dot(p, v_ref[0, pl.ds(ks, tk), :])
        m_scr[...] = m_new

    n_clear = (i * tq + 1) // tk
    n_diag = max(tq // tk, 1)

    def pair(jj, c):
        step(2 * jj, False)
        step(2 * jj + 1, False)
        return c

    lax.fori_loop(0, n_clear // 2, pair, 0)

    @pl.when(lax.rem(n_clear, 2) == 1)
    def _():
        step(n_clear - 1, False)

    for d in range(n_diag):
        step(n_clear + d, True)

    acc = acc_scr[...]
    _store_heads(o_ref, acc[:, :HEAD_DIM] / acc[:, HEAD_DIM:HEAD_DIM + 1], tq, grouped)


def _flash(q, k, v, tq, tk, name, bias=None, grouped_out=None):
    bh, nq, rows, kd = q.shape
    s = k.shape[2]
    assert tq % tk == 0 or tk % tq == 0
    in_specs = [pl.BlockSpec((1, 1, rows, kd), lambda b, i: (b, i, 0, 0))]
    args = [q]
    if bias is not None:
        in_specs.append(pl.BlockSpec((1, tq, bias.shape[2]), lambda b, i: (b, i, 0)))
        args.append(bias)
    in_specs += [pl.BlockSpec((1, k.shape[1], s), lambda b, i: (b, 0, 0)),
                 pl.BlockSpec((1, s, LANES), lambda b, i: (b, 0, 0))]
    if grouped_out is None:
        out_spec = pl.BlockSpec((1, 1, rows, HEAD_DIM), lambda b, i: (b, i, 0, 0))
        out_shape = jax.ShapeDtypeStruct((bh, nq, rows, HEAD_DIM), BF16)
    else:
        out_spec, out_shape = _grouped_out(grouped_out, nq, tq, rows)
    return pl.pallas_call(
        functools.partial(_flash_kernel, tq=tq, tk=tk, with_bias=bias is not None, grouped=grouped_out is not None),
        grid=(bh, nq),
        in_specs=in_specs,
        out_specs=out_spec,
        out_shape=out_shape,
        scratch_shapes=[pltpu.VMEM((rows, LANES), F32), pltpu.VMEM((rows, LANES), F32)],
        compiler_params=_cp("parallel", "arbitrary"),
        name=name,
    )(*args, k, v)


def _grouped_out(batch_groups, nq, tq, rows):
    batch, groups = batch_groups
    width = rows // tq * HEAD_DIM
    spec = pl.BlockSpec((tq, width), lambda b, i: ((b // groups) * nq + i, lax.rem(b, groups)))
    return spec, jax.ShapeDtypeStruct((batch * nq * tq, groups * width), BF16)


def _window_kernel(q_ref, k_ref, v_ref, o_ref, *, tq, window):
    i = pl.program_id(1)
    q = q_ref[0, 0]
    rows = q.shape[0]
    span = window + tq
    ks = pl.multiple_of(jnp.maximum(i * tq - window, 0), tq)
    q_pos = i * tq + jnp.bitwise_and(lax.broadcasted_iota(I32, (rows, 1), 0), tq - 1)
    k_pos = ks + lax.broadcasted_iota(I32, (1, span), 1)
    ok = (q_pos - k_pos).astype(jnp.uint32) < window
    s = jnp.where(ok, _dot(q, k_ref[0, :, pl.ds(ks, span)]), NEG)
    p = jnp.exp2(s - jnp.max(s, axis=-1, keepdims=True)).astype(BF16)
    acc = _dot(p, v_ref[0, pl.ds(ks, span), :])
    _store_heads(o_ref, acc[:, :HEAD_DIM] / acc[:, HEAD_DIM:HEAD_DIM + 1], tq, True)


def _window_attention(q, k, v, tq, window, batch_groups):
    bh, nq, rows, kd = q.shape
    s = k.shape[2]
    assert window % tq == 0 and window + tq <= s
    out_spec, out_shape = _grouped_out(batch_groups, nq, tq, rows)
    return pl.pallas_call(
        functools.partial(_window_kernel, tq=tq, window=window),
        grid=(bh, nq),
        in_specs=[pl.BlockSpec((1, 1, rows, kd), lambda b, i: (b, i, 0, 0)),
                  pl.BlockSpec((1, kd, s), lambda b, i: (b, 0, 0)),
                  pl.BlockSpec((1, s, LANES), lambda b, i: (b, 0, 0))],
        out_specs=out_spec,
        out_shape=out_shape,
        compiler_params=_cp("parallel", "parallel"),
        name="nsa_window",
    )(q, k, v)


def _oproj_kernel(o_ref, w_ref, h_ref, g_ref, out_ref, *, tiles_per_batch):
    b = pl.program_id(0) // tiles_per_batch
    out_ref[...] = h_ref[...] + g_ref[pl.ds(b, 1), :] * _dot(o_ref[...], w_ref[...])


def _oproj_residual(o, w, h, ada, gate_blk, seq):
    t, d = h.shape
    tm = ROW_TILE
    return pl.pallas_call(
        functools.partial(_oproj_kernel, tiles_per_batch=seq // tm),
        grid=(t // tm,),
        in_specs=[pl.BlockSpec((tm, o.shape[1]), lambda i: (i, 0)),
                  pl.BlockSpec(w.shape, lambda i: (0, 0)),
                  pl.BlockSpec((tm, d), lambda i: (i, 0)),
                  pl.BlockSpec((SUBLANES, d), lambda i: (0, gate_blk))],
        out_specs=pl.BlockSpec((tm, d), lambda i: (i, 0)),
        out_shape=jax.ShapeDtypeStruct((t, d), F32),
        compiler_params=_cp("parallel"),
        name="oproj_residual",
    )(o, w, h, ada)


def _seg_rms(x, seg_ones, gain_tiled):
    ss = _dot((x * x).astype(BF16), seg_ones)
    return x * lax.rsqrt(ss * (1.0 / HEAD_DIM) + EPS) * gain_tiled


def _rope(x, cos_t, sin_t):
    width = x.shape[1]
    reps = width // LANES
    lane = jnp.bitwise_and(lax.broadcasted_iota(I32, x.shape, 1), HEAD_DIM - 1)
    half = ROPE_DIM // 2
    partner = jnp.where(lane < half, pltpu.roll(x, width - half, 1), pltpu.roll(x, half, 1))
    return x * jnp.tile(cos_t, (1, reps)) + partner * jnp.tile(sin_t, (1, reps))


def _nsa_q_kernel(p_ref, z_ref, cos_ref, sin_ref, gain_ref, seg_ref, bg_ref, qp_ref, qr_ref, g_ref):
    q = _seg_rms(p_ref[...], seg_ref[...], gain_ref[...]) * QSCALE
    qp_ref[...] = q.astype(BF16)
    qr_ref[...] = _rope(q, cos_ref[...], sin_ref[...]).astype(BF16)
    g_ref[...] = _sigmoid(z_ref[...] + bg_ref[...])


def _nsa_q_prep(proj, cos_t, sin_t, q_gain, b_gate_pad):
    t = proj.shape[0]
    hd = N_HEADS_B * HEAD_DIM
    tm = ROW_TILE
    seg = (jnp.arange(hd)[:, None] // HEAD_DIM == jnp.arange(hd)[None, :] // HEAD_DIM).astype(BF16)
    row = lambda i: (i, 0)
    fix = lambda i: (0, 0)
    return pl.pallas_call(
        _nsa_q_kernel,
        grid=(t // tm,),
        in_specs=[pl.BlockSpec((tm, hd), row),
                  pl.BlockSpec((tm, LANES), lambda i: (i, hd // LANES)),
                  pl.BlockSpec((tm, LANES), row), pl.BlockSpec((tm, LANES), row),
                  pl.BlockSpec((1, hd), fix), pl.BlockSpec((hd, hd), fix), pl.BlockSpec((1, LANES), fix)],
        out_specs=[pl.BlockSpec((tm, hd), row), pl.BlockSpec((tm, hd), row), pl.BlockSpec((tm, LANES), row)],
        out_shape=[jax.ShapeDtypeStruct((t, hd), BF16), jax.ShapeDtypeStruct((t, hd), BF16),
                   jax.ShapeDtypeStruct((t, LANES), F32)],
        compiler_params=_cp("parallel"),
        name="nsa_q_prep",
    )(proj, proj, cos_t, sin_t, jnp.tile(q_gain, N_HEADS_B).reshape(1, hd), seg, b_gate_pad)


def _nsa_kv_kernel(kv_ref, cos_ref, sin_ref, g1_ref, g2_ref, seg_ref, ks_ref, vs_ref, kw_ref, vw_ref):
    gw = N_KV_GROUPS_B * HEAD_DIM
    ks = _seg_rms(kv_ref[:, 2 * gw:3 * gw], seg_ref[...], g1_ref[...])
    kw = _seg_rms(kv_ref[:, 4 * gw:5 * gw], seg_ref[...], g2_ref[...])
    ks_ref[...] = _rope(ks, cos_ref[...], sin_ref[...]).astype(BF16)
    kw_ref[...] = _rope(kw, cos_ref[...], sin_ref[...]).astype(BF16)
    vs_ref[...] = kv_ref[:, 3 * gw:4 * gw].astype(BF16)
    vw_ref[...] = kv_ref[:, 5 * gw:6 * gw].astype(BF16)


def _nsa_kv_prep(kv, cos_t, sin_t, k_gains):
    t, n = kv.shape
    gw = N_KV_GROUPS_B * HEAD_DIM
    tm = ROW_TILE
    seg = (jnp.arange(gw)[:, None] // HEAD_DIM == jnp.arange(gw)[None, :] // HEAD_DIM).astype(BF16)
    row = lambda i: (i, 0)
    fix = lambda i: (0, 0)
    out = jax.ShapeDtypeStruct((t, gw), BF16)
    return pl.pallas_call(
        _nsa_kv_kernel,
        grid=(t // tm,),
        in_specs=[pl.BlockSpec((tm, n), row), pl.BlockSpec((tm, LANES), row), pl.BlockSpec((tm, LANES), row),
                  pl.BlockSpec((1, gw), fix), pl.BlockSpec((1, gw), fix), pl.BlockSpec((gw, gw), fix)],
        out_specs=[pl.BlockSpec((tm, gw), row)] * 4,
        out_shape=[out] * 4,
        compiler_params=_cp("parallel"),
        name="nsa_kv_prep",
    )(kv, cos_t, sin_t, jnp.tile(k_gains[1], N_KV_GROUPS_B).reshape(1, gw),
      jnp.tile(k_gains[2], N_KV_GROUPS_B).reshape(1, gw), seg)


def _compress_kernel(x_ref, pe_ref, w1_ref, w2_ref, g_ref, o_ref, *, norm):
    x = (x_ref[...] + pe_ref[...]).astype(BF16)
    y = _dot(_silu(_dot(x, w1_ref[...])).astype(BF16), w2_ref[...])
    if norm:
        y = _head_rms(y, g_ref[...])
    o_ref[...] = y.astype(o_ref.dtype)


def _compress(x, pe_flat, w1, w2, gain, norm, name):
    r, kdim = x.shape
    tm = min(ROW_TILE, r)
    fix = lambda i: (0, 0)
    return pl.pallas_call(
        functools.partial(_compress_kernel, norm=norm),
        grid=(r // tm,),
        in_specs=[pl.BlockSpec((tm, kdim), lambda i: (i, 0)), pl.BlockSpec((1, kdim), fix),
                  pl.BlockSpec(w1.shape, fix), pl.BlockSpec(w2.shape, fix), pl.BlockSpec((1, HEAD_DIM), fix)],
        out_specs=pl.BlockSpec((tm, HEAD_DIM), lambda i: (i, 0)),
        out_shape=jax.ShapeDtypeStruct((r, HEAD_DIM), BF16),
        compiler_params=_cp("parallel"),
        name=name,
    )(x, pe_flat, w1, w2, gain.reshape(1, HEAD_DIM))


def _cmp_select_kernel(q_ref, k_ref, v_ref, o_ref, nb_ref, *, tq, n_sel):
    i = pl.program_id(1)
    q = q_ref[0, 0]
    rows = q.shape[0]
    nblk = k_ref.shape[1]
    logits = _dot_nt(q, k_ref[0])
    t_pos = i * tq + jnp.bitwise_and(lax.broadcasted_iota(I32, (rows, 1), 0), tq - 1)
    blk = lax.broadcasted_iota(I32, (1, nblk), 1)
    vis = (blk + 1) * CMP_BLOCK - 1 <= t_pos
    lm = jnp.where(vis, logits, NEG)
    e = jnp.exp2(lm - jnp.max(lm, axis=-1, keepdims=True))
    p = jnp.where(vis, e / jnp.sum(e, axis=-1, keepdims=True), 0.0)
    _store_heads(o_ref, _dot(p.astype(BF16), v_ref[0])[:, :HEAD_DIM], tq, True)

    imp = p[0:tq]
    for hh in range(1, rows // tq):
        imp = imp + p[hh * tq:(hh + 1) * tq]
    cur = lax.shift_right_logical(t_pos[0:tq], CMP_BLOCK.bit_length() - 1)
    forced = jnp.logical_or(blk == 0, jnp.logical_or(blk == cur, blk == cur - 1))
    cand = jnp.where(jnp.logical_or(forced, blk > cur), -jnp.inf, imp)
    sel = jnp.where(forced, 1.0, 0.0)
    blk_f = blk.astype(F32)
    for _ in range(max(n_sel - 3, 0)):
        best = jnp.max(cand, axis=-1, keepdims=True)
        first = jnp.min(jnp.where(cand == best, blk_f, float(nblk)), axis=-1, keepdims=True)
        hit = blk_f == first
        sel = jnp.where(jnp.logical_and(hit, best > -jnp.inf), 1.0, sel)
        cand = jnp.where(hit, -jnp.inf, cand)
    nb_ref[0] = jnp.where(sel > 0.5, 0.0, NEG).astype(nb_ref.dtype)


def _cmp_select(q, k_cmp, v_cmp, seq, tq, batch_groups):
    bg, nq, rows, _ = q.shape
    nblk = k_cmp.shape[1]
    assert min(N_SELECT, nblk) >= 3
    out_spec, out_shape = _grouped_out(batch_groups, nq, tq, rows)
    return pl.pallas_call(
        functools.partial(_cmp_select_kernel, tq=tq, n_sel=min(N_SELECT, nblk)),
        grid=(bg, nq),
        in_specs=[pl.BlockSpec((1, 1, rows, HEAD_DIM), lambda b, i: (b, i, 0, 0)),
                  pl.BlockSpec((1, nblk, HEAD_DIM), lambda b, i: (b, 0, 0)),
                  pl.BlockSpec((1, nblk, LANES), lambda b, i: (b, 0, 0))],
        out_specs=[out_spec, pl.BlockSpec((1, tq, nblk), lambda b, i: (b, i, 0))],
        out_shape=[out_shape, jax.ShapeDtypeStruct((bg, seq, nblk), BF16)],
        compiler_params=_cp("parallel", "parallel"),
        name="nsa_cmp_select",
    )(q, k_cmp, v_cmp)


def _nsa_out_kernel(oc_ref, os_ref, ow_ref, g_ref, e_ref, w_ref, h_ref, ada_ref, out_ref, *, tiles_per_batch):
    b = pl.program_id(0) // tiles_per_batch
    g_hi, g_lo = _split2(g_ref[...])
    o = None
    for j, o_ref in enumerate((oc_ref, os_ref, ow_ref)):
        gate = _dot(g_hi, e_ref[j]) + _dot(g_lo, e_ref[j])
        term = gate * o_ref[...].astype(F32)
        o = term if o is None else o + term
    out_ref[...] = h_ref[...] + ada_ref[pl.ds(b, 1), :] * _dot(o.astype(BF16), w_ref[...])


def _nsa_out(o_cmp, o_sel, o_win, gates, w_o, h, ada, gate_blk, seq):
    t, d = h.shape
    hd = o_cmp.shape[1]
    tm = ROW_TILE
    col = jnp.arange(LANES)[:, None]
    lane = jnp.arange(hd)[None, :]
    expand = jnp.stack([(col == (lane // HEAD_DIM) * 3 + j) for j in range(3)]).astype(BF16)
    row = lambda i: (i, 0)
    return pl.pallas_call(
        functools.partial(_nsa_out_kernel, tiles_per_batch=seq // tm),
        grid=(t // tm,),
        in_specs=[pl.BlockSpec((tm, hd), row), pl.BlockSpec((tm, hd), row), pl.BlockSpec((tm, hd), row),
                  pl.BlockSpec((tm, LANES), row),
                  pl.BlockSpec((3, LANES, hd), lambda i: (0, 0, 0)),
                  pl.BlockSpec(w_o.shape, lambda i: (0, 0)),
                  pl.BlockSpec((tm, d), row),
                  pl.BlockSpec((SUBLANES, d), lambda i: (0, gate_blk))],
        out_specs=pl.BlockSpec((tm, d), row),
        out_shape=jax.ShapeDtypeStruct((t, d), F32),
        compiler_params=_cp("parallel"),
        name="nsa_out",
    )(o_cmp, o_sel, o_win, gates, expand, w_o, h, ada)


def _route_kernel(h_ref, gain_ref, shift_ref, scale_ref, wr_ref, br_ref, tri_ref, upper_ref,
                  u_ref, pos_ref, w_ref, nu_ref, off_ref, *, tiles_per_batch):
    b = pl.program_id(0) // tiles_per_batch
    u = _norm_mod(h_ref[...], gain_ref[...], shift_ref[pl.ds(b, 1), :], scale_ref[pl.ds(b, 1), :])
    u_ref[...] = u.astype(BF16)
    scores = _sigmoid(_dot_f32(u, wr_ref[...]))
    lane = lax.broadcasted_iota(I32, scores.shape, 1)
    lane_f = lane.astype(F32)
    biased = jnp.where(lane < N_EXPERTS, scores + br_ref[...], -jnp.inf)
    chosen = jnp.zeros(scores.shape, F32)
    hits, val = [], []
    for _ in range(TOP_K):
        best = jnp.max(biased, axis=-1, keepdims=True)
        first = jnp.min(jnp.where(biased == best, lane_f, float(LANES)), axis=-1, keepdims=True)
        hit = lane_f == first
        hits.append(hit)
        val.append(jnp.sum(jnp.where(hit, scores, 0.0), axis=-1, keepdims=True))
        biased = jnp.where(hit, -jnp.inf, biased)
        chosen = jnp.where(hit, 1.0, chosen)
    top_s = jnp.concatenate(val, axis=1)
    w_ref[...] = top_s / jnp.sum(top_s, axis=-1, keepdims=True) * ROUTED_SCALE

    before = _dot(tri_ref[...], chosen.astype(BF16))
    count = jnp.sum(chosen, axis=0, keepdims=True)
    units = jnp.floor((count + (MOE_UNIT - 1)) * (1.0 / MOE_UNIT))
    padded = jnp.broadcast_to(units * MOE_UNIT, (SUBLANES, LANES))
    offset = _dot(padded.astype(BF16), upper_ref[...])[0:1]
    row_of = before + offset
    pos = [jnp.sum(jnp.where(hit, row_of, 0.0), axis=-1, keepdims=True) for hit in hits]
    pos_ref[...] = jnp.concatenate(pos, axis=1).astype(I32)
    nu_ref[0] = units.astype(I32)
    off_ref[0] = offset.astype(I32)


def _route(h, gain, ada, shift_blk, scale_blk, w_router_pad, b_router_pad, seq):
    t, d = h.shape
    tt = MOE_TOK_TILE
    n_tiles = t // tt
    row = lambda i: (i, 0)
    fix = lambda i: (0, 0)
    tri = (jnp.arange(tt)[:, None] > jnp.arange(tt)[None, :]).astype(BF16)
    upper = (jnp.arange(LANES)[:, None] < jnp.arange(LANES)[None, :]).astype(BF16)
    per_tile = jax.ShapeDtypeStruct((n_tiles, 1, LANES), I32)
    per_tile_spec = pl.BlockSpec((1, 1, LANES), lambda i: (i, 0, 0))
    return pl.pallas_call(
        functools.partial(_route_kernel, tiles_per_batch=seq // tt),
        grid=(n_tiles,),
        in_specs=[pl.BlockSpec((tt, d), row), pl.BlockSpec((1, d), fix),
                  pl.BlockSpec((SUBLANES, d), lambda i: (0, shift_blk)),
                  pl.BlockSpec((SUBLANES, d), lambda i: (0, scale_blk)),
                  pl.BlockSpec((d, LANES), fix), pl.BlockSpec((1, LANES), fix),
                  pl.BlockSpec((tt, tt), fix), pl.BlockSpec((LANES, LANES), fix)],
        out_specs=[pl.BlockSpec((tt, d), row), pl.BlockSpec((tt, TOP_K), row), pl.BlockSpec((tt, TOP_K), row),
                   per_tile_spec, per_tile_spec],
        out_shape=[jax.ShapeDtypeStruct((t, d), BF16), jax.ShapeDtypeStruct((t, TOP_K), I32),
                   jax.ShapeDtypeStruct((t, TOP_K), F32), per_tile, per_tile],
        compiler_params=_cp("parallel"),
        name="moe_route",
    )(h, gain.reshape(1, d), ada, ada, w_router_pad, b_router_pad, tri, upper)


PLAN_SLOTS = 256
PLAN_TILES = LANES


def _plan_kernel(nu_ref, off_ref, upper_ref, lower_ref, row_ref, exp_ref, nused_ref,
                 *, n_tiles, r_loc, upb, n_unit_slots, n_dump_units):
    nu = nu_ref[...]
    nu_t = nu.T
    off_t = off_ref[...].T
    upper = upper_ref[...]
    lower = lower_ref[...]
    lane = lax.broadcasted_iota(I32, (PLAN_SLOTS, LANES), 1)
    lane_f = lane.astype(F32)

    def exact_dot(a, b_bf16, left):
        parts = _split3(a)
        return sum(_dot(b_bf16, p) if left else _dot(p, b_bf16) for p in parts)

    units_before_t = _dot(nu_t.astype(BF16), upper)
    count_col = jnp.sum(nu_t, axis=1, keepdims=True)
    count_row = jnp.sum(nu, axis=0, keepdims=True)
    padded_col = jnp.floor((count_col + (upb - 1)) * (1.0 / upb)) * upb
    padded_row = jnp.floor((count_row + (upb - 1)) * (1.0 / upb)) * upb
    start_col = exact_dot(jnp.broadcast_to(padded_col, (LANES, LANES)), lower, True)
    real_col = exact_dot(jnp.broadcast_to(count_col, (LANES, LANES)), lower, True)
    start_row = exact_dot(jnp.broadcast_to(padded_row, (SUBLANES, LANES)), upper, False)[0:1]
    seg_start_t = start_col + units_before_t
    seg_real_t = real_col + units_before_t
    tile_id = lax.broadcasted_iota(I32, (LANES, LANES), 1).astype(F32)
    seg_row_t = tile_id * r_loc + off_t

    idx = pl.program_id(0) * PLAN_SLOTS + lax.broadcasted_iota(I32, (PLAN_SLOTS, 1), 0)
    slot = jnp.where(idx < n_unit_slots, idx, (idx - n_unit_slots) * upb).astype(F32)

    in_e = jnp.logical_and(start_row <= slot, lane < N_EXPERTS)
    expert = jnp.sum(jnp.where(in_e, 1.0, 0.0), axis=-1, keepdims=True) - 1.0
    pick_e = (lane_f == expert).astype(BF16)
    look = lambda tbl: exact_dot(tbl, pick_e, True)
    starts = look(seg_start_t)
    in_t = jnp.logical_and(starts <= slot, lane < n_tiles)
    tile = jnp.sum(jnp.where(in_t, 1.0, 0.0), axis=-1, keepdims=True) - 1.0
    at = lambda x: jnp.sum(jnp.where(lane_f == tile, x, 0.0), axis=-1, keepdims=True)
    m = slot - at(starts)
    n = at(look(nu_t))
    row = at(look(seg_row_t)) + m * MOE_UNIT
    pad_idx = slot - (at(look(seg_real_t)) + jnp.minimum(m, n))
    dump = n_tiles * r_loc + jnp.minimum(pad_idx, n_dump_units - 1.0) * MOE_UNIT
    row_ref[...] = jnp.where(m < n, row, dump).astype(I32)
    exp_ref[...] = expert.astype(I32)
    total = jnp.sum(padded_row, axis=-1, keepdims=True) * (1.0 / upb)
    nused_ref[...] = jnp.broadcast_to(total, (SUBLANES, LANES)).astype(I32)


def _plan(nu, off, n_tiles, r_loc, n_blocks, n_dump_units):
    assert n_tiles <= PLAN_TILES
    upb = MOE_BLOCK // MOE_UNIT
    n_unit_slots = -(-(n_blocks * upb) // PLAN_SLOTS) * PLAN_SLOTS
    n_steps = n_unit_slots // PLAN_SLOTS + -(-(n_blocks + 2) // PLAN_SLOTS)
    pad = lambda x: jnp.pad(x.astype(F32), ((0, PLAN_TILES - n_tiles), (0, 0)))
    upper = (jnp.arange(LANES)[:, None] < jnp.arange(LANES)[None, :]).astype(BF16)
    fix = lambda s: (0, 0)
    col = pl.BlockSpec((PLAN_SLOTS, 1), lambda s: (s, 0))
    rows, experts, n_used = pl.pallas_call(
        functools.partial(_plan_kernel, n_tiles=n_tiles, r_loc=r_loc, upb=upb, n_unit_slots=n_unit_slots,
                          n_dump_units=n_dump_units),
        grid=(n_steps,),
        in_specs=[pl.BlockSpec((PLAN_TILES, LANES), fix), pl.BlockSpec((PLAN_TILES, LANES), fix),
                  pl.BlockSpec((LANES, LANES), fix), pl.BlockSpec((LANES, LANES), fix)],
        out_specs=[col, col, pl.BlockSpec((SUBLANES, LANES), fix)],
        out_shape=[jax.ShapeDtypeStruct((n_steps * PLAN_SLOTS, 1), I32),
                   jax.ShapeDtypeStruct((n_steps * PLAN_SLOTS, 1), I32),
                   jax.ShapeDtypeStruct((SUBLANES, LANES), I32)],
        compiler_params=_cp("arbitrary"),
        name="moe_plan",
    )(pad(nu), pad(off), upper, upper.T)
    return (rows[:n_blocks * upb, 0], experts[n_unit_slots:n_unit_slots + n_blocks + 2, 0], n_used[0, :1])


def _one_hot_t(pos_ref, weight_ref, n_rows):
    tt = pos_ref.shape[0]
    col = lax.broadcasted_iota(I32, (tt, n_rows), 1)
    acc = jnp.zeros((tt, n_rows), F32)
    for k in range(TOP_K):
        acc = jnp.where(col == pos_ref[:, k:k + 1], 1.0 if weight_ref is None else weight_ref[:, k:k + 1], acc)
    return acc.astype(BF16)


def _dispatch_kernel(u_ref, pos_ref, x_ref):
    x_ref[0] = _dot_tn(_one_hot_t(pos_ref, None, x_ref.shape[1]), u_ref[...]).astype(x_ref.dtype)


def _dispatch(u, lpos, n_tiles_total, r_loc):
    t, d = u.shape
    tt = MOE_TOK_TILE
    n_real = t // tt
    return pl.pallas_call(
        _dispatch_kernel,
        grid=(n_tiles_total,),
        in_specs=[pl.BlockSpec((tt, d), lambda i: (jnp.minimum(i, n_real - 1), 0)),
                  pl.BlockSpec((tt, TOP_K), lambda i: (i, 0))],
        out_specs=pl.BlockSpec((1, r_loc, d), lambda i: (i, 0, 0)),
        out_shape=jax.ShapeDtypeStruct((n_tiles_total, r_loc, d), BF16),
        compiler_params=_cp("parallel"),
        name="moe_dispatch",
    )(u, lpos)


def _expert_kernel(tbl_ref, be_ref, nblk_ref, x_hbm, wg_ref, wu_ref, wd_ref, y_hbm, xbuf, ybuf, gsem, ssem, *, upb):
    del be_ref
    b = pl.program_id(0)
    nblk = nblk_ref[0]
    slot = lax.rem(b, 2)

    def gather(blk, sl):
        for m in range(upb):
            row = pl.multiple_of(tbl_ref[blk * upb + m], MOE_UNIT)
            yield pltpu.make_async_copy(x_hbm.at[pl.ds(row, MOE_UNIT), :],
                                        xbuf.at[sl, pl.ds(m * MOE_UNIT, MOE_UNIT), :], gsem.at[sl, m])

    def scatter(blk, sl):
        for m in range(upb):
            row = pl.multiple_of(tbl_ref[blk * upb + m], MOE_UNIT)
            yield pltpu.make_async_copy(ybuf.at[sl, pl.ds(m * MOE_UNIT, MOE_UNIT), :],
                                        y_hbm.at[pl.ds(row, MOE_UNIT), :], ssem.at[sl, m])

    @pl.when(jnp.logical_and(b == 0, nblk > 0))
    def _():
        for cp in gather(0, 0):
            cp.start()

    @pl.when(jnp.logical_and(b >= 2, b < nblk + 2))
    def _():
        for cp in scatter(b - 2, slot):
            cp.wait()

    @pl.when(b < nblk)
    def _():
        for cp in gather(b, slot):
            cp.wait()
        nxt = jnp.where(b + 1 < nblk, b + 1, 0)
        for cp in gather(nxt, 1 - slot):
            cp.start()
        x = xbuf[slot]
        hid = _silu(_dot(x, wg_ref[0])) * _dot(x, wu_ref[0])
        ybuf[slot] = _dot(hid.astype(BF16), wd_ref[0]).astype(ybuf.dtype)
        for cp in scatter(b, slot):
            cp.start()

    @pl.when(jnp.logical_and(b == nblk, nblk > 0))
    def _():
        for cp in gather(0, slot):
            cp.wait()


def _experts(x_loc, unit_rows, block_e, n_blocks_used, w_gate, w_up, w_down, n_blocks):
    rows, d = x_loc.shape
    de = w_gate.shape[2]
    upb = MOE_BLOCK // MOE_UNIT
    grid_spec = pltpu.PrefetchScalarGridSpec(
        num_scalar_prefetch=3,
        grid=(n_blocks + 2,),
        in_specs=[pl.BlockSpec(memory_space=pl.ANY),
                  pl.BlockSpec((1, d, de), lambda b, tbl, be, nb: (be[b], 0, 0)),
                  pl.BlockSpec((1, d, de), lambda b, tbl, be, nb: (be[b], 0, 0)),
                  pl.BlockSpec((1, de, d), lambda b, tbl, be, nb: (be[b], 0, 0))],
        out_specs=pl.BlockSpec(memory_space=pl.ANY),
        scratch_shapes=[pltpu.VMEM((2, MOE_BLOCK, d), BF16), pltpu.VMEM((2, MOE_BLOCK, d), BF16),
                        pltpu.SemaphoreType.DMA((2, upb)), pltpu.SemaphoreType.DMA((2, upb))],
    )
    return pl.pallas_call(
        functools.partial(_expert_kernel, upb=upb),
        grid_spec=grid_spec,
        out_shape=jax.ShapeDtypeStruct((rows, d), BF16),
        input_output_aliases={3: 0},
        compiler_params=_cp("arbitrary"),
        name="moe_experts",
    )(unit_rows, block_e, n_blocks_used, x_loc, w_gate, w_up, w_down)


def _combine_kernel(y_ref, pos_ref, tw_ref, u_ref, wsg_ref, wsu_ref, wsd_ref, h_ref, g_ref, out_ref, *, tiles_per_batch):
    b = pl.program_id(0) // tiles_per_batch
    routed = _dot(_one_hot_t(pos_ref, tw_ref, y_ref.shape[1]), y_ref[0])
    u = u_ref[...]
    hid = _silu(_dot(u, wsg_ref[...])) * _dot(u, wsu_ref[...])
    shared = _dot(hid.astype(BF16), wsd_ref[...])
    out_ref[...] = h_ref[...] + g_ref[pl.ds(b, 1), :] * (routed + shared)


def _combine(y_loc, lpos, top_w, u, w_sg, w_su, w_sd, h, ada, gate_blk, seq):
    t, d = h.shape
    tt = MOE_TOK_TILE
    r_loc = y_loc.shape[1]
    row = lambda i: (i, 0)
    fix = lambda i: (0, 0)
    return pl.pallas_call(
        functools.partial(_combine_kernel, tiles_per_batch=seq // tt),
        grid=(t // tt,),
        in_specs=[pl.BlockSpec((1, r_loc, d), lambda i: (i, 0, 0)),
                  pl.BlockSpec((tt, TOP_K), row), pl.BlockSpec((tt, TOP_K), row),
                  pl.BlockSpec((tt, d), row),
                  pl.BlockSpec(w_sg.shape, fix), pl.BlockSpec(w_su.shape, fix), pl.BlockSpec(w_sd.shape, fix),
                  pl.BlockSpec((tt, d), row),
                  pl.BlockSpec((SUBLANES, d), lambda i: (0, gate_blk))],
        out_specs=pl.BlockSpec((tt, d), row),
        out_shape=jax.ShapeDtypeStruct((t, d), F32),
        compiler_params=_cp("parallel"),
        name="moe_combine",
    )(y_loc, lpos, top_w, u, w_sg, w_su, w_sd, h, ada)


def _moe_ffn(h, gain, ada, blk0, w_router, b_router, w_eg, w_eu, w_ed, w_sg, w_su, w_sd, seq):
    t, d = h.shape
    tt, unit, upb = MOE_TOK_TILE, MOE_UNIT, MOE_BLOCK // MOE_UNIT
    wr = jnp.pad(w_router, ((0, 0), (0, LANES - N_EXPERTS)))
    br = jnp.pad(b_router.astype(F32), (0, LANES - N_EXPERTS)).reshape(1, LANES)
    u, lpos, top_w, nu, off = _route(h, gain, ada, blk0, blk0 + 1, wr, br, seq)

    n_tok_tiles = t // tt
    r_loc = -(-(tt * TOP_K + N_EXPERTS * (unit - 1)) // LANES) * LANES
    n_dummy_tiles = -(-(N_EXPERTS * (upb - 1) * unit) // r_loc)
    max_units = n_tok_tiles * (tt * TOP_K // unit + N_EXPERTS)
    n_blocks = -(-max_units // upb) + N_EXPERTS
    unit_rows, block_e, n_used = _plan(nu[:, 0, :], off[:, 0, :], n_tok_tiles, r_loc, n_blocks,
                                       n_dummy_tiles * r_loc // unit)
    lpos = jnp.concatenate([lpos, jnp.full((n_dummy_tiles * tt, TOP_K), -1, I32)], axis=0)

    x_loc = _dispatch(u, lpos, n_tok_tiles + n_dummy_tiles, r_loc)
    y_loc = _experts(x_loc.reshape(-1, d), unit_rows, block_e, n_used,
                     w_eg.astype(BF16), w_eu.astype(BF16), w_ed.astype(BF16), n_blocks)
    y_loc = y_loc.reshape(n_tok_tiles + n_dummy_tiles, r_loc, d)
    return _combine(y_loc, lpos, top_w, u, w_sg.astype(BF16), w_su.astype(BF16), w_sd.astype(BF16),
                    h, ada, blk0 + 2, seq)


def _fox_layer(h, ada, norm_gain, w_in, b_f, q_gain, k_gain, w_o, batch, seq):
    t, d = h.shape
    hd = N_HEADS_A * HEAD_DIM
    w = jnp.pad(w_in, ((0, 0), (0, LANES - N_HEADS_A))).astype(BF16)
    proj = _norm_mod_matmul(h, norm_gain, ada, 0, 1, w, seq, 640, "fox_in_proj")
    z = proj[:, 3 * hd:3 * hd + N_HEADS_A].reshape(batch, seq, N_HEADS_A)
    z = z.transpose(1, 0, 2).reshape(seq, batch * N_HEADS_A)
    cum = _logsig_cumsum(z, jnp.tile(b_f.astype(F32), batch).reshape(1, -1))
    q, k, v = _fox_prep(proj, cum, q_gain, k_gain, batch, seq)
    bh = batch * N_HEADS_A
    o = _flash(q.reshape(bh, seq // FOX_TQ, FOX_TQ, LANES), k.reshape(bh, LANES, seq),
               v.reshape(bh, seq, LANES), FOX_TQ, FOX_TK, "fox_attention")
    o = o.reshape(batch, N_HEADS_A, seq, HEAD_DIM).transpose(0, 2, 1, 3).reshape(t, hd)
    return _oproj_residual(o, w_o.astype(BF16), h, ada, 2, seq)


def _rope_tables(positions):
    inv_freq = ROPE_THETA ** (-jnp.arange(0, ROPE_DIM, 2, dtype=F32) / ROPE_DIM)
    ang = positions.astype(F32).reshape(-1, 1) * inv_freq
    half = ROPE_DIM // 2
    cos, sin = jnp.cos(ang), jnp.sin(ang)
    rest = HEAD_DIM - ROPE_DIM
    cos_h = jnp.concatenate([cos, cos, jnp.ones((ang.shape[0], rest), F32)], axis=1)
    sin_h = jnp.concatenate([-sin, sin, jnp.zeros((ang.shape[0], rest), F32)], axis=1)
    del half
    return jnp.tile(cos_h, (1, LANES // HEAD_DIM)), jnp.tile(sin_h, (1, LANES // HEAD_DIM))


def _group_rows(x, batch, seq, tq):
    g, hpg = N_KV_GROUPS_B, HEADS_PER_GROUP
    x = x.reshape(batch, seq // tq, tq, g, hpg, HEAD_DIM).transpose(0, 3, 1, 4, 2, 5)
    return x.reshape(batch * g, seq // tq, hpg * tq, HEAD_DIM)


def _per_group(x, batch, seq):
    g = N_KV_GROUPS_B
    return x.reshape(batch, seq, g, HEAD_DIM).transpose(0, 2, 1, 3).reshape(batch * g, seq, HEAD_DIM)


def _with_ones(v):
    pad = jnp.zeros(v.shape[:-1] + (LANES - HEAD_DIM,), v.dtype).at[..., 0].set(1)
    return jnp.concatenate([v, pad], axis=-1)


def _shared_kv(h, ada_kv, norm_kv, w_kv, pe_k, pe_v, w_ck1, w_ck2, w_cv1, w_cv2, k_gains, cos_t, sin_t, batch, seq):
    g = N_KV_GROUPS_B
    gw = g * HEAD_DIM
    nb = seq // CMP_BLOCK
    kv = _norm_mod_matmul(h, norm_kv, ada_kv, 0, 1, w_kv.astype(BF16), seq, w_kv.shape[1], "nsa_kv_proj")
    k_sel, v_sel, k_win, v_win = _nsa_kv_prep(kv, cos_t, sin_t, k_gains)

    def blocks(x):
        x = x.reshape(batch, nb, CMP_BLOCK, g, HEAD_DIM).transpose(0, 1, 3, 2, 4)
        return x.reshape(batch * nb * g, CMP_BLOCK * HEAD_DIM)

    k_cmp = _compress(blocks(kv[:, 0:gw]), pe_k.reshape(1, -1), w_ck1.astype(BF16), w_ck2.astype(BF16),
                      k_gains[0], True, "nsa_compress_k")
    v_cmp = _compress(blocks(kv[:, gw:2 * gw]), pe_v.reshape(1, -1), w_cv1.astype(BF16), w_cv2.astype(BF16),
                      k_gains[0], False, "nsa_compress_v")
    to_bg = lambda x: x.reshape(batch, nb, g, HEAD_DIM).transpose(0, 2, 1, 3).reshape(batch * g, nb, HEAD_DIM)
    k_cmp = to_bg(k_cmp)
    v_cmp = jnp.pad(to_bg(v_cmp), ((0, 0), (0, 0), (0, LANES - HEAD_DIM)))

    def per_group_t(x):
        return x.reshape(batch, seq, g, HEAD_DIM).transpose(0, 2, 3, 1).reshape(batch * g, HEAD_DIM, seq)

    blk_onehot = (jnp.arange(nb)[:, None] == jnp.arange(seq)[None, :] // CMP_BLOCK).astype(BF16)
    k_sel = jnp.concatenate([jnp.broadcast_to(blk_onehot, (batch * g, nb, seq)), per_group_t(k_sel)], axis=1)
    return (k_cmp, v_cmp, k_sel, _with_ones(_per_group(v_sel, batch, seq)),
            per_group_t(k_win), _with_ones(_per_group(v_win, batch, seq)))


def _nsa_layer(h, ada, norm_gain, kv, cos_t, sin_t, w_in, b_gate, q_gain, w_o, batch, seq):
    hd = N_HEADS_B * HEAD_DIM
    k_cmp, v_cmp, k_sel, v_sel, k_win, v_win = kv
    n_gate = 3 * N_HEADS_B
    w = jnp.pad(w_in, ((0, 0), (0, LANES - n_gate))).astype(BF16)
    proj = _norm_mod_matmul(h, norm_gain, ada, 0, 1, w, seq, hd + LANES, "nsa_in_proj")
    bg = jnp.pad(b_gate.astype(F32), (0, LANES - n_gate)).reshape(1, LANES)
    q_plain, q_rot, gates = _nsa_q_prep(proj, cos_t, sin_t, q_gain, bg)

    tq = NSA_TQ
    qp = _group_rows(q_plain, batch, seq, tq)
    qr = _group_rows(q_rot, batch, seq, tq)
    bgs = (batch, N_KV_GROUPS_B)
    o_cmp, negbias = _cmp_select(qp, k_cmp, v_cmp, seq, tq, bgs)
    o_sel = _flash(qr, k_sel, v_sel, tq, NSA_TK, "nsa_selected", bias=negbias, grouped_out=bgs)
    o_win = _window_attention(qr, k_win, v_win, tq, WINDOW, bgs)
    return _nsa_out(o_cmp, o_sel, o_win, gates, w_o.astype(BF16), h, ada, 2, seq)


def kernel(x, c, positions, w_ada, b_ada, norm_mix, norm_ffn, w_in_a, b_f_a, qnorm_a, knorm_a, w_o_a, norm_kv, w_ada_kv, b_ada_kv, w_kv_b, pe_cmp_k, pe_cmp_v, w_cmp_k1, w_cmp_k2, w_cmp_v1, w_cmp_v2, knorm_b, w_in_b, b_gate_b, qnorm_b, w_o_b, w_router, b_router, w_exp_gate, w_exp_up, w_exp_down, w_sh_gate, w_sh_up, w_sh_down):
    batch, seq, d = x.shape
    depth = w_ada.shape[0]
    n_a = w_in_a.shape[0]
    c_pad = jnp.pad(c, ((0, SUBLANES - batch), (0, 0)))
    cos_t, sin_t = _rope_tables(positions)
    h = x.reshape(batch * seq, d)
    kv = None
    for layer in range(depth):
        ada = _ada(c_pad, w_ada[layer], b_ada[layer])
        if layer < n_a:
            h = _fox_layer(h, ada, norm_mix[layer], w_in_a[layer], b_f_a[layer], qnorm_a[layer],
                           knorm_a[layer], w_o_a[layer], batch, seq)
        else:
            if layer == n_a:
                ada_kv = _ada(c_pad, w_ada_kv, b_ada_kv)
                kv = _shared_kv(h, ada_kv, norm_kv, w_kv_b, pe_cmp_k, pe_cmp_v, w_cmp_k1, w_cmp_k2,
                                w_cmp_v1, w_cmp_v2, knorm_b, cos_t, sin_t, batch, seq)
            j = layer - n_a
            h = _nsa_layer(h, ada, norm_mix[layer], kv, cos_t, sin_t, w_in_b[j], b_gate_b[j], qnorm_b[j],
                           w_o_b[j], batch, seq)
        h = _moe_ffn(h, norm_ffn[layer], ada, 3, w_router[layer], b_router[layer], w_exp_gate[layer],
                     w_exp_up[layer], w_exp_down[layer], w_sh_gate[layer], w_sh_up[layer], w_sh_down[layer], seq)
    return h.reshape(batch, seq, d)
```

```python
import functools

import jax
import jax.numpy as jnp
from jax import lax
from jax.experimental import pallas as pl
from jax.experimental.pallas import tpu as pltpu

F32 = jnp.float32
BF16 = jnp.bfloat16
I32 = jnp.int32

HEAD_DIM = 64
N_HEADS_A = 16
N_HEADS_B = 16
N_KV_GROUPS_B = 4
HEADS_PER_GROUP = N_HEADS_B // N_KV_GROUPS_B
ROPE_THETA = 500000.0
ROPE_DIM = HEAD_DIM // 4
CMP_BLOCK = 64
N_SELECT = 16
WINDOW = 512
N_EXPERTS = 64
TOP_K = 8
ROUTED_SCALE = 2.5
EPS = 1e-6
FORCE_SCORE = 1e4
NEG = -1e30
LOG2E = 1.4426950408889634
QSCALE = HEAD_DIM ** -0.5 * LOG2E

LANES = 128
SUBLANES = 8
BF16_ROWS = 16
MXU_COLS = 256

ROW_TILE = 512
FOX_TQ = 1024
FOX_TK = 512
NSA_TQ = 256
NSA_TK = 512
CMP_TQ = 512
MOE_TOK_TILE = 256
ROUTE_TILES = 2
MOE_UNIT = BF16_ROWS
MOE_BLOCK = 256
VMEM_LIMIT = 56 * 1024 * 1024


def _cp(*sem):
    return pltpu.CompilerParams(dimension_semantics=sem, vmem_limit_bytes=VMEM_LIMIT)


def _dot(a, b):
    return jnp.dot(a, b, preferred_element_type=F32)


def _dot_nt(a, b):
    return lax.dot_general(a, b, (((1,), (1,)), ((), ())), preferred_element_type=F32)


def _dot_tn(a, b):
    return lax.dot_general(a, b, (((0,), (0,)), ((), ())), preferred_element_type=F32)


def _split2(x):
    hi = x.astype(BF16)
    lo = (x - hi.astype(F32)).astype(BF16)
    return hi, lo


def _split3(x):
    hi = x.astype(BF16)
    r = x - hi.astype(F32)
    mid = r.astype(BF16)
    lo = (r - mid.astype(F32)).astype(BF16)
    return hi, mid, lo


def _dot_f32(a, b):
    a_hi, a_lo = _split2(a)
    b_hi, b_lo = _split2(b)
    return _dot(a_hi, b_hi) + _dot(a_lo, b_hi) + _dot(a_hi, b_lo)


def _sigmoid(x):
    return 1.0 / (1.0 + jnp.exp(-x))


def _silu(x):
    return x * _sigmoid(x)


def _ada_kernel(c_ref, w_ref, b_ref, o_ref):
    o_ref[...] = _dot_f32(_silu(c_ref[...]), w_ref[...]) + b_ref[...]


def _ada(c_pad, w, b):
    d, n = w.shape
    tn = 1024
    return pl.pallas_call(
        _ada_kernel,
        grid=(n // tn,),
        in_specs=[pl.BlockSpec((SUBLANES, d), lambda j: (0, 0)),
                  pl.BlockSpec((d, tn), lambda j: (0, j)),
                  pl.BlockSpec((1, tn), lambda j: (0, j))],
        out_specs=pl.BlockSpec((SUBLANES, tn), lambda j: (0, j)),
        out_shape=jax.ShapeDtypeStruct((SUBLANES, n), F32),
        compiler_params=_cp("parallel"),
        name="ada",
    )(c_pad, w, b.reshape(1, n))


def _norm_mod(x, gain, shift, scale):
    xn = x * lax.rsqrt(jnp.mean(x * x, axis=-1, keepdims=True) + EPS) * gain
    return xn * (1.0 + scale) + shift


def _nm_kernel(h_ref, gain_ref, shift_ref, scale_ref, w_ref, o_ref, u_scr, *, tiles_per_batch):
    i = pl.program_id(0)

    @pl.when(pl.program_id(1) == 0)
    def _():
        b = i // tiles_per_batch
        u = _norm_mod(h_ref[...], gain_ref[...], shift_ref[pl.ds(b, 1), :], scale_ref[pl.ds(b, 1), :])
        u_scr[...] = u.astype(BF16)

    o_ref[...] = _dot(u_scr[...], w_ref[...]).astype(o_ref.dtype)


def _norm_mod_matmul(h, gain, ada, shift_blk, scale_blk, w, seq, tn, name):
    t, d = h.shape
    n = w.shape[1]
    tm = ROW_TILE
    return pl.pallas_call(
        functools.partial(_nm_kernel, tiles_per_batch=seq // tm),
        grid=(t // tm, n // tn),
        in_specs=[pl.BlockSpec((tm, d), lambda i, j: (i, 0)),
                  pl.BlockSpec((1, d), lambda i, j: (0, 0)),
                  pl.BlockSpec((SUBLANES, d), lambda i, j: (0, shift_blk)),
                  pl.BlockSpec((SUBLANES, d), lambda i, j: (0, scale_blk)),
                  pl.BlockSpec((d, tn), lambda i, j: (0, j))],
        out_specs=pl.BlockSpec((tm, tn), lambda i, j: (i, j)),
        out_shape=jax.ShapeDtypeStruct((t, n), F32),
        scratch_shapes=[pltpu.VMEM((tm, d), BF16)],
        compiler_params=_cp("parallel", "arbitrary"),
        name=name,
    )(h, gain.reshape(1, d), ada, ada, w)


def _logsig_cumsum_kernel(z_ref, b_ref, tri_ref, o_ref, carry):
    @pl.when(pl.program_id(0) == 0)
    def _():
        carry[...] = jnp.zeros_like(carry)

    z = z_ref[...] + b_ref[...]
    lf = jnp.minimum(z, 0.0) - jnp.log1p(jnp.exp(-jnp.abs(z)))
    hi, mid, lo = _split3(lf)
    tri = tri_ref[...]
    c = _dot(tri, hi) + _dot(tri, mid) + _dot(tri, lo) + carry[...]
    o_ref[...] = c
    ts = c.shape[0]
    carry[...] = c[ts - 1:ts, :]


def _logsig_cumsum(z, bias):
    s, c = z.shape
    ts = 256
    tri = (jnp.arange(ts)[:, None] >= jnp.arange(ts)[None, :]).astype(BF16)
    return pl.pallas_call(
        _logsig_cumsum_kernel,
        grid=(s // ts,),
        in_specs=[pl.BlockSpec((ts, c), lambda i: (i, 0)),
                  pl.BlockSpec((1, c), lambda i: (0, 0)),
                  pl.BlockSpec((ts, ts), lambda i: (0, 0))],
        out_specs=pl.BlockSpec((ts, c), lambda i: (i, 0)),
        out_shape=jax.ShapeDtypeStruct((s, c), F32),
        scratch_shapes=[pltpu.VMEM((1, c), F32)],
        compiler_params=_cp("arbitrary"),
        name="fox_logsig_cumsum",
    )(z, bias, tri)


def _head_rms(x, gain):
    return x * lax.rsqrt(jnp.mean(x * x, axis=-1, keepdims=True) + EPS) * gain


def _fox_prep_kernel(q_ref, k_ref, v_ref, cum_ref, qg_ref, kg_ref, qo_ref, ko_ref, vo_ref):
    bh0 = (pl.program_id(0) * (N_HEADS_A // 2) + pl.program_id(1)) * 2
    ts = q_ref.shape[0]
    cum = cum_ref[...]
    lane_c = lax.broadcasted_iota(I32, cum.shape, 1)
    lane = lax.broadcasted_iota(I32, (ts, HEAD_DIM), 1)
    for h2 in range(2):
        sl = slice(h2 * HEAD_DIM, (h2 + 1) * HEAD_DIM)
        c = jnp.sum(jnp.where(lane_c == bh0 + h2, cum, 0.0), axis=-1, keepdims=True) * LOG2E
        hi, mid, lo = (p.astype(F32) for p in _split3(c))
        qn = _head_rms(q_ref[:, sl], qg_ref[...]) * QSCALE
        kn = _head_rms(k_ref[:, sl], kg_ref[...])
        ext_q = jnp.where(lane == 0, hi, jnp.where(lane == 1, mid, jnp.where(lane == 2, lo,
                          jnp.where(lane < 6, 1.0, 0.0))))
        ext_k = jnp.where(lane < 3, 1.0, jnp.where(lane == 3, -hi, jnp.where(lane == 4, -mid,
                          jnp.where(lane == 5, -lo, 0.0))))
        ext_v = jnp.where(lane == 0, 1.0, 0.0)
        qo_ref[0, h2] = jnp.concatenate([qn, ext_q], axis=1).astype(BF16)
        ko_ref[0, h2] = jnp.concatenate([kn, ext_k], axis=1).T.astype(BF16)
        vo_ref[0, h2] = jnp.concatenate([v_ref[:, sl], ext_v], axis=1).astype(BF16)


def _fox_prep(proj, cum, q_gain, k_gain, batch, seq):
    ts = ROW_TILE
    nsb = seq // ts
    hp = N_HEADS_A // 2
    out = jax.ShapeDtypeStruct((batch, N_HEADS_A, seq, LANES), BF16)
    out_t = jax.ShapeDtypeStruct((batch, N_HEADS_A, LANES, seq), BF16)
    ospec = pl.BlockSpec((1, 2, ts, LANES), lambda b, p, i: (b, p, i, 0))
    ospec_t = pl.BlockSpec((1, 2, LANES, ts), lambda b, p, i: (b, p, 0, i))
    return pl.pallas_call(
        _fox_prep_kernel,
        grid=(batch, hp, nsb),
        in_specs=[pl.BlockSpec((ts, LANES), lambda b, p, i: (b * nsb + i, p)),
                  pl.BlockSpec((ts, LANES), lambda b, p, i: (b * nsb + i, hp + p)),
                  pl.BlockSpec((ts, LANES), lambda b, p, i: (b * nsb + i, 2 * hp + p)),
                  pl.BlockSpec((ts, cum.shape[1]), lambda b, p, i: (i, 0)),
                  pl.BlockSpec((1, HEAD_DIM), lambda b, p, i: (0, 0)),
                  pl.BlockSpec((1, HEAD_DIM), lambda b, p, i: (0, 0))],
        out_specs=[ospec, ospec_t, ospec],
        out_shape=[out, out_t, out],
        compiler_params=_cp("parallel", "parallel", "parallel"),
        name="fox_prep",
    )(proj, proj, proj, cum, q_gain.reshape(1, HEAD_DIM), k_gain.reshape(1, HEAD_DIM))


def _store_heads(o_ref, o, tq, grouped):
    if not grouped:
        o_ref[0, 0] = o.astype(o_ref.dtype)
        return
    for hh in range(o.shape[0] // tq):
        o_ref[:, hh * HEAD_DIM:(hh + 1) * HEAD_DIM] = o[hh * tq:(hh + 1) * tq].astype(o_ref.dtype)


def _flash_kernel(*refs, tq, tk, with_bias, grouped):
    if with_bias:
        q_ref, nb_ref, k_ref, v_ref, o_ref, m_scr, acc_scr = refs
    else:
        q_ref, k_ref, v_ref, o_ref, m_scr, acc_scr = refs
    i = pl.program_id(1)
    q = q_ref[0, 0]
    rows = q.shape[0]
    if with_bias:
        q = jnp.concatenate([jnp.tile(nb_ref[0], (rows // tq, 1)), q], axis=1)
    m_scr[...] = jnp.full(m_scr.shape, NEG, F32)
    acc_scr[...] = jnp.zeros(acc_scr.shape, F32)
    q_pos = i * tq + jnp.bitwise_and(lax.broadcasted_iota(I32, (rows, 1), 0), tq - 1)

    def step(j, masked):
        ks = pl.multiple_of(j * tk, tk)
        s = _dot(q, k_ref[0, :, pl.ds(ks, tk)])
        if masked:
            k_pos = ks + lax.broadcasted_iota(I32, (1, tk), 1)
            s = jnp.where(k_pos <= q_pos, s, NEG)
        m_old = m_scr[...]
        m_new = jnp.maximum(m_old, jnp.max(s, axis=-1, keepdims=True))
        p = jnp.exp2(s - jnp.tile(m_new, (1, tk // LANES))).astype(BF16)
        acc_scr[...] = jnp.exp2(m_old - m_new) * acc_scr[...] + _dot(p, v_ref[0, pl.ds(ks, tk), :])
        m_scr[...] = m_new

    n_clear = (i * tq + 1) // tk
    n_diag = max(tq // tk, 1)

    def pair(jj, c):
        step(2 * jj, False)
        step(2 * jj + 1, False)
        return c

    lax.fori_loop(0, n_clear // 2, pair, 0)

    @pl.when(lax.rem(n_clear, 2) == 1)
    def _():
        step(n_clear - 1, False)

    for d in range(n_diag):
        step(n_clear + d, True)

    acc = acc_scr[...]
    _store_heads(o_ref, acc[:, :HEAD_DIM] / acc[:, HEAD_DIM:HEAD_DIM + 1], tq, grouped)


def _flash(q, k, v, tq, tk, name, bias=None, grouped_out=None):
    bh, nq, rows, kd = q.shape
    s = k.shape[2]
    assert tq % tk == 0 or tk % tq == 0
    in_specs = [pl.BlockSpec((1, 1, rows, kd), lambda b, i: (b, i, 0, 0))]
    args = [q]
    if bias is not None:
        in_specs.append(pl.BlockSpec((1, tq, bias.shape[2]), lambda b, i: (b, i, 0)))
        args.append(bias)
    in_specs += [pl.BlockSpec((1, k.shape[1], s), lambda b, i: (b, 0, 0)),
                 pl.BlockSpec((1, s, LANES), lambda b, i: (b, 0, 0))]
    if grouped_out is None:
        out_spec = pl.BlockSpec((1, 1, rows, HEAD_DIM), lambda b, i: (b, i, 0, 0))
        out_shape = jax.ShapeDtypeStruct((bh, nq, rows, HEAD_DIM), BF16)
    else:
        out_spec, out_shape = _grouped_out(grouped_out, nq, tq, rows)
    return pl.pallas_call(
        functools.partial(_flash_kernel, tq=tq, tk=tk, with_bias=bias is not None, grouped=grouped_out is not None),
        grid=(bh, nq),
        in_specs=in_specs,
        out_specs=out_spec,
        out_shape=out_shape,
        scratch_shapes=[pltpu.VMEM((rows, LANES), F32), pltpu.VMEM((rows, LANES), F32)],
        compiler_params=_cp("parallel", "arbitrary"),
        name=name,
    )(*args, k, v)


def _grouped_out(batch_groups, nq, tq, rows):
    batch, groups = batch_groups
    width = rows // tq * HEAD_DIM
    spec = pl.BlockSpec((tq, width), lambda b, i: ((b // groups) * nq + i, lax.rem(b, groups)))
    return spec, jax.ShapeDtypeStruct((batch * nq * tq, groups * width), BF16)


def _window_kernel(q_ref, k_ref, v_ref, o_ref, *, tq, window):
    i = pl.program_id(1)
    q = q_ref[0, 0]
    rows = q.shape[0]
    span = window + tq
    ks = pl.multiple_of(jnp.maximum(i * tq - window, 0), tq)
    q_pos = i * tq + jnp.bitwise_and(lax.broadcasted_iota(I32, (rows, 1), 0), tq - 1)
    k_pos = ks + lax.broadcasted_iota(I32, (1, span), 1)
    ok = (q_pos - k_pos).astype(jnp.uint32) < window
    s = jnp.where(ok, _dot(q, k_ref[0, :, pl.ds(ks, span)]), NEG)
    p = jnp.exp2(s - jnp.max(s, axis=-1, keepdims=True)).astype(BF16)
    acc = _dot(p, v_ref[0, pl.ds(ks, span), :])
    _store_heads(o_ref, acc[:, :HEAD_DIM] / acc[:, HEAD_DIM:HEAD_DIM + 1], tq, True)


def _window_attention(q, k, v, tq, window, batch_groups):
    bh, nq, rows, kd = q.shape
    s = k.shape[2]
    assert window % tq == 0 and window + tq <= s
    out_spec, out_shape = _grouped_out(batch_groups, nq, tq, rows)
    return pl.pallas_call(
        functools.partial(_window_kernel, tq=tq, window=window),
        grid=(bh, nq),
        in_specs=[pl.BlockSpec((1, 1, rows, kd), lambda b, i: (b, i, 0, 0)),
                  pl.BlockSpec((1, kd, s), lambda b, i: (b, 0, 0)),
                  pl.BlockSpec((1, s, LANES), lambda b, i: (b, 0, 0))],
        out_specs=out_spec,
        out_shape=out_shape,
        compiler_params=_cp("parallel", "parallel"),
        name="nsa_window",
    )(q, k, v)


def _oproj_kernel(o_ref, w_ref, h_ref, g_ref, out_ref, *, tiles_per_batch):
    b = pl.program_id(0) // tiles_per_batch
    out_ref[...] = h_ref[...] + g_ref[pl.ds(b, 1), :] * _dot(o_ref[...], w_ref[...])


def _oproj_residual(o, w, h, ada, gate_blk, seq):
    t, d = h.shape
    tm = ROW_TILE
    return pl.pallas_call(
        functools.partial(_oproj_kernel, tiles_per_batch=seq // tm),
        grid=(t // tm,),
        in_specs=[pl.BlockSpec((tm, o.shape[1]), lambda i: (i, 0)),
                  pl.BlockSpec(w.shape, lambda i: (0, 0)),
                  pl.BlockSpec((tm, d), lambda i: (i, 0)),
                  pl.BlockSpec((SUBLANES, d), lambda i: (0, gate_blk))],
        out_specs=pl.BlockSpec((tm, d), lambda i: (i, 0)),
        out_shape=jax.ShapeDtypeStruct((t, d), F32),
        compiler_params=_cp("parallel"),
        name="oproj_residual",
    )(o, w, h, ada)


def _seg_rms(x, seg_ones, gain_tiled):
    ss = _dot((x * x).astype(BF16), seg_ones)
    return x * lax.rsqrt(ss * (1.0 / HEAD_DIM) + EPS) * gain_tiled


def _rope(x, cos_t, sin_t):
    width = x.shape[1]
    reps = width // LANES
    lane = jnp.bitwise_and(lax.broadcasted_iota(I32, x.shape, 1), HEAD_DIM - 1)
    half = ROPE_DIM // 2
    partner = jnp.where(lane < half, pltpu.roll(x, width - half, 1), pltpu.roll(x, half, 1))
    return x * jnp.tile(cos_t, (1, reps)) + partner * jnp.tile(sin_t, (1, reps))


def _nsa_q_kernel(p_ref, z_ref, cos_ref, sin_ref, gain_ref, seg_ref, bg_ref, qp_ref, qr_ref, g_ref):
    q = _seg_rms(p_ref[...], seg_ref[...], gain_ref[...]) * QSCALE
    qp_ref[...] = q.astype(BF16)
    qr_ref[...] = _rope(q, cos_ref[...], sin_ref[...]).astype(BF16)
    g_ref[...] = _sigmoid(z_ref[...] + bg_ref[...])


def _nsa_q_prep(proj, cos_t, sin_t, q_gain, b_gate_pad):
    t = proj.shape[0]
    hd = N_HEADS_B * HEAD_DIM
    tm = ROW_TILE
    seg = (jnp.arange(hd)[:, None] // HEAD_DIM == jnp.arange(hd)[None, :] // HEAD_DIM).astype(BF16)
    row = lambda i: (i, 0)
    fix = lambda i: (0, 0)
    return pl.pallas_call(
        _nsa_q_kernel,
        grid=(t // tm,),
        in_specs=[pl.BlockSpec((tm, hd), row),
                  pl.BlockSpec((tm, LANES), lambda i: (i, hd // LANES)),
                  pl.BlockSpec((tm, LANES), row), pl.BlockSpec((tm, LANES), row),
                  pl.BlockSpec((1, hd), fix), pl.BlockSpec((hd, hd), fix), pl.BlockSpec((1, LANES), fix)],
        out_specs=[pl.BlockSpec((tm, hd), row), pl.BlockSpec((tm, hd), row), pl.BlockSpec((tm, LANES), row)],
        out_shape=[jax.ShapeDtypeStruct((t, hd), BF16), jax.ShapeDtypeStruct((t, hd), BF16),
                   jax.ShapeDtypeStruct((t, LANES), F32)],
        compiler_params=_cp("parallel"),
        name="nsa_q_prep",
    )(proj, proj, cos_t, sin_t, jnp.tile(q_gain, N_HEADS_B).reshape(1, hd), seg, b_gate_pad)


def _nsa_kv_kernel(kv_ref, cos_ref, sin_ref, g1_ref, g2_ref, seg_ref, ks_ref, vs_ref, kw_ref, vw_ref):
    gw = N_KV_GROUPS_B * HEAD_DIM
    ks = _seg_rms(kv_ref[:, 2 * gw:3 * gw], seg_ref[...], g1_ref[...])
    kw = _seg_rms(kv_ref[:, 4 * gw:5 * gw], seg_ref[...], g2_ref[...])
    ks_ref[...] = _rope(ks, cos_ref[...], sin_ref[...]).astype(BF16)
    kw_ref[...] = _rope(kw, cos_ref[...], sin_ref[...]).astype(BF16)
    vs_ref[...] = kv_ref[:, 3 * gw:4 * gw].astype(BF16)
    vw_ref[...] = kv_ref[:, 5 * gw:6 * gw].astype(BF16)


def _nsa_kv_prep(kv, cos_t, sin_t, k_gains):
    t, n = kv.shape
    gw = N_KV_GROUPS_B * HEAD_DIM
    tm = ROW_TILE
    seg = (jnp.arange(gw)[:, None] // HEAD_DIM == jnp.arange(gw)[None, :] // HEAD_DIM).astype(BF16)
    row = lambda i: (i, 0)
    fix = lambda i: (0, 0)
    out = jax.ShapeDtypeStruct((t, gw), BF16)
    return pl.pallas_call(
        _nsa_kv_kernel,
        grid=(t // tm,),
        in_specs=[pl.BlockSpec((tm, n), row), pl.BlockSpec((tm, LANES), row), pl.BlockSpec((tm, LANES), row),
                  pl.BlockSpec((1, gw), fix), pl.BlockSpec((1, gw), fix), pl.BlockSpec((gw, gw), fix)],
        out_specs=[pl.BlockSpec((tm, gw), row)] * 4,
        out_shape=[out] * 4,
        compiler_params=_cp("parallel"),
        name="nsa_kv_prep",
    )(kv, cos_t, sin_t, jnp.tile(k_gains[1], N_KV_GROUPS_B).reshape(1, gw),
      jnp.tile(k_gains[2], N_KV_GROUPS_B).reshape(1, gw), seg)


def _compress_kernel(x_ref, pe_ref, w1_ref, w2_ref, g_ref, o_ref, *, norm):
    x = (x_ref[...] + pe_ref[...]).astype(BF16)
    y = _dot(_silu(_dot(x, w1_ref[...])).astype(BF16), w2_ref[...])
    if norm:
        y = _head_rms(y, g_ref[...])
    o_ref[...] = y.astype(o_ref.dtype)


def _compress(x, pe_flat, w1, w2, gain, norm, name):
    r, kdim = x.shape
    tm = min(ROW_TILE, r)
    fix = lambda i: (0, 0)
    return pl.pallas_call(
        functools.partial(_compress_kernel, norm=norm),
        grid=(r // tm,),
        in_specs=[pl.BlockSpec((tm, kdim), lambda i: (i, 0)), pl.BlockSpec((1, kdim), fix),
                  pl.BlockSpec(w1.shape, fix), pl.BlockSpec(w2.shape, fix), pl.BlockSpec((1, HEAD_DIM), fix)],
        out_specs=pl.BlockSpec((tm, HEAD_DIM), lambda i: (i, 0)),
        out_shape=jax.ShapeDtypeStruct((r, HEAD_DIM), BF16),
        compiler_params=_cp("parallel"),
        name=name,
    )(x, pe_flat, w1, w2, gain.reshape(1, HEAD_DIM))


def _cmp_select_kernel(q_ref, k_ref, v_ref, o_ref, nb_ref, *, tq, n_sel):
    i = pl.program_id(1)
    q = q_ref[0, 0]
    rows = q.shape[0]
    nblk = k_ref.shape[1]
    logits = _dot_nt(q, k_ref[0])
    t_pos = i * tq + jnp.bitwise_and(lax.broadcasted_iota(I32, (rows, 1), 0), tq - 1)
    blk = lax.broadcasted_iota(I32, (1, nblk), 1)
    vis = (blk + 1) * CMP_BLOCK - 1 <= t_pos
    lm = jnp.where(vis, logits, NEG)
    e = jnp.exp2(lm - jnp.max(lm, axis=-1, keepdims=True))
    p = jnp.where(vis, e / jnp.sum(e, axis=-1, keepdims=True), 0.0)
    _store_heads(o_ref, _dot(p.astype(BF16), v_ref[0])[:, :HEAD_DIM], tq, True)

    imp = p[0:tq]
    for hh in range(1, rows // tq):
        imp = imp + p[hh * tq:(hh + 1) * tq]
    cur = lax.shift_right_logical(t_pos[0:tq], CMP_BLOCK.bit_length() - 1)
    forced = jnp.logical_or(blk == 0, jnp.logical_or(blk == cur, blk == cur - 1))
    cand = jnp.where(jnp.logical_or(forced, blk > cur), -jnp.inf, imp)
    sel = jnp.where(forced, 1.0, 0.0)
    blk_f = blk.astype(F32)
    for _ in range(max(n_sel - 3, 0)):
        best = jnp.max(cand, axis=-1, keepdims=True)
        first = jnp.min(jnp.where(cand == best, blk_f, float(nblk)), axis=-1, keepdims=True)
        hit = blk_f == first
        sel = jnp.where(jnp.logical_and(hit, best > -jnp.inf), 1.0, sel)
        cand = jnp.where(hit, -jnp.inf, cand)
    nb_ref[0] = jnp.where(sel > 0.5, 0.0, NEG).astype(nb_ref.dtype)


def _cmp_select(q, k_cmp, v_cmp, seq, tq, batch_groups):
    bg, nq, rows, _ = q.shape
    nblk = k_cmp.shape[1]
    assert min(N_SELECT, nblk) >= 3
    out_spec, out_shape = _grouped_out(batch_groups, nq, tq, rows)
    return pl.pallas_call(
        functools.partial(_cmp_select_kernel, tq=tq, n_sel=min(N_SELECT, nblk)),
        grid=(bg, nq),
        in_specs=[pl.BlockSpec((1, 1, rows, HEAD_DIM), lambda b, i: (b, i, 0, 0)),
                  pl.BlockSpec((1, nblk, HEAD_DIM), lambda b, i: (b, 0, 0)),
                  pl.BlockSpec((1, nblk, LANES), lambda b, i: (b, 0, 0))],
        out_specs=[out_spec, pl.BlockSpec((1, tq, nblk), lambda b, i: (b, i, 0))],
        out_shape=[out_shape, jax.ShapeDtypeStruct((bg, seq, nblk), BF16)],
        compiler_params=_cp("parallel", "parallel"),
        name="nsa_cmp_select",
    )(q, k_cmp, v_cmp)


def _nsa_out_kernel(oc_ref, os_ref, ow_ref, g_ref, e_ref, w_ref, h_ref, ada_ref, out_ref, *, tiles_per_batch):
    b = pl.program_id(0) // tiles_per_batch
    g_hi, g_lo = _split2(g_ref[...])
    o = None
    for j, o_ref in enumerate((oc_ref, os_ref, ow_ref)):
        gate = _dot(g_hi, e_ref[j]) + _dot(g_lo, e_ref[j])
        term = gate * o_ref[...].astype(F32)
        o = term if o is None else o + term
    out_ref[...] = h_ref[...] + ada_ref[pl.ds(b, 1), :] * _dot(o.astype(BF16), w_ref[...])


def _nsa_out(o_cmp, o_sel, o_win, gates, w_o, h, ada, gate_blk, seq):
    t, d = h.shape
    hd = o_cmp.shape[1]
    tm = ROW_TILE
    col = jnp.arange(LANES)[:, None]
    lane = jnp.arange(hd)[None, :]
    expand = jnp.stack([(col == (lane // HEAD_DIM) * 3 + j) for j in range(3)]).astype(BF16)
    row = lambda i: (i, 0)
    return pl.pallas_call(
        functools.partial(_nsa_out_kernel, tiles_per_batch=seq // tm),
        grid=(t // tm,),
        in_specs=[pl.BlockSpec((tm, hd), row), pl.BlockSpec((tm, hd), row), pl.BlockSpec((tm, hd), row),
                  pl.BlockSpec((tm, LANES), row),
                  pl.BlockSpec((3, LANES, hd), lambda i: (0, 0, 0)),
                  pl.BlockSpec(w_o.shape, lambda i: (0, 0)),
                  pl.BlockSpec((tm, d), row),
                  pl.BlockSpec((SUBLANES, d), lambda i: (0, gate_blk))],
        out_specs=pl.BlockSpec((tm, d), row),
        out_shape=jax.ShapeDtypeStruct((t, d), F32),
        compiler_params=_cp("parallel"),
        name="nsa_out",
    )(o_cmp, o_sel, o_win, gates, expand, w_o, h, ada)


def _route_kernel(h_ref, gain_ref, shift_ref, scale_ref, wr_ref, br_ref, tri_ref, upper_ref,
                  u_ref, pos_ref, w_ref, nu_ref, off_ref, *, tiles_per_batch):
    b = (pl.program_id(0) * ROUTE_TILES) // tiles_per_batch
    shift, scale = shift_ref[pl.ds(b, 1), :], scale_ref[pl.ds(b, 1), :]
    for tile in range(ROUTE_TILES):
        rows = slice(tile * MOE_TOK_TILE, (tile + 1) * MOE_TOK_TILE)
        _route_tile(h_ref[rows, :], gain_ref[...], shift, scale, wr_ref, br_ref, tri_ref, upper_ref,
                    u_ref.at[rows, :], pos_ref.at[rows, :], w_ref.at[rows, :], nu_ref.at[tile], off_ref.at[tile])


def _route_tile(h, gain, shift, scale, wr_ref, br_ref, tri_ref, upper_ref, u_ref, pos_ref, w_ref, nu_ref, off_ref):
    u = _norm_mod(h, gain, shift, scale)
    u_ref[...] = u.astype(BF16)
    scores = _sigmoid(_dot_f32(u, wr_ref[...]))
    lane = lax.broadcasted_iota(I32, scores.shape, 1)
    lane_f = lane.astype(F32)
    biased = jnp.where(lane < N_EXPERTS, scores + br_ref[...], -jnp.inf)
    chosen = jnp.zeros(scores.shape, F32)
    hits, val = [], []
    upper = upper_ref[...]
    for _ in range(TOP_K):
        best = jnp.max(biased, axis=-1, keepdims=True)
        first = jnp.min(jnp.where(biased == best, lane_f, float(LANES)), axis=-1, keepdims=True)
        hit = lane_f == first
        hits.append(hit)
        val.append(jnp.sum(jnp.where(hit, scores, 0.0), axis=-1, keepdims=True))
        biased = jnp.where(hit, -jnp.inf, biased)
        chosen = jnp.where(hit, 1.0, chosen)
    top_s = jnp.concatenate(val, axis=1)
    w_ref[...] = top_s / jnp.sum(top_s, axis=-1, keepdims=True) * ROUTED_SCALE

    before = _dot(tri_ref[...], chosen.astype(BF16))
    count = jnp.sum(chosen, axis=0, keepdims=True)
    units = jnp.floor((count + (MOE_UNIT - 1)) * (1.0 / MOE_UNIT))
    padded = jnp.broadcast_to(units * MOE_UNIT, (SUBLANES, LANES))
    offset = _dot(padded.astype(BF16), upper)[0:1]
    row_of = before + offset
    pos = [jnp.sum(jnp.where(hit, row_of, 0.0), axis=-1, keepdims=True) for hit in hits]
    pos_ref[...] = jnp.concatenate(pos, axis=1).astype(I32)
    nu_ref[...] = units.astype(I32)
    off_ref[...] = offset.astype(I32)


def _route(h, gain, ada, shift_blk, scale_blk, w_router_pad, b_router_pad, seq):
    t, d = h.shape
    tt = MOE_TOK_TILE
    n_tiles = t // tt
    rt = ROUTE_TILES * tt
    assert n_tiles % ROUTE_TILES == 0 and seq % rt == 0
    row = lambda i: (i, 0)
    fix = lambda i: (0, 0)
    tri = (jnp.arange(tt)[:, None] > jnp.arange(tt)[None, :]).astype(BF16)
    upper = (jnp.arange(LANES)[:, None] < jnp.arange(LANES)[None, :]).astype(BF16)
    per_tile = jax.ShapeDtypeStruct((n_tiles, 1, LANES), I32)
    per_tile_spec = pl.BlockSpec((ROUTE_TILES, 1, LANES), lambda i: (i, 0, 0))
    return pl.pallas_call(
        functools.partial(_route_kernel, tiles_per_batch=seq // tt),
        grid=(n_tiles // ROUTE_TILES,),
        in_specs=[pl.BlockSpec((rt, d), row), pl.BlockSpec((1, d), fix),
                  pl.BlockSpec((SUBLANES, d), lambda i: (0, shift_blk)),
                  pl.BlockSpec((SUBLANES, d), lambda i: (0, scale_blk)),
                  pl.BlockSpec((d, LANES), fix), pl.BlockSpec((1, LANES), fix),
                  pl.BlockSpec((tt, tt), fix), pl.BlockSpec((LANES, LANES), fix)],
        out_specs=[pl.BlockSpec((rt, d), row), pl.BlockSpec((rt, TOP_K), row), pl.BlockSpec((rt, TOP_K), row),
                   per_tile_spec, per_tile_spec],
        out_shape=[jax.ShapeDtypeStruct((t, d), BF16), jax.ShapeDtypeStruct((t, TOP_K), I32),
                   jax.ShapeDtypeStruct((t, TOP_K), F32), per_tile, per_tile],
        compiler_params=_cp("parallel"),
        name="moe_route",
    )(h, gain.reshape(1, d), ada, ada, w_router_pad, b_router_pad, tri, upper)


PLAN_SLOTS = 256
PLAN_TILES = LANES


def _plan_kernel(nu_ref, off_ref, upper_ref, lower_ref, row_ref, exp_ref, nused_ref,
                 *, n_tiles, r_loc, upb, n_unit_slots, n_dump_units):
    nu = nu_ref[...]
    nu_t = nu.T
    off_t = off_ref[...].T
    upper = upper_ref[...]
    lower = lower_ref[...]
    lane = lax.broadcasted_iota(I32, (PLAN_SLOTS, LANES), 1)
    lane_f = lane.astype(F32)

    def exact_dot(a, b_bf16, left):
        parts = _split3(a)
        return sum(_dot(b_bf16, p) if left else _dot(p, b_bf16) for p in parts)

    units_before_t = _dot(nu_t.astype(BF16), upper)
    count_col = jnp.sum(nu_t, axis=1, keepdims=True)
    count_row = jnp.sum(nu, axis=0, keepdims=True)
    padded_col = jnp.floor((count_col + (upb - 1)) * (1.0 / upb)) * upb
    padded_row = jnp.floor((count_row + (upb - 1)) * (1.0 / upb)) * upb
    start_col = exact_dot(jnp.broadcast_to(padded_col, (LANES, LANES)), lower, True)
    real_col = exact_dot(jnp.broadcast_to(count_col, (LANES, LANES)), lower, True)
    start_row = exact_dot(jnp.broadcast_to(padded_row, (SUBLANES, LANES)), upper, False)[0:1]
    seg_start_t = start_col + units_before_t
    seg_real_t = real_col + units_before_t
    tile_id = lax.broadcasted_iota(I32, (LANES, LANES), 1).astype(F32)
    seg_row_t = tile_id * r_loc + off_t

    idx = pl.program_id(0) * PLAN_SLOTS + lax.broadcasted_iota(I32, (PLAN_SLOTS, 1), 0)
    slot = jnp.where(idx < n_unit_slots, idx, (idx - n_unit_slots) * upb).astype(F32)

    in_e = jnp.logical_and(start_row <= slot, lane < N_EXPERTS)
    expert = jnp.sum(jnp.where(in_e, 1.0, 0.0), axis=-1, keepdims=True) - 1.0
    pick_e = (lane_f == expert).astype(BF16)
    look = lambda tbl: exact_dot(tbl, pick_e, True)
    starts = look(seg_start_t)
    in_t = jnp.logical_and(starts <= slot, lane < n_tiles)
    tile = jnp.sum(jnp.where(in_t, 1.0, 0.0), axis=-1, keepdims=True) - 1.0
    at = lambda x: jnp.sum(jnp.where(lane_f == tile, x, 0.0), axis=-1, keepdims=True)
    m = slot - at(starts)
    n = at(look(nu_t))
    row = at(look(seg_row_t)) + m * MOE_UNIT
    pad_idx = slot - (at(look(seg_real_t)) + jnp.minimum(m, n))
    dump = n_tiles * r_loc + jnp.minimum(pad_idx, n_dump_units - 1.0) * MOE_UNIT
    row_ref[...] = jnp.where(m < n, row, dump).astype(I32)
    exp_ref[...] = expert.astype(I32)
    total = jnp.sum(padded_row, axis=-1, keepdims=True) * (1.0 / upb)
    nused_ref[...] = jnp.broadcast_to(total, (SUBLANES, LANES)).astype(I32)


def _plan(nu, off, n_tiles, r_loc, n_blocks, n_dump_units):
    assert n_tiles <= PLAN_TILES
    upb = MOE_BLOCK // MOE_UNIT
    n_unit_slots = -(-(n_blocks * upb) // PLAN_SLOTS) * PLAN_SLOTS
    n_steps = n_unit_slots // PLAN_SLOTS + -(-(n_blocks + 2) // PLAN_SLOTS)
    pad = lambda x: jnp.pad(x.astype(F32), ((0, PLAN_TILES - n_tiles), (0, 0)))
    upper = (jnp.arange(LANES)[:, None] < jnp.arange(LANES)[None, :]).astype(BF16)
    fix = lambda s: (0, 0)
    col = pl.BlockSpec((PLAN_SLOTS, 1), lambda s: (s, 0))
    rows, experts, n_used = pl.pallas_call(
        functools.partial(_plan_kernel, n_tiles=n_tiles, r_loc=r_loc, upb=upb, n_unit_slots=n_unit_slots,
                          n_dump_units=n_dump_units),
        grid=(n_steps,),
        in_specs=[pl.BlockSpec((PLAN_TILES, LANES), fix), pl.BlockSpec((PLAN_TILES, LANES), fix),
                  pl.BlockSpec((LANES, LANES), fix), pl.BlockSpec((LANES, LANES), fix)],
        out_specs=[col, col, pl.BlockSpec((SUBLANES, LANES), fix)],
        out_shape=[jax.ShapeDtypeStruct((n_steps * PLAN_SLOTS, 1), I32),
                   jax.ShapeDtypeStruct((n_steps * PLAN_SLOTS, 1), I32),
                   jax.ShapeDtypeStruct((SUBLANES, LANES), I32)],
        compiler_params=_cp("arbitrary"),
        name="moe_plan",
    )(pad(nu), pad(off), upper, upper.T)
    return (rows[:n_blocks * upb, 0], experts[n_unit_slots:n_unit_slots + n_blocks + 2, 0], n_used[0, :1])


def _one_hot_t(pos_ref, weight_ref, n_rows):
    tt = pos_ref.shape[0]
    col = lax.broadcasted_iota(I32, (tt, n_rows), 1)
    acc = jnp.zeros((tt, n_rows), F32)
    for k in range(TOP_K):
        acc = jnp.where(col == pos_ref[:, k:k + 1], 1.0 if weight_ref is None else weight_ref[:, k:k + 1], acc)
    return acc.astype(BF16)


def _dispatch_kernel(u_ref, pos_ref, x_ref):
    x_ref[0] = _dot_tn(_one_hot_t(pos_ref, None, x_ref.shape[1]), u_ref[...]).astype(x_ref.dtype)


def _dispatch(u, lpos, n_tiles_total, r_loc):
    t, d = u.shape
    tt = MOE_TOK_TILE
    n_real = t // tt
    return pl.pallas_call(
        _dispatch_kernel,
        grid=(n_tiles_total,),
        in_specs=[pl.BlockSpec((tt, d), lambda i: (jnp.minimum(i, n_real - 1), 0)),
                  pl.BlockSpec((tt, TOP_K), lambda i: (i, 0))],
        out_specs=pl.BlockSpec((1, r_loc, d), lambda i: (i, 0, 0)),
        out_shape=jax.ShapeDtypeStruct((n_tiles_total, r_loc, d), BF16),
        compiler_params=_cp("parallel"),
        name="moe_dispatch",
    )(u, lpos)


def _expert_kernel(tbl_ref, be_ref, nblk_ref, x_hbm, wg_ref, wu_ref, wd_ref, y_hbm, xbuf, ybuf, gsem, ssem, *, upb):
    del be_ref
    b = pl.program_id(0)
    nblk = nblk_ref[0]
    slot = lax.rem(b, 2)

    def gather(blk, sl):
        for m in range(upb):
            row = pl.multiple_of(tbl_ref[blk * upb + m], MOE_UNIT)
            yield pltpu.make_async_copy(x_hbm.at[pl.ds(row, MOE_UNIT), :],
                                        xbuf.at[sl, pl.ds(m * MOE_UNIT, MOE_UNIT), :], gsem.at[sl, m])

    def scatter(blk, sl):
        for m in range(upb):
            row = pl.multiple_of(tbl_ref[blk * upb + m], MOE_UNIT)
            yield pltpu.make_async_copy(ybuf.at[sl, pl.ds(m * MOE_UNIT, MOE_UNIT), :],
                                        y_hbm.at[pl.ds(row, MOE_UNIT), :], ssem.at[sl, m])

    @pl.when(jnp.logical_and(b == 0, nblk > 0))
    def _():
        for cp in gather(0, 0):
            cp.start()

    @pl.when(b + 1 < nblk)
    def _():
        for cp in gather(b + 1, 1 - slot):
            cp.start()

    @pl.when(jnp.logical_and(b >= 2, b < nblk + 2))
    def _():
        for cp in scatter(b - 2, slot):
            cp.wait()

    @pl.when(b < nblk)
    def _():
        for cp in gather(b, slot):
            cp.wait()
        x = xbuf[slot]
        hid = _silu(_dot(x, wg_ref[0])) * _dot(x, wu_ref[0])
        ybuf[slot] = _dot(hid.astype(BF16), wd_ref[0]).astype(ybuf.dtype)
        for cp in scatter(b, slot):
            cp.start()


def _experts(x_loc, unit_rows, block_e, n_blocks_used, w_gate, w_up, w_down, n_blocks):
    rows, d = x_loc.shape
    de = w_gate.shape[2]
    upb = MOE_BLOCK // MOE_UNIT
    grid_spec = pltpu.PrefetchScalarGridSpec(
        num_scalar_prefetch=3,
        grid=(n_blocks + 2,),
        in_specs=[pl.BlockSpec(memory_space=pl.ANY),
                  pl.BlockSpec((1, d, de), lambda b, tbl, be, nb: (be[b], 0, 0)),
                  pl.BlockSpec((1, d, de), lambda b, tbl, be, nb: (be[b], 0, 0)),
                  pl.BlockSpec((1, de, d), lambda b, tbl, be, nb: (be[b], 0, 0))],
        out_specs=pl.BlockSpec(memory_space=pl.ANY),
        scratch_shapes=[pltpu.VMEM((2, MOE_BLOCK, d), BF16), pltpu.VMEM((2, MOE_BLOCK, d), BF16),
                        pltpu.SemaphoreType.DMA((2, upb)), pltpu.SemaphoreType.DMA((2, upb))],
    )
    return pl.pallas_call(
        functools.partial(_expert_kernel, upb=upb),
        grid_spec=grid_spec,
        out_shape=jax.ShapeDtypeStruct((rows, d), BF16),
        input_output_aliases={3: 0},
        compiler_params=_cp("arbitrary"),
        name="moe_experts",
    )(unit_rows, block_e, n_blocks_used, x_loc, w_gate, w_up, w_down)


def _combine_kernel(y_ref, pos_ref, tw_ref, u_ref, wsg_ref, wsu_ref, wsd_ref, h_ref, g_ref, out_ref, *, tiles_per_batch):
    b = pl.program_id(0) // tiles_per_batch
    routed = _dot(_one_hot_t(pos_ref, tw_ref, y_ref.shape[1]), y_ref[0])
    u = u_ref[...]
    hid = _silu(_dot(u, wsg_ref[...])) * _dot(u, wsu_ref[...])
    shared = _dot(hid.astype(BF16), wsd_ref[...])
    out_ref[...] = h_ref[...] + g_ref[pl.ds(b, 1), :] * (routed + shared)


def _combine(y_loc, lpos, top_w, u, w_sg, w_su, w_sd, h, ada, gate_blk, seq):
    t, d = h.shape
    tt = MOE_TOK_TILE
    r_loc = y_loc.shape[1]
    row = lambda i: (i, 0)
    fix = lambda i: (0, 0)
    return pl.pallas_call(
        functools.partial(_combine_kernel, tiles_per_batch=seq // tt),
        grid=(t // tt,),
        in_specs=[pl.BlockSpec((1, r_loc, d), lambda i: (i, 0, 0)),
                  pl.BlockSpec((tt, TOP_K), row), pl.BlockSpec((tt, TOP_K), row),
                  pl.BlockSpec((tt, d), row),
                  pl.BlockSpec(w_sg.shape, fix), pl.BlockSpec(w_su.shape, fix), pl.BlockSpec(w_sd.shape, fix),
                  pl.BlockSpec((tt, d), row),
                  pl.BlockSpec((SUBLANES, d), lambda i: (0, gate_blk))],
        out_specs=pl.BlockSpec((tt, d), row),
        out_shape=jax.ShapeDtypeStruct((t, d), F32),
        compiler_params=_cp("parallel"),
        name="moe_combine",
    )(y_loc, lpos, top_w, u, w_sg, w_su, w_sd, h, ada)


def _moe_ffn(h, gain, ada, blk0, w_router, b_router, w_eg, w_eu, w_ed, w_sg, w_su, w_sd, seq):
    t, d = h.shape
    tt, unit, upb = MOE_TOK_TILE, MOE_UNIT, MOE_BLOCK // MOE_UNIT
    wr = jnp.pad(w_router, ((0, 0), (0, LANES - N_EXPERTS)))
    br = jnp.pad(b_router.astype(F32), (0, LANES - N_EXPERTS)).reshape(1, LANES)
    u, lpos, top_w, nu, off = _route(h, gain, ada, blk0, blk0 + 1, wr, br, seq)

    n_tok_tiles = t // tt
    r_loc = -(-(tt * TOP_K + N_EXPERTS * (unit - 1)) // LANES) * LANES
    n_dummy_tiles = -(-(N_EXPERTS * (upb - 1) * unit) // r_loc)
    max_units = n_tok_tiles * (tt * TOP_K // unit + N_EXPERTS)
    n_blocks = -(-max_units // upb) + N_EXPERTS
    unit_rows, block_e, n_used = _plan(nu[:, 0, :], off[:, 0, :], n_tok_tiles, r_loc, n_blocks,
                                       n_dummy_tiles * r_loc // unit)
    lpos = jnp.concatenate([lpos, jnp.full((n_dummy_tiles * tt, TOP_K), -1, I32)], axis=0)

    x_loc = _dispatch(u, lpos, n_tok_tiles + n_dummy_tiles, r_loc)
    y_loc = _experts(x_loc.reshape(-1, d), unit_rows, block_e, n_used,
                     w_eg.astype(BF16), w_eu.astype(BF16), w_ed.astype(BF16), n_blocks)
    y_loc = y_loc.reshape(n_tok_tiles + n_dummy_tiles, r_loc, d)
    return _combine(y_loc, lpos, top_w, u, w_sg.astype(BF16), w_su.astype(BF16), w_sd.astype(BF16),
                    h, ada, blk0 + 2, seq)


def _fox_layer(h, ada, norm_gain, w_in, b_f, q_gain, k_gain, w_o, batch, seq):
    t, d = h.shape
    hd = N_HEADS_A * HEAD_DIM
    w = jnp.pad(w_in, ((0, 0), (0, MXU_COLS - N_HEADS_A))).astype(BF16)
    proj = _norm_mod_matmul(h, norm_gain, ada, 0, 1, w, seq, w.shape[1], "fox_in_proj")
    z = proj[:, 3 * hd:3 * hd + N_HEADS_A].reshape(batch, seq, N_HEADS_A)
    z = z.transpose(1, 0, 2).reshape(seq, batch * N_HEADS_A)
    cum = _logsig_cumsum(z, jnp.tile(b_f.astype(F32), batch).reshape(1, -1))
    q, k, v = _fox_prep(proj, cum, q_gain, k_gain, batch, seq)
    bh = batch * N_HEADS_A
    o = _flash(q.reshape(bh, seq // FOX_TQ, FOX_TQ, LANES), k.reshape(bh, LANES, seq),
               v.reshape(bh, seq, LANES), FOX_TQ, FOX_TK, "fox_attention")
    o = o.reshape(batch, N_HEADS_A, seq, HEAD_DIM).transpose(0, 2, 1, 3).reshape(t, hd)
    return _oproj_residual(o, w_o.astype(BF16), h, ada, 2, seq)


def _rope_tables(positions):
    inv_freq = ROPE_THETA ** (-jnp.arange(0, ROPE_DIM, 2, dtype=F32) / ROPE_DIM)
    ang = positions.astype(F32).reshape(-1, 1) * inv_freq
    half = ROPE_DIM // 2
    cos, sin = jnp.cos(ang), jnp.sin(ang)
    rest = HEAD_DIM - ROPE_DIM
    cos_h = jnp.concatenate([cos, cos, jnp.ones((ang.shape[0], rest), F32)], axis=1)
    sin_h = jnp.concatenate([-sin, sin, jnp.zeros((ang.shape[0], rest), F32)], axis=1)
    del half
    return jnp.tile(cos_h, (1, LANES // HEAD_DIM)), jnp.tile(sin_h, (1, LANES // HEAD_DIM))


def _group_rows(x, batch, seq, tq):
    g, hpg = N_KV_GROUPS_B, HEADS_PER_GROUP
    x = x.reshape(batch, seq // tq, tq, g, hpg, HEAD_DIM).transpose(0, 3, 1, 4, 2, 5)
    return x.reshape(batch * g, seq // tq, hpg * tq, HEAD_DIM)


def _per_group(x, batch, seq):
    g = N_KV_GROUPS_B
    return x.reshape(batch, seq, g, HEAD_DIM).transpose(0, 2, 1, 3).reshape(batch * g, seq, HEAD_DIM)


def _with_ones(v):
    pad = jnp.zeros(v.shape[:-1] + (LANES - HEAD_DIM,), v.dtype).at[..., 0].set(1)
    return jnp.concatenate([v, pad], axis=-1)


def _shared_kv(h, ada_kv, norm_kv, w_kv, pe_k, pe_v, w_ck1, w_ck2, w_cv1, w_cv2, k_gains, cos_t, sin_t, batch, seq):
    g = N_KV_GROUPS_B
    gw = g * HEAD_DIM
    nb = seq // CMP_BLOCK
    kv = _norm_mod_matmul(h, norm_kv, ada_kv, 0, 1, w_kv.astype(BF16), seq, w_kv.shape[1], "nsa_kv_proj")
    k_sel, v_sel, k_win, v_win = _nsa_kv_prep(kv, cos_t, sin_t, k_gains)

    def blocks(x):
        x = x.reshape(batch, nb, CMP_BLOCK, g, HEAD_DIM).transpose(0, 1, 3, 2, 4)
        return x.reshape(batch * nb * g, CMP_BLOCK * HEAD_DIM)

    k_cmp = _compress(blocks(kv[:, 0:gw]), pe_k.reshape(1, -1), w_ck1.astype(BF16), w_ck2.astype(BF16),
                      k_gains[0], True, "nsa_compress_k")
    v_cmp = _compress(blocks(kv[:, gw:2 * gw]), pe_v.reshape(1, -1), w_cv1.astype(BF16), w_cv2.astype(BF16),
                      k_gains[0], False, "nsa_compress_v")
    to_bg = lambda x: x.reshape(batch, nb, g, HEAD_DIM).transpose(0, 2, 1, 3).reshape(batch * g, nb, HEAD_DIM)
    k_cmp = to_bg(k_cmp)
    v_cmp = jnp.pad(to_bg(v_cmp), ((0, 0), (0, 0), (0, LANES - HEAD_DIM)))

    def per_group_t(x):
        return x.reshape(batch, seq, g, HEAD_DIM).transpose(0, 2, 3, 1).reshape(batch * g, HEAD_DIM, seq)

    blk_onehot = (jnp.arange(nb)[:, None] == jnp.arange(seq)[None, :] // CMP_BLOCK).astype(BF16)
    k_sel = jnp.concatenate([jnp.broadcast_to(blk_onehot, (batch * g, nb, seq)), per_group_t(k_sel)], axis=1)
    return (k_cmp, v_cmp, k_sel, _with_ones(_per_group(v_sel, batch, seq)),
            per_group_t(k_win), _with_ones(_per_group(v_win, batch, seq)))


def _nsa_layer(h, ada, norm_gain, kv, cos_t, sin_t, w_in, b_gate, q_gain, w_o, batch, seq):
    hd = N_HEADS_B * HEAD_DIM
    k_cmp, v_cmp, k_sel, v_sel, k_win, v_win = kv
    n_gate = 3 * N_HEADS_B
    w = jnp.pad(w_in, ((0, 0), (0, LANES - n_gate))).astype(BF16)
    proj = _norm_mod_matmul(h, norm_gain, ada, 0, 1, w, seq, hd + LANES, "nsa_in_proj")
    bg = jnp.pad(b_gate.astype(F32), (0, LANES - n_gate)).reshape(1, LANES)
    q_plain, q_rot, gates = _nsa_q_prep(proj, cos_t, sin_t, q_gain, bg)

    tq = NSA_TQ
    qr = _group_rows(q_rot, batch, seq, tq)
    bgs = (batch, N_KV_GROUPS_B)
    o_cmp, negbias = _cmp_select(_group_rows(q_plain, batch, seq, CMP_TQ), k_cmp, v_cmp, seq, CMP_TQ, bgs)
    o_sel = _flash(qr, k_sel, v_sel, tq, NSA_TK, "nsa_selected", bias=negbias, grouped_out=bgs)
    o_win = _window_attention(qr, k_win, v_win, tq, WINDOW, bgs)
    return _nsa_out(o_cmp, o_sel, o_win, gates, w_o.astype(BF16), h, ada, 2, seq)


def kernel(x, c, positions, w_ada, b_ada, norm_mix, norm_ffn, w_in_a, b_f_a, qnorm_a, knorm_a, w_o_a, norm_kv, w_ada_kv, b_ada_kv, w_kv_b, pe_cmp_k, pe_cmp_v, w_cmp_k1, w_cmp_k2, w_cmp_v1, w_cmp_v2, knorm_b, w_in_b, b_gate_b, qnorm_b, w_o_b, w_router, b_router, w_exp_gate, w_exp_up, w_exp_down, w_sh_gate, w_sh_up, w_sh_down):
    batch, seq, d = x.shape
    depth = w_ada.shape[0]
    n_a = w_in_a.shape[0]
    c_pad = jnp.pad(c, ((0, SUBLANES - batch), (0, 0)))
    cos_t, sin_t = _rope_tables(positions)
    h = x.reshape(batch * seq, d)
    kv = None
    for layer in range(depth):
        ada = _ada(c_pad, w_ada[layer], b_ada[layer])
        if layer < n_a:
            h = _fox_layer(h, ada, norm_mix[layer], w_in_a[layer], b_f_a[layer], qnorm_a[layer],
                           knorm_a[layer], w_o_a[layer], batch, seq)
        else:
            if layer == n_a:
                ada_kv = _ada(c_pad, w_ada_kv, b_ada_kv)
                kv = _shared_kv(h, ada_kv, norm_kv, w_kv_b, pe_cmp_k, pe_cmp_v, w_cmp_k1, w_cmp_k2,
                                w_cmp_v1, w_cmp_v2, knorm_b, cos_t, sin_t, batch, seq)
            j = layer - n_a
            h = _nsa_layer(h, ada, norm_mix[layer], kv, cos_t, sin_t, w_in_b[j], b_gate_b[j], qnorm_b[j],
                           w_o_b[j], batch, seq)
        h = _moe_ffn(h, norm_ffn[layer], ada, 3, w_router[layer], b_router[layer], w_exp_gate[layer],
                     w_exp_up[layer], w_exp_down[layer], w_sh_gate[layer], w_sh_up[layer], w_sh_down[layer], seq)
    return h.reshape(batch, seq, d)
```

```python
import functools

import jax
import jax.numpy as jnp
from jax import lax
from jax.experimental import pallas as pl
from jax.experimental.pallas import tpu as pltpu

F32 = jnp.float32
BF16 = jnp.bfloat16
I32 = jnp.int32

HEAD_DIM = 64
N_HEADS_A = 16
N_HEADS_B = 16
N_KV_GROUPS_B = 4
HEADS_PER_GROUP = N_HEADS_B // N_KV_GROUPS_B
ROPE_THETA = 500000.0
ROPE_DIM = HEAD_DIM // 4
CMP_BLOCK = 64
N_SELECT = 16
WINDOW = 512
N_EXPERTS = 64
TOP_K = 8
ROUTED_SCALE = 2.5
EPS = 1e-6
FORCE_SCORE = 1e4
NEG = -1e30
LOG2E = 1.4426950408889634
QSCALE = HEAD_DIM ** -0.5 * LOG2E

LANES = 128
SUBLANES = 8
BF16_ROWS = 16
MXU_COLS = 256

ROW_TILE = 512
FOX_TQ = 1024
FOX_TK = 512
NSA_TQ = 256
NSA_TK = 512
CMP_TQ = 512
MOE_TOK_TILE = 256
ROUTE_TILES = 2
MOE_UNIT = BF16_ROWS
MOE_BLOCK = 256
VMEM_LIMIT = 56 * 1024 * 1024


def _cp(*sem):
    return pltpu.CompilerParams(dimension_semantics=sem, vmem_limit_bytes=VMEM_LIMIT)


def _dot(a, b):
    return jnp.dot(a, b, preferred_element_type=F32)


def _dot_nt(a, b):
    return lax.dot_general(a, b, (((1,), (1,)), ((), ())), preferred_element_type=F32)


def _dot_tn(a, b):
    return lax.dot_general(a, b, (((0,), (0,)), ((), ())), preferred_element_type=F32)


def _split2(x):
    hi = x.astype(BF16)
    lo = (x - hi.astype(F32)).astype(BF16)
    return hi, lo


def _split3(x):
    hi = x.astype(BF16)
    r = x - hi.astype(F32)
    mid = r.astype(BF16)
    lo = (r - mid.astype(F32)).astype(BF16)
    return hi, mid, lo


def _dot_f32(a, b):
    a_hi, a_lo = _split2(a)
    b_hi, b_lo = _split2(b)
    return _dot(a_hi, b_hi) + _dot(a_lo, b_hi) + _dot(a_hi, b_lo)


def _sigmoid(x):
    return 1.0 / (1.0 + jnp.exp(-x))


def _silu(x):
    return x * _sigmoid(x)


def _ada_kernel(c_ref, w_ref, b_ref, o_ref):
    o_ref[...] = _dot_f32(_silu(c_ref[...]), w_ref[...]) + b_ref[...]


def _ada(c_pad, w, b):
    d, n = w.shape
    tn = 1024
    return pl.pallas_call(
        _ada_kernel,
        grid=(n // tn,),
        in_specs=[pl.BlockSpec((SUBLANES, d), lambda j: (0, 0)),
                  pl.BlockSpec((d, tn), lambda j: (0, j)),
                  pl.BlockSpec((1, tn), lambda j: (0, j))],
        out_specs=pl.BlockSpec((SUBLANES, tn), lambda j: (0, j)),
        out_shape=jax.ShapeDtypeStruct((SUBLANES, n), F32),
        compiler_params=_cp("parallel"),
        name="ada",
    )(c_pad, w, b.reshape(1, n))


def _norm_mod(x, gain, shift, scale):
    xn = x * lax.rsqrt(jnp.mean(x * x, axis=-1, keepdims=True) + EPS) * gain
    return xn * (1.0 + scale) + shift


def _nm_kernel(h_ref, gain_ref, shift_ref, scale_ref, w_ref, o_ref, u_scr, *, tiles_per_batch):
    i = pl.program_id(0)

    @pl.when(pl.program_id(1) == 0)
    def _():
        b = i // tiles_per_batch
        u = _norm_mod(h_ref[...], gain_ref[...], shift_ref[pl.ds(b, 1), :], scale_ref[pl.ds(b, 1), :])
        u_scr[...] = u.astype(BF16)

    o_ref[...] = _dot(u_scr[...], w_ref[...]).astype(o_ref.dtype)


def _norm_mod_matmul(h, gain, ada, shift_blk, scale_blk, w, seq, tn, name):
    t, d = h.shape
    n = w.shape[1]
    tm = ROW_TILE
    return pl.pallas_call(
        functools.partial(_nm_kernel, tiles_per_batch=seq // tm),
        grid=(t // tm, n // tn),
        in_specs=[pl.BlockSpec((tm, d), lambda i, j: (i, 0)),
                  pl.BlockSpec((1, d), lambda i, j: (0, 0)),
                  pl.BlockSpec((SUBLANES, d), lambda i, j: (0, shift_blk)),
                  pl.BlockSpec((SUBLANES, d), lambda i, j: (0, scale_blk)),
                  pl.BlockSpec((d, tn), lambda i, j: (0, j))],
        out_specs=pl.BlockSpec((tm, tn), lambda i, j: (i, j)),
        out_shape=jax.ShapeDtypeStruct((t, n), F32),
        scratch_shapes=[pltpu.VMEM((tm, d), BF16)],
        compiler_params=_cp("parallel", "arbitrary"),
        name=name,
    )(h, gain.reshape(1, d), ada, ada, w)


def _logsig_cumsum_kernel(z_ref, b_ref, tri_ref, o_ref, carry):
    @pl.when(pl.program_id(0) == 0)
    def _():
        carry[...] = jnp.zeros_like(carry)

    z = z_ref[...] + b_ref[...]
    lf = jnp.minimum(z, 0.0) - jnp.log1p(jnp.exp(-jnp.abs(z)))
    hi, mid, lo = _split3(lf)
    tri = tri_ref[...]
    c = _dot(tri, hi) + _dot(tri, mid) + _dot(tri, lo) + carry[...]
    o_ref[...] = c
    ts = c.shape[0]
    carry[...] = c[ts - 1:ts, :]


def _logsig_cumsum(z, bias):
    s, c = z.shape
    ts = 256
    tri = (jnp.arange(ts)[:, None] >= jnp.arange(ts)[None, :]).astype(BF16)
    return pl.pallas_call(
        _logsig_cumsum_kernel,
        grid=(s // ts,),
        in_specs=[pl.BlockSpec((ts, c), lambda i: (i, 0)),
                  pl.BlockSpec((1, c), lambda i: (0, 0)),
                  pl.BlockSpec((ts, ts), lambda i: (0, 0))],
        out_specs=pl.BlockSpec((ts, c), lambda i: (i, 0)),
        out_shape=jax.ShapeDtypeStruct((s, c), F32),
        scratch_shapes=[pltpu.VMEM((1, c), F32)],
        compiler_params=_cp("arbitrary"),
        name="fox_logsig_cumsum",
    )(z, bias, tri)


def _head_rms(x, gain):
    return x * lax.rsqrt(jnp.mean(x * x, axis=-1, keepdims=True) + EPS) * gain


def _fox_prep_kernel(q_ref, k_ref, v_ref, cum_ref, qg_ref, kg_ref, qo_ref, ko_ref, vo_ref):
    bh0 = (pl.program_id(0) * (N_HEADS_A // 2) + pl.program_id(1)) * 2
    ts = q_ref.shape[0]
    cum = cum_ref[...]
    lane_c = lax.broadcasted_iota(I32, cum.shape, 1)
    lane = lax.broadcasted_iota(I32, (ts, HEAD_DIM), 1)
    for h2 in range(2):
        sl = slice(h2 * HEAD_DIM, (h2 + 1) * HEAD_DIM)
        c = jnp.sum(jnp.where(lane_c == bh0 + h2, cum, 0.0), axis=-1, keepdims=True) * LOG2E
        hi, mid, lo = (p.astype(F32) for p in _split3(c))
        qn = _head_rms(q_ref[:, sl], qg_ref[...]) * QSCALE
        kn = _head_rms(k_ref[:, sl], kg_ref[...])
        ext_q = jnp.where(lane == 0, hi, jnp.where(lane == 1, mid, jnp.where(lane == 2, lo,
                          jnp.where(lane < 6, 1.0, 0.0))))
        ext_k = jnp.where(lane < 3, 1.0, jnp.where(lane == 3, -hi, jnp.where(lane == 4, -mid,
                          jnp.where(lane == 5, -lo, 0.0))))
        ext_v = jnp.where(lane == 0, 1.0, 0.0)
        qo_ref[0, h2] = jnp.concatenate([qn, ext_q], axis=1).astype(BF16)
        ko_ref[0, h2] = jnp.concatenate([kn, ext_k], axis=1).T.astype(BF16)
        vo_ref[0, h2] = jnp.concatenate([v_ref[:, sl], ext_v], axis=1).astype(BF16)


def _fox_prep(proj, cum, q_gain, k_gain, batch, seq):
    ts = ROW_TILE
    nsb = seq // ts
    hp = N_HEADS_A // 2
    out = jax.ShapeDtypeStruct((batch, N_HEADS_A, seq, LANES), BF16)
    out_t = jax.ShapeDtypeStruct((batch, N_HEADS_A, LANES, seq), BF16)
    ospec = pl.BlockSpec((1, 2, ts, LANES), lambda b, p, i: (b, p, i, 0))
    ospec_t = pl.BlockSpec((1, 2, LANES, ts), lambda b, p, i: (b, p, 0, i))
    return pl.pallas_call(
        _fox_prep_kernel,
        grid=(batch, hp, nsb),
        in_specs=[pl.BlockSpec((ts, LANES), lambda b, p, i: (b * nsb + i, p)),
                  pl.BlockSpec((ts, LANES), lambda b, p, i: (b * nsb + i, hp + p)),
                  pl.BlockSpec((ts, LANES), lambda b, p, i: (b * nsb + i, 2 * hp + p)),
                  pl.BlockSpec((ts, cum.shape[1]), lambda b, p, i: (i, 0)),
                  pl.BlockSpec((1, HEAD_DIM), lambda b, p, i: (0, 0)),
                  pl.BlockSpec((1, HEAD_DIM), lambda b, p, i: (0, 0))],
        out_specs=[ospec, ospec_t, ospec],
        out_shape=[out, out_t, out],
        compiler_params=_cp("parallel", "parallel", "parallel"),
        name="fox_prep",
    )(proj, proj, proj, cum, q_gain.reshape(1, HEAD_DIM), k_gain.reshape(1, HEAD_DIM))


def _store_heads(o_ref, o, tq, grouped):
    if not grouped:
        o_ref[0, 0] = o.astype(o_ref.dtype)
        return
    for hh in range(o.shape[0] // tq):
        o_ref[:, hh * HEAD_DIM:(hh + 1) * HEAD_DIM] = o[hh * tq:(hh + 1) * tq].astype(o_ref.dtype)


def _flash_kernel(*refs, tq, tk, with_bias, grouped):
    if with_bias:
        q_ref, nb_ref, k_ref, v_ref, o_ref, m_scr, acc_scr = refs
    else:
        q_ref, k_ref, v_ref, o_ref, m_scr, acc_scr = refs
    i = pl.program_id(1)
    q = q_ref[0, 0]
    rows = q.shape[0]
    if with_bias:
        q = jnp.concatenate([jnp.tile(nb_ref[0], (rows // tq, 1)), q], axis=1)
    m_scr[...] = jnp.full(m_scr.shape, NEG, F32)
    acc_scr[...] = jnp.zeros(acc_scr.shape, F32)
    q_pos = i * tq + jnp.bitwise_and(lax.broadcasted_iota(I32, (rows, 1), 0), tq - 1)

    def step(j, masked, row_lo=0):
        ks = pl.multiple_of(j * tk, tk)
        s = _dot(q[row_lo:], k_ref[0, :, pl.ds(ks, tk)])
        if masked:
            k_pos = ks + lax.broadcasted_iota(I32, (1, tk), 1)
            s = jnp.where(k_pos <= q_pos[row_lo:], s, NEG)
        m_old = m_scr[row_lo:, :]
        m_new = jnp.maximum(m_old, jnp.max(s, axis=-1, keepdims=True))
        p = jnp.exp2(s - jnp.tile(m_new, (1, tk // LANES))).astype(BF16)
        acc_scr[row_lo:, :] = jnp.exp2(m_old - m_new) * acc_scr[row_lo:, :] + _dot(p, v_ref[0, pl.ds(ks, tk), :])
        m_scr[row_lo:, :] = m_new

    n_clear = (i * tq + 1) // tk
    n_diag = max(tq // tk, 1)

    def pair(jj, c):
        step(2 * jj, False)
        step(2 * jj + 1, False)
        return c

    lax.fori_loop(0, n_clear // 2, pair, 0)

    @pl.when(lax.rem(n_clear, 2) == 1)
    def _():
        step(n_clear - 1, False)

    for d in range(n_diag):
        step(n_clear + d, True, row_lo=d * tk if rows == tq else 0)

    acc = acc_scr[...]
    _store_heads(o_ref, acc[:, :HEAD_DIM] / acc[:, HEAD_DIM:HEAD_DIM + 1], tq, grouped)


def _flash(q, k, v, tq, tk, name, bias=None, grouped_out=None):
    bh, nq, rows, kd = q.shape
    s = k.shape[2]
    assert tq % tk == 0 or tk % tq == 0
    in_specs = [pl.BlockSpec((1, 1, rows, kd), lambda b, i: (b, i, 0, 0))]
    args = [q]
    if bias is not None:
        in_specs.append(pl.BlockSpec((1, tq, bias.shape[2]), lambda b, i: (b, i, 0)))
        args.append(bias)
    in_specs += [pl.BlockSpec((1, k.shape[1], s), lambda b, i: (b, 0, 0)),
                 pl.BlockSpec((1, s, LANES), lambda b, i: (b, 0, 0))]
    if grouped_out is None:
        out_spec = pl.BlockSpec((1, 1, rows, HEAD_DIM), lambda b, i: (b, i, 0, 0))
        out_shape = jax.ShapeDtypeStruct((bh, nq, rows, HEAD_DIM), BF16)
    else:
        out_spec, out_shape = _grouped_out(grouped_out, nq, tq, rows)
    return pl.pallas_call(
        functools.partial(_flash_kernel, tq=tq, tk=tk, with_bias=bias is not None, grouped=grouped_out is not None),
        grid=(bh, nq),
        in_specs=in_specs,
        out_specs=out_spec,
        out_shape=out_shape,
        scratch_shapes=[pltpu.VMEM((rows, LANES), F32), pltpu.VMEM((rows, LANES), F32)],
        compiler_params=_cp("parallel", "arbitrary"),
        name=name,
    )(*args, k, v)


def _grouped_out(batch_groups, nq, tq, rows):
    batch, groups = batch_groups
    width = rows // tq * HEAD_DIM
    spec = pl.BlockSpec((tq, width), lambda b, i: ((b // groups) * nq + i, lax.rem(b, groups)))
    return spec, jax.ShapeDtypeStruct((batch * nq * tq, groups * width), BF16)


def _window_kernel(q_ref, k_ref, v_ref, o_ref, *, tq, window):
    i = pl.program_id(1)
    q = q_ref[0, 0]
    rows = q.shape[0]
    span = window + tq
    ks = pl.multiple_of(jnp.maximum(i * tq - window, 0), tq)
    q_pos = i * tq + jnp.bitwise_and(lax.broadcasted_iota(I32, (rows, 1), 0), tq - 1)
    k_pos = ks + lax.broadcasted_iota(I32, (1, span), 1)
    ok = (q_pos - k_pos).astype(jnp.uint32) < window
    s = jnp.where(ok, _dot(q, k_ref[0, :, pl.ds(ks, span)]), NEG)
    p = jnp.exp2(s - jnp.max(s, axis=-1, keepdims=True)).astype(BF16)
    acc = _dot(p, v_ref[0, pl.ds(ks, span), :])
    _store_heads(o_ref, acc[:, :HEAD_DIM] / acc[:, HEAD_DIM:HEAD_DIM + 1], tq, True)


def _window_attention(q, k, v, tq, window, batch_groups):
    bh, nq, rows, kd = q.shape
    s = k.shape[2]
    assert window % tq == 0 and window + tq <= s
    out_spec, out_shape = _grouped_out(batch_groups, nq, tq, rows)
    return pl.pallas_call(
        functools.partial(_window_kernel, tq=tq, window=window),
        grid=(bh, nq),
        in_specs=[pl.BlockSpec((1, 1, rows, kd), lambda b, i: (b, i, 0, 0)),
                  pl.BlockSpec((1, kd, s), lambda b, i: (b, 0, 0)),
                  pl.BlockSpec((1, s, LANES), lambda b, i: (b, 0, 0))],
        out_specs=out_spec,
        out_shape=out_shape,
        compiler_params=_cp("parallel", "parallel"),
        name="nsa_window",
    )(q, k, v)


def _oproj_kernel(o_ref, w_ref, h_ref, g_ref, out_ref, *, tiles_per_batch):
    b = pl.program_id(0) // tiles_per_batch
    out_ref[...] = h_ref[...] + g_ref[pl.ds(b, 1), :] * _dot(o_ref[...], w_ref[...])


def _oproj_residual(o, w, h, ada, gate_blk, seq):
    t, d = h.shape
    tm = ROW_TILE
    return pl.pallas_call(
        functools.partial(_oproj_kernel, tiles_per_batch=seq // tm),
        grid=(t // tm,),
        in_specs=[pl.BlockSpec((tm, o.shape[1]), lambda i: (i, 0)),
                  pl.BlockSpec(w.shape, lambda i: (0, 0)),
                  pl.BlockSpec((tm, d), lambda i: (i, 0)),
                  pl.BlockSpec((SUBLANES, d), lambda i: (0, gate_blk))],
        out_specs=pl.BlockSpec((tm, d), lambda i: (i, 0)),
        out_shape=jax.ShapeDtypeStruct((t, d), F32),
        compiler_params=_cp("parallel"),
        name="oproj_residual",
    )(o, w, h, ada)


def _seg_rms(x, seg_ones, gain_tiled):
    ss = _dot((x * x).astype(BF16), seg_ones)
    return x * lax.rsqrt(ss * (1.0 / HEAD_DIM) + EPS) * gain_tiled


def _rope(x, cos_t, sin_t):
    width = x.shape[1]
    reps = width // LANES
    lane = jnp.bitwise_and(lax.broadcasted_iota(I32, x.shape, 1), HEAD_DIM - 1)
    half = ROPE_DIM // 2
    partner = jnp.where(lane < half, pltpu.roll(x, width - half, 1), pltpu.roll(x, half, 1))
    return x * jnp.tile(cos_t, (1, reps)) + partner * jnp.tile(sin_t, (1, reps))


def _nsa_q_kernel(p_ref, z_ref, cos_ref, sin_ref, gain_ref, seg_ref, bg_ref, qp_ref, qr_ref, g_ref):
    q = _seg_rms(p_ref[...], seg_ref[...], gain_ref[...]) * QSCALE
    qp_ref[...] = q.astype(BF16)
    qr_ref[...] = _rope(q, cos_ref[...], sin_ref[...]).astype(BF16)
    g_ref[...] = _sigmoid(z_ref[...] + bg_ref[...])


def _nsa_q_prep(proj, cos_t, sin_t, q_gain, b_gate_pad):
    t = proj.shape[0]
    hd = N_HEADS_B * HEAD_DIM
    tm = ROW_TILE
    seg = (jnp.arange(hd)[:, None] // HEAD_DIM == jnp.arange(hd)[None, :] // HEAD_DIM).astype(BF16)
    row = lambda i: (i, 0)
    fix = lambda i: (0, 0)
    return pl.pallas_call(
        _nsa_q_kernel,
        grid=(t // tm,),
        in_specs=[pl.BlockSpec((tm, hd), row),
                  pl.BlockSpec((tm, LANES), lambda i: (i, hd // LANES)),
                  pl.BlockSpec((tm, LANES), row), pl.BlockSpec((tm, LANES), row),
                  pl.BlockSpec((1, hd), fix), pl.BlockSpec((hd, hd), fix), pl.BlockSpec((1, LANES), fix)],
        out_specs=[pl.BlockSpec((tm, hd), row), pl.BlockSpec((tm, hd), row), pl.BlockSpec((tm, LANES), row)],
        out_shape=[jax.ShapeDtypeStruct((t, hd), BF16), jax.ShapeDtypeStruct((t, hd), BF16),
                   jax.ShapeDtypeStruct((t, LANES), F32)],
        compiler_params=_cp("parallel"),
        name="nsa_q_prep",
    )(proj, proj, cos_t, sin_t, jnp.tile(q_gain, N_HEADS_B).reshape(1, hd), seg, b_gate_pad)


def _nsa_kv_kernel(kv_ref, cos_ref, sin_ref, g1_ref, g2_ref, seg_ref, ks_ref, vs_ref, kw_ref, vw_ref):
    gw = N_KV_GROUPS_B * HEAD_DIM
    ks = _seg_rms(kv_ref[:, 2 * gw:3 * gw], seg_ref[...], g1_ref[...])
    kw = _seg_rms(kv_ref[:, 4 * gw:5 * gw], seg_ref[...], g2_ref[...])
    ks_ref[...] = _rope(ks, cos_ref[...], sin_ref[...]).astype(BF16)
    kw_ref[...] = _rope(kw, cos_ref[...], sin_ref[...]).astype(BF16)
    vs_ref[...] = kv_ref[:, 3 * gw:4 * gw].astype(BF16)
    vw_ref[...] = kv_ref[:, 5 * gw:6 * gw].astype(BF16)


def _nsa_kv_prep(kv, cos_t, sin_t, k_gains):
    t, n = kv.shape
    gw = N_KV_GROUPS_B * HEAD_DIM
    tm = ROW_TILE
    seg = (jnp.arange(gw)[:, None] // HEAD_DIM == jnp.arange(gw)[None, :] // HEAD_DIM).astype(BF16)
    row = lambda i: (i, 0)
    fix = lambda i: (0, 0)
    out = jax.ShapeDtypeStruct((t, gw), BF16)
    return pl.pallas_call(
        _nsa_kv_kernel,
        grid=(t // tm,),
        in_specs=[pl.BlockSpec((tm, n), row), pl.BlockSpec((tm, LANES), row), pl.BlockSpec((tm, LANES), row),
                  pl.BlockSpec((1, gw), fix), pl.BlockSpec((1, gw), fix), pl.BlockSpec((gw, gw), fix)],
        out_specs=[pl.BlockSpec((tm, gw), row)] * 4,
        out_shape=[out] * 4,
        compiler_params=_cp("parallel"),
        name="nsa_kv_prep",
    )(kv, cos_t, sin_t, jnp.tile(k_gains[1], N_KV_GROUPS_B).reshape(1, gw),
      jnp.tile(k_gains[2], N_KV_GROUPS_B).reshape(1, gw), seg)


def _compress_kernel(x_ref, pe_ref, w1_ref, w2_ref, g_ref, o_ref, *, norm):
    x = (x_ref[...] + pe_ref[...]).astype(BF16)
    y = _dot(_silu(_dot(x, w1_ref[...])).astype(BF16), w2_ref[...])
    if norm:
        y = _head_rms(y, g_ref[...])
    o_ref[...] = y.astype(o_ref.dtype)


def _compress(x, pe_flat, w1, w2, gain, norm, name):
    r, kdim = x.shape
    tm = min(ROW_TILE, r)
    fix = lambda i: (0, 0)
    return pl.pallas_call(
        functools.partial(_compress_kernel, norm=norm),
        grid=(r // tm,),
        in_specs=[pl.BlockSpec((tm, kdim), lambda i: (i, 0)), pl.BlockSpec((1, kdim), fix),
                  pl.BlockSpec(w1.shape, fix), pl.BlockSpec(w2.shape, fix), pl.BlockSpec((1, HEAD_DIM), fix)],
        out_specs=pl.BlockSpec((tm, HEAD_DIM), lambda i: (i, 0)),
        out_shape=jax.ShapeDtypeStruct((r, HEAD_DIM), BF16),
        compiler_params=_cp("parallel"),
        name=name,
    )(x, pe_flat, w1, w2, gain.reshape(1, HEAD_DIM))


def _cmp_select_kernel(q_ref, k_ref, v_ref, o_ref, nb_ref, *, tq, n_sel):
    i = pl.program_id(1)
    q = q_ref[0, 0]
    rows = q.shape[0]
    nblk = k_ref.shape[1]
    logits = _dot_nt(q, k_ref[0])
    t_pos = i * tq + jnp.bitwise_and(lax.broadcasted_iota(I32, (rows, 1), 0), tq - 1)
    blk = lax.broadcasted_iota(I32, (1, nblk), 1)
    vis = (blk + 1) * CMP_BLOCK - 1 <= t_pos
    lm = jnp.where(vis, logits, NEG)
    e = jnp.exp2(lm - jnp.max(lm, axis=-1, keepdims=True))
    p = jnp.where(vis, e / jnp.sum(e, axis=-1, keepdims=True), 0.0)
    _store_heads(o_ref, _dot(p.astype(BF16), v_ref[0])[:, :HEAD_DIM], tq, True)

    imp = p[0:tq]
    for hh in range(1, rows // tq):
        imp = imp + p[hh * tq:(hh + 1) * tq]
    cur = lax.shift_right_logical(t_pos[0:tq], CMP_BLOCK.bit_length() - 1)
    forced = jnp.logical_or(blk == 0, jnp.logical_or(blk == cur, blk == cur - 1))
    cand = jnp.where(jnp.logical_or(forced, blk > cur), -jnp.inf, imp)
    sel = jnp.where(forced, 1.0, 0.0)
    blk_f = blk.astype(F32)
    for _ in range(max(n_sel - 3, 0)):
        best = jnp.max(cand, axis=-1, keepdims=True)
        first = jnp.min(jnp.where(cand == best, blk_f, float(nblk)), axis=-1, keepdims=True)
        hit = blk_f == first
        sel = jnp.where(jnp.logical_and(hit, best > -jnp.inf), 1.0, sel)
        cand = jnp.where(hit, -jnp.inf, cand)
    nb_ref[0] = jnp.where(sel > 0.5, 0.0, NEG).astype(nb_ref.dtype)


def _cmp_select(q, k_cmp, v_cmp, seq, tq, batch_groups):
    bg, nq, rows, _ = q.shape
    nblk = k_cmp.shape[1]
    assert min(N_SELECT, nblk) >= 3
    out_spec, out_shape = _grouped_out(batch_groups, nq, tq, rows)
    return pl.pallas_call(
        functools.partial(_cmp_select_kernel, tq=tq, n_sel=min(N_SELECT, nblk)),
        grid=(bg, nq),
        in_specs=[pl.BlockSpec((1, 1, rows, HEAD_DIM), lambda b, i: (b, i, 0, 0)),
                  pl.BlockSpec((1, nblk, HEAD_DIM), lambda b, i: (b, 0, 0)),
                  pl.BlockSpec((1, nblk, LANES), lambda b, i: (b, 0, 0))],
        out_specs=[out_spec, pl.BlockSpec((1, tq, nblk), lambda b, i: (b, i, 0))],
        out_shape=[out_shape, jax.ShapeDtypeStruct((bg, seq, nblk), BF16)],
        compiler_params=_cp("parallel", "parallel"),
        name="nsa_cmp_select",
    )(q, k_cmp, v_cmp)


def _nsa_out_kernel(oc_ref, os_ref, ow_ref, g_ref, e_ref, w_ref, h_ref, ada_ref, out_ref, *, tiles_per_batch):
    b = pl.program_id(0) // tiles_per_batch
    g_hi, g_lo = _split2(g_ref[...])
    o = None
    for j, o_ref in enumerate((oc_ref, os_ref, ow_ref)):
        gate = _dot(g_hi, e_ref[j]) + _dot(g_lo, e_ref[j])
        term = gate * o_ref[...].astype(F32)
        o = term if o is None else o + term
    out_ref[...] = h_ref[...] + ada_ref[pl.ds(b, 1), :] * _dot(o.astype(BF16), w_ref[...])


def _nsa_out(o_cmp, o_sel, o_win, gates, w_o, h, ada, gate_blk, seq):
    t, d = h.shape
    hd = o_cmp.shape[1]
    tm = ROW_TILE
    col = jnp.arange(LANES)[:, None]
    lane = jnp.arange(hd)[None, :]
    expand = jnp.stack([(col == (lane // HEAD_DIM) * 3 + j) for j in range(3)]).astype(BF16)
    row = lambda i: (i, 0)
    return pl.pallas_call(
        functools.partial(_nsa_out_kernel, tiles_per_batch=seq // tm),
        grid=(t // tm,),
        in_specs=[pl.BlockSpec((tm, hd), row), pl.BlockSpec((tm, hd), row), pl.BlockSpec((tm, hd), row),
                  pl.BlockSpec((tm, LANES), row),
                  pl.BlockSpec((3, LANES, hd), lambda i: (0, 0, 0)),
                  pl.BlockSpec(w_o.shape, lambda i: (0, 0)),
                  pl.BlockSpec((tm, d), row),
                  pl.BlockSpec((SUBLANES, d), lambda i: (0, gate_blk))],
        out_specs=pl.BlockSpec((tm, d), row),
        out_shape=jax.ShapeDtypeStruct((t, d), F32),
        compiler_params=_cp("parallel"),
        name="nsa_out",
    )(o_cmp, o_sel, o_win, gates, expand, w_o, h, ada)


def _route_kernel(h_ref, gain_ref, shift_ref, scale_ref, wr_ref, br_ref, tri_ref, upper_ref,
                  u_ref, pos_ref, w_ref, nu_ref, off_ref, *, tiles_per_batch):
    b = (pl.program_id(0) * ROUTE_TILES) // tiles_per_batch
    shift, scale = shift_ref[pl.ds(b, 1), :], scale_ref[pl.ds(b, 1), :]
    for tile in range(ROUTE_TILES):
        rows = slice(tile * MOE_TOK_TILE, (tile + 1) * MOE_TOK_TILE)
        _route_tile(h_ref[rows, :], gain_ref[...], shift, scale, wr_ref, br_ref, tri_ref, upper_ref,
                    u_ref.at[rows, :], pos_ref.at[rows, :], w_ref.at[rows, :], nu_ref.at[tile], off_ref.at[tile])


def _route_tile(h, gain, shift, scale, wr_ref, br_ref, tri_ref, upper_ref, u_ref, pos_ref, w_ref, nu_ref, off_ref):
    u = _norm_mod(h, gain, shift, scale)
    u_ref[...] = u.astype(BF16)
    scores = _sigmoid(_dot_f32(u, wr_ref[...]))
    lane = lax.broadcasted_iota(I32, scores.shape, 1)
    lane_f = lane.astype(F32)
    biased = jnp.where(lane < N_EXPERTS, scores + br_ref[...], -jnp.inf)
    chosen = jnp.zeros(scores.shape, F32)
    hits, val = [], []
    upper = upper_ref[...]
    for _ in range(TOP_K):
        best = jnp.max(biased, axis=-1, keepdims=True)
        first = jnp.min(jnp.where(biased == best, lane_f, float(LANES)), axis=-1, keepdims=True)
        hit = lane_f == first
        hits.append(hit)
        val.append(jnp.sum(jnp.where(hit, scores, 0.0), axis=-1, keepdims=True))
        biased = jnp.where(hit, -jnp.inf, biased)
        chosen = jnp.where(hit, 1.0, chosen)
    top_s = jnp.concatenate(val, axis=1)
    w_ref[...] = top_s / jnp.sum(top_s, axis=-1, keepdims=True) * ROUTED_SCALE

    before = _dot(tri_ref[...], chosen.astype(BF16))
    count = jnp.sum(chosen, axis=0, keepdims=True)
    units = jnp.floor((count + (MOE_UNIT - 1)) * (1.0 / MOE_UNIT))
    padded = jnp.broadcast_to(units * MOE_UNIT, (SUBLANES, LANES))
    offset = _dot(padded.astype(BF16), upper)[0:1]
    row_of = before + offset
    pos = [jnp.sum(jnp.where(hit, row_of, 0.0), axis=-1, keepdims=True) for hit in hits]
    pos_ref[...] = jnp.concatenate(pos, axis=1).astype(I32)
    nu_ref[...] = units.astype(I32)
    off_ref[...] = offset.astype(I32)


def _route(h, gain, ada, shift_blk, scale_blk, w_router_pad, b_router_pad, seq):
    t, d = h.shape
    tt = MOE_TOK_TILE
    n_tiles = t // tt
    rt = ROUTE_TILES * tt
    assert n_tiles % ROUTE_TILES == 0 and seq % rt == 0
    row = lambda i: (i, 0)
    fix = lambda i: (0, 0)
    tri = (jnp.arange(tt)[:, None] > jnp.arange(tt)[None, :]).astype(BF16)
    upper = (jnp.arange(LANES)[:, None] < jnp.arange(LANES)[None, :]).astype(BF16)
    per_tile = jax.ShapeDtypeStruct((n_tiles, 1, LANES), I32)
    per_tile_spec = pl.BlockSpec((ROUTE_TILES, 1, LANES), lambda i: (i, 0, 0))
    return pl.pallas_call(
        functools.partial(_route_kernel, tiles_per_batch=seq // tt),
        grid=(n_tiles // ROUTE_TILES,),
        in_specs=[pl.BlockSpec((rt, d), row), pl.BlockSpec((1, d), fix),
                  pl.BlockSpec((SUBLANES, d), lambda i: (0, shift_blk)),
                  pl.BlockSpec((SUBLANES, d), lambda i: (0, scale_blk)),
                  pl.BlockSpec((d, LANES), fix), pl.BlockSpec((1, LANES), fix),
                  pl.BlockSpec((tt, tt), fix), pl.BlockSpec((LANES, LANES), fix)],
        out_specs=[pl.BlockSpec((rt, d), row), pl.BlockSpec((rt, TOP_K), row), pl.BlockSpec((rt, TOP_K), row),
                   per_tile_spec, per_tile_spec],
        out_shape=[jax.ShapeDtypeStruct((t, d), BF16), jax.ShapeDtypeStruct((t, TOP_K), I32),
                   jax.ShapeDtypeStruct((t, TOP_K), F32), per_tile, per_tile],
        compiler_params=_cp("parallel"),
        name="moe_route",
    )(h, gain.reshape(1, d), ada, ada, w_router_pad, b_router_pad, tri, upper)


PLAN_SLOTS = 256
PLAN_TILES = LANES


def _plan_kernel(nu_ref, off_ref, upper_ref, lower_ref, row_ref, exp_ref, nused_ref,
                 *, n_tiles, r_loc, upb, n_unit_slots, n_dump_units):
    nu = nu_ref[...]
    nu_t = nu.T
    off_t = off_ref[...].T
    upper = upper_ref[...]
    lower = lower_ref[...]
    lane = lax.broadcasted_iota(I32, (PLAN_SLOTS, LANES), 1)
    lane_f = lane.astype(F32)

    def exact_dot(a, b_bf16, left):
        parts = _split3(a)
        return sum(_dot(b_bf16, p) if left else _dot(p, b_bf16) for p in parts)

    units_before_t = _dot(nu_t.astype(BF16), upper)
    count_col = jnp.sum(nu_t, axis=1, keepdims=True)
    count_row = jnp.sum(nu, axis=0, keepdims=True)
    padded_col = jnp.floor((count_col + (upb - 1)) * (1.0 / upb)) * upb
    padded_row = jnp.floor((count_row + (upb - 1)) * (1.0 / upb)) * upb
    start_col = exact_dot(jnp.broadcast_to(padded_col, (LANES, LANES)), lower, True)
    real_col = exact_dot(jnp.broadcast_to(count_col, (LANES, LANES)), lower, True)
    start_row = exact_dot(jnp.broadcast_to(padded_row, (SUBLANES, LANES)), upper, False)[0:1]
    seg_start_t = start_col + units_before_t
    seg_real_t = real_col + units_before_t
    tile_id = lax.broadcasted_iota(I32, (LANES, LANES), 1).astype(F32)
    seg_row_t = tile_id * r_loc + off_t

    idx = pl.program_id(0) * PLAN_SLOTS + lax.broadcasted_iota(I32, (PLAN_SLOTS, 1), 0)
    slot = jnp.where(idx < n_unit_slots, idx, (idx - n_unit_slots) * upb).astype(F32)

    in_e = jnp.logical_and(start_row <= slot, lane < N_EXPERTS)
    expert = jnp.sum(jnp.where(in_e, 1.0, 0.0), axis=-1, keepdims=True) - 1.0
    pick_e = (lane_f == expert).astype(BF16)
    look = lambda tbl: exact_dot(tbl, pick_e, True)
    starts = look(seg_start_t)
    in_t = jnp.logical_and(starts <= slot, lane < n_tiles)
    tile = jnp.sum(jnp.where(in_t, 1.0, 0.0), axis=-1, keepdims=True) - 1.0
    at = lambda x: jnp.sum(jnp.where(lane_f == tile, x, 0.0), axis=-1, keepdims=True)
    m = slot - at(starts)
    n = at(look(nu_t))
    row = at(look(seg_row_t)) + m * MOE_UNIT
    pad_idx = slot - (at(look(seg_real_t)) + jnp.minimum(m, n))
    dump = n_tiles * r_loc + jnp.minimum(pad_idx, n_dump_units - 1.0) * MOE_UNIT
    row_ref[...] = jnp.where(m < n, row, dump).astype(I32)
    exp_ref[...] = expert.astype(I32)
    total = jnp.sum(padded_row, axis=-1, keepdims=True) * (1.0 / upb)
    nused_ref[...] = jnp.broadcast_to(total, (SUBLANES, LANES)).astype(I32)


def _plan(nu, off, n_tiles, r_loc, n_blocks, n_dump_units):
    assert n_tiles <= PLAN_TILES
    upb = MOE_BLOCK // MOE_UNIT
    n_unit_slots = -(-(n_blocks * upb) // PLAN_SLOTS) * PLAN_SLOTS
    n_steps = n_unit_slots // PLAN_SLOTS + -(-(n_blocks + 2) // PLAN_SLOTS)
    pad = lambda x: jnp.pad(x.astype(F32), ((0, PLAN_TILES - n_tiles), (0, 0)))
    upper = (jnp.arange(LANES)[:, None] < jnp.arange(LANES)[None, :]).astype(BF16)
    fix = lambda s: (0, 0)
    col = pl.BlockSpec((PLAN_SLOTS, 1), lambda s: (s, 0))
    rows, experts, n_used = pl.pallas_call(
        functools.partial(_plan_kernel, n_tiles=n_tiles, r_loc=r_loc, upb=upb, n_unit_slots=n_unit_slots,
                          n_dump_units=n_dump_units),
        grid=(n_steps,),
        in_specs=[pl.BlockSpec((PLAN_TILES, LANES), fix), pl.BlockSpec((PLAN_TILES, LANES), fix),
                  pl.BlockSpec((LANES, LANES), fix), pl.BlockSpec((LANES, LANES), fix)],
        out_specs=[col, col, pl.BlockSpec((SUBLANES, LANES), fix)],
        out_shape=[jax.ShapeDtypeStruct((n_steps * PLAN_SLOTS, 1), I32),
                   jax.ShapeDtypeStruct((n_steps * PLAN_SLOTS, 1), I32),
                   jax.ShapeDtypeStruct((SUBLANES, LANES), I32)],
        compiler_params=_cp("arbitrary"),
        name="moe_plan",
    )(pad(nu), pad(off), upper, upper.T)
    return (rows[:n_blocks * upb, 0], experts[n_unit_slots:n_unit_slots + n_blocks + 2, 0], n_used[0, :1])


def _one_hot_t(pos_ref, weight_ref, n_rows):
    tt = pos_ref.shape[0]
    col = lax.broadcasted_iota(I32, (tt, n_rows), 1)
    acc = jnp.zeros((tt, n_rows), F32)
    for k in range(TOP_K):
        acc = jnp.where(col == pos_ref[:, k:k + 1], 1.0 if weight_ref is None else weight_ref[:, k:k + 1], acc)
    return acc.astype(BF16)


def _dispatch_kernel(u_ref, pos_ref, x_ref):
    x_ref[0] = _dot_tn(_one_hot_t(pos_ref, None, x_ref.shape[1]), u_ref[...]).astype(x_ref.dtype)


def _dispatch(u, lpos, n_tiles_total, r_loc):
    t, d = u.shape
    tt = MOE_TOK_TILE
    n_real = t // tt
    return pl.pallas_call(
        _dispatch_kernel,
        grid=(n_tiles_total,),
        in_specs=[pl.BlockSpec((tt, d), lambda i: (jnp.minimum(i, n_real - 1), 0)),
                  pl.BlockSpec((tt, TOP_K), lambda i: (i, 0))],
        out_specs=pl.BlockSpec((1, r_loc, d), lambda i: (i, 0, 0)),
        out_shape=jax.ShapeDtypeStruct((n_tiles_total, r_loc, d), BF16),
        compiler_params=_cp("parallel"),
        name="moe_dispatch",
    )(u, lpos)


def _expert_kernel(tbl_ref, be_ref, nblk_ref, x_hbm, wg_ref, wu_ref, wd_ref, y_hbm,
                   xbuf, ybuf, wg_bf, wu_bf, wd_bf, gsem, ssem, *, upb):
    b = pl.program_id(0)
    nblk = nblk_ref[0]

    def gather(blk, sl):
        for m in range(upb):
            row = pl.multiple_of(tbl_ref[blk * upb + m], MOE_UNIT)
            yield pltpu.make_async_copy(x_hbm.at[pl.ds(row, MOE_UNIT), :],
                                        xbuf.at[sl, pl.ds(m * MOE_UNIT, MOE_UNIT), :], gsem.at[sl, m])

    def scatter(blk, sl):
        for m in range(upb):
            row = pl.multiple_of(tbl_ref[blk * upb + m], MOE_UNIT)
            yield pltpu.make_async_copy(ybuf.at[sl, pl.ds(m * MOE_UNIT, MOE_UNIT), :],
                                        y_hbm.at[pl.ds(row, MOE_UNIT), :], ssem.at[sl, m])

    @pl.when(jnp.logical_and(b == 0, nblk > 0))
    def _():
        for cp in gather(0, 0):
            cp.start()

    @pl.when(jnp.logical_or(b == 0, be_ref[b] != be_ref[jnp.maximum(b - 1, 0)]))
    def _():
        wg_bf[...] = wg_ref[0].astype(BF16)
        wu_bf[...] = wu_ref[0].astype(BF16)
        wd_bf[...] = wd_ref[0].astype(BF16)

    slot = lax.rem(b, 2)

    @pl.when(b + 1 < nblk)
    def _():
        for cp in gather(b + 1, 1 - slot):
            cp.start()

    @pl.when(jnp.logical_and(b >= 2, b < nblk + 2))
    def _():
        for cp in scatter(b - 2, slot):
            cp.wait()

    @pl.when(b < nblk)
    def _():
        for cp in gather(b, slot):
            cp.wait()
        x = xbuf[slot]
        hid = _silu(_dot(x, wg_bf[...])) * _dot(x, wu_bf[...])
        ybuf[slot] = _dot(hid.astype(BF16), wd_bf[...]).astype(ybuf.dtype)
        for cp in scatter(b, slot):
            cp.start()


def _experts(x_loc, unit_rows, block_e, n_blocks_used, w_gate, w_up, w_down, n_blocks):
    rows, d = x_loc.shape
    de = w_gate.shape[2]
    upb = MOE_BLOCK // MOE_UNIT
    grid_spec = pltpu.PrefetchScalarGridSpec(
        num_scalar_prefetch=3,
        grid=(n_blocks + 2,),
        in_specs=[pl.BlockSpec(memory_space=pl.ANY),
                  pl.BlockSpec((1, d, de), lambda b, tbl, be, nb: (be[b], 0, 0)),
                  pl.BlockSpec((1, d, de), lambda b, tbl, be, nb: (be[b], 0, 0)),
                  pl.BlockSpec((1, de, d), lambda b, tbl, be, nb: (be[b], 0, 0))],
        out_specs=pl.BlockSpec(memory_space=pl.ANY),
        scratch_shapes=[pltpu.VMEM((2, MOE_BLOCK, d), BF16), pltpu.VMEM((2, MOE_BLOCK, d), BF16),
                        pltpu.VMEM((d, de), BF16), pltpu.VMEM((d, de), BF16), pltpu.VMEM((de, d), BF16),
                        pltpu.SemaphoreType.DMA((2, upb)), pltpu.SemaphoreType.DMA((2, upb))],
    )
    return pl.pallas_call(
        functools.partial(_expert_kernel, upb=upb),
        grid_spec=grid_spec,
        out_shape=jax.ShapeDtypeStruct((rows, d), BF16),
        input_output_aliases={3: 0},
        compiler_params=_cp("arbitrary"),
        name="moe_experts",
    )(unit_rows, block_e, n_blocks_used, x_loc, w_gate, w_up, w_down)


def _combine_kernel(y_ref, pos_ref, tw_ref, u_ref, wsg_ref, wsu_ref, wsd_ref, h_ref, g_ref, out_ref, *, tiles_per_batch):
    b = pl.program_id(0) // tiles_per_batch
    routed = _dot(_one_hot_t(pos_ref, tw_ref, y_ref.shape[1]), y_ref[0])
    u = u_ref[...]
    hid = _silu(_dot(u, wsg_ref[...])) * _dot(u, wsu_ref[...])
    shared = _dot(hid.astype(BF16), wsd_ref[...])
    out_ref[...] = h_ref[...] + g_ref[pl.ds(b, 1), :] * (routed + shared)


def _combine(y_loc, lpos, top_w, u, w_sg, w_su, w_sd, h, ada, gate_blk, seq):
    t, d = h.shape
    tt = MOE_TOK_TILE
    r_loc = y_loc.shape[1]
    row = lambda i: (i, 0)
    fix = lambda i: (0, 0)
    return pl.pallas_call(
        functools.partial(_combine_kernel, tiles_per_batch=seq // tt),
        grid=(t // tt,),
        in_specs=[pl.BlockSpec((1, r_loc, d), lambda i: (i, 0, 0)),
                  pl.BlockSpec((tt, TOP_K), row), pl.BlockSpec((tt, TOP_K), row),
                  pl.BlockSpec((tt, d), row),
                  pl.BlockSpec(w_sg.shape, fix), pl.BlockSpec(w_su.shape, fix), pl.BlockSpec(w_sd.shape, fix),
                  pl.BlockSpec((tt, d), row),
                  pl.BlockSpec((SUBLANES, d), lambda i: (0, gate_blk))],
        out_specs=pl.BlockSpec((tt, d), row),
        out_shape=jax.ShapeDtypeStruct((t, d), F32),
        compiler_params=_cp("parallel"),
        name="moe_combine",
    )(y_loc, lpos, top_w, u, w_sg, w_su, w_sd, h, ada)


def _moe_ffn(h, gain, ada, blk0, w_router, b_router, w_eg, w_eu, w_ed, w_sg, w_su, w_sd, seq):
    t, d = h.shape
    tt, unit, upb = MOE_TOK_TILE, MOE_UNIT, MOE_BLOCK // MOE_UNIT
    wr = jnp.pad(w_router, ((0, 0), (0, LANES - N_EXPERTS)))
    br = jnp.pad(b_router.astype(F32), (0, LANES - N_EXPERTS)).reshape(1, LANES)
    u, lpos, top_w, nu, off = _route(h, gain, ada, blk0, blk0 + 1, wr, br, seq)

    n_tok_tiles = t // tt
    r_loc = -(-(tt * TOP_K + N_EXPERTS * (unit - 1)) // LANES) * LANES
    n_dummy_tiles = -(-(N_EXPERTS * (upb - 1) * unit) // r_loc)
    max_units = n_tok_tiles * (tt * TOP_K // unit + N_EXPERTS)
    n_blocks = -(-max_units // upb) + N_EXPERTS
    unit_rows, block_e, n_used = _plan(nu[:, 0, :], off[:, 0, :], n_tok_tiles, r_loc, n_blocks,
                                       n_dummy_tiles * r_loc // unit)
    lpos = jnp.concatenate([lpos, jnp.full((n_dummy_tiles * tt, TOP_K), -1, I32)], axis=0)

    x_loc = _dispatch(u, lpos, n_tok_tiles + n_dummy_tiles, r_loc)
    y_loc = _experts(x_loc.reshape(-1, d), unit_rows, block_e, n_used, w_eg, w_eu, w_ed, n_blocks)
    y_loc = y_loc.reshape(n_tok_tiles + n_dummy_tiles, r_loc, d)
    return _combine(y_loc, lpos, top_w, u, w_sg.astype(BF16), w_su.astype(BF16), w_sd.astype(BF16),
                    h, ada, blk0 + 2, seq)


def _fox_layer(h, ada, norm_gain, w_in, b_f, q_gain, k_gain, w_o, batch, seq):
    t, d = h.shape
    hd = N_HEADS_A * HEAD_DIM
    w = jnp.pad(w_in, ((0, 0), (0, MXU_COLS - N_HEADS_A))).astype(BF16)
    proj = _norm_mod_matmul(h, norm_gain, ada, 0, 1, w, seq, w.shape[1], "fox_in_proj")
    z = proj[:, 3 * hd:3 * hd + N_HEADS_A].reshape(batch, seq, N_HEADS_A)
    z = z.transpose(1, 0, 2).reshape(seq, batch * N_HEADS_A)
    cum = _logsig_cumsum(z, jnp.tile(b_f.astype(F32), batch).reshape(1, -1))
    q, k, v = _fox_prep(proj, cum, q_gain, k_gain, batch, seq)
    bh = batch * N_HEADS_A
    o = _flash(q.reshape(bh, seq // FOX_TQ, FOX_TQ, LANES), k.reshape(bh, LANES, seq),
               v.reshape(bh, seq, LANES), FOX_TQ, FOX_TK, "fox_attention")
    o = o.reshape(batch, N_HEADS_A, seq, HEAD_DIM).transpose(0, 2, 1, 3).reshape(t, hd)
    return _oproj_residual(o, w_o.astype(BF16), h, ada, 2, seq)


def _rope_tables(positions):
    inv_freq = ROPE_THETA ** (-jnp.arange(0, ROPE_DIM, 2, dtype=F32) / ROPE_DIM)
    ang = positions.astype(F32).reshape(-1, 1) * inv_freq
    half = ROPE_DIM // 2
    cos, sin = jnp.cos(ang), jnp.sin(ang)
    rest = HEAD_DIM - ROPE_DIM
    cos_h = jnp.concatenate([cos, cos, jnp.ones((ang.shape[0], rest), F32)], axis=1)
    sin_h = jnp.concatenate([-sin, sin, jnp.zeros((ang.shape[0], rest), F32)], axis=1)
    del half
    return jnp.tile(cos_h, (1, LANES // HEAD_DIM)), jnp.tile(sin_h, (1, LANES // HEAD_DIM))


def _group_rows(x, batch, seq, tq):
    g, hpg = N_KV_GROUPS_B, HEADS_PER_GROUP
    x = x.reshape(batch, seq // tq, tq, g, hpg, HEAD_DIM).transpose(0, 3, 1, 4, 2, 5)
    return x.reshape(batch * g, seq // tq, hpg * tq, HEAD_DIM)


def _per_group(x, batch, seq):
    g = N_KV_GROUPS_B
    return x.reshape(batch, seq, g, HEAD_DIM).transpose(0, 2, 1, 3).reshape(batch * g, seq, HEAD_DIM)


def _with_ones(v):
    pad = jnp.zeros(v.shape[:-1] + (LANES - HEAD_DIM,), v.dtype).at[..., 0].set(1)
    return jnp.concatenate([v, pad], axis=-1)


def _shared_kv(h, ada_kv, norm_kv, w_kv, pe_k, pe_v, w_ck1, w_ck2, w_cv1, w_cv2, k_gains, cos_t, sin_t, batch, seq):
    g = N_KV_GROUPS_B
    gw = g * HEAD_DIM
    nb = seq // CMP_BLOCK
    kv = _norm_mod_matmul(h, norm_kv, ada_kv, 0, 1, w_kv.astype(BF16), seq, w_kv.shape[1], "nsa_kv_proj")
    k_sel, v_sel, k_win, v_win = _nsa_kv_prep(kv, cos_t, sin_t, k_gains)

    def blocks(x):
        x = x.reshape(batch, nb, CMP_BLOCK, g, HEAD_DIM).transpose(0, 1, 3, 2, 4)
        return x.reshape(batch * nb * g, CMP_BLOCK * HEAD_DIM)

    k_cmp = _compress(blocks(kv[:, 0:gw]), pe_k.reshape(1, -1), w_ck1.astype(BF16), w_ck2.astype(BF16),
                      k_gains[0], True, "nsa_compress_k")
    v_cmp = _compress(blocks(kv[:, gw:2 * gw]), pe_v.reshape(1, -1), w_cv1.astype(BF16), w_cv2.astype(BF16),
                      k_gains[0], False, "nsa_compress_v")
    to_bg = lambda x: x.reshape(batch, nb, g, HEAD_DIM).transpose(0, 2, 1, 3).reshape(batch * g, nb, HEAD_DIM)
    k_cmp = to_bg(k_cmp)
    v_cmp = jnp.pad(to_bg(v_cmp), ((0, 0), (0, 0), (0, LANES - HEAD_DIM)))

    def per_group_t(x):
        return x.reshape(batch, seq, g, HEAD_DIM).transpose(0, 2, 3, 1).reshape(batch * g, HEAD_DIM, seq)

    blk_onehot = (jnp.arange(nb)[:, None] == jnp.arange(seq)[None, :] // CMP_BLOCK).astype(BF16)
    k_sel = jnp.concatenate([jnp.broadcast_to(blk_onehot, (batch * g, nb, seq)), per_group_t(k_sel)], axis=1)
    return (k_cmp, v_cmp, k_sel, _with_ones(_per_group(v_sel, batch, seq)),
            per_group_t(k_win), _with_ones(_per_group(v_win, batch, seq)))


def _nsa_layer(h, ada, norm_gain, kv, cos_t, sin_t, w_in, b_gate, q_gain, w_o, batch, seq):
    hd = N_HEADS_B * HEAD_DIM
    k_cmp, v_cmp, k_sel, v_sel, k_win, v_win = kv
    n_gate = 3 * N_HEADS_B
    w = jnp.pad(w_in, ((0, 0), (0, LANES - n_gate))).astype(BF16)
    proj = _norm_mod_matmul(h, norm_gain, ada, 0, 1, w, seq, hd + LANES, "nsa_in_proj")
    bg = jnp.pad(b_gate.astype(F32), (0, LANES - n_gate)).reshape(1, LANES)
    q_plain, q_rot, gates = _nsa_q_prep(proj, cos_t, sin_t, q_gain, bg)

    tq = NSA_TQ
    qr = _group_rows(q_rot, batch, seq, tq)
    bgs = (batch, N_KV_GROUPS_B)
    o_cmp, negbias = _cmp_select(_group_rows(q_plain, batch, seq, CMP_TQ), k_cmp, v_cmp, seq, CMP_TQ, bgs)
    o_sel = _flash(qr, k_sel, v_sel, tq, NSA_TK, "nsa_selected", bias=negbias, grouped_out=bgs)
    o_win = _window_attention(qr, k_win, v_win, tq, WINDOW, bgs)
    return _nsa_out(o_cmp, o_sel, o_win, gates, w_o.astype(BF16), h, ada, 2, seq)


def kernel(x, c, positions, w_ada, b_ada, norm_mix, norm_ffn, w_in_a, b_f_a, qnorm_a, knorm_a, w_o_a, norm_kv, w_ada_kv, b_ada_kv, w_kv_b, pe_cmp_k, pe_cmp_v, w_cmp_k1, w_cmp_k2, w_cmp_v1, w_cmp_v2, knorm_b, w_in_b, b_gate_b, qnorm_b, w_o_b, w_router, b_router, w_exp_gate, w_exp_up, w_exp_down, w_sh_gate, w_sh_up, w_sh_down):
    batch, seq, d = x.shape
    depth = w_ada.shape[0]
    n_a = w_in_a.shape[0]
    c_pad = jnp.pad(c, ((0, SUBLANES - batch), (0, 0)))
    cos_t, sin_t = _rope_tables(positions)
    h = x.reshape(batch * seq, d)
    kv = None
    for layer in range(depth):
        ada = _ada(c_pad, w_ada[layer], b_ada[layer])
        if layer < n_a:
            h = _fox_layer(h, ada, norm_mix[layer], w_in_a[layer], b_f_a[layer], qnorm_a[layer],
                           knorm_a[layer], w_o_a[layer], batch, seq)
        else:
            if layer == n_a:
                ada_kv = _ada(c_pad, w_ada_kv, b_ada_kv)
                kv = _shared_kv(h, ada_kv, norm_kv, w_kv_b, pe_cmp_k, pe_cmp_v, w_cmp_k1, w_cmp_k2,
                                w_cmp_v1, w_cmp_v2, knorm_b, cos_t, sin_t, batch, seq)
            j = layer - n_a
            h = _nsa_layer(h, ada, norm_mix[layer], kv, cos_t, sin_t, w_in_b[j], b_gate_b[j], qnorm_b[j],
                           w_o_b[j], batch, seq)
        h = _moe_ffn(h, norm_ffn[layer], ada, 3, w_router[layer], b_router[layer], w_exp_gate[layer],
                     w_exp_up[layer], w_exp_down[layer], w_sh_gate[layer], w_sh_up[layer], w_sh_down[layer], seq)
    return h.reshape(batch, seq, d)
```

```python
import functools

import jax
import jax.numpy as jnp
from jax import lax
from jax.experimental import pallas as pl
from jax.experimental.pallas import tpu as pltpu

F32 = jnp.float32
BF16 = jnp.bfloat16
I32 = jnp.int32

HEAD_DIM = 64
N_HEADS_A = 16
N_HEADS_B = 16
N_KV_GROUPS_B = 4
HEADS_PER_GROUP = N_HEADS_B // N_KV_GROUPS_B
ROPE_THETA = 500000.0
ROPE_DIM = HEAD_DIM // 4
CMP_BLOCK = 64
N_SELECT = 16
WINDOW = 512
N_EXPERTS = 64
TOP_K = 8
ROUTED_SCALE = 2.5
EPS = 1e-6
FORCE_SCORE = 1e4
NEG = -1e30
LOG2E = 1.4426950408889634
QSCALE = HEAD_DIM ** -0.5 * LOG2E

LANES = 128
SUBLANES = 8
BF16_ROWS = 16
MXU_COLS = 256

ROW_TILE = 512
FOX_TQ = 2048
FOX_TK = 512
SEL_TQ = 512
WIN_TQ = 256
NSA_TK = 512
CMP_TQ = 512
MOE_TOK_TILE = 256
ROUTE_TILES = 2
MOE_UNIT = BF16_ROWS
MOE_BLOCK = 256
VMEM_LIMIT = 56 * 1024 * 1024


def _cp(*sem):
    return pltpu.CompilerParams(dimension_semantics=sem, vmem_limit_bytes=VMEM_LIMIT)


def _dot(a, b):
    return jnp.dot(a, b, preferred_element_type=F32)


def _dot_nt(a, b):
    return lax.dot_general(a, b, (((1,), (1,)), ((), ())), preferred_element_type=F32)


def _dot_tn(a, b):
    return lax.dot_general(a, b, (((0,), (0,)), ((), ())), preferred_element_type=F32)


def _split2(x):
    hi = x.astype(BF16)
    lo = (x - hi.astype(F32)).astype(BF16)
    return hi, lo


def _split3(x):
    hi = x.astype(BF16)
    r = x - hi.astype(F32)
    mid = r.astype(BF16)
    lo = (r - mid.astype(F32)).astype(BF16)
    return hi, mid, lo


def _dot_f32(a, b):
    a_hi, a_lo = _split2(a)
    b_hi, b_lo = _split2(b)
    return _dot(a_hi, b_hi) + _dot(a_lo, b_hi) + _dot(a_hi, b_lo)


def _sigmoid(x):
    return 1.0 / (1.0 + jnp.exp(-x))


def _silu(x):
    return x * _sigmoid(x)


def _ada_kernel(c_ref, w_ref, b_ref, o_ref):
    o_ref[...] = _dot_f32(_silu(c_ref[...]), w_ref[...]) + b_ref[...]


def _ada(c_pad, w, b):
    d, n = w.shape
    tn = 1024
    return pl.pallas_call(
        _ada_kernel,
        grid=(n // tn,),
        in_specs=[pl.BlockSpec((SUBLANES, d), lambda j: (0, 0)),
                  pl.BlockSpec((d, tn), lambda j: (0, j)),
                  pl.BlockSpec((1, tn), lambda j: (0, j))],
        out_specs=pl.BlockSpec((SUBLANES, tn), lambda j: (0, j)),
        out_shape=jax.ShapeDtypeStruct((SUBLANES, n), F32),
        compiler_params=_cp("parallel"),
        name="ada",
    )(c_pad, w, b.reshape(1, n))


def _norm_mod(x, gain, shift, scale):
    xn = x * lax.rsqrt(jnp.mean(x * x, axis=-1, keepdims=True) + EPS) * gain
    return xn * (1.0 + scale) + shift


def _nm_kernel(h_ref, gain_ref, shift_ref, scale_ref, w_ref, o_ref, u_scr, *, tiles_per_batch):
    i = pl.program_id(0)

    @pl.when(pl.program_id(1) == 0)
    def _():
        b = i // tiles_per_batch
        u = _norm_mod(h_ref[...], gain_ref[...], shift_ref[pl.ds(b, 1), :], scale_ref[pl.ds(b, 1), :])
        u_scr[...] = u.astype(BF16)

    o_ref[...] = _dot(u_scr[...], w_ref[...]).astype(o_ref.dtype)


def _norm_mod_matmul(h, gain, ada, shift_blk, scale_blk, w, seq, tn, name):
    t, d = h.shape
    n = w.shape[1]
    tm = ROW_TILE
    return pl.pallas_call(
        functools.partial(_nm_kernel, tiles_per_batch=seq // tm),
        grid=(t // tm, n // tn),
        in_specs=[pl.BlockSpec((tm, d), lambda i, j: (i, 0)),
                  pl.BlockSpec((1, d), lambda i, j: (0, 0)),
                  pl.BlockSpec((SUBLANES, d), lambda i, j: (0, shift_blk)),
                  pl.BlockSpec((SUBLANES, d), lambda i, j: (0, scale_blk)),
                  pl.BlockSpec((d, tn), lambda i, j: (0, j))],
        out_specs=pl.BlockSpec((tm, tn), lambda i, j: (i, j)),
        out_shape=jax.ShapeDtypeStruct((t, n), F32),
        scratch_shapes=[pltpu.VMEM((tm, d), BF16)],
        compiler_params=_cp("parallel", "arbitrary"),
        name=name,
    )(h, gain.reshape(1, d), ada, ada, w)


def _logsig_cumsum_kernel(z_ref, b_ref, tri_ref, o_ref, carry):
    @pl.when(pl.program_id(0) == 0)
    def _():
        carry[...] = jnp.zeros_like(carry)

    z = z_ref[...] + b_ref[...]
    lf = jnp.minimum(z, 0.0) - jnp.log1p(jnp.exp(-jnp.abs(z)))
    hi, mid, lo = _split3(lf)
    tri = tri_ref[...]
    c = _dot(tri, hi) + _dot(tri, mid) + _dot(tri, lo) + carry[...]
    o_ref[...] = c
    ts = c.shape[0]
    carry[...] = c[ts - 1:ts, :]


def _logsig_cumsum(z, bias):
    s, c = z.shape
    ts = 256
    tri = (jnp.arange(ts)[:, None] >= jnp.arange(ts)[None, :]).astype(BF16)
    return pl.pallas_call(
        _logsig_cumsum_kernel,
        grid=(s // ts,),
        in_specs=[pl.BlockSpec((ts, c), lambda i: (i, 0)),
                  pl.BlockSpec((1, c), lambda i: (0, 0)),
                  pl.BlockSpec((ts, ts), lambda i: (0, 0))],
        out_specs=pl.BlockSpec((ts, c), lambda i: (i, 0)),
        out_shape=jax.ShapeDtypeStruct((s, c), F32),
        scratch_shapes=[pltpu.VMEM((1, c), F32)],
        compiler_params=_cp("arbitrary"),
        name="fox_logsig_cumsum",
    )(z, bias, tri)


def _head_rms(x, gain):
    return x * lax.rsqrt(jnp.mean(x * x, axis=-1, keepdims=True) + EPS) * gain


def _fox_prep_kernel(q_ref, k_ref, v_ref, cum_ref, qg_ref, kg_ref, qo_ref, ko_ref, vo_ref):
    bh0 = (pl.program_id(0) * (N_HEADS_A // 2) + pl.program_id(1)) * 2
    ts = q_ref.shape[0]
    cum = cum_ref[...]
    lane_c = lax.broadcasted_iota(I32, cum.shape, 1)
    lane = lax.broadcasted_iota(I32, (ts, HEAD_DIM), 1)
    for h2 in range(2):
        sl = slice(h2 * HEAD_DIM, (h2 + 1) * HEAD_DIM)
        c = jnp.sum(jnp.where(lane_c == bh0 + h2, cum, 0.0), axis=-1, keepdims=True) * LOG2E
        hi, mid, lo = (p.astype(F32) for p in _split3(c))
        qn = _head_rms(q_ref[:, sl], qg_ref[...]) * QSCALE
        kn = _head_rms(k_ref[:, sl], kg_ref[...])
        ext_q = jnp.where(lane == 0, hi, jnp.where(lane == 1, mid, jnp.where(lane == 2, lo,
                          jnp.where(lane < 6, 1.0, 0.0))))
        ext_k = jnp.where(lane < 3, 1.0, jnp.where(lane == 3, -hi, jnp.where(lane == 4, -mid,
                          jnp.where(lane == 5, -lo, 0.0))))
        ext_v = jnp.where(lane == 0, 1.0, 0.0)
        qo_ref[0, h2] = jnp.concatenate([qn, ext_q], axis=1).astype(BF16)
        ko_ref[0, h2] = jnp.concatenate([kn, ext_k], axis=1).T.astype(BF16)
        vo_ref[0, h2] = jnp.concatenate([v_ref[:, sl], ext_v], axis=1).astype(BF16)


def _fox_prep(proj, cum, q_gain, k_gain, batch, seq):
    ts = ROW_TILE
    nsb = seq // ts
    hp = N_HEADS_A // 2
    out = jax.ShapeDtypeStruct((batch, N_HEADS_A, seq, LANES), BF16)
    out_t = jax.ShapeDtypeStruct((batch, N_HEADS_A, LANES, seq), BF16)
    ospec = pl.BlockSpec((1, 2, ts, LANES), lambda b, p, i: (b, p, i, 0))
    ospec_t = pl.BlockSpec((1, 2, LANES, ts), lambda b, p, i: (b, p, 0, i))
    return pl.pallas_call(
        _fox_prep_kernel,
        grid=(batch, hp, nsb),
        in_specs=[pl.BlockSpec((ts, LANES), lambda b, p, i: (b * nsb + i, p)),
                  pl.BlockSpec((ts, LANES), lambda b, p, i: (b * nsb + i, hp + p)),
                  pl.BlockSpec((ts, LANES), lambda b, p, i: (b * nsb + i, 2 * hp + p)),
                  pl.BlockSpec((ts, cum.shape[1]), lambda b, p, i: (i, 0)),
                  pl.BlockSpec((1, HEAD_DIM), lambda b, p, i: (0, 0)),
                  pl.BlockSpec((1, HEAD_DIM), lambda b, p, i: (0, 0))],
        out_specs=[ospec, ospec_t, ospec],
        out_shape=[out, out_t, out],
        compiler_params=_cp("parallel", "parallel", "parallel"),
        name="fox_prep",
    )(proj, proj, proj, cum, q_gain.reshape(1, HEAD_DIM), k_gain.reshape(1, HEAD_DIM))


def _store_heads(o_ref, o, tq, grouped):
    if not grouped:
        o_ref[0, 0] = o.astype(o_ref.dtype)
        return
    for hh in range(o.shape[0] // tq):
        o_ref[:, hh * HEAD_DIM:(hh + 1) * HEAD_DIM] = o[hh * tq:(hh + 1) * tq].astype(o_ref.dtype)


def _flash_kernel(*refs, tq, tk, with_bias, grouped):
    if with_bias:
        q_ref, nb_ref, k_ref, v_ref, o_ref, m_scr, acc_scr = refs
    else:
        q_ref, k_ref, v_ref, o_ref, m_scr, acc_scr = refs
    i = pl.program_id(1)
    q = q_ref[0, 0]
    rows = q.shape[0]
    if with_bias:
        q = jnp.concatenate([jnp.tile(nb_ref[0], (rows // tq, 1)), q], axis=1)
    m_scr[...] = jnp.full(m_scr.shape, NEG, F32)
    acc_scr[...] = jnp.zeros(acc_scr.shape, F32)
    q_pos = i * tq + jnp.bitwise_and(lax.broadcasted_iota(I32, (rows, 1), 0), tq - 1)

    def step(j, masked, row_lo=0):
        ks = pl.multiple_of(j * tk, tk)
        s = _dot(q[row_lo:], k_ref[0, :, pl.ds(ks, tk)])
        if masked:
            k_pos = ks + lax.broadcasted_iota(I32, (1, tk), 1)
            s = jnp.where(k_pos <= q_pos[row_lo:], s, NEG)
        m_old = m_scr[row_lo:, :]
        m_new = jnp.maximum(m_old, jnp.max(s, axis=-1, keepdims=True))
        p = jnp.exp2(s - jnp.tile(m_new, (1, tk // LANES))).astype(BF16)
        acc_scr[row_lo:, :] = jnp.exp2(m_old - m_new) * acc_scr[row_lo:, :] + _dot(p, v_ref[0, pl.ds(ks, tk), :])
        m_scr[row_lo:, :] = m_new

    n_clear = (i * tq + 1) // tk
    n_diag = max(tq // tk, 1)

    def pair(jj, c):
        step(2 * jj, False)
        step(2 * jj + 1, False)
        return c

    lax.fori_loop(0, n_clear // 2, pair, 0)

    @pl.when(lax.rem(n_clear, 2) == 1)
    def _():
        step(n_clear - 1, False)

    for d in range(n_diag):
        step(n_clear + d, True, row_lo=d * tk if rows == tq else 0)

    acc = acc_scr[...]
    _store_heads(o_ref, acc[:, :HEAD_DIM] / acc[:, HEAD_DIM:HEAD_DIM + 1], tq, grouped)


def _flash(q, k, v, tq, tk, name, bias=None, grouped_out=None):
    bh, nq, rows, kd = q.shape
    s = k.shape[2]
    assert tq % tk == 0 or tk % tq == 0
    in_specs = [pl.BlockSpec((1, 1, rows, kd), lambda b, i: (b, i, 0, 0))]
    args = [q]
    if bias is not None:
        in_specs.append(pl.BlockSpec((1, tq, bias.shape[2]), lambda b, i: (b, i, 0)))
        args.append(bias)
    in_specs += [pl.BlockSpec((1, k.shape[1], s), lambda b, i: (b, 0, 0)),
                 pl.BlockSpec((1, s, LANES), lambda b, i: (b, 0, 0))]
    if grouped_out is None:
        out_spec = pl.BlockSpec((1, 1, rows, HEAD_DIM), lambda b, i: (b, i, 0, 0))
        out_shape = jax.ShapeDtypeStruct((bh, nq, rows, HEAD_DIM), BF16)
    else:
        out_spec, out_shape = _grouped_out(grouped_out, nq, tq, rows)
    return pl.pallas_call(
        functools.partial(_flash_kernel, tq=tq, tk=tk, with_bias=bias is not None, grouped=grouped_out is not None),
        grid=(bh, nq),
        in_specs=in_specs,
        out_specs=out_spec,
        out_shape=out_shape,
        scratch_shapes=[pltpu.VMEM((rows, LANES), F32), pltpu.VMEM((rows, LANES), F32)],
        compiler_params=_cp("parallel", "arbitrary"),
        name=name,
    )(*args, k, v)


def _grouped_out(batch_groups, nq, tq, rows):
    batch, groups = batch_groups
    width = rows // tq * HEAD_DIM
    spec = pl.BlockSpec((tq, width), lambda b, i: ((b // groups) * nq + i, lax.rem(b, groups)))
    return spec, jax.ShapeDtypeStruct((batch * nq * tq, groups * width), BF16)


def _window_kernel(q_ref, k_ref, v_ref, o_ref, *, tq, window):
    i = pl.program_id(1)
    q = q_ref[0, 0]
    rows = q.shape[0]
    span = window + tq
    ks = pl.multiple_of(jnp.maximum(i * tq - window, 0), tq)
    q_pos = i * tq + jnp.bitwise_and(lax.broadcasted_iota(I32, (rows, 1), 0), tq - 1)
    k_pos = ks + lax.broadcasted_iota(I32, (1, span), 1)
    ok = (q_pos - k_pos).astype(jnp.uint32) < window
    s = jnp.where(ok, _dot(q, k_ref[0, :, pl.ds(ks, span)]), NEG)
    p = jnp.exp2(s - jnp.max(s, axis=-1, keepdims=True)).astype(BF16)
    acc = _dot(p, v_ref[0, pl.ds(ks, span), :])
    _store_heads(o_ref, acc[:, :HEAD_DIM] / acc[:, HEAD_DIM:HEAD_DIM + 1], tq, True)


def _window_attention(q, k, v, tq, window, batch_groups):
    bh, nq, rows, kd = q.shape
    s = k.shape[2]
    assert window % tq == 0 and window + tq <= s
    out_spec, out_shape = _grouped_out(batch_groups, nq, tq, rows)
    return pl.pallas_call(
        functools.partial(_window_kernel, tq=tq, window=window),
        grid=(bh, nq),
        in_specs=[pl.BlockSpec((1, 1, rows, kd), lambda b, i: (b, i, 0, 0)),
                  pl.BlockSpec((1, kd, s), lambda b, i: (b, 0, 0)),
                  pl.BlockSpec((1, s, LANES), lambda b, i: (b, 0, 0))],
        out_specs=out_spec,
        out_shape=out_shape,
        compiler_params=_cp("parallel", "parallel"),
        name="nsa_window",
    )(q, k, v)


def _oproj_kernel(o_ref, w_ref, h_ref, g_ref, out_ref, *, tiles_per_batch):
    b = pl.program_id(0) // tiles_per_batch
    out_ref[...] = h_ref[...] + g_ref[pl.ds(b, 1), :] * _dot(o_ref[...], w_ref[...])


def _oproj_residual(o, w, h, ada, gate_blk, seq):
    t, d = h.shape
    tm = ROW_TILE
    return pl.pallas_call(
        functools.partial(_oproj_kernel, tiles_per_batch=seq // tm),
        grid=(t // tm,),
        in_specs=[pl.BlockSpec((tm, o.shape[1]), lambda i: (i, 0)),
                  pl.BlockSpec(w.shape, lambda i: (0, 0)),
                  pl.BlockSpec((tm, d), lambda i: (i, 0)),
                  pl.BlockSpec((SUBLANES, d), lambda i: (0, gate_blk))],
        out_specs=pl.BlockSpec((tm, d), lambda i: (i, 0)),
        out_shape=jax.ShapeDtypeStruct((t, d), F32),
        compiler_params=_cp("parallel"),
        name="oproj_residual",
    )(o, w, h, ada)


def _seg_rms(x, seg_ones, gain_tiled):
    ss = _dot((x * x).astype(BF16), seg_ones)
    return x * lax.rsqrt(ss * (1.0 / HEAD_DIM) + EPS) * gain_tiled


def _rope(x, cos_t, sin_t):
    width = x.shape[1]
    reps = width // LANES
    lane = jnp.bitwise_and(lax.broadcasted_iota(I32, x.shape, 1), HEAD_DIM - 1)
    half = ROPE_DIM // 2
    partner = jnp.where(lane < half, pltpu.roll(x, width - half, 1), pltpu.roll(x, half, 1))
    return x * jnp.tile(cos_t, (1, reps)) + partner * jnp.tile(sin_t, (1, reps))


def _nsa_q_kernel(p_ref, z_ref, cos_ref, sin_ref, gain_ref, seg_ref, bg_ref, qp_ref, qr_ref, g_ref):
    q = _seg_rms(p_ref[...], seg_ref[...], gain_ref[...]) * QSCALE
    qp_ref[...] = q.astype(BF16)
    qr_ref[...] = _rope(q, cos_ref[...], sin_ref[...]).astype(BF16)
    g_ref[...] = _sigmoid(z_ref[...] + bg_ref[...])


def _nsa_q_prep(proj, cos_t, sin_t, q_gain, b_gate_pad):
    t = proj.shape[0]
    hd = N_HEADS_B * HEAD_DIM
    tm = ROW_TILE
    seg = (jnp.arange(hd)[:, None] // HEAD_DIM == jnp.arange(hd)[None, :] // HEAD_DIM).astype(BF16)
    row = lambda i: (i, 0)
    fix = lambda i: (0, 0)
    return pl.pallas_call(
        _nsa_q_kernel,
        grid=(t // tm,),
        in_specs=[pl.BlockSpec((tm, hd), row),
                  pl.BlockSpec((tm, LANES), lambda i: (i, hd // LANES)),
                  pl.BlockSpec((tm, LANES), row), pl.BlockSpec((tm, LANES), row),
                  pl.BlockSpec((1, hd), fix), pl.BlockSpec((hd, hd), fix), pl.BlockSpec((1, LANES), fix)],
        out_specs=[pl.BlockSpec((tm, hd), row), pl.BlockSpec((tm, hd), row), pl.BlockSpec((tm, LANES), row)],
        out_shape=[jax.ShapeDtypeStruct((t, hd), BF16), jax.ShapeDtypeStruct((t, hd), BF16),
                   jax.ShapeDtypeStruct((t, LANES), F32)],
        compiler_params=_cp("parallel"),
        name="nsa_q_prep",
    )(proj, proj, cos_t, sin_t, jnp.tile(q_gain, N_HEADS_B).reshape(1, hd), seg, b_gate_pad)


def _nsa_kv_kernel(kv_ref, cos_ref, sin_ref, g1_ref, g2_ref, seg_ref, ks_ref, vs_ref, kw_ref, vw_ref):
    gw = N_KV_GROUPS_B * HEAD_DIM
    ks = _seg_rms(kv_ref[:, 2 * gw:3 * gw], seg_ref[...], g1_ref[...])
    kw = _seg_rms(kv_ref[:, 4 * gw:5 * gw], seg_ref[...], g2_ref[...])
    ks_ref[...] = _rope(ks, cos_ref[...], sin_ref[...]).astype(BF16)
    kw_ref[...] = _rope(kw, cos_ref[...], sin_ref[...]).astype(BF16)
    vs_ref[...] = kv_ref[:, 3 * gw:4 * gw].astype(BF16)
    vw_ref[...] = kv_ref[:, 5 * gw:6 * gw].astype(BF16)


def _nsa_kv_prep(kv, cos_t, sin_t, k_gains):
    t, n = kv.shape
    gw = N_KV_GROUPS_B * HEAD_DIM
    tm = ROW_TILE
    seg = (jnp.arange(gw)[:, None] // HEAD_DIM == jnp.arange(gw)[None, :] // HEAD_DIM).astype(BF16)
    row = lambda i: (i, 0)
    fix = lambda i: (0, 0)
    out = jax.ShapeDtypeStruct((t, gw), BF16)
    return pl.pallas_call(
        _nsa_kv_kernel,
        grid=(t // tm,),
        in_specs=[pl.BlockSpec((tm, n), row), pl.BlockSpec((tm, LANES), row), pl.BlockSpec((tm, LANES), row),
                  pl.BlockSpec((1, gw), fix), pl.BlockSpec((1, gw), fix), pl.BlockSpec((gw, gw), fix)],
        out_specs=[pl.BlockSpec((tm, gw), row)] * 4,
        out_shape=[out] * 4,
        compiler_params=_cp("parallel"),
        name="nsa_kv_prep",
    )(kv, cos_t, sin_t, jnp.tile(k_gains[1], N_KV_GROUPS_B).reshape(1, gw),
      jnp.tile(k_gains[2], N_KV_GROUPS_B).reshape(1, gw), seg)


def _compress_kernel(x_ref, pe_ref, w1_ref, w2_ref, g_ref, o_ref, *, norm):
    x = (x_ref[...] + pe_ref[...]).astype(BF16)
    y = _dot(_silu(_dot(x, w1_ref[...])).astype(BF16), w2_ref[...])
    if norm:
        y = _head_rms(y, g_ref[...])
    o_ref[...] = y.astype(o_ref.dtype)


def _compress(x, pe_flat, w1, w2, gain, norm, name):
    r, kdim = x.shape
    tm = min(ROW_TILE, r)
    fix = lambda i: (0, 0)
    return pl.pallas_call(
        functools.partial(_compress_kernel, norm=norm),
        grid=(r // tm,),
        in_specs=[pl.BlockSpec((tm, kdim), lambda i: (i, 0)), pl.BlockSpec((1, kdim), fix),
                  pl.BlockSpec(w1.shape, fix), pl.BlockSpec(w2.shape, fix), pl.BlockSpec((1, HEAD_DIM), fix)],
        out_specs=pl.BlockSpec((tm, HEAD_DIM), lambda i: (i, 0)),
        out_shape=jax.ShapeDtypeStruct((r, HEAD_DIM), BF16),
        compiler_params=_cp("parallel"),
        name=name,
    )(x, pe_flat, w1, w2, gain.reshape(1, HEAD_DIM))


def _cmp_select_kernel(q_ref, k_ref, v_ref, o_ref, nb_ref, *, tq, n_sel):
    i = pl.program_id(1)
    q = q_ref[0, 0]
    rows = q.shape[0]
    nblk = k_ref.shape[1]
    logits = _dot_nt(q, k_ref[0])
    t_pos = i * tq + jnp.bitwise_and(lax.broadcasted_iota(I32, (rows, 1), 0), tq - 1)
    blk = lax.broadcasted_iota(I32, (1, nblk), 1)
    vis = (blk + 1) * CMP_BLOCK - 1 <= t_pos
    lm = jnp.where(vis, logits, NEG)
    e = jnp.exp2(lm - jnp.max(lm, axis=-1, keepdims=True))
    p = jnp.where(vis, e / jnp.sum(e, axis=-1, keepdims=True), 0.0)
    _store_heads(o_ref, _dot(p.astype(BF16), v_ref[0])[:, :HEAD_DIM], tq, True)

    imp = p[0:tq]
    for hh in range(1, rows // tq):
        imp = imp + p[hh * tq:(hh + 1) * tq]
    cur = lax.shift_right_logical(t_pos[0:tq], CMP_BLOCK.bit_length() - 1)
    forced = jnp.logical_or(blk == 0, jnp.logical_or(blk == cur, blk == cur - 1))
    cand = jnp.where(jnp.logical_or(forced, blk > cur), -jnp.inf, imp)
    sel = jnp.where(forced, 1.0, 0.0)
    blk_f = blk.astype(F32)
    for _ in range(max(n_sel - 3, 0)):
        best = jnp.max(cand, axis=-1, keepdims=True)
        first = jnp.min(jnp.where(cand == best, blk_f, float(nblk)), axis=-1, keepdims=True)
        hit = blk_f == first
        sel = jnp.where(jnp.logical_and(hit, best > -jnp.inf), 1.0, sel)
        cand = jnp.where(hit, -jnp.inf, cand)
    nb_ref[0] = jnp.where(sel > 0.5, 0.0, NEG).astype(nb_ref.dtype)


def _cmp_select(q, k_cmp, v_cmp, seq, tq, batch_groups):
    bg, nq, rows, _ = q.shape
    nblk = k_cmp.shape[1]
    assert min(N_SELECT, nblk) >= 3
    out_spec, out_shape = _grouped_out(batch_groups, nq, tq, rows)
    return pl.pallas_call(
        functools.partial(_cmp_select_kernel, tq=tq, n_sel=min(N_SELECT, nblk)),
        grid=(bg, nq),
        in_specs=[pl.BlockSpec((1, 1, rows, HEAD_DIM), lambda b, i: (b, i, 0, 0)),
                  pl.BlockSpec((1, nblk, HEAD_DIM), lambda b, i: (b, 0, 0)),
                  pl.BlockSpec((1, nblk, LANES), lambda b, i: (b, 0, 0))],
        out_specs=[out_spec, pl.BlockSpec((1, tq, nblk), lambda b, i: (b, i, 0))],
        out_shape=[out_shape, jax.ShapeDtypeStruct((bg, seq, nblk), BF16)],
        compiler_params=_cp("parallel", "parallel"),
        name="nsa_cmp_select",
    )(q, k_cmp, v_cmp)


def _nsa_out_kernel(oc_ref, os_ref, ow_ref, g_ref, e_ref, w_ref, h_ref, ada_ref, out_ref, *, tiles_per_batch):
    b = pl.program_id(0) // tiles_per_batch
    g_hi, g_lo = _split2(g_ref[...])
    o = None
    for j, o_ref in enumerate((oc_ref, os_ref, ow_ref)):
        gate = _dot(g_hi, e_ref[j]) + _dot(g_lo, e_ref[j])
        term = gate * o_ref[...].astype(F32)
        o = term if o is None else o + term
    out_ref[...] = h_ref[...] + ada_ref[pl.ds(b, 1), :] * _dot(o.astype(BF16), w_ref[...])


def _nsa_out(o_cmp, o_sel, o_win, gates, w_o, h, ada, gate_blk, seq):
    t, d = h.shape
    hd = o_cmp.shape[1]
    tm = ROW_TILE
    col = jnp.arange(LANES)[:, None]
    lane = jnp.arange(hd)[None, :]
    expand = jnp.stack([(col == (lane // HEAD_DIM) * 3 + j) for j in range(3)]).astype(BF16)
    row = lambda i: (i, 0)
    return pl.pallas_call(
        functools.partial(_nsa_out_kernel, tiles_per_batch=seq // tm),
        grid=(t // tm,),
        in_specs=[pl.BlockSpec((tm, hd), row), pl.BlockSpec((tm, hd), row), pl.BlockSpec((tm, hd), row),
                  pl.BlockSpec((tm, LANES), row),
                  pl.BlockSpec((3, LANES, hd), lambda i: (0, 0, 0)),
                  pl.BlockSpec(w_o.shape, lambda i: (0, 0)),
                  pl.BlockSpec((tm, d), row),
                  pl.BlockSpec((SUBLANES, d), lambda i: (0, gate_blk))],
        out_specs=pl.BlockSpec((tm, d), row),
        out_shape=jax.ShapeDtypeStruct((t, d), F32),
        compiler_params=_cp("parallel"),
        name="nsa_out",
    )(o_cmp, o_sel, o_win, gates, expand, w_o, h, ada)


def _route_kernel(h_ref, gain_ref, shift_ref, scale_ref, wr_ref, br_ref, tri_ref, upper_ref,
                  u_ref, pos_ref, w_ref, nu_ref, off_ref, *, tiles_per_batch):
    b = (pl.program_id(0) * ROUTE_TILES) // tiles_per_batch
    shift, scale = shift_ref[pl.ds(b, 1), :], scale_ref[pl.ds(b, 1), :]
    for tile in range(ROUTE_TILES):
        rows = slice(tile * MOE_TOK_TILE, (tile + 1) * MOE_TOK_TILE)
        _route_tile(h_ref[rows, :], gain_ref[...], shift, scale, wr_ref, br_ref, tri_ref, upper_ref,
                    u_ref.at[rows, :], pos_ref.at[rows, :], w_ref.at[rows, :], nu_ref.at[tile], off_ref.at[tile])


def _route_tile(h, gain, shift, scale, wr_ref, br_ref, tri_ref, upper_ref, u_ref, pos_ref, w_ref, nu_ref, off_ref):
    u = _norm_mod(h, gain, shift, scale)
    u_ref[...] = u.astype(BF16)
    scores = _sigmoid(_dot_f32(u, wr_ref[...]))
    lane = lax.broadcasted_iota(I32, scores.shape, 1)
    lane_f = lane.astype(F32)
    biased = jnp.where(lane < N_EXPERTS, scores + br_ref[...], -jnp.inf)
    chosen = jnp.zeros(scores.shape, F32)
    hits, val = [], []
    upper = upper_ref[...]
    for _ in range(TOP_K):
        best = jnp.max(biased, axis=-1, keepdims=True)
        first = jnp.min(jnp.where(biased == best, lane_f, float(LANES)), axis=-1, keepdims=True)
        hit = lane_f == first
        hits.append(hit)
        val.append(jnp.sum(jnp.where(hit, scores, 0.0), axis=-1, keepdims=True))
        biased = jnp.where(hit, -jnp.inf, biased)
        chosen = jnp.where(hit, 1.0, chosen)
    top_s = jnp.concatenate(val, axis=1)
    w_ref[...] = top_s / jnp.sum(top_s, axis=-1, keepdims=True) * ROUTED_SCALE

    before = _dot(tri_ref[...], chosen.astype(BF16))
    count = jnp.sum(chosen, axis=0, keepdims=True)
    units = jnp.floor((count + (MOE_UNIT - 1)) * (1.0 / MOE_UNIT))
    padded = jnp.broadcast_to(units * MOE_UNIT, (SUBLANES, LANES))
    offset = _dot(padded.astype(BF16), upper)[0:1]
    row_of = before + offset
    pos = [jnp.sum(jnp.where(hit, row_of, 0.0), axis=-1, keepdims=True) for hit in hits]
    pos_ref[...] = jnp.concatenate(pos, axis=1).astype(I32)
    nu_ref[...] = units.astype(I32)
    off_ref[...] = offset.astype(I32)


def _route(h, gain, ada, shift_blk, scale_blk, w_router_pad, b_router_pad, seq):
    t, d = h.shape
    tt = MOE_TOK_TILE
    n_tiles = t // tt
    rt = ROUTE_TILES * tt
    assert n_tiles % ROUTE_TILES == 0 and seq % rt == 0
    row = lambda i: (i, 0)
    fix = lambda i: (0, 0)
    tri = (jnp.arange(tt)[:, None] > jnp.arange(tt)[None, :]).astype(BF16)
    upper = (jnp.arange(LANES)[:, None] < jnp.arange(LANES)[None, :]).astype(BF16)
    per_tile = jax.ShapeDtypeStruct((n_tiles, 1, LANES), I32)
    per_tile_spec = pl.BlockSpec((ROUTE_TILES, 1, LANES), lambda i: (i, 0, 0))
    return pl.pallas_call(
        functools.partial(_route_kernel, tiles_per_batch=seq // tt),
        grid=(n_tiles // ROUTE_TILES,),
        in_specs=[pl.BlockSpec((rt, d), row), pl.BlockSpec((1, d), fix),
                  pl.BlockSpec((SUBLANES, d), lambda i: (0, shift_blk)),
                  pl.BlockSpec((SUBLANES, d), lambda i: (0, scale_blk)),
                  pl.BlockSpec((d, LANES), fix), pl.BlockSpec((1, LANES), fix),
                  pl.BlockSpec((tt, tt), fix), pl.BlockSpec((LANES, LANES), fix)],
        out_specs=[pl.BlockSpec((rt, d), row), pl.BlockSpec((rt, TOP_K), row), pl.BlockSpec((rt, TOP_K), row),
                   per_tile_spec, per_tile_spec],
        out_shape=[jax.ShapeDtypeStruct((t, d), BF16), jax.ShapeDtypeStruct((t, TOP_K), I32),
                   jax.ShapeDtypeStruct((t, TOP_K), F32), per_tile, per_tile],
        compiler_params=_cp("parallel"),
        name="moe_route",
    )(h, gain.reshape(1, d), ada, ada, w_router_pad, b_router_pad, tri, upper)


PLAN_SLOTS = 256
PLAN_TILES = LANES


def _plan_kernel(nu_ref, off_ref, upper_ref, lower_ref, row_ref, exp_ref, nused_ref,
                 *, n_tiles, r_loc, upb, n_unit_slots, n_dump_units):
    nu = nu_ref[...]
    nu_t = nu.T
    off_t = off_ref[...].T
    upper = upper_ref[...]
    lower = lower_ref[...]
    lane = lax.broadcasted_iota(I32, (PLAN_SLOTS, LANES), 1)
    lane_f = lane.astype(F32)

    def exact_dot(a, b_bf16, left):
        parts = _split3(a)
        return sum(_dot(b_bf16, p) if left else _dot(p, b_bf16) for p in parts)

    units_before_t = _dot(nu_t.astype(BF16), upper)
    count_col = jnp.sum(nu_t, axis=1, keepdims=True)
    count_row = jnp.sum(nu, axis=0, keepdims=True)
    padded_col = jnp.floor((count_col + (upb - 1)) * (1.0 / upb)) * upb
    padded_row = jnp.floor((count_row + (upb - 1)) * (1.0 / upb)) * upb
    start_col = exact_dot(jnp.broadcast_to(padded_col, (LANES, LANES)), lower, True)
    real_col = exact_dot(jnp.broadcast_to(count_col, (LANES, LANES)), lower, True)
    start_row = exact_dot(jnp.broadcast_to(padded_row, (SUBLANES, LANES)), upper, False)[0:1]
    seg_start_t = start_col + units_before_t
    seg_real_t = real_col + units_before_t
    tile_id = lax.broadcasted_iota(I32, (LANES, LANES), 1).astype(F32)
    seg_row_t = tile_id * r_loc + off_t

    idx = pl.program_id(0) * PLAN_SLOTS + lax.broadcasted_iota(I32, (PLAN_SLOTS, 1), 0)
    slot = jnp.where(idx < n_unit_slots, idx, (idx - n_unit_slots) * upb).astype(F32)

    in_e = jnp.logical_and(start_row <= slot, lane < N_EXPERTS)
    expert = jnp.sum(jnp.where(in_e, 1.0, 0.0), axis=-1, keepdims=True) - 1.0
    pick_e = (lane_f == expert).astype(BF16)
    look = lambda tbl: exact_dot(tbl, pick_e, True)
    starts = look(seg_start_t)
    in_t = jnp.logical_and(starts <= slot, lane < n_tiles)
    tile = jnp.sum(jnp.where(in_t, 1.0, 0.0), axis=-1, keepdims=True) - 1.0
    at = lambda x: jnp.sum(jnp.where(lane_f == tile, x, 0.0), axis=-1, keepdims=True)
    m = slot - at(starts)
    n = at(look(nu_t))
    row = at(look(seg_row_t)) + m * MOE_UNIT
    pad_idx = slot - (at(look(seg_real_t)) + jnp.minimum(m, n))
    dump = n_tiles * r_loc + jnp.minimum(pad_idx, n_dump_units - 1.0) * MOE_UNIT
    row_ref[...] = jnp.where(m < n, row, dump).astype(I32)
    exp_ref[...] = expert.astype(I32)
    total = jnp.sum(padded_row, axis=-1, keepdims=True) * (1.0 / upb)
    nused_ref[...] = jnp.broadcast_to(total, (SUBLANES, LANES)).astype(I32)


def _plan(nu, off, n_tiles, r_loc, n_blocks, n_dump_units):
    assert n_tiles <= PLAN_TILES
    upb = MOE_BLOCK // MOE_UNIT
    n_unit_slots = -(-(n_blocks * upb) // PLAN_SLOTS) * PLAN_SLOTS
    n_steps = n_unit_slots // PLAN_SLOTS + -(-(n_blocks + 2) // PLAN_SLOTS)
    pad = lambda x: jnp.pad(x.astype(F32), ((0, PLAN_TILES - n_tiles), (0, 0)))
    upper = (jnp.arange(LANES)[:, None] < jnp.arange(LANES)[None, :]).astype(BF16)
    fix = lambda s: (0, 0)
    col = pl.BlockSpec((PLAN_SLOTS, 1), lambda s: (s, 0))
    rows, experts, n_used = pl.pallas_call(
        functools.partial(_plan_kernel, n_tiles=n_tiles, r_loc=r_loc, upb=upb, n_unit_slots=n_unit_slots,
                          n_dump_units=n_dump_units),
        grid=(n_steps,),
        in_specs=[pl.BlockSpec((PLAN_TILES, LANES), fix), pl.BlockSpec((PLAN_TILES, LANES), fix),
                  pl.BlockSpec((LANES, LANES), fix), pl.BlockSpec((LANES, LANES), fix)],
        out_specs=[col, col, pl.BlockSpec((SUBLANES, LANES), fix)],
        out_shape=[jax.ShapeDtypeStruct((n_steps * PLAN_SLOTS, 1), I32),
                   jax.ShapeDtypeStruct((n_steps * PLAN_SLOTS, 1), I32),
                   jax.ShapeDtypeStruct((SUBLANES, LANES), I32)],
        compiler_params=_cp("arbitrary"),
        name="moe_plan",
    )(pad(nu), pad(off), upper, upper.T)
    return (rows[:n_blocks * upb, 0], experts[n_unit_slots:n_unit_slots + n_blocks + 2, 0], n_used[0, :1])


def _one_hot_t(pos_ref, weight_ref, n_rows):
    tt = pos_ref.shape[0]
    col = lax.broadcasted_iota(I32, (tt, n_rows), 1)
    acc = jnp.zeros((tt, n_rows), F32)
    for k in range(TOP_K):
        acc = jnp.where(col == pos_ref[:, k:k + 1], 1.0 if weight_ref is None else weight_ref[:, k:k + 1], acc)
    return acc.astype(BF16)


def _dispatch_kernel(u_ref, pos_ref, x_ref):
    x_ref[0] = _dot_tn(_one_hot_t(pos_ref, None, x_ref.shape[1]), u_ref[...]).astype(x_ref.dtype)


def _dispatch(u, lpos, n_tiles_total, r_loc):
    t, d = u.shape
    tt = MOE_TOK_TILE
    n_real = t // tt
    return pl.pallas_call(
        _dispatch_kernel,
        grid=(n_tiles_total,),
        in_specs=[pl.BlockSpec((tt, d), lambda i: (jnp.minimum(i, n_real - 1), 0)),
                  pl.BlockSpec((tt, TOP_K), lambda i: (i, 0))],
        out_specs=pl.BlockSpec((1, r_loc, d), lambda i: (i, 0, 0)),
        out_shape=jax.ShapeDtypeStruct((n_tiles_total, r_loc, d), BF16),
        compiler_params=_cp("parallel"),
        name="moe_dispatch",
    )(u, lpos)


def _expert_kernel(tbl_ref, be_ref, nblk_ref, x_hbm, wg_ref, wu_ref, wd_ref, y_hbm,
                   xbuf, ybuf, wg_bf, wu_bf, wd_bf, gsem, ssem, *, upb):
    b = pl.program_id(0)
    nblk = nblk_ref[0]

    def gather(blk, sl):
        for m in range(upb):
            row = pl.multiple_of(tbl_ref[blk * upb + m], MOE_UNIT)
            yield pltpu.make_async_copy(x_hbm.at[pl.ds(row, MOE_UNIT), :],
                                        xbuf.at[sl, pl.ds(m * MOE_UNIT, MOE_UNIT), :], gsem.at[sl, m])

    def scatter(blk, sl):
        for m in range(upb):
            row = pl.multiple_of(tbl_ref[blk * upb + m], MOE_UNIT)
            yield pltpu.make_async_copy(ybuf.at[sl, pl.ds(m * MOE_UNIT, MOE_UNIT), :],
                                        y_hbm.at[pl.ds(row, MOE_UNIT), :], ssem.at[sl, m])

    @pl.when(jnp.logical_and(b == 0, nblk > 0))
    def _():
        for cp in gather(0, 0):
            cp.start()

    @pl.when(jnp.logical_or(b == 0, be_ref[b] != be_ref[jnp.maximum(b - 1, 0)]))
    def _():
        wg_bf[...] = wg_ref[0].astype(BF16)
        wu_bf[...] = wu_ref[0].astype(BF16)
        wd_bf[...] = wd_ref[0].astype(BF16)

    slot = lax.rem(b, 2)

    @pl.when(b + 1 < nblk)
    def _():
        for cp in gather(b + 1, 1 - slot):
            cp.start()

    @pl.when(jnp.logical_and(b >= 2, b < nblk + 2))
    def _():
        for cp in scatter(b - 2, slot):
            cp.wait()

    @pl.when(b < nblk)
    def _():
        for cp in gather(b, slot):
            cp.wait()
        x = xbuf[slot]
        hid = _silu(_dot(x, wg_bf[...])) * _dot(x, wu_bf[...])
        ybuf[slot] = _dot(hid.astype(BF16), wd_bf[...]).astype(ybuf.dtype)
        for cp in scatter(b, slot):
            cp.start()


def _experts(x_loc, unit_rows, block_e, n_blocks_used, w_gate, w_up, w_down, n_blocks):
    rows, d = x_loc.shape
    de = w_gate.shape[2]
    upb = MOE_BLOCK // MOE_UNIT
    grid_spec = pltpu.PrefetchScalarGridSpec(
        num_scalar_prefetch=3,
        grid=(n_blocks + 2,),
        in_specs=[pl.BlockSpec(memory_space=pl.ANY),
                  pl.BlockSpec((1, d, de), lambda b, tbl, be, nb: (be[b], 0, 0)),
                  pl.BlockSpec((1, d, de), lambda b, tbl, be, nb: (be[b], 0, 0)),
                  pl.BlockSpec((1, de, d), lambda b, tbl, be, nb: (be[b], 0, 0))],
        out_specs=pl.BlockSpec(memory_space=pl.ANY),
        scratch_shapes=[pltpu.VMEM((2, MOE_BLOCK, d), BF16), pltpu.VMEM((2, MOE_BLOCK, d), BF16),
                        pltpu.VMEM((d, de), BF16), pltpu.VMEM((d, de), BF16), pltpu.VMEM((de, d), BF16),
                        pltpu.SemaphoreType.DMA((2, upb)), pltpu.SemaphoreType.DMA((2, upb))],
    )
    return pl.pallas_call(
        functools.partial(_expert_kernel, upb=upb),
        grid_spec=grid_spec,
        out_shape=jax.ShapeDtypeStruct((rows, d), BF16),
        input_output_aliases={3: 0},
        compiler_params=_cp("arbitrary"),
        name="moe_experts",
    )(unit_rows, block_e, n_blocks_used, x_loc, w_gate, w_up, w_down)


def _combine_kernel(y_ref, pos_ref, tw_ref, u_ref, wsg_ref, wsu_ref, wsd_ref, h_ref, g_ref, out_ref, *, tiles_per_batch):
    b = pl.program_id(0) // tiles_per_batch
    routed = _dot(_one_hot_t(pos_ref, tw_ref, y_ref.shape[1]), y_ref[0])
    u = u_ref[...]
    hid = _silu(_dot(u, wsg_ref[...])) * _dot(u, wsu_ref[...])
    shared = _dot(hid.astype(BF16), wsd_ref[...])
    out_ref[...] = h_ref[...] + g_ref[pl.ds(b, 1), :] * (routed + shared)


def _combine(y_loc, lpos, top_w, u, w_sg, w_su, w_sd, h, ada, gate_blk, seq):
    t, d = h.shape
    tt = MOE_TOK_TILE
    r_loc = y_loc.shape[1]
    row = lambda i: (i, 0)
    fix = lambda i: (0, 0)
    return pl.pallas_call(
        functools.partial(_combine_kernel, tiles_per_batch=seq // tt),
        grid=(t // tt,),
        in_specs=[pl.BlockSpec((1, r_loc, d), lambda i: (i, 0, 0)),
                  pl.BlockSpec((tt, TOP_K), row), pl.BlockSpec((tt, TOP_K), row),
                  pl.BlockSpec((tt, d), row),
                  pl.BlockSpec(w_sg.shape, fix), pl.BlockSpec(w_su.shape, fix), pl.BlockSpec(w_sd.shape, fix),
                  pl.BlockSpec((tt, d), row),
                  pl.BlockSpec((SUBLANES, d), lambda i: (0, gate_blk))],
        out_specs=pl.BlockSpec((tt, d), row),
        out_shape=jax.ShapeDtypeStruct((t, d), F32),
        compiler_params=_cp("parallel"),
        name="moe_combine",
    )(y_loc, lpos, top_w, u, w_sg, w_su, w_sd, h, ada)


def _moe_ffn(h, gain, ada, blk0, w_router, b_router, w_eg, w_eu, w_ed, w_sg, w_su, w_sd, seq):
    t, d = h.shape
    tt, unit, upb = MOE_TOK_TILE, MOE_UNIT, MOE_BLOCK // MOE_UNIT
    wr = jnp.pad(w_router, ((0, 0), (0, LANES - N_EXPERTS)))
    br = jnp.pad(b_router.astype(F32), (0, LANES - N_EXPERTS)).reshape(1, LANES)
    u, lpos, top_w, nu, off = _route(h, gain, ada, blk0, blk0 + 1, wr, br, seq)

    n_tok_tiles = t // tt
    r_loc = -(-(tt * TOP_K + N_EXPERTS * (unit - 1)) // LANES) * LANES
    n_dummy_tiles = -(-(N_EXPERTS * (upb - 1) * unit) // r_loc)
    max_units = n_tok_tiles * (tt * TOP_K // unit + N_EXPERTS)
    n_blocks = -(-max_units // upb) + N_EXPERTS
    unit_rows, block_e, n_used = _plan(nu[:, 0, :], off[:, 0, :], n_tok_tiles, r_loc, n_blocks,
                                       n_dummy_tiles * r_loc // unit)
    lpos = jnp.concatenate([lpos, jnp.full((n_dummy_tiles * tt, TOP_K), -1, I32)], axis=0)

    x_loc = _dispatch(u, lpos, n_tok_tiles + n_dummy_tiles, r_loc)
    y_loc = _experts(x_loc.reshape(-1, d), unit_rows, block_e, n_used, w_eg, w_eu, w_ed, n_blocks)
    y_loc = y_loc.reshape(n_tok_tiles + n_dummy_tiles, r_loc, d)
    return _combine(y_loc, lpos, top_w, u, w_sg.astype(BF16), w_su.astype(BF16), w_sd.astype(BF16),
                    h, ada, blk0 + 2, seq)


def _fox_layer(h, ada, norm_gain, w_in, b_f, q_gain, k_gain, w_o, batch, seq):
    t, d = h.shape
    hd = N_HEADS_A * HEAD_DIM
    w = jnp.pad(w_in, ((0, 0), (0, MXU_COLS - N_HEADS_A))).astype(BF16)
    proj = _norm_mod_matmul(h, norm_gain, ada, 0, 1, w, seq, w.shape[1], "fox_in_proj")
    z = proj[:, 3 * hd:3 * hd + N_HEADS_A].reshape(batch, seq, N_HEADS_A)
    z = z.transpose(1, 0, 2).reshape(seq, batch * N_HEADS_A)
    cum = _logsig_cumsum(z, jnp.tile(b_f.astype(F32), batch).reshape(1, -1))
    q, k, v = _fox_prep(proj, cum, q_gain, k_gain, batch, seq)
    bh = batch * N_HEADS_A
    o = _flash(q.reshape(bh, seq // FOX_TQ, FOX_TQ, LANES), k.reshape(bh, LANES, seq),
               v.reshape(bh, seq, LANES), FOX_TQ, FOX_TK, "fox_attention")
    o = o.reshape(batch, N_HEADS_A, seq, HEAD_DIM).transpose(0, 2, 1, 3).reshape(t, hd)
    return _oproj_residual(o, w_o.astype(BF16), h, ada, 2, seq)


def _rope_tables(positions):
    inv_freq = ROPE_THETA ** (-jnp.arange(0, ROPE_DIM, 2, dtype=F32) / ROPE_DIM)
    ang = positions.astype(F32).reshape(-1, 1) * inv_freq
    half = ROPE_DIM // 2
    cos, sin = jnp.cos(ang), jnp.sin(ang)
    rest = HEAD_DIM - ROPE_DIM
    cos_h = jnp.concatenate([cos, cos, jnp.ones((ang.shape[0], rest), F32)], axis=1)
    sin_h = jnp.concatenate([-sin, sin, jnp.zeros((ang.shape[0], rest), F32)], axis=1)
    del half
    return jnp.tile(cos_h, (1, LANES // HEAD_DIM)), jnp.tile(sin_h, (1, LANES // HEAD_DIM))


def _group_rows(x, batch, seq, tq):
    g, hpg = N_KV_GROUPS_B, HEADS_PER_GROUP
    x = x.reshape(batch, seq // tq, tq, g, hpg, HEAD_DIM).transpose(0, 3, 1, 4, 2, 5)
    return x.reshape(batch * g, seq // tq, hpg * tq, HEAD_DIM)


def _per_group(x, batch, seq):
    g = N_KV_GROUPS_B
    return x.reshape(batch, seq, g, HEAD_DIM).transpose(0, 2, 1, 3).reshape(batch * g, seq, HEAD_DIM)


def _with_ones(v):
    pad = jnp.zeros(v.shape[:-1] + (LANES - HEAD_DIM,), v.dtype).at[..., 0].set(1)
    return jnp.concatenate([v, pad], axis=-1)


def _shared_kv(h, ada_kv, norm_kv, w_kv, pe_k, pe_v, w_ck1, w_ck2, w_cv1, w_cv2, k_gains, cos_t, sin_t, batch, seq):
    g = N_KV_GROUPS_B
    gw = g * HEAD_DIM
    nb = seq // CMP_BLOCK
    kv = _norm_mod_matmul(h, norm_kv, ada_kv, 0, 1, w_kv.astype(BF16), seq, w_kv.shape[1], "nsa_kv_proj")
    k_sel, v_sel, k_win, v_win = _nsa_kv_prep(kv, cos_t, sin_t, k_gains)

    def blocks(x):
        x = x.reshape(batch, nb, CMP_BLOCK, g, HEAD_DIM).transpose(0, 1, 3, 2, 4)
        return x.reshape(batch * nb * g, CMP_BLOCK * HEAD_DIM)

    k_cmp = _compress(blocks(kv[:, 0:gw]), pe_k.reshape(1, -1), w_ck1.astype(BF16), w_ck2.astype(BF16),
                      k_gains[0], True, "nsa_compress_k")
    v_cmp = _compress(blocks(kv[:, gw:2 * gw]), pe_v.reshape(1, -1), w_cv1.astype(BF16), w_cv2.astype(BF16),
                      k_gains[0], False, "nsa_compress_v")
    to_bg = lambda x: x.reshape(batch, nb, g, HEAD_DIM).transpose(0, 2, 1, 3).reshape(batch * g, nb, HEAD_DIM)
    k_cmp = to_bg(k_cmp)
    v_cmp = jnp.pad(to_bg(v_cmp), ((0, 0), (0, 0), (0, LANES - HEAD_DIM)))

    def per_group_t(x):
        return x.reshape(batch, seq, g, HEAD_DIM).transpose(0, 2, 3, 1).reshape(batch * g, HEAD_DIM, seq)

    blk_onehot = (jnp.arange(nb)[:, None] == jnp.arange(seq)[None, :] // CMP_BLOCK).astype(BF16)
    k_sel = jnp.concatenate([jnp.broadcast_to(blk_onehot, (batch * g, nb, seq)), per_group_t(k_sel)], axis=1)
    return (k_cmp, v_cmp, k_sel, _with_ones(_per_group(v_sel, batch, seq)),
            per_group_t(k_win), _with_ones(_per_group(v_win, batch, seq)))


def _nsa_layer(h, ada, norm_gain, kv, cos_t, sin_t, w_in, b_gate, q_gain, w_o, batch, seq):
    hd = N_HEADS_B * HEAD_DIM
    k_cmp, v_cmp, k_sel, v_sel, k_win, v_win = kv
    n_gate = 3 * N_HEADS_B
    w = jnp.pad(w_in, ((0, 0), (0, LANES - n_gate))).astype(BF16)
    proj = _norm_mod_matmul(h, norm_gain, ada, 0, 1, w, seq, hd + LANES, "nsa_in_proj")
    bg = jnp.pad(b_gate.astype(F32), (0, LANES - n_gate)).reshape(1, LANES)
    q_plain, q_rot, gates = _nsa_q_prep(proj, cos_t, sin_t, q_gain, bg)

    bgs = (batch, N_KV_GROUPS_B)
    o_cmp, negbias = _cmp_select(_group_rows(q_plain, batch, seq, CMP_TQ), k_cmp, v_cmp, seq, CMP_TQ, bgs)
    o_sel = _flash(_group_rows(q_rot, batch, seq, SEL_TQ), k_sel, v_sel, SEL_TQ, NSA_TK, "nsa_selected",
                   bias=negbias, grouped_out=bgs)
    o_win = _window_attention(_group_rows(q_rot, batch, seq, WIN_TQ), k_win, v_win, WIN_TQ, WINDOW, bgs)
    return _nsa_out(o_cmp, o_sel, o_win, gates, w_o.astype(BF16), h, ada, 2, seq)


def kernel(x, c, positions, w_ada, b_ada, norm_mix, norm_ffn, w_in_a, b_f_a, qnorm_a, knorm_a, w_o_a, norm_kv, w_ada_kv, b_ada_kv, w_kv_b, pe_cmp_k, pe_cmp_v, w_cmp_k1, w_cmp_k2, w_cmp_v1, w_cmp_v2, knorm_b, w_in_b, b_gate_b, qnorm_b, w_o_b, w_router, b_router, w_exp_gate, w_exp_up, w_exp_down, w_sh_gate, w_sh_up, w_sh_down):
    batch, seq, d = x.shape
    depth = w_ada.shape[0]
    n_a = w_in_a.shape[0]
    c_pad = jnp.pad(c, ((0, SUBLANES - batch), (0, 0)))
    cos_t, sin_t = _rope_tables(positions)
    h = x.reshape(batch * seq, d)
    kv = None
    for layer in range(depth):
        ada = _ada(c_pad, w_ada[layer], b_ada[layer])
        if layer < n_a:
            h = _fox_layer(h, ada, norm_mix[layer], w_in_a[layer], b_f_a[layer], qnorm_a[layer],
                           knorm_a[layer], w_o_a[layer], batch, seq)
        else:
            if layer == n_a:
                ada_kv = _ada(c_pad, w_ada_kv, b_ada_kv)
                kv = _shared_kv(h, ada_kv, norm_kv, w_kv_b, pe_cmp_k, pe_cmp_v, w_cmp_k1, w_cmp_k2,
                                w_cmp_v1, w_cmp_v2, knorm_b, cos_t, sin_t, batch, seq)
            j = layer - n_a
            h = _nsa_layer(h, ada, norm_mix[layer], kv, cos_t, sin_t, w_in_b[j], b_gate_b[j], qnorm_b[j],
                           w_o_b[j], batch, seq)
        h = _moe_ffn(h, norm_ffn[layer], ada, 3, w_router[layer], b_router[layer], w_exp_gate[layer],
                     w_exp_up[layer], w_exp_down[layer], w_sh_gate[layer], w_sh_up[layer], w_sh_down[layer], seq)
    return h.reshape(batch, seq, d)
```

```python
import functools

import jax
import jax.numpy as jnp
from jax import lax
from jax.experimental import pallas as pl
from jax.experimental.pallas import tpu as pltpu

F32 = jnp.float32
BF16 = jnp.bfloat16
I32 = jnp.int32

HEAD_DIM = 64
N_HEADS_A = 16
N_HEADS_B = 16
N_KV_GROUPS_B = 4
HEADS_PER_GROUP = N_HEADS_B // N_KV_GROUPS_B
ROPE_THETA = 500000.0
ROPE_DIM = HEAD_DIM // 4
CMP_BLOCK = 64
N_SELECT = 16
WINDOW = 512
N_EXPERTS = 64
TOP_K = 8
ROUTED_SCALE = 2.5
EPS = 1e-6
FORCE_SCORE = 1e4
NEG = -1e30
LOG2E = 1.4426950408889634
QSCALE = HEAD_DIM ** -0.5 * LOG2E

LANES = 128
SUBLANES = 8
BF16_ROWS = 16
MXU_COLS = 256

ROW_TILE = 512
FOX_TQ = 2048
FOX_TK = 512
SEL_TQ = 512
WIN_TQ = 256
NSA_TK = 512
CMP_TQ = 512
MOE_TOK_TILE = 256
ROUTE_TILES = 2
MOE_UNIT = BF16_ROWS
MOE_BLOCK = 512
VMEM_LIMIT = 56 * 1024 * 1024


def _cp(*sem):
    return pltpu.CompilerParams(dimension_semantics=sem, vmem_limit_bytes=VMEM_LIMIT)


def _dot(a, b):
    return jnp.dot(a, b, preferred_element_type=F32)


def _dot_nt(a, b):
    return lax.dot_general(a, b, (((1,), (1,)), ((), ())), preferred_element_type=F32)


def _dot_tn(a, b):
    return lax.dot_general(a, b, (((0,), (0,)), ((), ())), preferred_element_type=F32)


def _split2(x):
    hi = x.astype(BF16)
    lo = (x - hi.astype(F32)).astype(BF16)
    return hi, lo


def _split3(x):
    hi = x.astype(BF16)
    r = x - hi.astype(F32)
    mid = r.astype(BF16)
    lo = (r - mid.astype(F32)).astype(BF16)
    return hi, mid, lo


def _dot_f32(a, b):
    a_hi, a_lo = _split2(a)
    b_hi, b_lo = _split2(b)
    return _dot(a_hi, b_hi) + _dot(a_lo, b_hi) + _dot(a_hi, b_lo)


def _sigmoid(x):
    return 1.0 / (1.0 + jnp.exp(-x))


def _silu(x):
    return x * _sigmoid(x)


def _ada_kernel(c_ref, w_ref, b_ref, o_ref):
    o_ref[...] = _dot_f32(_silu(c_ref[...]), w_ref[...]) + b_ref[...]


def _ada(c_pad, w, b):
    d, n = w.shape
    tn = 1024
    return pl.pallas_call(
        _ada_kernel,
        grid=(n // tn,),
        in_specs=[pl.BlockSpec((SUBLANES, d), lambda j: (0, 0)),
                  pl.BlockSpec((d, tn), lambda j: (0, j)),
                  pl.BlockSpec((1, tn), lambda j: (0, j))],
        out_specs=pl.BlockSpec((SUBLANES, tn), lambda j: (0, j)),
        out_shape=jax.ShapeDtypeStruct((SUBLANES, n), F32),
        compiler_params=_cp("parallel"),
        name="ada",
    )(c_pad, w, b.reshape(1, n))


def _norm_mod(x, gain, shift, scale):
    xn = x * lax.rsqrt(jnp.mean(x * x, axis=-1, keepdims=True) + EPS) * gain
    return xn * (1.0 + scale) + shift


def _nm_kernel(h_ref, gain_ref, shift_ref, scale_ref, w_ref, o_ref, u_scr, *, tiles_per_batch):
    i = pl.program_id(0)

    @pl.when(pl.program_id(1) == 0)
    def _():
        b = i // tiles_per_batch
        u = _norm_mod(h_ref[...], gain_ref[...], shift_ref[pl.ds(b, 1), :], scale_ref[pl.ds(b, 1), :])
        u_scr[...] = u.astype(BF16)

    o_ref[...] = _dot(u_scr[...], w_ref[...]).astype(o_ref.dtype)


def _norm_mod_matmul(h, gain, ada, shift_blk, scale_blk, w, seq, tn, name):
    t, d = h.shape
    n = w.shape[1]
    tm = ROW_TILE
    return pl.pallas_call(
        functools.partial(_nm_kernel, tiles_per_batch=seq // tm),
        grid=(t // tm, n // tn),
        in_specs=[pl.BlockSpec((tm, d), lambda i, j: (i, 0)),
                  pl.BlockSpec((1, d), lambda i, j: (0, 0)),
                  pl.BlockSpec((SUBLANES, d), lambda i, j: (0, shift_blk)),
                  pl.BlockSpec((SUBLANES, d), lambda i, j: (0, scale_blk)),
                  pl.BlockSpec((d, tn), lambda i, j: (0, j))],
        out_specs=pl.BlockSpec((tm, tn), lambda i, j: (i, j)),
        out_shape=jax.ShapeDtypeStruct((t, n), F32),
        scratch_shapes=[pltpu.VMEM((tm, d), BF16)],
        compiler_params=_cp("parallel", "arbitrary"),
        name=name,
    )(h, gain.reshape(1, d), ada, ada, w)


def _logsig_cumsum_kernel(z_ref, b_ref, tri_ref, o_ref, carry):
    @pl.when(pl.program_id(0) == 0)
    def _():
        carry[...] = jnp.zeros_like(carry)

    z = z_ref[...] + b_ref[...]
    lf = jnp.minimum(z, 0.0) - jnp.log1p(jnp.exp(-jnp.abs(z)))
    hi, mid, lo = _split3(lf)
    tri = tri_ref[...]
    c = _dot(tri, hi) + _dot(tri, mid) + _dot(tri, lo) + carry[...]
    o_ref[...] = c
    ts = c.shape[0]
    carry[...] = c[ts - 1:ts, :]


def _logsig_cumsum(z, bias):
    s, c = z.shape
    ts = 256
    tri = (jnp.arange(ts)[:, None] >= jnp.arange(ts)[None, :]).astype(BF16)
    return pl.pallas_call(
        _logsig_cumsum_kernel,
        grid=(s // ts,),
        in_specs=[pl.BlockSpec((ts, c), lambda i: (i, 0)),
                  pl.BlockSpec((1, c), lambda i: (0, 0)),
                  pl.BlockSpec((ts, ts), lambda i: (0, 0))],
        out_specs=pl.BlockSpec((ts, c), lambda i: (i, 0)),
        out_shape=jax.ShapeDtypeStruct((s, c), F32),
        scratch_shapes=[pltpu.VMEM((1, c), F32)],
        compiler_params=_cp("arbitrary"),
        name="fox_logsig_cumsum",
    )(z, bias, tri)


def _head_rms(x, gain):
    return x * lax.rsqrt(jnp.mean(x * x, axis=-1, keepdims=True) + EPS) * gain


def _fox_prep_kernel(q_ref, k_ref, v_ref, cum_ref, qg_ref, kg_ref, qo_ref, ko_ref, vo_ref):
    bh0 = (pl.program_id(0) * (N_HEADS_A // 2) + pl.program_id(1)) * 2
    ts = q_ref.shape[0]
    cum = cum_ref[...]
    lane_c = lax.broadcasted_iota(I32, cum.shape, 1)
    lane = lax.broadcasted_iota(I32, (ts, LANES), 1)
    low = lane < HEAD_DIM
    ext = lane - HEAD_DIM

    def pair_rms(x, gain):
        sq = x * x
        ss = jnp.where(low, jnp.sum(jnp.where(low, sq, 0.0), axis=-1, keepdims=True),
                       jnp.sum(jnp.where(low, 0.0, sq), axis=-1, keepdims=True))
        return x * lax.rsqrt(ss * (1.0 / HEAD_DIM) + EPS) * gain

    qn = pair_rms(q_ref[...], qg_ref[...]) * QSCALE
    kn = pair_rms(k_ref[...], kg_ref[...])
    v = v_ref[...]
    for h2 in range(2):
        c = jnp.sum(jnp.where(lane_c == bh0 + h2, cum, 0.0), axis=-1, keepdims=True) * LOG2E
        hi, mid, lo = (p.astype(F32) for p in _split3(c))
        head = (lambda x: x) if h2 == 0 else (lambda x: pltpu.roll(x, HEAD_DIM, 1))
        ext_q = jnp.where(ext == 0, hi, jnp.where(ext == 1, mid, jnp.where(ext == 2, lo,
                          jnp.where(ext < 6, 1.0, 0.0))))
        ext_k = jnp.where(ext < 3, 1.0, jnp.where(ext == 3, -hi, jnp.where(ext == 4, -mid,
                          jnp.where(ext == 5, -lo, 0.0))))
        qo_ref[0, h2] = jnp.where(low, head(qn), ext_q).astype(BF16)
        ko_ref[0, h2] = jnp.where(low, head(kn), ext_k).T.astype(BF16)
        vo_ref[0, h2] = jnp.where(low, head(v), jnp.where(ext == 0, 1.0, 0.0)).astype(BF16)


def _fox_prep(proj, cum, q_gain, k_gain, batch, seq):
    ts = ROW_TILE
    nsb = seq // ts
    hp = N_HEADS_A // 2
    out = jax.ShapeDtypeStruct((batch, N_HEADS_A, seq, LANES), BF16)
    out_t = jax.ShapeDtypeStruct((batch, N_HEADS_A, LANES, seq), BF16)
    ospec = pl.BlockSpec((1, 2, ts, LANES), lambda b, p, i: (b, p, i, 0))
    ospec_t = pl.BlockSpec((1, 2, LANES, ts), lambda b, p, i: (b, p, 0, i))
    return pl.pallas_call(
        _fox_prep_kernel,
        grid=(batch, hp, nsb),
        in_specs=[pl.BlockSpec((ts, LANES), lambda b, p, i: (b * nsb + i, p)),
                  pl.BlockSpec((ts, LANES), lambda b, p, i: (b * nsb + i, hp + p)),
                  pl.BlockSpec((ts, LANES), lambda b, p, i: (b * nsb + i, 2 * hp + p)),
                  pl.BlockSpec((ts, cum.shape[1]), lambda b, p, i: (i, 0)),
                  pl.BlockSpec((1, LANES), lambda b, p, i: (0, 0)),
                  pl.BlockSpec((1, LANES), lambda b, p, i: (0, 0))],
        out_specs=[ospec, ospec_t, ospec],
        out_shape=[out, out_t, out],
        compiler_params=_cp("parallel", "parallel", "parallel"),
        name="fox_prep",
    )(proj, proj, proj, cum, jnp.tile(q_gain, LANES // HEAD_DIM).reshape(1, LANES),
      jnp.tile(k_gain, LANES // HEAD_DIM).reshape(1, LANES))


def _store_heads(o_ref, o, tq, grouped):
    if not grouped:
        o_ref[0, 0] = o.astype(o_ref.dtype)
        return
    for hh in range(o.shape[0] // tq):
        o_ref[:, hh * HEAD_DIM:(hh + 1) * HEAD_DIM] = o[hh * tq:(hh + 1) * tq].astype(o_ref.dtype)


def _flash_kernel(*refs, tq, tk, with_bias, grouped):
    if with_bias:
        q_ref, nb_ref, k_ref, v_ref, o_ref, m_scr, acc_scr = refs
    else:
        q_ref, k_ref, v_ref, o_ref, m_scr, acc_scr = refs
    i = pl.program_id(1)
    q = q_ref[0, 0]
    rows = q.shape[0]
    if with_bias:
        q = jnp.concatenate([jnp.tile(nb_ref[0], (rows // tq, 1)), q], axis=1)
    m_scr[...] = jnp.full(m_scr.shape, NEG, F32)
    acc_scr[...] = jnp.zeros(acc_scr.shape, F32)
    q_pos = i * tq + jnp.bitwise_and(lax.broadcasted_iota(I32, (rows, 1), 0), tq - 1)

    def step(j, masked, row_lo=0):
        ks = pl.multiple_of(j * tk, tk)
        s = _dot(q[row_lo:], k_ref[0, :, pl.ds(ks, tk)])
        if masked:
            k_pos = ks + lax.broadcasted_iota(I32, (1, tk), 1)
            s = jnp.where(k_pos <= q_pos[row_lo:], s, NEG)
        m_old = m_scr[row_lo:, :]
        m_new = jnp.maximum(m_old, jnp.max(s, axis=-1, keepdims=True))
        p = jnp.exp2(s - jnp.tile(m_new, (1, tk // LANES))).astype(BF16)
        acc_scr[row_lo:, :] = jnp.exp2(m_old - m_new) * acc_scr[row_lo:, :] + _dot(p, v_ref[0, pl.ds(ks, tk), :])
        m_scr[row_lo:, :] = m_new

    n_clear = (i * tq + 1) // tk
    n_diag = max(tq // tk, 1)

    def pair(jj, c):
        step(2 * jj, False)
        step(2 * jj + 1, False)
        return c

    lax.fori_loop(0, n_clear // 2, pair, 0)

    @pl.when(lax.rem(n_clear, 2) == 1)
    def _():
        step(n_clear - 1, False)

    for d in range(n_diag):
        step(n_clear + d, True, row_lo=d * tk if rows == tq else 0)

    acc = acc_scr[...]
    _store_heads(o_ref, acc[:, :HEAD_DIM] / acc[:, HEAD_DIM:HEAD_DIM + 1], tq, grouped)


def _flash(q, k, v, tq, tk, name, bias=None, grouped_out=None):
    bh, nq, rows, kd = q.shape
    s = k.shape[2]
    assert tq % tk == 0 or tk % tq == 0
    in_specs = [pl.BlockSpec((1, 1, rows, kd), lambda b, i: (b, i, 0, 0))]
    args = [q]
    if bias is not None:
        in_specs.append(pl.BlockSpec((1, tq, bias.shape[2]), lambda b, i: (b, i, 0)))
        args.append(bias)
    in_specs += [pl.BlockSpec((1, k.shape[1], s), lambda b, i: (b, 0, 0)),
                 pl.BlockSpec((1, s, LANES), lambda b, i: (b, 0, 0))]
    if grouped_out is None:
        out_spec = pl.BlockSpec((1, 1, rows, HEAD_DIM), lambda b, i: (b, i, 0, 0))
        out_shape = jax.ShapeDtypeStruct((bh, nq, rows, HEAD_DIM), BF16)
    else:
        out_spec, out_shape = _grouped_out(grouped_out, nq, tq, rows)
    return pl.pallas_call(
        functools.partial(_flash_kernel, tq=tq, tk=tk, with_bias=bias is not None, grouped=grouped_out is not None),
        grid=(bh, nq),
        in_specs=in_specs,
        out_specs=out_spec,
        out_shape=out_shape,
        scratch_shapes=[pltpu.VMEM((rows, LANES), F32), pltpu.VMEM((rows, LANES), F32)],
        compiler_params=_cp("parallel", "arbitrary"),
        name=name,
    )(*args, k, v)


def _grouped_out(batch_groups, nq, tq, rows):
    batch, groups = batch_groups
    width = rows // tq * HEAD_DIM
    spec = pl.BlockSpec((tq, width), lambda b, i: ((b // groups) * nq + i, lax.rem(b, groups)))
    return spec, jax.ShapeDtypeStruct((batch * nq * tq, groups * width), BF16)


def _window_kernel(q_ref, k_ref, v_ref, o_ref, *, tq, window):
    i = pl.program_id(1)
    q = q_ref[0, 0]
    rows = q.shape[0]
    span = window + tq
    ks = pl.multiple_of(jnp.maximum(i * tq - window, 0), tq)
    q_pos = i * tq + jnp.bitwise_and(lax.broadcasted_iota(I32, (rows, 1), 0), tq - 1)
    k_pos = ks + lax.broadcasted_iota(I32, (1, span), 1)
    ok = (q_pos - k_pos).astype(jnp.uint32) < window
    s = jnp.where(ok, _dot(q, k_ref[0, :, pl.ds(ks, span)]), NEG)
    p = jnp.exp2(s - jnp.max(s, axis=-1, keepdims=True)).astype(BF16)
    acc = _dot(p, v_ref[0, pl.ds(ks, span), :])
    _store_heads(o_ref, acc[:, :HEAD_DIM] / acc[:, HEAD_DIM:HEAD_DIM + 1], tq, True)


def _window_attention(q, k, v, tq, window, batch_groups):
    bh, nq, rows, kd = q.shape
    s = k.shape[2]
    assert window % tq == 0 and window + tq <= s
    out_spec, out_shape = _grouped_out(batch_groups, nq, tq, rows)
    return pl.pallas_call(
        functools.partial(_window_kernel, tq=tq, window=window),
        grid=(bh, nq),
        in_specs=[pl.BlockSpec((1, 1, rows, kd), lambda b, i: (b, i, 0, 0)),
                  pl.BlockSpec((1, kd, s), lambda b, i: (b, 0, 0)),
                  pl.BlockSpec((1, s, LANES), lambda b, i: (b, 0, 0))],
        out_specs=out_spec,
        out_shape=out_shape,
        compiler_params=_cp("parallel", "parallel"),
        name="nsa_window",
    )(q, k, v)


def _oproj_kernel(o_ref, w_ref, h_ref, g_ref, out_ref, *, tiles_per_batch):
    b = pl.program_id(0) // tiles_per_batch
    out_ref[...] = h_ref[...] + g_ref[pl.ds(b, 1), :] * _dot(o_ref[...], w_ref[...])


def _oproj_residual(o, w, h, ada, gate_blk, seq):
    t, d = h.shape
    tm = ROW_TILE
    return pl.pallas_call(
        functools.partial(_oproj_kernel, tiles_per_batch=seq // tm),
        grid=(t // tm,),
        in_specs=[pl.BlockSpec((tm, o.shape[1]), lambda i: (i, 0)),
                  pl.BlockSpec(w.shape, lambda i: (0, 0)),
                  pl.BlockSpec((tm, d), lambda i: (i, 0)),
                  pl.BlockSpec((SUBLANES, d), lambda i: (0, gate_blk))],
        out_specs=pl.BlockSpec((tm, d), lambda i: (i, 0)),
        out_shape=jax.ShapeDtypeStruct((t, d), F32),
        compiler_params=_cp("parallel"),
        name="oproj_residual",
    )(o, w, h, ada)


def _seg_rms(x, seg_ones, gain_tiled):
    ss = _dot((x * x).astype(BF16), seg_ones)
    return x * lax.rsqrt(ss * (1.0 / HEAD_DIM) + EPS) * gain_tiled


def _rope(x, cos_t, sin_t):
    width = x.shape[1]
    reps = width // LANES
    lane = jnp.bitwise_and(lax.broadcasted_iota(I32, x.shape, 1), HEAD_DIM - 1)
    half = ROPE_DIM // 2
    partner = jnp.where(lane < half, pltpu.roll(x, width - half, 1), pltpu.roll(x, half, 1))
    return x * jnp.tile(cos_t, (1, reps)) + partner * jnp.tile(sin_t, (1, reps))


def _nsa_q_kernel(p_ref, z_ref, cos_ref, sin_ref, gain_ref, seg_ref, bg_ref, qp_ref, qr_ref, g_ref):
    q = _seg_rms(p_ref[...], seg_ref[...], gain_ref[...]) * QSCALE
    qp_ref[...] = q.astype(BF16)
    qr_ref[...] = _rope(q, cos_ref[...], sin_ref[...]).astype(BF16)
    g_ref[...] = _sigmoid(z_ref[...] + bg_ref[...])


def _nsa_q_prep(proj, cos_t, sin_t, q_gain, b_gate_pad):
    t = proj.shape[0]
    hd = N_HEADS_B * HEAD_DIM
    tm = ROW_TILE
    seg = (jnp.arange(hd)[:, None] // HEAD_DIM == jnp.arange(hd)[None, :] // HEAD_DIM).astype(BF16)
    row = lambda i: (i, 0)
    fix = lambda i: (0, 0)
    return pl.pallas_call(
        _nsa_q_kernel,
        grid=(t // tm,),
        in_specs=[pl.BlockSpec((tm, hd), row),
                  pl.BlockSpec((tm, LANES), lambda i: (i, hd // LANES)),
                  pl.BlockSpec((tm, LANES), row), pl.BlockSpec((tm, LANES), row),
                  pl.BlockSpec((1, hd), fix), pl.BlockSpec((hd, hd), fix), pl.BlockSpec((1, LANES), fix)],
        out_specs=[pl.BlockSpec((tm, hd), row), pl.BlockSpec((tm, hd), row), pl.BlockSpec((tm, LANES), row)],
        out_shape=[jax.ShapeDtypeStruct((t, hd), BF16), jax.ShapeDtypeStruct((t, hd), BF16),
                   jax.ShapeDtypeStruct((t, LANES), F32)],
        compiler_params=_cp("parallel"),
        name="nsa_q_prep",
    )(proj, proj, cos_t, sin_t, jnp.tile(q_gain, N_HEADS_B).reshape(1, hd), seg, b_gate_pad)


def _nsa_kv_kernel(kv_ref, cos_ref, sin_ref, g1_ref, g2_ref, seg_ref, ks_ref, vs_ref, kw_ref, vw_ref):
    gw = N_KV_GROUPS_B * HEAD_DIM
    ks = _seg_rms(kv_ref[:, 2 * gw:3 * gw], seg_ref[...], g1_ref[...])
    kw = _seg_rms(kv_ref[:, 4 * gw:5 * gw], seg_ref[...], g2_ref[...])
    ks_ref[...] = _rope(ks, cos_ref[...], sin_ref[...]).astype(BF16)
    kw_ref[...] = _rope(kw, cos_ref[...], sin_ref[...]).astype(BF16)
    vs_ref[...] = kv_ref[:, 3 * gw:4 * gw].astype(BF16)
    vw_ref[...] = kv_ref[:, 5 * gw:6 * gw].astype(BF16)


def _nsa_kv_prep(kv, cos_t, sin_t, k_gains):
    t, n = kv.shape
    gw = N_KV_GROUPS_B * HEAD_DIM
    tm = ROW_TILE
    seg = (jnp.arange(gw)[:, None] // HEAD_DIM == jnp.arange(gw)[None, :] // HEAD_DIM).astype(BF16)
    row = lambda i: (i, 0)
    fix = lambda i: (0, 0)
    out = jax.ShapeDtypeStruct((t, gw), BF16)
    return pl.pallas_call(
        _nsa_kv_kernel,
        grid=(t // tm,),
        in_specs=[pl.BlockSpec((tm, n), row), pl.BlockSpec((tm, LANES), row), pl.BlockSpec((tm, LANES), row),
                  pl.BlockSpec((1, gw), fix), pl.BlockSpec((1, gw), fix), pl.BlockSpec((gw, gw), fix)],
        out_specs=[pl.BlockSpec((tm, gw), row)] * 4,
        out_shape=[out] * 4,
        compiler_params=_cp("parallel"),
        name="nsa_kv_prep",
    )(kv, cos_t, sin_t, jnp.tile(k_gains[1], N_KV_GROUPS_B).reshape(1, gw),
      jnp.tile(k_gains[2], N_KV_GROUPS_B).reshape(1, gw), seg)


def _compress_kernel(x_ref, pe_ref, w1_ref, w2_ref, g_ref, o_ref, *, norm):
    x = (x_ref[...] + pe_ref[...]).astype(BF16)
    y = _dot(_silu(_dot(x, w1_ref[...])).astype(BF16), w2_ref[...])
    if norm:
        y = _head_rms(y, g_ref[...])
    o_ref[...] = y.astype(o_ref.dtype)


def _compress(x, pe_flat, w1, w2, gain, norm, name):
    r, kdim = x.shape
    tm = min(ROW_TILE, r)
    fix = lambda i: (0, 0)
    return pl.pallas_call(
        functools.partial(_compress_kernel, norm=norm),
        grid=(r // tm,),
        in_specs=[pl.BlockSpec((tm, kdim), lambda i: (i, 0)), pl.BlockSpec((1, kdim), fix),
                  pl.BlockSpec(w1.shape, fix), pl.BlockSpec(w2.shape, fix), pl.BlockSpec((1, HEAD_DIM), fix)],
        out_specs=pl.BlockSpec((tm, HEAD_DIM), lambda i: (i, 0)),
        out_shape=jax.ShapeDtypeStruct((r, HEAD_DIM), BF16),
        compiler_params=_cp("parallel"),
        name=name,
    )(x, pe_flat, w1, w2, gain.reshape(1, HEAD_DIM))


def _cmp_select_kernel(q_ref, k_ref, v_ref, o_ref, nb_ref, *, tq, n_sel):
    i = pl.program_id(1)
    q = q_ref[0, 0]
    rows = q.shape[0]
    nblk = k_ref.shape[1]
    logits = _dot_nt(q, k_ref[0])
    t_pos = i * tq + jnp.bitwise_and(lax.broadcasted_iota(I32, (rows, 1), 0), tq - 1)
    blk = lax.broadcasted_iota(I32, (1, nblk), 1)
    vis = (blk + 1) * CMP_BLOCK - 1 <= t_pos
    lm = jnp.where(vis, logits, NEG)
    e = jnp.exp2(lm - jnp.max(lm, axis=-1, keepdims=True))
    p = jnp.where(vis, e / jnp.sum(e, axis=-1, keepdims=True), 0.0)
    _store_heads(o_ref, _dot(p.astype(BF16), v_ref[0])[:, :HEAD_DIM], tq, True)

    imp = p[0:tq]
    for hh in range(1, rows // tq):
        imp = imp + p[hh * tq:(hh + 1) * tq]
    cur = lax.shift_right_logical(t_pos[0:tq], CMP_BLOCK.bit_length() - 1)
    forced = jnp.logical_or(blk == 0, jnp.logical_or(blk == cur, blk == cur - 1))
    cand = jnp.where(jnp.logical_or(forced, blk > cur), -jnp.inf, imp)
    sel = jnp.where(forced, 1.0, 0.0)
    blk_f = blk.astype(F32)
    for _ in range(max(n_sel - 3, 0)):
        best = jnp.max(cand, axis=-1, keepdims=True)
        first = jnp.min(jnp.where(cand == best, blk_f, float(nblk)), axis=-1, keepdims=True)
        hit = blk_f == first
        sel = jnp.where(jnp.logical_and(hit, best > -jnp.inf), 1.0, sel)
        cand = jnp.where(hit, -jnp.inf, cand)
    nb_ref[0] = jnp.where(sel > 0.5, 0.0, NEG).astype(nb_ref.dtype)


def _cmp_select(q, k_cmp, v_cmp, seq, tq, batch_groups):
    bg, nq, rows, _ = q.shape
    nblk = k_cmp.shape[1]
    assert min(N_SELECT, nblk) >= 3
    out_spec, out_shape = _grouped_out(batch_groups, nq, tq, rows)
    return pl.pallas_call(
        functools.partial(_cmp_select_kernel, tq=tq, n_sel=min(N_SELECT, nblk)),
        grid=(bg, nq),
        in_specs=[pl.BlockSpec((1, 1, rows, HEAD_DIM), lambda b, i: (b, i, 0, 0)),
                  pl.BlockSpec((1, nblk, HEAD_DIM), lambda b, i: (b, 0, 0)),
                  pl.BlockSpec((1, nblk, LANES), lambda b, i: (b, 0, 0))],
        out_specs=[out_spec, pl.BlockSpec((1, tq, nblk), lambda b, i: (b, i, 0))],
        out_shape=[out_shape, jax.ShapeDtypeStruct((bg, seq, nblk), BF16)],
        compiler_params=_cp("parallel", "parallel"),
        name="nsa_cmp_select",
    )(q, k_cmp, v_cmp)


def _nsa_out_kernel(oc_ref, os_ref, ow_ref, g_ref, e_ref, w_ref, h_ref, ada_ref, out_ref, *, tiles_per_batch):
    b = pl.program_id(0) // tiles_per_batch
    g_hi, g_lo = _split2(g_ref[...])
    o = None
    for j, o_ref in enumerate((oc_ref, os_ref, ow_ref)):
        gate = _dot(g_hi, e_ref[j]) + _dot(g_lo, e_ref[j])
        term = gate * o_ref[...].astype(F32)
        o = term if o is None else o + term
    out_ref[...] = h_ref[...] + ada_ref[pl.ds(b, 1), :] * _dot(o.astype(BF16), w_ref[...])


def _nsa_out(o_cmp, o_sel, o_win, gates, w_o, h, ada, gate_blk, seq):
    t, d = h.shape
    hd = o_cmp.shape[1]
    tm = ROW_TILE
    col = jnp.arange(LANES)[:, None]
    lane = jnp.arange(hd)[None, :]
    expand = jnp.stack([(col == (lane // HEAD_DIM) * 3 + j) for j in range(3)]).astype(BF16)
    row = lambda i: (i, 0)
    return pl.pallas_call(
        functools.partial(_nsa_out_kernel, tiles_per_batch=seq // tm),
        grid=(t // tm,),
        in_specs=[pl.BlockSpec((tm, hd), row), pl.BlockSpec((tm, hd), row), pl.BlockSpec((tm, hd), row),
                  pl.BlockSpec((tm, LANES), row),
                  pl.BlockSpec((3, LANES, hd), lambda i: (0, 0, 0)),
                  pl.BlockSpec(w_o.shape, lambda i: (0, 0)),
                  pl.BlockSpec((tm, d), row),
                  pl.BlockSpec((SUBLANES, d), lambda i: (0, gate_blk))],
        out_specs=pl.BlockSpec((tm, d), row),
        out_shape=jax.ShapeDtypeStruct((t, d), F32),
        compiler_params=_cp("parallel"),
        name="nsa_out",
    )(o_cmp, o_sel, o_win, gates, expand, w_o, h, ada)


def _route_kernel(h_ref, gain_ref, shift_ref, scale_ref, wr_ref, br_ref, tri_ref, upper_ref,
                  u_ref, pos_ref, w_ref, nu_ref, off_ref, *, tiles_per_batch):
    b = (pl.program_id(0) * ROUTE_TILES) // tiles_per_batch
    shift, scale = shift_ref[pl.ds(b, 1), :], scale_ref[pl.ds(b, 1), :]
    for tile in range(ROUTE_TILES):
        rows = slice(tile * MOE_TOK_TILE, (tile + 1) * MOE_TOK_TILE)
        _route_tile(h_ref[rows, :], gain_ref[...], shift, scale, wr_ref, br_ref, tri_ref, upper_ref,
                    u_ref.at[rows, :], pos_ref.at[rows, :], w_ref.at[rows, :], nu_ref.at[tile], off_ref.at[tile])


def _route_tile(h, gain, shift, scale, wr_ref, br_ref, tri_ref, upper_ref, u_ref, pos_ref, w_ref, nu_ref, off_ref):
    u = _norm_mod(h, gain, shift, scale)
    u_ref[...] = u.astype(BF16)
    scores = _sigmoid(_dot_f32(u, wr_ref[...]))
    lane = lax.broadcasted_iota(I32, scores.shape, 1)
    lane_f = lane.astype(F32)
    biased = jnp.where(lane < N_EXPERTS, scores + br_ref[...], -jnp.inf)
    chosen = jnp.zeros(scores.shape, F32)
    hits, val = [], []
    upper = upper_ref[...]
    for _ in range(TOP_K):
        best = jnp.max(biased, axis=-1, keepdims=True)
        first = jnp.min(jnp.where(biased == best, lane_f, float(LANES)), axis=-1, keepdims=True)
        hit = lane_f == first
        hits.append(hit)
        val.append(jnp.sum(jnp.where(hit, scores, 0.0), axis=-1, keepdims=True))
        biased = jnp.where(hit, -jnp.inf, biased)
        chosen = jnp.where(hit, 1.0, chosen)
    top_s = jnp.concatenate(val, axis=1)
    w_ref[...] = top_s / jnp.sum(top_s, axis=-1, keepdims=True) * ROUTED_SCALE

    before = _dot(tri_ref[...], chosen.astype(BF16))
    count = jnp.sum(chosen, axis=0, keepdims=True)
    units = jnp.floor((count + (MOE_UNIT - 1)) * (1.0 / MOE_UNIT))
    padded = jnp.broadcast_to(units * MOE_UNIT, (SUBLANES, LANES))
    offset = _dot(padded.astype(BF16), upper)[0:1]
    row_of = before + offset
    pos = [jnp.sum(jnp.where(hit, row_of, 0.0), axis=-1, keepdims=True) for hit in hits]
    pos_ref[...] = jnp.concatenate(pos, axis=1).astype(I32)
    nu_ref[...] = units.astype(I32)
    off_ref[...] = offset.astype(I32)


def _route(h, gain, ada, shift_blk, scale_blk, w_router_pad, b_router_pad, seq):
    t, d = h.shape
    tt = MOE_TOK_TILE
    n_tiles = t // tt
    rt = ROUTE_TILES * tt
    assert n_tiles % ROUTE_TILES == 0 and seq % rt == 0
    row = lambda i: (i, 0)
    fix = lambda i: (0, 0)
    tri = (jnp.arange(tt)[:, None] > jnp.arange(tt)[None, :]).astype(BF16)
    upper = (jnp.arange(LANES)[:, None] < jnp.arange(LANES)[None, :]).astype(BF16)
    per_tile = jax.ShapeDtypeStruct((n_tiles, 1, LANES), I32)
    per_tile_spec = pl.BlockSpec((ROUTE_TILES, 1, LANES), lambda i: (i, 0, 0))
    return pl.pallas_call(
        functools.partial(_route_kernel, tiles_per_batch=seq // tt),
        grid=(n_tiles // ROUTE_TILES,),
        in_specs=[pl.BlockSpec((rt, d), row), pl.BlockSpec((1, d), fix),
                  pl.BlockSpec((SUBLANES, d), lambda i: (0, shift_blk)),
                  pl.BlockSpec((SUBLANES, d), lambda i: (0, scale_blk)),
                  pl.BlockSpec((d, LANES), fix), pl.BlockSpec((1, LANES), fix),
                  pl.BlockSpec((tt, tt), fix), pl.BlockSpec((LANES, LANES), fix)],
        out_specs=[pl.BlockSpec((rt, d), row), pl.BlockSpec((rt, TOP_K), row), pl.BlockSpec((rt, TOP_K), row),
                   per_tile_spec, per_tile_spec],
        out_shape=[jax.ShapeDtypeStruct((t, d), BF16), jax.ShapeDtypeStruct((t, TOP_K), I32),
                   jax.ShapeDtypeStruct((t, TOP_K), F32), per_tile, per_tile],
        compiler_params=_cp("parallel"),
        name="moe_route",
    )(h, gain.reshape(1, d), ada, ada, w_router_pad, b_router_pad, tri, upper)


PLAN_SLOTS = 256
PLAN_TILES = LANES


def _plan_kernel(nu_ref, off_ref, upper_ref, lower_ref, row_ref, exp_ref, nused_ref,
                 *, n_tiles, r_loc, upb, n_unit_slots, n_dump_units):
    nu = nu_ref[...]
    nu_t = nu.T
    off_t = off_ref[...].T
    upper = upper_ref[...]
    lower = lower_ref[...]
    lane = lax.broadcasted_iota(I32, (PLAN_SLOTS, LANES), 1)
    lane_f = lane.astype(F32)

    def exact_dot(a, b_bf16, left):
        parts = _split3(a)
        return sum(_dot(b_bf16, p) if left else _dot(p, b_bf16) for p in parts)

    units_before_t = _dot(nu_t.astype(BF16), upper)
    count_col = jnp.sum(nu_t, axis=1, keepdims=True)
    count_row = jnp.sum(nu, axis=0, keepdims=True)
    padded_col = jnp.floor((count_col + (upb - 1)) * (1.0 / upb)) * upb
    padded_row = jnp.floor((count_row + (upb - 1)) * (1.0 / upb)) * upb
    start_col = exact_dot(jnp.broadcast_to(padded_col, (LANES, LANES)), lower, True)
    real_col = exact_dot(jnp.broadcast_to(count_col, (LANES, LANES)), lower, True)
    start_row = exact_dot(jnp.broadcast_to(padded_row, (SUBLANES, LANES)), upper, False)[0:1]
    seg_start_t = start_col + units_before_t
    seg_real_t = real_col + units_before_t
    tile_id = lax.broadcasted_iota(I32, (LANES, LANES), 1).astype(F32)
    seg_row_t = tile_id * r_loc + off_t

    idx = pl.program_id(0) * PLAN_SLOTS + lax.broadcasted_iota(I32, (PLAN_SLOTS, 1), 0)
    slot = jnp.where(idx < n_unit_slots, idx, (idx - n_unit_slots) * upb).astype(F32)

    in_e = jnp.logical_and(start_row <= slot, lane < N_EXPERTS)
    expert = jnp.sum(jnp.where(in_e, 1.0, 0.0), axis=-1, keepdims=True) - 1.0
    pick_e = (lane_f == expert).astype(BF16)
    look = lambda tbl: exact_dot(tbl, pick_e, True)
    starts = look(seg_start_t)
    in_t = jnp.logical_and(starts <= slot, lane < n_tiles)
    tile = jnp.sum(jnp.where(in_t, 1.0, 0.0), axis=-1, keepdims=True) - 1.0
    at = lambda x: jnp.sum(jnp.where(lane_f == tile, x, 0.0), axis=-1, keepdims=True)
    m = slot - at(starts)
    n = at(look(nu_t))
    row = at(look(seg_row_t)) + m * MOE_UNIT
    pad_idx = slot - (at(look(seg_real_t)) + jnp.minimum(m, n))
    dump = n_tiles * r_loc + jnp.minimum(pad_idx, n_dump_units - 1.0) * MOE_UNIT
    row_ref[...] = jnp.where(m < n, row, dump).astype(I32)
    exp_ref[...] = expert.astype(I32)
    total = jnp.sum(padded_row, axis=-1, keepdims=True) * (1.0 / upb)
    nused_ref[...] = jnp.broadcast_to(total, (SUBLANES, LANES)).astype(I32)


def _plan(nu, off, n_tiles, r_loc, n_blocks, n_dump_units):
    assert n_tiles <= PLAN_TILES
    upb = MOE_BLOCK // MOE_UNIT
    n_unit_slots = -(-(n_blocks * upb) // PLAN_SLOTS) * PLAN_SLOTS
    n_steps = n_unit_slots // PLAN_SLOTS + -(-(n_blocks + 2) // PLAN_SLOTS)
    pad = lambda x: jnp.pad(x.astype(F32), ((0, PLAN_TILES - n_tiles), (0, 0)))
    upper = (jnp.arange(LANES)[:, None] < jnp.arange(LANES)[None, :]).astype(BF16)
    fix = lambda s: (0, 0)
    col = pl.BlockSpec((PLAN_SLOTS, 1), lambda s: (s, 0))
    rows, experts, n_used = pl.pallas_call(
        functools.partial(_plan_kernel, n_tiles=n_tiles, r_loc=r_loc, upb=upb, n_unit_slots=n_unit_slots,
                          n_dump_units=n_dump_units),
        grid=(n_steps,),
        in_specs=[pl.BlockSpec((PLAN_TILES, LANES), fix), pl.BlockSpec((PLAN_TILES, LANES), fix),
                  pl.BlockSpec((LANES, LANES), fix), pl.BlockSpec((LANES, LANES), fix)],
        out_specs=[col, col, pl.BlockSpec((SUBLANES, LANES), fix)],
        out_shape=[jax.ShapeDtypeStruct((n_steps * PLAN_SLOTS, 1), I32),
                   jax.ShapeDtypeStruct((n_steps * PLAN_SLOTS, 1), I32),
                   jax.ShapeDtypeStruct((SUBLANES, LANES), I32)],
        compiler_params=_cp("arbitrary"),
        name="moe_plan",
    )(pad(nu), pad(off), upper, upper.T)
    return (rows[:n_blocks * upb, 0], experts[n_unit_slots:n_unit_slots + n_blocks + 2, 0], n_used[0, :1])


def _one_hot_t(pos_ref, weight_ref, n_rows):
    tt = pos_ref.shape[0]
    col = lax.broadcasted_iota(I32, (tt, n_rows), 1)
    acc = jnp.zeros((tt, n_rows), F32)
    for k in range(TOP_K):
        acc = jnp.where(col == pos_ref[:, k:k + 1], 1.0 if weight_ref is None else weight_ref[:, k:k + 1], acc)
    return acc.astype(BF16)


def _dispatch_kernel(u_ref, pos_ref, x_ref):
    x_ref[0] = _dot_tn(_one_hot_t(pos_ref, None, x_ref.shape[1]), u_ref[...]).astype(x_ref.dtype)


def _dispatch(u, lpos, n_tiles_total, r_loc):
    t, d = u.shape
    tt = MOE_TOK_TILE
    n_real = t // tt
    return pl.pallas_call(
        _dispatch_kernel,
        grid=(n_tiles_total,),
        in_specs=[pl.BlockSpec((tt, d), lambda i: (jnp.minimum(i, n_real - 1), 0)),
                  pl.BlockSpec((tt, TOP_K), lambda i: (i, 0))],
        out_specs=pl.BlockSpec((1, r_loc, d), lambda i: (i, 0, 0)),
        out_shape=jax.ShapeDtypeStruct((n_tiles_total, r_loc, d), BF16),
        compiler_params=_cp("parallel"),
        name="moe_dispatch",
    )(u, lpos)


def _expert_kernel(tbl_ref, be_ref, nblk_ref, x_hbm, wg_ref, wu_ref, wd_ref, y_hbm,
                   xbuf, ybuf, wg_bf, wu_bf, wd_bf, gsem, ssem, *, upb):
    b = pl.program_id(0)
    nblk = nblk_ref[0]

    def gather(blk, sl):
        for m in range(upb):
            row = pl.multiple_of(tbl_ref[blk * upb + m], MOE_UNIT)
            yield pltpu.make_async_copy(x_hbm.at[pl.ds(row, MOE_UNIT), :],
                                        xbuf.at[sl, pl.ds(m * MOE_UNIT, MOE_UNIT), :], gsem.at[sl, m])

    def scatter(blk, sl):
        for m in range(upb):
            row = pl.multiple_of(tbl_ref[blk * upb + m], MOE_UNIT)
            yield pltpu.make_async_copy(ybuf.at[sl, pl.ds(m * MOE_UNIT, MOE_UNIT), :],
                                        y_hbm.at[pl.ds(row, MOE_UNIT), :], ssem.at[sl, m])

    @pl.when(jnp.logical_and(b == 0, nblk > 0))
    def _():
        for cp in gather(0, 0):
            cp.start()

    @pl.when(jnp.logical_or(b == 0, be_ref[b] != be_ref[jnp.maximum(b - 1, 0)]))
    def _():
        wg_bf[...] = wg_ref[0].astype(BF16)
        wu_bf[...] = wu_ref[0].astype(BF16)
        wd_bf[...] = wd_ref[0].astype(BF16)

    slot = lax.rem(b, 2)

    @pl.when(b + 1 < nblk)
    def _():
        for cp in gather(b + 1, 1 - slot):
            cp.start()

    @pl.when(jnp.logical_and(b >= 2, b < nblk + 2))
    def _():
        for cp in scatter(b - 2, slot):
            cp.wait()

    @pl.when(b < nblk)
    def _():
        for cp in gather(b, slot):
            cp.wait()
        x = xbuf[slot]
        hid = _silu(_dot(x, wg_bf[...])) * _dot(x, wu_bf[...])
        ybuf[slot] = _dot(hid.astype(BF16), wd_bf[...]).astype(ybuf.dtype)
        for cp in scatter(b, slot):
            cp.start()


def _experts(x_loc, unit_rows, block_e, n_blocks_used, w_gate, w_up, w_down, n_blocks):
    rows, d = x_loc.shape
    de = w_gate.shape[2]
    upb = MOE_BLOCK // MOE_UNIT
    grid_spec = pltpu.PrefetchScalarGridSpec(
        num_scalar_prefetch=3,
        grid=(n_blocks + 2,),
        in_specs=[pl.BlockSpec(memory_space=pl.ANY),
                  pl.BlockSpec((1, d, de), lambda b, tbl, be, nb: (be[b], 0, 0)),
                  pl.BlockSpec((1, d, de), lambda b, tbl, be, nb: (be[b], 0, 0)),
                  pl.BlockSpec((1, de, d), lambda b, tbl, be, nb: (be[b], 0, 0))],
        out_specs=pl.BlockSpec(memory_space=pl.ANY),
        scratch_shapes=[pltpu.VMEM((2, MOE_BLOCK, d), BF16), pltpu.VMEM((2, MOE_BLOCK, d), BF16),
                        pltpu.VMEM((d, de), BF16), pltpu.VMEM((d, de), BF16), pltpu.VMEM((de, d), BF16),
                        pltpu.SemaphoreType.DMA((2, upb)), pltpu.SemaphoreType.DMA((2, upb))],
    )
    return pl.pallas_call(
        functools.partial(_expert_kernel, upb=upb),
        grid_spec=grid_spec,
        out_shape=jax.ShapeDtypeStruct((rows, d), BF16),
        input_output_aliases={3: 0},
        compiler_params=_cp("arbitrary"),
        name="moe_experts",
    )(unit_rows, block_e, n_blocks_used, x_loc, w_gate, w_up, w_down)


def _combine_kernel(y_ref, pos_ref, tw_ref, u_ref, wsg_ref, wsu_ref, wsd_ref, h_ref, g_ref, out_ref, *, tiles_per_batch):
    b = pl.program_id(0) // tiles_per_batch
    routed = _dot(_one_hot_t(pos_ref, tw_ref, y_ref.shape[1]), y_ref[0])
    u = u_ref[...]
    hid = _silu(_dot(u, wsg_ref[...])) * _dot(u, wsu_ref[...])
    shared = _dot(hid.astype(BF16), wsd_ref[...])
    out_ref[...] = h_ref[...] + g_ref[pl.ds(b, 1), :] * (routed + shared)


def _combine(y_loc, lpos, top_w, u, w_sg, w_su, w_sd, h, ada, gate_blk, seq):
    t, d = h.shape
    tt = MOE_TOK_TILE
    r_loc = y_loc.shape[1]
    row = lambda i: (i, 0)
    fix = lambda i: (0, 0)
    return pl.pallas_call(
        functools.partial(_combine_kernel, tiles_per_batch=seq // tt),
        grid=(t // tt,),
        in_specs=[pl.BlockSpec((1, r_loc, d), lambda i: (i, 0, 0)),
                  pl.BlockSpec((tt, TOP_K), row), pl.BlockSpec((tt, TOP_K), row),
                  pl.BlockSpec((tt, d), row),
                  pl.BlockSpec(w_sg.shape, fix), pl.BlockSpec(w_su.shape, fix), pl.BlockSpec(w_sd.shape, fix),
                  pl.BlockSpec((tt, d), row),
                  pl.BlockSpec((SUBLANES, d), lambda i: (0, gate_blk))],
        out_specs=pl.BlockSpec((tt, d), row),
        out_shape=jax.ShapeDtypeStruct((t, d), F32),
        compiler_params=_cp("parallel"),
        name="moe_combine",
    )(y_loc, lpos, top_w, u, w_sg, w_su, w_sd, h, ada)


def _moe_ffn(h, gain, ada, blk0, w_router, b_router, w_eg, w_eu, w_ed, w_sg, w_su, w_sd, seq):
    t, d = h.shape
    tt, unit, upb = MOE_TOK_TILE, MOE_UNIT, MOE_BLOCK // MOE_UNIT
    wr = jnp.pad(w_router, ((0, 0), (0, LANES - N_EXPERTS)))
    br = jnp.pad(b_router.astype(F32), (0, LANES - N_EXPERTS)).reshape(1, LANES)
    u, lpos, top_w, nu, off = _route(h, gain, ada, blk0, blk0 + 1, wr, br, seq)

    n_tok_tiles = t // tt
    r_loc = -(-(tt * TOP_K + N_EXPERTS * (unit - 1)) // LANES) * LANES
    n_dummy_tiles = -(-(N_EXPERTS * (upb - 1) * unit) // r_loc)
    max_units = n_tok_tiles * (tt * TOP_K // unit + N_EXPERTS)
    n_blocks = -(-max_units // upb) + N_EXPERTS
    unit_rows, block_e, n_used = _plan(nu[:, 0, :], off[:, 0, :], n_tok_tiles, r_loc, n_blocks,
                                       n_dummy_tiles * r_loc // unit)
    lpos = jnp.concatenate([lpos, jnp.full((n_dummy_tiles * tt, TOP_K), -1, I32)], axis=0)

    x_loc = _dispatch(u, lpos, n_tok_tiles + n_dummy_tiles, r_loc)
    y_loc = _experts(x_loc.reshape(-1, d), unit_rows, block_e, n_used, w_eg, w_eu, w_ed, n_blocks)
    y_loc = y_loc.reshape(n_tok_tiles + n_dummy_tiles, r_loc, d)
    return _combine(y_loc, lpos, top_w, u, w_sg.astype(BF16), w_su.astype(BF16), w_sd.astype(BF16),
                    h, ada, blk0 + 2, seq)


def _fox_layer(h, ada, norm_gain, w_in, b_f, q_gain, k_gain, w_o, batch, seq):
    t, d = h.shape
    hd = N_HEADS_A * HEAD_DIM
    w = jnp.pad(w_in, ((0, 0), (0, MXU_COLS - N_HEADS_A))).astype(BF16)
    proj = _norm_mod_matmul(h, norm_gain, ada, 0, 1, w, seq, w.shape[1], "fox_in_proj")
    z = proj[:, 3 * hd:3 * hd + N_HEADS_A].reshape(batch, seq, N_HEADS_A)
    z = z.transpose(1, 0, 2).reshape(seq, batch * N_HEADS_A)
    cum = _logsig_cumsum(z, jnp.tile(b_f.astype(F32), batch).reshape(1, -1))
    q, k, v = _fox_prep(proj, cum, q_gain, k_gain, batch, seq)
    bh = batch * N_HEADS_A
    o = _flash(q.reshape(bh, seq // FOX_TQ, FOX_TQ, LANES), k.reshape(bh, LANES, seq),
               v.reshape(bh, seq, LANES), FOX_TQ, FOX_TK, "fox_attention")
    o = o.reshape(batch, N_HEADS_A, seq, HEAD_DIM).transpose(0, 2, 1, 3).reshape(t, hd)
    return _oproj_residual(o, w_o.astype(BF16), h, ada, 2, seq)


def _rope_tables(positions):
    inv_freq = ROPE_THETA ** (-jnp.arange(0, ROPE_DIM, 2, dtype=F32) / ROPE_DIM)
    ang = positions.astype(F32).reshape(-1, 1) * inv_freq
    half = ROPE_DIM // 2
    cos, sin = jnp.cos(ang), jnp.sin(ang)
    rest = HEAD_DIM - ROPE_DIM
    cos_h = jnp.concatenate([cos, cos, jnp.ones((ang.shape[0], rest), F32)], axis=1)
    sin_h = jnp.concatenate([-sin, sin, jnp.zeros((ang.shape[0], rest), F32)], axis=1)
    del half
    return jnp.tile(cos_h, (1, LANES // HEAD_DIM)), jnp.tile(sin_h, (1, LANES // HEAD_DIM))


def _group_rows(x, batch, seq, tq):
    g, hpg = N_KV_GROUPS_B, HEADS_PER_GROUP
    x = x.reshape(batch, seq // tq, tq, g, hpg, HEAD_DIM).transpose(0, 3, 1, 4, 2, 5)
    return x.reshape(batch * g, seq // tq, hpg * tq, HEAD_DIM)


def _per_group(x, batch, seq):
    g = N_KV_GROUPS_B
    return x.reshape(batch, seq, g, HEAD_DIM).transpose(0, 2, 1, 3).reshape(batch * g, seq, HEAD_DIM)


def _with_ones(v):
    pad = jnp.zeros(v.shape[:-1] + (LANES - HEAD_DIM,), v.dtype).at[..., 0].set(1)
    return jnp.concatenate([v, pad], axis=-1)


def _shared_kv(h, ada_kv, norm_kv, w_kv, pe_k, pe_v, w_ck1, w_ck2, w_cv1, w_cv2, k_gains, cos_t, sin_t, batch, seq):
    g = N_KV_GROUPS_B
    gw = g * HEAD_DIM
    nb = seq // CMP_BLOCK
    kv = _norm_mod_matmul(h, norm_kv, ada_kv, 0, 1, w_kv.astype(BF16), seq, w_kv.shape[1], "nsa_kv_proj")
    k_sel, v_sel, k_win, v_win = _nsa_kv_prep(kv, cos_t, sin_t, k_gains)

    def blocks(x):
        x = x.reshape(batch, nb, CMP_BLOCK, g, HEAD_DIM).transpose(0, 1, 3, 2, 4)
        return x.reshape(batch * nb * g, CMP_BLOCK * HEAD_DIM)

    k_cmp = _compress(blocks(kv[:, 0:gw]), pe_k.reshape(1, -1), w_ck1.astype(BF16), w_ck2.astype(BF16),
                      k_gains[0], True, "nsa_compress_k")
    v_cmp = _compress(blocks(kv[:, gw:2 * gw]), pe_v.reshape(1, -1), w_cv1.astype(BF16), w_cv2.astype(BF16),
                      k_gains[0], False, "nsa_compress_v")
    to_bg = lambda x: x.reshape(batch, nb, g, HEAD_DIM).transpose(0, 2, 1, 3).reshape(batch * g, nb, HEAD_DIM)
    k_cmp = to_bg(k_cmp)
    v_cmp = jnp.pad(to_bg(v_cmp), ((0, 0), (0, 0), (0, LANES - HEAD_DIM)))

    def per_group_t(x):
        return x.reshape(batch, seq, g, HEAD_DIM).transpose(0, 2, 3, 1).reshape(batch * g, HEAD_DIM, seq)

    blk_onehot = (jnp.arange(nb)[:, None] == jnp.arange(seq)[None, :] // CMP_BLOCK).astype(BF16)
    k_sel = jnp.concatenate([jnp.broadcast_to(blk_onehot, (batch * g, nb, seq)), per_group_t(k_sel)], axis=1)
    return (k_cmp, v_cmp, k_sel, _with_ones(_per_group(v_sel, batch, seq)),
            per_group_t(k_win), _with_ones(_per_group(v_win, batch, seq)))


def _nsa_layer(h, ada, norm_gain, kv, cos_t, sin_t, w_in, b_gate, q_gain, w_o, batch, seq):
    hd = N_HEADS_B * HEAD_DIM
    k_cmp, v_cmp, k_sel, v_sel, k_win, v_win = kv
    n_gate = 3 * N_HEADS_B
    w = jnp.pad(w_in, ((0, 0), (0, LANES - n_gate))).astype(BF16)
    proj = _norm_mod_matmul(h, norm_gain, ada, 0, 1, w, seq, hd + LANES, "nsa_in_proj")
    bg = jnp.pad(b_gate.astype(F32), (0, LANES - n_gate)).reshape(1, LANES)
    q_plain, q_rot, gates = _nsa_q_prep(proj, cos_t, sin_t, q_gain, bg)

    bgs = (batch, N_KV_GROUPS_B)
    o_cmp, negbias = _cmp_select(_group_rows(q_plain, batch, seq, CMP_TQ), k_cmp, v_cmp, seq, CMP_TQ, bgs)
    o_sel = _flash(_group_rows(q_rot, batch, seq, SEL_TQ), k_sel, v_sel, SEL_TQ, NSA_TK, "nsa_selected",
                   bias=negbias, grouped_out=bgs)
    o_win = _window_attention(_group_rows(q_rot, batch, seq, WIN_TQ), k_win, v_win, WIN_TQ, WINDOW, bgs)
    return _nsa_out(o_cmp, o_sel, o_win, gates, w_o.astype(BF16), h, ada, 2, seq)


def kernel(x, c, positions, w_ada, b_ada, norm_mix, norm_ffn, w_in_a, b_f_a, qnorm_a, knorm_a, w_o_a, norm_kv, w_ada_kv, b_ada_kv, w_kv_b, pe_cmp_k, pe_cmp_v, w_cmp_k1, w_cmp_k2, w_cmp_v1, w_cmp_v2, knorm_b, w_in_b, b_gate_b, qnorm_b, w_o_b, w_router, b_router, w_exp_gate, w_exp_up, w_exp_down, w_sh_gate, w_sh_up, w_sh_down):
    batch, seq, d = x.shape
    depth = w_ada.shape[0]
    n_a = w_in_a.shape[0]
    c_pad = jnp.pad(c, ((0, SUBLANES - batch), (0, 0)))
    cos_t, sin_t = _rope_tables(positions)
    h = x.reshape(batch * seq, d)
    kv = None
    for layer in range(depth):
        ada = _ada(c_pad, w_ada[layer], b_ada[layer])
        if layer < n_a:
            h = _fox_layer(h, ada, norm_mix[layer], w_in_a[layer], b_f_a[layer], qnorm_a[layer],
                           knorm_a[layer], w_o_a[layer], batch, seq)
        else:
            if layer == n_a:
                ada_kv = _ada(c_pad, w_ada_kv, b_ada_kv)
                kv = _shared_kv(h, ada_kv, norm_kv, w_kv_b, pe_cmp_k, pe_cmp_v, w_cmp_k1, w_cmp_k2,
                                w_cmp_v1, w_cmp_v2, knorm_b, cos_t, sin_t, batch, seq)
            j = layer - n_a
            h = _nsa_layer(h, ada, norm_mix[layer], kv, cos_t, sin_t, w_in_b[j], b_gate_b[j], qnorm_b[j],
                           w_o_b[j], batch, seq)
        h = _moe_ffn(h, norm_ffn[layer], ada, 3, w_router[layer], b_router[layer], w_exp_gate[layer],
                     w_exp_up[layer], w_exp_down[layer], w_sh_gate[layer], w_sh_up[layer], w_sh_down[layer], seq)
    return h.reshape(batch, seq, d)
```

```python
import functools

import jax
import jax.numpy as jnp
from jax import lax
from jax.experimental import pallas as pl
from jax.experimental.pallas import tpu as pltpu

F32 = jnp.float32
BF16 = jnp.bfloat16
I32 = jnp.int32

HEAD_DIM = 64
N_HEADS_A = 16
N_HEADS_B = 16
N_KV_GROUPS_B = 4
HEADS_PER_GROUP = N_HEADS_B // N_KV_GROUPS_B
ROPE_THETA = 500000.0
ROPE_DIM = HEAD_DIM // 4
CMP_BLOCK = 64
N_SELECT = 16
WINDOW = 512
N_EXPERTS = 64
TOP_K = 8
ROUTED_SCALE = 2.5
EPS = 1e-6
FORCE_SCORE = 1e4
NEG = -1e30
LOG2E = 1.4426950408889634
QSCALE = HEAD_DIM ** -0.5 * LOG2E

LANES = 128
SUBLANES = 8
BF16_ROWS = 16
MXU_COLS = 256

ROW_TILE = 512
FOX_TQ = 2048
FOX_TK = 512
SEL_TQ = 512
WIN_TQ = 256
NSA_TK = 512
CMP_TQ = 512
MOE_TOK_TILE = 256
ROUTE_TILES = 2
MOE_UNIT = BF16_ROWS
MOE_BLOCK = 512
VMEM_LIMIT = 56 * 1024 * 1024


def _cp(*sem):
    return pltpu.CompilerParams(dimension_semantics=sem, vmem_limit_bytes=VMEM_LIMIT)


def _dot(a, b):
    return jnp.dot(a, b, preferred_element_type=F32)


def _dot_nt(a, b):
    return lax.dot_general(a, b, (((1,), (1,)), ((), ())), preferred_element_type=F32)


def _dot_tn(a, b):
    return lax.dot_general(a, b, (((0,), (0,)), ((), ())), preferred_element_type=F32)


def _split2(x):
    hi = x.astype(BF16)
    lo = (x - hi.astype(F32)).astype(BF16)
    return hi, lo


def _split3(x):
    hi = x.astype(BF16)
    r = x - hi.astype(F32)
    mid = r.astype(BF16)
    lo = (r - mid.astype(F32)).astype(BF16)
    return hi, mid, lo


def _dot_f32(a, b):
    a_hi, a_lo = _split2(a)
    b_hi, b_lo = _split2(b)
    return _dot(a_hi, b_hi) + _dot(a_lo, b_hi) + _dot(a_hi, b_lo)


def _sigmoid(x):
    return 1.0 / (1.0 + jnp.exp(-x))


def _silu(x):
    return x * _sigmoid(x)


def _ada_kernel(c_ref, w_ref, b_ref, o_ref):
    o_ref[...] = _dot_f32(_silu(c_ref[...]), w_ref[...]) + b_ref[...]


def _ada(c_pad, w, b):
    d, n = w.shape
    tn = 1024
    return pl.pallas_call(
        _ada_kernel,
        grid=(n // tn,),
        in_specs=[pl.BlockSpec((SUBLANES, d), lambda j: (0, 0)),
                  pl.BlockSpec((d, tn), lambda j: (0, j)),
                  pl.BlockSpec((1, tn), lambda j: (0, j))],
        out_specs=pl.BlockSpec((SUBLANES, tn), lambda j: (0, j)),
        out_shape=jax.ShapeDtypeStruct((SUBLANES, n), F32),
        compiler_params=_cp("parallel"),
        name="ada",
    )(c_pad, w, b.reshape(1, n))


def _norm_mod(x, gain, shift, scale):
    xn = x * lax.rsqrt(jnp.mean(x * x, axis=-1, keepdims=True) + EPS) * gain
    return xn * (1.0 + scale) + shift


def _nm_kernel(h_ref, gain_ref, shift_ref, scale_ref, w_ref, o_ref, u_scr, *, tiles_per_batch):
    i = pl.program_id(0)

    @pl.when(pl.program_id(1) == 0)
    def _():
        b = i // tiles_per_batch
        u = _norm_mod(h_ref[...], gain_ref[...], shift_ref[pl.ds(b, 1), :], scale_ref[pl.ds(b, 1), :])
        u_scr[...] = u.astype(BF16)

    o_ref[...] = _dot(u_scr[...], w_ref[...]).astype(o_ref.dtype)


def _norm_mod_matmul(h, gain, ada, shift_blk, scale_blk, w, seq, tn, name):
    t, d = h.shape
    n = w.shape[1]
    tm = ROW_TILE
    return pl.pallas_call(
        functools.partial(_nm_kernel, tiles_per_batch=seq // tm),
        grid=(t // tm, n // tn),
        in_specs=[pl.BlockSpec((tm, d), lambda i, j: (i, 0)),
                  pl.BlockSpec((1, d), lambda i, j: (0, 0)),
                  pl.BlockSpec((SUBLANES, d), lambda i, j: (0, shift_blk)),
                  pl.BlockSpec((SUBLANES, d), lambda i, j: (0, scale_blk)),
                  pl.BlockSpec((d, tn), lambda i, j: (0, j))],
        out_specs=pl.BlockSpec((tm, tn), lambda i, j: (i, j)),
        out_shape=jax.ShapeDtypeStruct((t, n), F32),
        scratch_shapes=[pltpu.VMEM((tm, d), BF16)],
        compiler_params=_cp("parallel", "arbitrary"),
        name=name,
    )(h, gain.reshape(1, d), ada, ada, w)


def _logsig_cumsum_kernel(z_ref, b_ref, tri_ref, o_ref, carry):
    @pl.when(pl.program_id(0) == 0)
    def _():
        carry[...] = jnp.zeros_like(carry)

    z = z_ref[...] + b_ref[...]
    lf = jnp.minimum(z, 0.0) - jnp.log1p(jnp.exp(-jnp.abs(z)))
    hi, mid, lo = _split3(lf)
    tri = tri_ref[...]
    c = _dot(tri, hi) + _dot(tri, mid) + _dot(tri, lo) + carry[...]
    o_ref[...] = c
    ts = c.shape[0]
    carry[...] = c[ts - 1:ts, :]


def _logsig_cumsum(z, bias):
    s, c = z.shape
    ts = 256
    tri = (jnp.arange(ts)[:, None] >= jnp.arange(ts)[None, :]).astype(BF16)
    return pl.pallas_call(
        _logsig_cumsum_kernel,
        grid=(s // ts,),
        in_specs=[pl.BlockSpec((ts, c), lambda i: (i, 0)),
                  pl.BlockSpec((1, c), lambda i: (0, 0)),
                  pl.BlockSpec((ts, ts), lambda i: (0, 0))],
        out_specs=pl.BlockSpec((ts, c), lambda i: (i, 0)),
        out_shape=jax.ShapeDtypeStruct((s, c), F32),
        scratch_shapes=[pltpu.VMEM((1, c), F32)],
        compiler_params=_cp("arbitrary"),
        name="fox_logsig_cumsum",
    )(z, bias, tri)


def _head_rms(x, gain):
    return x * lax.rsqrt(jnp.mean(x * x, axis=-1, keepdims=True) + EPS) * gain


def _fox_prep_kernel(q_ref, k_ref, v_ref, cum_ref, qg_ref, kg_ref, qo_ref, ko_ref, vo_ref):
    bh0 = (pl.program_id(0) * (N_HEADS_A // 2) + pl.program_id(1)) * 2
    ts = q_ref.shape[0]
    cum = cum_ref[...]
    lane_c = lax.broadcasted_iota(I32, cum.shape, 1)
    lane = lax.broadcasted_iota(I32, (ts, LANES), 1)
    low = lane < HEAD_DIM
    ext = lane - HEAD_DIM

    def pair_rms(x, gain):
        sq = x * x
        ss = jnp.where(low, jnp.sum(jnp.where(low, sq, 0.0), axis=-1, keepdims=True),
                       jnp.sum(jnp.where(low, 0.0, sq), axis=-1, keepdims=True))
        return x * lax.rsqrt(ss * (1.0 / HEAD_DIM) + EPS) * gain

    qn = pair_rms(q_ref[...], qg_ref[...]) * QSCALE
    kn = pair_rms(k_ref[...], kg_ref[...])
    v = v_ref[...]
    for h2 in range(2):
        c = jnp.sum(jnp.where(lane_c == bh0 + h2, cum, 0.0), axis=-1, keepdims=True) * LOG2E
        hi, mid, lo = (p.astype(F32) for p in _split3(c))
        head = (lambda x: x) if h2 == 0 else (lambda x: pltpu.roll(x, HEAD_DIM, 1))
        ext_q = jnp.where(ext == 0, hi, jnp.where(ext == 1, mid, jnp.where(ext == 2, lo,
                          jnp.where(ext < 6, 1.0, 0.0))))
        ext_k = jnp.where(ext < 3, 1.0, jnp.where(ext == 3, -hi, jnp.where(ext == 4, -mid,
                          jnp.where(ext == 5, -lo, 0.0))))
        qo_ref[0, h2] = jnp.where(low, head(qn), ext_q).astype(BF16)
        ko_ref[0, h2] = jnp.where(low, head(kn), ext_k).T.astype(BF16)
        vo_ref[0, h2] = jnp.where(low, head(v), jnp.where(ext == 0, 1.0, 0.0)).astype(BF16)


def _fox_prep(proj, cum, q_gain, k_gain, batch, seq):
    ts = ROW_TILE
    nsb = seq // ts
    hp = N_HEADS_A // 2
    out = jax.ShapeDtypeStruct((batch, N_HEADS_A, seq, LANES), BF16)
    out_t = jax.ShapeDtypeStruct((batch, N_HEADS_A, LANES, seq), BF16)
    ospec = pl.BlockSpec((1, 2, ts, LANES), lambda b, p, i: (b, p, i, 0))
    ospec_t = pl.BlockSpec((1, 2, LANES, ts), lambda b, p, i: (b, p, 0, i))
    return pl.pallas_call(
        _fox_prep_kernel,
        grid=(batch, hp, nsb),
        in_specs=[pl.BlockSpec((ts, LANES), lambda b, p, i: (b * nsb + i, p)),
                  pl.BlockSpec((ts, LANES), lambda b, p, i: (b * nsb + i, hp + p)),
                  pl.BlockSpec((ts, LANES), lambda b, p, i: (b * nsb + i, 2 * hp + p)),
                  pl.BlockSpec((ts, cum.shape[1]), lambda b, p, i: (i, 0)),
                  pl.BlockSpec((1, LANES), lambda b, p, i: (0, 0)),
                  pl.BlockSpec((1, LANES), lambda b, p, i: (0, 0))],
        out_specs=[ospec, ospec_t, ospec],
        out_shape=[out, out_t, out],
        compiler_params=_cp("parallel", "parallel", "parallel"),
        name="fox_prep",
    )(proj, proj, proj, cum, jnp.tile(q_gain, LANES // HEAD_DIM).reshape(1, LANES),
      jnp.tile(k_gain, LANES // HEAD_DIM).reshape(1, LANES))


def _store_heads(o_ref, o, tq, grouped):
    if not grouped:
        o_ref[0, 0] = o.astype(o_ref.dtype)
        return
    for hh in range(o.shape[0] // tq):
        o_ref[:, hh * HEAD_DIM:(hh + 1) * HEAD_DIM] = o[hh * tq:(hh + 1) * tq].astype(o_ref.dtype)


def _flash_kernel(*refs, tq, tk, with_bias, grouped):
    if with_bias:
        q_ref, nb_ref, k_ref, v_ref, o_ref, m_scr, acc_scr = refs
    else:
        q_ref, k_ref, v_ref, o_ref, m_scr, acc_scr = refs
    i = pl.program_id(1)
    q = q_ref[0, 0]
    rows = q.shape[0]
    if with_bias:
        q = jnp.concatenate([jnp.tile(nb_ref[0], (rows // tq, 1)), q], axis=1)
    m_scr[...] = jnp.full(m_scr.shape, NEG, F32)
    acc_scr[...] = jnp.zeros(acc_scr.shape, F32)
    q_pos = i * tq + jnp.bitwise_and(lax.broadcasted_iota(I32, (rows, 1), 0), tq - 1)

    def step(j, masked, row_lo=0):
        ks = pl.multiple_of(j * tk, tk)
        s = _dot(q[row_lo:], k_ref[0, :, pl.ds(ks, tk)])
        if masked:
            k_pos = ks + lax.broadcasted_iota(I32, (1, tk), 1)
            s = jnp.where(k_pos <= q_pos[row_lo:], s, NEG)
        m_old = m_scr[row_lo:, :]
        m_new = jnp.maximum(m_old, jnp.max(s, axis=-1, keepdims=True))
        p = jnp.exp2(s - jnp.tile(m_new, (1, tk // LANES))).astype(BF16)
        acc_scr[row_lo:, :] = jnp.exp2(m_old - m_new) * acc_scr[row_lo:, :] + _dot(p, v_ref[0, pl.ds(ks, tk), :])
        m_scr[row_lo:, :] = m_new

    n_clear = (i * tq + 1) // tk
    n_diag = max(tq // tk, 1)

    def pair(jj, c):
        step(2 * jj, False)
        step(2 * jj + 1, False)
        return c

    lax.fori_loop(0, n_clear // 2, pair, 0)

    @pl.when(lax.rem(n_clear, 2) == 1)
    def _():
        step(n_clear - 1, False)

    for d in range(n_diag):
        step(n_clear + d, True, row_lo=d * tk if rows == tq else 0)

    acc = acc_scr[...]
    _store_heads(o_ref, acc[:, :HEAD_DIM] / acc[:, HEAD_DIM:HEAD_DIM + 1], tq, grouped)


def _flash(q, k, v, tq, tk, name, bias=None, grouped_out=None):
    bh, nq, rows, kd = q.shape
    s = k.shape[2]
    assert tq % tk == 0 or tk % tq == 0
    in_specs = [pl.BlockSpec((1, 1, rows, kd), lambda b, i: (b, i, 0, 0))]
    args = [q]
    if bias is not None:
        in_specs.append(pl.BlockSpec((1, tq, bias.shape[2]), lambda b, i: (b, i, 0)))
        args.append(bias)
    in_specs += [pl.BlockSpec((1, k.shape[1], s), lambda b, i: (b, 0, 0)),
                 pl.BlockSpec((1, s, LANES), lambda b, i: (b, 0, 0))]
    if grouped_out is None:
        out_spec = pl.BlockSpec((1, 1, rows, HEAD_DIM), lambda b, i: (b, i, 0, 0))
        out_shape = jax.ShapeDtypeStruct((bh, nq, rows, HEAD_DIM), BF16)
    else:
        out_spec, out_shape = _grouped_out(grouped_out, nq, tq, rows)
    return pl.pallas_call(
        functools.partial(_flash_kernel, tq=tq, tk=tk, with_bias=bias is not None, grouped=grouped_out is not None),
        grid=(bh, nq),
        in_specs=in_specs,
        out_specs=out_spec,
        out_shape=out_shape,
        scratch_shapes=[pltpu.VMEM((rows, LANES), F32), pltpu.VMEM((rows, LANES), F32)],
        compiler_params=_cp("parallel", "arbitrary"),
        name=name,
    )(*args, k, v)


def _grouped_out(batch_groups, nq, tq, rows):
    batch, groups = batch_groups
    width = rows // tq * HEAD_DIM
    spec = pl.BlockSpec((tq, width), lambda b, i: ((b // groups) * nq + i, lax.rem(b, groups)))
    return spec, jax.ShapeDtypeStruct((batch * nq * tq, groups * width), BF16)


def _window_kernel(q_ref, k_ref, v_ref, o_ref, *, tq, window):
    i = pl.program_id(1)
    q = q_ref[0, 0]
    rows = q.shape[0]
    span = window + tq
    ks = pl.multiple_of(jnp.maximum(i * tq - window, 0), tq)
    q_pos = i * tq + jnp.bitwise_and(lax.broadcasted_iota(I32, (rows, 1), 0), tq - 1)
    k_pos = ks + lax.broadcasted_iota(I32, (1, span), 1)
    ok = (q_pos - k_pos).astype(jnp.uint32) < window
    s = jnp.where(ok, _dot(q, k_ref[0, :, pl.ds(ks, span)]), NEG)
    p = jnp.exp2(s - jnp.max(s, axis=-1, keepdims=True)).astype(BF16)
    acc = _dot(p, v_ref[0, pl.ds(ks, span), :])
    _store_heads(o_ref, acc[:, :HEAD_DIM] / acc[:, HEAD_DIM:HEAD_DIM + 1], tq, True)


def _window_attention(q, k, v, tq, window, batch_groups):
    bh, nq, rows, kd = q.shape
    s = k.shape[2]
    assert window % tq == 0 and window + tq <= s
    out_spec, out_shape = _grouped_out(batch_groups, nq, tq, rows)
    return pl.pallas_call(
        functools.partial(_window_kernel, tq=tq, window=window),
        grid=(bh, nq),
        in_specs=[pl.BlockSpec((1, 1, rows, kd), lambda b, i: (b, i, 0, 0)),
                  pl.BlockSpec((1, kd, s), lambda b, i: (b, 0, 0)),
                  pl.BlockSpec((1, s, LANES), lambda b, i: (b, 0, 0))],
        out_specs=out_spec,
        out_shape=out_shape,
        compiler_params=_cp("parallel", "parallel"),
        name="nsa_window",
    )(q, k, v)


def _oproj_kernel(o_ref, w_ref, h_ref, g_ref, out_ref, *, tiles_per_batch):
    b = pl.program_id(0) // tiles_per_batch
    out_ref[...] = h_ref[...] + g_ref[pl.ds(b, 1), :] * _dot(o_ref[...], w_ref[...])


def _oproj_residual(o, w, h, ada, gate_blk, seq):
    t, d = h.shape
    tm = ROW_TILE
    return pl.pallas_call(
        functools.partial(_oproj_kernel, tiles_per_batch=seq // tm),
        grid=(t // tm,),
        in_specs=[pl.BlockSpec((tm, o.shape[1]), lambda i: (i, 0)),
                  pl.BlockSpec(w.shape, lambda i: (0, 0)),
                  pl.BlockSpec((tm, d), lambda i: (i, 0)),
                  pl.BlockSpec((SUBLANES, d), lambda i: (0, gate_blk))],
        out_specs=pl.BlockSpec((tm, d), lambda i: (i, 0)),
        out_shape=jax.ShapeDtypeStruct((t, d), F32),
        compiler_params=_cp("parallel"),
        name="oproj_residual",
    )(o, w, h, ada)


def _seg_rms(x, seg_ones, gain_tiled):
    ss = _dot((x * x).astype(BF16), seg_ones)
    return x * lax.rsqrt(ss * (1.0 / HEAD_DIM) + EPS) * gain_tiled


def _rope(x, cos_t, sin_t):
    width = x.shape[1]
    reps = width // LANES
    lane = jnp.bitwise_and(lax.broadcasted_iota(I32, x.shape, 1), HEAD_DIM - 1)
    half = ROPE_DIM // 2
    partner = jnp.where(lane < half, pltpu.roll(x, width - half, 1), pltpu.roll(x, half, 1))
    return x * jnp.tile(cos_t, (1, reps)) + partner * jnp.tile(sin_t, (1, reps))


def _nsa_q_kernel(p_ref, z_ref, cos_ref, sin_ref, gain_ref, seg_ref, bg_ref, qp_ref, qr_ref, g_ref):
    q = _seg_rms(p_ref[...], seg_ref[...], gain_ref[...]) * QSCALE
    qp_ref[...] = q.astype(BF16)
    qr_ref[...] = _rope(q, cos_ref[...], sin_ref[...]).astype(BF16)
    g_ref[...] = _sigmoid(z_ref[...] + bg_ref[...])


def _nsa_q_prep(proj, cos_t, sin_t, q_gain, b_gate_pad):
    t = proj.shape[0]
    hd = N_HEADS_B * HEAD_DIM
    tm = ROW_TILE
    seg = (jnp.arange(hd)[:, None] // HEAD_DIM == jnp.arange(hd)[None, :] // HEAD_DIM).astype(BF16)
    row = lambda i: (i, 0)
    fix = lambda i: (0, 0)
    return pl.pallas_call(
        _nsa_q_kernel,
        grid=(t // tm,),
        in_specs=[pl.BlockSpec((tm, hd), row),
                  pl.BlockSpec((tm, LANES), lambda i: (i, hd // LANES)),
                  pl.BlockSpec((tm, LANES), row), pl.BlockSpec((tm, LANES), row),
                  pl.BlockSpec((1, hd), fix), pl.BlockSpec((hd, hd), fix), pl.BlockSpec((1, LANES), fix)],
        out_specs=[pl.BlockSpec((tm, hd), row), pl.BlockSpec((tm, hd), row), pl.BlockSpec((tm, LANES), row)],
        out_shape=[jax.ShapeDtypeStruct((t, hd), BF16), jax.ShapeDtypeStruct((t, hd), BF16),
                   jax.ShapeDtypeStruct((t, LANES), F32)],
        compiler_params=_cp("parallel"),
        name="nsa_q_prep",
    )(proj, proj, cos_t, sin_t, jnp.tile(q_gain, N_HEADS_B).reshape(1, hd), seg, b_gate_pad)


def _nsa_kv_kernel(kv_ref, cos_ref, sin_ref, g1_ref, g2_ref, seg_ref, ks_ref, vs_ref, kw_ref, vw_ref):
    gw = N_KV_GROUPS_B * HEAD_DIM
    ks = _seg_rms(kv_ref[:, 2 * gw:3 * gw], seg_ref[...], g1_ref[...])
    kw = _seg_rms(kv_ref[:, 4 * gw:5 * gw], seg_ref[...], g2_ref[...])
    ks_ref[...] = _rope(ks, cos_ref[...], sin_ref[...]).astype(BF16)
    kw_ref[...] = _rope(kw, cos_ref[...], sin_ref[...]).astype(BF16)
    vs_ref[...] = kv_ref[:, 3 * gw:4 * gw].astype(BF16)
    vw_ref[...] = kv_ref[:, 5 * gw:6 * gw].astype(BF16)


def _nsa_kv_prep(kv, cos_t, sin_t, k_gains):
    t, n = kv.shape
    gw = N_KV_GROUPS_B * HEAD_DIM
    tm = ROW_TILE
    seg = (jnp.arange(gw)[:, None] // HEAD_DIM == jnp.arange(gw)[None, :] // HEAD_DIM).astype(BF16)
    row = lambda i: (i, 0)
    fix = lambda i: (0, 0)
    out = jax.ShapeDtypeStruct((t, gw), BF16)
    return pl.pallas_call(
        _nsa_kv_kernel,
        grid=(t // tm,),
        in_specs=[pl.BlockSpec((tm, n), row), pl.BlockSpec((tm, LANES), row), pl.BlockSpec((tm, LANES), row),
                  pl.BlockSpec((1, gw), fix), pl.BlockSpec((1, gw), fix), pl.BlockSpec((gw, gw), fix)],
        out_specs=[pl.BlockSpec((tm, gw), row)] * 4,
        out_shape=[out] * 4,
        compiler_params=_cp("parallel"),
        name="nsa_kv_prep",
    )(kv, cos_t, sin_t, jnp.tile(k_gains[1], N_KV_GROUPS_B).reshape(1, gw),
      jnp.tile(k_gains[2], N_KV_GROUPS_B).reshape(1, gw), seg)


def _compress_kernel(x_ref, pe_ref, w1_ref, w2_ref, g_ref, o_ref, *, norm):
    x = (x_ref[...] + pe_ref[...]).astype(BF16)
    y = _dot(_silu(_dot(x, w1_ref[...])).astype(BF16), w2_ref[...])
    if norm:
        y = _head_rms(y, g_ref[...])
    o_ref[...] = y.astype(o_ref.dtype)


def _compress(x, pe_flat, w1, w2, gain, norm, name):
    r, kdim = x.shape
    tm = min(ROW_TILE, r)
    fix = lambda i: (0, 0)
    return pl.pallas_call(
        functools.partial(_compress_kernel, norm=norm),
        grid=(r // tm,),
        in_specs=[pl.BlockSpec((tm, kdim), lambda i: (i, 0)), pl.BlockSpec((1, kdim), fix),
                  pl.BlockSpec(w1.shape, fix), pl.BlockSpec(w2.shape, fix), pl.BlockSpec((1, HEAD_DIM), fix)],
        out_specs=pl.BlockSpec((tm, HEAD_DIM), lambda i: (i, 0)),
        out_shape=jax.ShapeDtypeStruct((r, HEAD_DIM), BF16),
        compiler_params=_cp("parallel"),
        name=name,
    )(x, pe_flat, w1, w2, gain.reshape(1, HEAD_DIM))


def _cmp_select_kernel(q_ref, k_ref, v_ref, o_ref, nb_ref, *, tq, n_sel):
    i = pl.program_id(1)
    q = q_ref[0, 0]
    rows = q.shape[0]
    nblk = k_ref.shape[1]
    logits = _dot_nt(q, k_ref[0])
    t_pos = i * tq + jnp.bitwise_and(lax.broadcasted_iota(I32, (rows, 1), 0), tq - 1)
    blk = lax.broadcasted_iota(I32, (1, nblk), 1)
    vis = (blk + 1) * CMP_BLOCK - 1 <= t_pos
    lm = jnp.where(vis, logits, NEG)
    e = jnp.exp2(lm - jnp.max(lm, axis=-1, keepdims=True))
    p = jnp.where(vis, e / jnp.sum(e, axis=-1, keepdims=True), 0.0)
    _store_heads(o_ref, _dot(p.astype(BF16), v_ref[0])[:, :HEAD_DIM], tq, True)

    imp = p[0:tq]
    for hh in range(1, rows // tq):
        imp = imp + p[hh * tq:(hh + 1) * tq]
    cur = lax.shift_right_logical(t_pos[0:tq], CMP_BLOCK.bit_length() - 1)
    forced = jnp.logical_or(blk == 0, jnp.logical_or(blk == cur, blk == cur - 1))
    cand = jnp.where(jnp.logical_or(forced, blk > cur), -jnp.inf, imp)
    sel = jnp.where(forced, 1.0, 0.0)
    blk_f = blk.astype(F32)
    for _ in range(max(n_sel - 3, 0)):
        best = jnp.max(cand, axis=-1, keepdims=True)
        first = jnp.min(jnp.where(cand == best, blk_f, float(nblk)), axis=-1, keepdims=True)
        hit = blk_f == first
        sel = jnp.where(jnp.logical_and(hit, best > -jnp.inf), 1.0, sel)
        cand = jnp.where(hit, -jnp.inf, cand)
    nb_ref[0] = jnp.where(sel > 0.5, 0.0, NEG).astype(nb_ref.dtype)


def _cmp_select(q, k_cmp, v_cmp, seq, tq, batch_groups):
    bg, nq, rows, _ = q.shape
    nblk = k_cmp.shape[1]
    assert min(N_SELECT, nblk) >= 3
    out_spec, out_shape = _grouped_out(batch_groups, nq, tq, rows)
    return pl.pallas_call(
        functools.partial(_cmp_select_kernel, tq=tq, n_sel=min(N_SELECT, nblk)),
        grid=(bg, nq),
        in_specs=[pl.BlockSpec((1, 1, rows, HEAD_DIM), lambda b, i: (b, i, 0, 0)),
                  pl.BlockSpec((1, nblk, HEAD_DIM), lambda b, i: (b, 0, 0)),
                  pl.BlockSpec((1, nblk, LANES), lambda b, i: (b, 0, 0))],
        out_specs=[out_spec, pl.BlockSpec((1, tq, nblk), lambda b, i: (b, i, 0))],
        out_shape=[out_shape, jax.ShapeDtypeStruct((bg, seq, nblk), BF16)],
        compiler_params=_cp("parallel", "parallel"),
        name="nsa_cmp_select",
    )(q, k_cmp, v_cmp)


def _nsa_out_kernel(oc_ref, os_ref, ow_ref, g_ref, e_ref, w_ref, h_ref, ada_ref, out_ref, *, tiles_per_batch):
    b = pl.program_id(0) // tiles_per_batch
    g_hi, g_lo = _split2(g_ref[...])
    o = None
    for j, o_ref in enumerate((oc_ref, os_ref, ow_ref)):
        gate = _dot(g_hi, e_ref[j]) + _dot(g_lo, e_ref[j])
        term = gate * o_ref[...].astype(F32)
        o = term if o is None else o + term
    out_ref[...] = h_ref[...] + ada_ref[pl.ds(b, 1), :] * _dot(o.astype(BF16), w_ref[...])


def _nsa_out(o_cmp, o_sel, o_win, gates, w_o, h, ada, gate_blk, seq):
    t, d = h.shape
    hd = o_cmp.shape[1]
    tm = ROW_TILE
    col = jnp.arange(LANES)[:, None]
    lane = jnp.arange(hd)[None, :]
    expand = jnp.stack([(col == (lane // HEAD_DIM) * 3 + j) for j in range(3)]).astype(BF16)
    row = lambda i: (i, 0)
    return pl.pallas_call(
        functools.partial(_nsa_out_kernel, tiles_per_batch=seq // tm),
        grid=(t // tm,),
        in_specs=[pl.BlockSpec((tm, hd), row), pl.BlockSpec((tm, hd), row), pl.BlockSpec((tm, hd), row),
                  pl.BlockSpec((tm, LANES), row),
                  pl.BlockSpec((3, LANES, hd), lambda i: (0, 0, 0)),
                  pl.BlockSpec(w_o.shape, lambda i: (0, 0)),
                  pl.BlockSpec((tm, d), row),
                  pl.BlockSpec((SUBLANES, d), lambda i: (0, gate_blk))],
        out_specs=pl.BlockSpec((tm, d), row),
        out_shape=jax.ShapeDtypeStruct((t, d), F32),
        compiler_params=_cp("parallel"),
        name="nsa_out",
    )(o_cmp, o_sel, o_win, gates, expand, w_o, h, ada)


def _route_kernel(h_ref, gain_ref, shift_ref, scale_ref, wr_ref, br_ref, tri_ref, upper_ref,
                  u_ref, pos_ref, w_ref, nu_ref, off_ref, *, tiles_per_batch):
    b = (pl.program_id(0) * ROUTE_TILES) // tiles_per_batch
    shift, scale = shift_ref[pl.ds(b, 1), :], scale_ref[pl.ds(b, 1), :]
    for tile in range(ROUTE_TILES):
        rows = slice(tile * MOE_TOK_TILE, (tile + 1) * MOE_TOK_TILE)
        _route_tile(h_ref[rows, :], gain_ref[...], shift, scale, wr_ref, br_ref, tri_ref, upper_ref,
                    u_ref.at[rows, :], pos_ref.at[rows, :], w_ref.at[rows, :], nu_ref.at[tile], off_ref.at[tile])


def _route_tile(h, gain, shift, scale, wr_ref, br_ref, tri_ref, upper_ref, u_ref, pos_ref, w_ref, nu_ref, off_ref):
    u = _norm_mod(h, gain, shift, scale)
    u_ref[...] = u.astype(BF16)
    scores = _sigmoid(_dot_f32(u, wr_ref[...]))
    lane = lax.broadcasted_iota(I32, scores.shape, 1)
    lane_f = lane.astype(F32)
    biased = jnp.where(lane < N_EXPERTS, scores + br_ref[...], -jnp.inf)
    chosen = jnp.zeros(scores.shape, F32)
    hits, val = [], []
    upper = upper_ref[...]
    for _ in range(TOP_K):
        best = jnp.max(biased, axis=-1, keepdims=True)
        first = jnp.min(jnp.where(biased == best, lane_f, float(LANES)), axis=-1, keepdims=True)
        hit = lane_f == first
        hits.append(hit)
        val.append(jnp.sum(jnp.where(hit, scores, 0.0), axis=-1, keepdims=True))
        biased = jnp.where(hit, -jnp.inf, biased)
        chosen = jnp.where(hit, 1.0, chosen)
    top_s = jnp.concatenate(val, axis=1)
    w_ref[...] = top_s / jnp.sum(top_s, axis=-1, keepdims=True) * ROUTED_SCALE

    before = _dot(tri_ref[...], chosen.astype(BF16))
    count = jnp.sum(chosen, axis=0, keepdims=True)
    units = jnp.floor((count + (MOE_UNIT - 1)) * (1.0 / MOE_UNIT))
    padded = jnp.broadcast_to(units * MOE_UNIT, (SUBLANES, LANES))
    offset = _dot(padded.astype(BF16), upper)[0:1]
    row_of = before + offset
    pos = [jnp.sum(jnp.where(hit, row_of, 0.0), axis=-1, keepdims=True) for hit in hits]
    pos_ref[...] = jnp.concatenate(pos, axis=1).astype(I32)
    nu_ref[...] = units.astype(I32)
    off_ref[...] = offset.astype(I32)


def _route(h, gain, ada, shift_blk, scale_blk, w_router_pad, b_router_pad, seq):
    t, d = h.shape
    tt = MOE_TOK_TILE
    n_tiles = t // tt
    rt = ROUTE_TILES * tt
    assert n_tiles % ROUTE_TILES == 0 and seq % rt == 0
    row = lambda i: (i, 0)
    fix = lambda i: (0, 0)
    tri = (jnp.arange(tt)[:, None] > jnp.arange(tt)[None, :]).astype(BF16)
    upper = (jnp.arange(LANES)[:, None] < jnp.arange(LANES)[None, :]).astype(BF16)
    per_tile = jax.ShapeDtypeStruct((n_tiles, 1, LANES), I32)
    per_tile_spec = pl.BlockSpec((ROUTE_TILES, 1, LANES), lambda i: (i, 0, 0))
    return pl.pallas_call(
        functools.partial(_route_kernel, tiles_per_batch=seq // tt),
        grid=(n_tiles // ROUTE_TILES,),
        in_specs=[pl.BlockSpec((rt, d), row), pl.BlockSpec((1, d), fix),
                  pl.BlockSpec((SUBLANES, d), lambda i: (0, shift_blk)),
                  pl.BlockSpec((SUBLANES, d), lambda i: (0, scale_blk)),
                  pl.BlockSpec((d, LANES), fix), pl.BlockSpec((1, LANES), fix),
                  pl.BlockSpec((tt, tt), fix), pl.BlockSpec((LANES, LANES), fix)],
        out_specs=[pl.BlockSpec((rt, d), row), pl.BlockSpec((rt, TOP_K), row), pl.BlockSpec((rt, TOP_K), row),
                   per_tile_spec, per_tile_spec],
        out_shape=[jax.ShapeDtypeStruct((t, d), BF16), jax.ShapeDtypeStruct((t, TOP_K), I32),
                   jax.ShapeDtypeStruct((t, TOP_K), F32), per_tile, per_tile],
        compiler_params=_cp("parallel"),
        name="moe_route",
    )(h, gain.reshape(1, d), ada, ada, w_router_pad, b_router_pad, tri, upper)


PLAN_SLOTS = 256
PLAN_TILES = LANES


def _plan_kernel(nu_ref, off_ref, upper_ref, lower_ref, row_ref, exp_ref, nused_ref,
                 *, n_tiles, r_loc, upb, n_unit_slots, n_dump_units):
    nu = nu_ref[...]
    nu_t = nu.T
    off_t = off_ref[...].T
    upper = upper_ref[...]
    lower = lower_ref[...]
    lane = lax.broadcasted_iota(I32, (PLAN_SLOTS, LANES), 1)
    lane_f = lane.astype(F32)

    def exact_dot(a, b_bf16, left):
        parts = _split3(a)
        return sum(_dot(b_bf16, p) if left else _dot(p, b_bf16) for p in parts)

    units_before_t = _dot(nu_t.astype(BF16), upper)
    count_col = jnp.sum(nu_t, axis=1, keepdims=True)
    count_row = jnp.sum(nu, axis=0, keepdims=True)
    padded_col = jnp.floor((count_col + (upb - 1)) * (1.0 / upb)) * upb
    padded_row = jnp.floor((count_row + (upb - 1)) * (1.0 / upb)) * upb
    start_col = exact_dot(jnp.broadcast_to(padded_col, (LANES, LANES)), lower, True)
    real_col = exact_dot(jnp.broadcast_to(count_col, (LANES, LANES)), lower, True)
    start_row = exact_dot(jnp.broadcast_to(padded_row, (SUBLANES, LANES)), upper, False)[0:1]
    seg_start_t = start_col + units_before_t
    seg_real_t = real_col + units_before_t
    tile_id = lax.broadcasted_iota(I32, (LANES, LANES), 1).astype(F32)
    seg_row_t = tile_id * r_loc + off_t

    idx = pl.program_id(0) * PLAN_SLOTS + lax.broadcasted_iota(I32, (PLAN_SLOTS, 1), 0)
    slot = jnp.where(idx < n_unit_slots, idx, (idx - n_unit_slots) * upb).astype(F32)

    in_e = jnp.logical_and(start_row <= slot, lane < N_EXPERTS)
    expert = jnp.sum(jnp.where(in_e, 1.0, 0.0), axis=-1, keepdims=True) - 1.0
    pick_e = (lane_f == expert).astype(BF16)
    look = lambda tbl: exact_dot(tbl, pick_e, True)
    starts = look(seg_start_t)
    in_t = jnp.logical_and(starts <= slot, lane < n_tiles)
    tile = jnp.sum(jnp.where(in_t, 1.0, 0.0), axis=-1, keepdims=True) - 1.0
    at = lambda x: jnp.sum(jnp.where(lane_f == tile, x, 0.0), axis=-1, keepdims=True)
    m = slot - at(starts)
    n = at(look(nu_t))
    row = at(look(seg_row_t)) + m * MOE_UNIT
    pad_idx = slot - (at(look(seg_real_t)) + jnp.minimum(m, n))
    dump = n_tiles * r_loc + jnp.minimum(pad_idx, n_dump_units - 1.0) * MOE_UNIT
    row_ref[...] = jnp.where(m < n, row, dump).astype(I32)
    exp_ref[...] = expert.astype(I32)
    total = jnp.sum(padded_row, axis=-1, keepdims=True) * (1.0 / upb)
    nused_ref[...] = jnp.broadcast_to(total, (SUBLANES, LANES)).astype(I32)


def _plan(nu, off, n_tiles, r_loc, n_blocks, n_dump_units):
    assert n_tiles <= PLAN_TILES
    upb = MOE_BLOCK // MOE_UNIT
    n_unit_slots = -(-(n_blocks * upb) // PLAN_SLOTS) * PLAN_SLOTS
    n_steps = n_unit_slots // PLAN_SLOTS + -(-(n_blocks + 2) // PLAN_SLOTS)
    pad = lambda x: jnp.pad(x.astype(F32), ((0, PLAN_TILES - n_tiles), (0, 0)))
    upper = (jnp.arange(LANES)[:, None] < jnp.arange(LANES)[None, :]).astype(BF16)
    fix = lambda s: (0, 0)
    col = pl.BlockSpec((PLAN_SLOTS, 1), lambda s: (s, 0))
    rows, experts, n_used = pl.pallas_call(
        functools.partial(_plan_kernel, n_tiles=n_tiles, r_loc=r_loc, upb=upb, n_unit_slots=n_unit_slots,
                          n_dump_units=n_dump_units),
        grid=(n_steps,),
        in_specs=[pl.BlockSpec((PLAN_TILES, LANES), fix), pl.BlockSpec((PLAN_TILES, LANES), fix),
                  pl.BlockSpec((LANES, LANES), fix), pl.BlockSpec((LANES, LANES), fix)],
        out_specs=[col, col, pl.BlockSpec((SUBLANES, LANES), fix)],
        out_shape=[jax.ShapeDtypeStruct((n_steps * PLAN_SLOTS, 1), I32),
                   jax.ShapeDtypeStruct((n_steps * PLAN_SLOTS, 1), I32),
                   jax.ShapeDtypeStruct((SUBLANES, LANES), I32)],
        compiler_params=_cp("arbitrary"),
        name="moe_plan",
    )(pad(nu), pad(off), upper, upper.T)
    return (rows[:n_blocks * upb, 0], experts[n_unit_slots:n_unit_slots + n_blocks + 2, 0], n_used[0, :1])


def _one_hot_t(pos_ref, weight_ref, n_rows):
    tt = pos_ref.shape[0]
    col = lax.broadcasted_iota(I32, (tt, n_rows), 1)
    acc = jnp.zeros((tt, n_rows), F32)
    for k in range(TOP_K):
        acc = jnp.where(col == pos_ref[:, k:k + 1], 1.0 if weight_ref is None else weight_ref[:, k:k + 1], acc)
    return acc.astype(BF16)


def _dispatch_kernel(u_ref, pos_ref, x_ref):
    x_ref[0] = _dot_tn(_one_hot_t(pos_ref, None, x_ref.shape[1]), u_ref[...]).astype(x_ref.dtype)


def _dispatch(u, lpos, n_tiles_total, r_loc):
    t, d = u.shape
    tt = MOE_TOK_TILE
    n_real = t // tt
    return pl.pallas_call(
        _dispatch_kernel,
        grid=(n_tiles_total,),
        in_specs=[pl.BlockSpec((tt, d), lambda i: (jnp.minimum(i, n_real - 1), 0)),
                  pl.BlockSpec((tt, TOP_K), lambda i: (i, 0))],
        out_specs=pl.BlockSpec((1, r_loc, d), lambda i: (i, 0, 0)),
        out_shape=jax.ShapeDtypeStruct((n_tiles_total, r_loc, d), BF16),
        compiler_params=_cp("parallel"),
        name="moe_dispatch",
    )(u, lpos)


def _expert_kernel(tbl_ref, be_ref, nblk_ref, x_hbm, wg_ref, wu_ref, wd_ref, y_hbm,
                   xbuf, ybuf, wg_bf, wu_bf, wd_bf, gsem, ssem, *, upb):
    b = pl.program_id(0)
    nblk = nblk_ref[0]

    def gather(blk, sl):
        for m in range(upb):
            row = pl.multiple_of(tbl_ref[blk * upb + m], MOE_UNIT)
            yield pltpu.make_async_copy(x_hbm.at[pl.ds(row, MOE_UNIT), :],
                                        xbuf.at[sl, pl.ds(m * MOE_UNIT, MOE_UNIT), :], gsem.at[sl, m])

    def scatter(blk, sl):
        for m in range(upb):
            row = pl.multiple_of(tbl_ref[blk * upb + m], MOE_UNIT)
            yield pltpu.make_async_copy(ybuf.at[sl, pl.ds(m * MOE_UNIT, MOE_UNIT), :],
                                        y_hbm.at[pl.ds(row, MOE_UNIT), :], ssem.at[sl, m])

    @pl.when(jnp.logical_and(b == 0, nblk > 0))
    def _():
        for cp in gather(0, 0):
            cp.start()

    @pl.when(jnp.logical_or(b == 0, be_ref[b] != be_ref[jnp.maximum(b - 1, 0)]))
    def _():
        wg_bf[...] = wg_ref[0, 0].astype(BF16)
        wu_bf[...] = wu_ref[0, 0].astype(BF16)
        wd_bf[...] = wd_ref[0, 0].astype(BF16)

    slot = lax.rem(b, 2)

    @pl.when(b + 1 < nblk)
    def _():
        for cp in gather(b + 1, 1 - slot):
            cp.start()

    @pl.when(jnp.logical_and(b >= 2, b < nblk + 2))
    def _():
        for cp in scatter(b - 2, slot):
            cp.wait()

    @pl.when(b < nblk)
    def _():
        for cp in gather(b, slot):
            cp.wait()
        x = xbuf[slot]
        hid = _silu(_dot(x, wg_bf[...])) * _dot(x, wu_bf[...])
        ybuf[slot] = _dot(hid.astype(BF16), wd_bf[...]).astype(ybuf.dtype)
        for cp in scatter(b, slot):
            cp.start()


def _experts(x_loc, unit_rows, block_e, n_blocks_used, w_gate, w_up, w_down, layer, n_blocks):
    rows, d = x_loc.shape
    de = w_gate.shape[3]
    upb = MOE_BLOCK // MOE_UNIT
    grid_spec = pltpu.PrefetchScalarGridSpec(
        num_scalar_prefetch=3,
        grid=(n_blocks + 2,),
        in_specs=[pl.BlockSpec(memory_space=pl.ANY),
                  pl.BlockSpec((1, 1, d, de), lambda b, tbl, be, nb: (layer, be[b], 0, 0)),
                  pl.BlockSpec((1, 1, d, de), lambda b, tbl, be, nb: (layer, be[b], 0, 0)),
                  pl.BlockSpec((1, 1, de, d), lambda b, tbl, be, nb: (layer, be[b], 0, 0))],
        out_specs=pl.BlockSpec(memory_space=pl.ANY),
        scratch_shapes=[pltpu.VMEM((2, MOE_BLOCK, d), BF16), pltpu.VMEM((2, MOE_BLOCK, d), BF16),
                        pltpu.VMEM((d, de), BF16), pltpu.VMEM((d, de), BF16), pltpu.VMEM((de, d), BF16),
                        pltpu.SemaphoreType.DMA((2, upb)), pltpu.SemaphoreType.DMA((2, upb))],
    )
    return pl.pallas_call(
        functools.partial(_expert_kernel, upb=upb),
        grid_spec=grid_spec,
        out_shape=jax.ShapeDtypeStruct((rows, d), BF16),
        input_output_aliases={3: 0},
        compiler_params=_cp("arbitrary"),
        name="moe_experts",
    )(unit_rows, block_e, n_blocks_used, x_loc, w_gate, w_up, w_down)


def _combine_kernel(y_ref, pos_ref, tw_ref, u_ref, wsg_ref, wsu_ref, wsd_ref, h_ref, g_ref, out_ref, *, tiles_per_batch):
    b = pl.program_id(0) // tiles_per_batch
    routed = _dot(_one_hot_t(pos_ref, tw_ref, y_ref.shape[1]), y_ref[0])
    u = u_ref[...]
    hid = _silu(_dot(u, wsg_ref[...])) * _dot(u, wsu_ref[...])
    shared = _dot(hid.astype(BF16), wsd_ref[...])
    out_ref[...] = h_ref[...] + g_ref[pl.ds(b, 1), :] * (routed + shared)


def _combine(y_loc, lpos, top_w, u, w_sg, w_su, w_sd, h, ada, gate_blk, seq):
    t, d = h.shape
    tt = MOE_TOK_TILE
    r_loc = y_loc.shape[1]
    row = lambda i: (i, 0)
    fix = lambda i: (0, 0)
    return pl.pallas_call(
        functools.partial(_combine_kernel, tiles_per_batch=seq // tt),
        grid=(t // tt,),
        in_specs=[pl.BlockSpec((1, r_loc, d), lambda i: (i, 0, 0)),
                  pl.BlockSpec((tt, TOP_K), row), pl.BlockSpec((tt, TOP_K), row),
                  pl.BlockSpec((tt, d), row),
                  pl.BlockSpec(w_sg.shape, fix), pl.BlockSpec(w_su.shape, fix), pl.BlockSpec(w_sd.shape, fix),
                  pl.BlockSpec((tt, d), row),
                  pl.BlockSpec((SUBLANES, d), lambda i: (0, gate_blk))],
        out_specs=pl.BlockSpec((tt, d), row),
        out_shape=jax.ShapeDtypeStruct((t, d), F32),
        compiler_params=_cp("parallel"),
        name="moe_combine",
    )(y_loc, lpos, top_w, u, w_sg, w_su, w_sd, h, ada)


def _moe_ffn(h, gain, ada, blk0, w_router, b_router, w_eg, w_eu, w_ed, layer, w_sg, w_su, w_sd, seq):
    t, d = h.shape
    tt, unit, upb = MOE_TOK_TILE, MOE_UNIT, MOE_BLOCK // MOE_UNIT
    wr = jnp.pad(w_router, ((0, 0), (0, LANES - N_EXPERTS)))
    br = jnp.pad(b_router.astype(F32), (0, LANES - N_EXPERTS)).reshape(1, LANES)
    u, lpos, top_w, nu, off = _route(h, gain, ada, blk0, blk0 + 1, wr, br, seq)

    n_tok_tiles = t // tt
    r_loc = -(-(tt * TOP_K + N_EXPERTS * (unit - 1)) // LANES) * LANES
    n_dummy_tiles = -(-(N_EXPERTS * (upb - 1) * unit) // r_loc)
    max_units = n_tok_tiles * (tt * TOP_K // unit + N_EXPERTS)
    n_blocks = -(-max_units // upb) + N_EXPERTS
    unit_rows, block_e, n_used = _plan(nu[:, 0, :], off[:, 0, :], n_tok_tiles, r_loc, n_blocks,
                                       n_dummy_tiles * r_loc // unit)
    lpos = jnp.concatenate([lpos, jnp.full((n_dummy_tiles * tt, TOP_K), -1, I32)], axis=0)

    x_loc = _dispatch(u, lpos, n_tok_tiles + n_dummy_tiles, r_loc)
    y_loc = _experts(x_loc.reshape(-1, d), unit_rows, block_e, n_used, w_eg, w_eu, w_ed, layer, n_blocks)
    y_loc = y_loc.reshape(n_tok_tiles + n_dummy_tiles, r_loc, d)
    return _combine(y_loc, lpos, top_w, u, w_sg.astype(BF16), w_su.astype(BF16), w_sd.astype(BF16),
                    h, ada, blk0 + 2, seq)


def _fox_layer(h, ada, norm_gain, w_in, b_f, q_gain, k_gain, w_o, batch, seq):
    t, d = h.shape
    hd = N_HEADS_A * HEAD_DIM
    w = jnp.pad(w_in, ((0, 0), (0, MXU_COLS - N_HEADS_A))).astype(BF16)
    proj = _norm_mod_matmul(h, norm_gain, ada, 0, 1, w, seq, w.shape[1], "fox_in_proj")
    z = proj[:, 3 * hd:3 * hd + N_HEADS_A].reshape(batch, seq, N_HEADS_A)
    z = z.transpose(1, 0, 2).reshape(seq, batch * N_HEADS_A)
    cum = _logsig_cumsum(z, jnp.tile(b_f.astype(F32), batch).reshape(1, -1))
    q, k, v = _fox_prep(proj, cum, q_gain, k_gain, batch, seq)
    bh = batch * N_HEADS_A
    o = _flash(q.reshape(bh, seq // FOX_TQ, FOX_TQ, LANES), k.reshape(bh, LANES, seq),
               v.reshape(bh, seq, LANES), FOX_TQ, FOX_TK, "fox_attention")
    o = o.reshape(batch, N_HEADS_A, seq, HEAD_DIM).transpose(0, 2, 1, 3).reshape(t, hd)
    return _oproj_residual(o, w_o.astype(BF16), h, ada, 2, seq)


def _rope_tables(positions):
    inv_freq = ROPE_THETA ** (-jnp.arange(0, ROPE_DIM, 2, dtype=F32) / ROPE_DIM)
    ang = positions.astype(F32).reshape(-1, 1) * inv_freq
    half = ROPE_DIM // 2
    cos, sin = jnp.cos(ang), jnp.sin(ang)
    rest = HEAD_DIM - ROPE_DIM
    cos_h = jnp.concatenate([cos, cos, jnp.ones((ang.shape[0], rest), F32)], axis=1)
    sin_h = jnp.concatenate([-sin, sin, jnp.zeros((ang.shape[0], rest), F32)], axis=1)
    del half
    return jnp.tile(cos_h, (1, LANES // HEAD_DIM)), jnp.tile(sin_h, (1, LANES // HEAD_DIM))


def _group_rows(x, batch, seq, tq):
    g, hpg = N_KV_GROUPS_B, HEADS_PER_GROUP
    x = x.reshape(batch, seq // tq, tq, g, hpg, HEAD_DIM).transpose(0, 3, 1, 4, 2, 5)
    return x.reshape(batch * g, seq // tq, hpg * tq, HEAD_DIM)


def _per_group(x, batch, seq):
    g = N_KV_GROUPS_B
    return x.reshape(batch, seq, g, HEAD_DIM).transpose(0, 2, 1, 3).reshape(batch * g, seq, HEAD_DIM)


def _with_ones(v):
    pad = jnp.zeros(v.shape[:-1] + (LANES - HEAD_DIM,), v.dtype).at[..., 0].set(1)
    return jnp.concatenate([v, pad], axis=-1)


def _shared_kv(h, ada_kv, norm_kv, w_kv, pe_k, pe_v, w_ck1, w_ck2, w_cv1, w_cv2, k_gains, cos_t, sin_t, batch, seq):
    g = N_KV_GROUPS_B
    gw = g * HEAD_DIM
    nb = seq // CMP_BLOCK
    kv = _norm_mod_matmul(h, norm_kv, ada_kv, 0, 1, w_kv.astype(BF16), seq, w_kv.shape[1], "nsa_kv_proj")
    k_sel, v_sel, k_win, v_win = _nsa_kv_prep(kv, cos_t, sin_t, k_gains)

    def blocks(x):
        x = x.reshape(batch, nb, CMP_BLOCK, g, HEAD_DIM).transpose(0, 1, 3, 2, 4)
        return x.reshape(batch * nb * g, CMP_BLOCK * HEAD_DIM)

    k_cmp = _compress(blocks(kv[:, 0:gw]), pe_k.reshape(1, -1), w_ck1.astype(BF16), w_ck2.astype(BF16),
                      k_gains[0], True, "nsa_compress_k")
    v_cmp = _compress(blocks(kv[:, gw:2 * gw]), pe_v.reshape(1, -1), w_cv1.astype(BF16), w_cv2.astype(BF16),
                      k_gains[0], False, "nsa_compress_v")
    to_bg = lambda x: x.reshape(batch, nb, g, HEAD_DIM).transpose(0, 2, 1, 3).reshape(batch * g, nb, HEAD_DIM)
    k_cmp = to_bg(k_cmp)
    v_cmp = jnp.pad(to_bg(v_cmp), ((0, 0), (0, 0), (0, LANES - HEAD_DIM)))

    def per_group_t(x):
        return x.reshape(batch, seq, g, HEAD_DIM).transpose(0, 2, 3, 1).reshape(batch * g, HEAD_DIM, seq)

    blk_onehot = (jnp.arange(nb)[:, None] == jnp.arange(seq)[None, :] // CMP_BLOCK).astype(BF16)
    k_sel = jnp.concatenate([jnp.broadcast_to(blk_onehot, (batch * g, nb, seq)), per_group_t(k_sel)], axis=1)
    return (k_cmp, v_cmp, k_sel, _with_ones(_per_group(v_sel, batch, seq)),
            per_group_t(k_win), _with_ones(_per_group(v_win, batch, seq)))


def _nsa_layer(h, ada, norm_gain, kv, cos_t, sin_t, w_in, b_gate, q_gain, w_o, batch, seq):
    hd = N_HEADS_B * HEAD_DIM
    k_cmp, v_cmp, k_sel, v_sel, k_win, v_win = kv
    n_gate = 3 * N_HEADS_B
    w = jnp.pad(w_in, ((0, 0), (0, LANES - n_gate))).astype(BF16)
    proj = _norm_mod_matmul(h, norm_gain, ada, 0, 1, w, seq, hd + LANES, "nsa_in_proj")
    bg = jnp.pad(b_gate.astype(F32), (0, LANES - n_gate)).reshape(1, LANES)
    q_plain, q_rot, gates = _nsa_q_prep(proj, cos_t, sin_t, q_gain, bg)

    bgs = (batch, N_KV_GROUPS_B)
    o_cmp, negbias = _cmp_select(_group_rows(q_plain, batch, seq, CMP_TQ), k_cmp, v_cmp, seq, CMP_TQ, bgs)
    o_sel = _flash(_group_rows(q_rot, batch, seq, SEL_TQ), k_sel, v_sel, SEL_TQ, NSA_TK, "nsa_selected",
                   bias=negbias, grouped_out=bgs)
    o_win = _window_attention(_group_rows(q_rot, batch, seq, WIN_TQ), k_win, v_win, WIN_TQ, WINDOW, bgs)
    return _nsa_out(o_cmp, o_sel, o_win, gates, w_o.astype(BF16), h, ada, 2, seq)


def kernel(x, c, positions, w_ada, b_ada, norm_mix, norm_ffn, w_in_a, b_f_a, qnorm_a, knorm_a, w_o_a, norm_kv, w_ada_kv, b_ada_kv, w_kv_b, pe_cmp_k, pe_cmp_v, w_cmp_k1, w_cmp_k2, w_cmp_v1, w_cmp_v2, knorm_b, w_in_b, b_gate_b, qnorm_b, w_o_b, w_router, b_router, w_exp_gate, w_exp_up, w_exp_down, w_sh_gate, w_sh_up, w_sh_down):
    batch, seq, d = x.shape
    depth = w_ada.shape[0]
    n_a = w_in_a.shape[0]
    c_pad = jnp.pad(c, ((0, SUBLANES - batch), (0, 0)))
    cos_t, sin_t = _rope_tables(positions)
    h = x.reshape(batch * seq, d)
    kv = None
    for layer in range(depth):
        ada = _ada(c_pad, w_ada[layer], b_ada[layer])
        if layer < n_a:
            h = _fox_layer(h, ada, norm_mix[layer], w_in_a[layer], b_f_a[layer], qnorm_a[layer],
                           knorm_a[layer], w_o_a[layer], batch, seq)
        else:
            if layer == n_a:
                ada_kv = _ada(c_pad, w_ada_kv, b_ada_kv)
                kv = _shared_kv(h, ada_kv, norm_kv, w_kv_b, pe_cmp_k, pe_cmp_v, w_cmp_k1, w_cmp_k2,
                                w_cmp_v1, w_cmp_v2, knorm_b, cos_t, sin_t, batch, seq)
            j = layer - n_a
            h = _nsa_layer(h, ada, norm_mix[layer], kv, cos_t, sin_t, w_in_b[j], b_gate_b[j], qnorm_b[j],
                           w_o_b[j], batch, seq)
        h = _moe_ffn(h, norm_ffn[layer], ada, 3, w_router[layer], b_router[layer], w_exp_gate, w_exp_up,
                     w_exp_down, layer, w_sh_gate[layer], w_sh_up[layer], w_sh_down[layer], seq)
    return h.reshape(batch, seq, d)
```

```python
import functools

import jax
import jax.numpy as jnp
from jax import lax
from jax.experimental import pallas as pl
from jax.experimental.pallas import tpu as pltpu

F32 = jnp.float32
BF16 = jnp.bfloat16
I32 = jnp.int32

HEAD_DIM = 64
N_HEADS_A = 16
N_HEADS_B = 16
N_KV_GROUPS_B = 4
HEADS_PER_GROUP = N_HEADS_B // N_KV_GROUPS_B
ROPE_THETA = 500000.0
ROPE_DIM = HEAD_DIM // 4
CMP_BLOCK = 64
N_SELECT = 16
WINDOW = 512
N_EXPERTS = 64
TOP_K = 8
ROUTED_SCALE = 2.5
EPS = 1e-6
FORCE_SCORE = 1e4
NEG = -1e30
LOG2E = 1.4426950408889634
QSCALE = HEAD_DIM ** -0.5 * LOG2E

LANES = 128
SUBLANES = 8
BF16_ROWS = 16
MXU_COLS = 256

ROW_TILE = 512
FOX_TQ = 2048
FOX_TK = 512
SEL_TQ = 512
WIN_TQ = 256
NSA_TK = 512
CMP_TQ = 1024
MOE_TOK_TILE = 256
ROUTE_TILES = 2
MOE_UNIT = BF16_ROWS
MOE_BLOCK = 512
VMEM_LIMIT = 56 * 1024 * 1024


def _cp(*sem):
    return pltpu.CompilerParams(dimension_semantics=sem, vmem_limit_bytes=VMEM_LIMIT)


def _dot(a, b):
    return jnp.dot(a, b, preferred_element_type=F32)


def _dot_nt(a, b):
    return lax.dot_general(a, b, (((1,), (1,)), ((), ())), preferred_element_type=F32)


def _dot_tn(a, b):
    return lax.dot_general(a, b, (((0,), (0,)), ((), ())), preferred_element_type=F32)


def _split2(x):
    hi = x.astype(BF16)
    lo = (x - hi.astype(F32)).astype(BF16)
    return hi, lo


def _split3(x):
    hi = x.astype(BF16)
    r = x - hi.astype(F32)
    mid = r.astype(BF16)
    lo = (r - mid.astype(F32)).astype(BF16)
    return hi, mid, lo


def _dot_f32(a, b):
    a_hi, a_lo = _split2(a)
    b_hi, b_lo = _split2(b)
    return _dot(a_hi, b_hi) + _dot(a_lo, b_hi) + _dot(a_hi, b_lo)


def _sigmoid(x):
    return 1.0 / (1.0 + jnp.exp(-x))


def _silu(x):
    return x * _sigmoid(x)


def _ada_kernel(c_ref, w_ref, b_ref, o_ref):
    o_ref[...] = _dot_f32(_silu(c_ref[...]), w_ref[...]) + b_ref[...]


def _ada(c_pad, w, b):
    d, n = w.shape
    tn = 1024
    return pl.pallas_call(
        _ada_kernel,
        grid=(n // tn,),
        in_specs=[pl.BlockSpec((SUBLANES, d), lambda j: (0, 0)),
                  pl.BlockSpec((d, tn), lambda j: (0, j)),
                  pl.BlockSpec((1, tn), lambda j: (0, j))],
        out_specs=pl.BlockSpec((SUBLANES, tn), lambda j: (0, j)),
        out_shape=jax.ShapeDtypeStruct((SUBLANES, n), F32),
        compiler_params=_cp("parallel"),
        name="ada",
    )(c_pad, w, b.reshape(1, n))


def _norm_mod(x, gain, shift, scale):
    xn = x * lax.rsqrt(jnp.mean(x * x, axis=-1, keepdims=True) + EPS) * gain
    return xn * (1.0 + scale) + shift


def _nm_kernel(h_ref, gain_ref, shift_ref, scale_ref, w_ref, o_ref, u_scr, *, tiles_per_batch):
    i = pl.program_id(0)

    @pl.when(pl.program_id(1) == 0)
    def _():
        b = i // tiles_per_batch
        u = _norm_mod(h_ref[...], gain_ref[...], shift_ref[pl.ds(b, 1), :], scale_ref[pl.ds(b, 1), :])
        u_scr[...] = u.astype(BF16)

    o_ref[...] = _dot(u_scr[...], w_ref[...]).astype(o_ref.dtype)


def _norm_mod_matmul(h, gain, ada, shift_blk, scale_blk, w, seq, tn, name):
    t, d = h.shape
    n = w.shape[1]
    tm = ROW_TILE
    return pl.pallas_call(
        functools.partial(_nm_kernel, tiles_per_batch=seq // tm),
        grid=(t // tm, n // tn),
        in_specs=[pl.BlockSpec((tm, d), lambda i, j: (i, 0)),
                  pl.BlockSpec((1, d), lambda i, j: (0, 0)),
                  pl.BlockSpec((SUBLANES, d), lambda i, j: (0, shift_blk)),
                  pl.BlockSpec((SUBLANES, d), lambda i, j: (0, scale_blk)),
                  pl.BlockSpec((d, tn), lambda i, j: (0, j))],
        out_specs=pl.BlockSpec((tm, tn), lambda i, j: (i, j)),
        out_shape=jax.ShapeDtypeStruct((t, n), F32),
        scratch_shapes=[pltpu.VMEM((tm, d), BF16)],
        compiler_params=_cp("parallel", "arbitrary"),
        name=name,
    )(h, gain.reshape(1, d), ada, ada, w)


def _logsig_cumsum_kernel(z_ref, b_ref, tri_ref, o_ref, carry):
    @pl.when(pl.program_id(0) == 0)
    def _():
        carry[...] = jnp.zeros_like(carry)

    z = z_ref[...] + b_ref[...]
    lf = jnp.minimum(z, 0.0) - jnp.log1p(jnp.exp(-jnp.abs(z)))
    hi, mid, lo = _split3(lf)
    tri = tri_ref[...]
    c = _dot(tri, hi) + _dot(tri, mid) + _dot(tri, lo) + carry[...]
    o_ref[...] = c
    ts = c.shape[0]
    carry[...] = c[ts - 1:ts, :]


def _logsig_cumsum(z, bias):
    s, c = z.shape
    ts = 256
    tri = (jnp.arange(ts)[:, None] >= jnp.arange(ts)[None, :]).astype(BF16)
    return pl.pallas_call(
        _logsig_cumsum_kernel,
        grid=(s // ts,),
        in_specs=[pl.BlockSpec((ts, c), lambda i: (i, 0)),
                  pl.BlockSpec((1, c), lambda i: (0, 0)),
                  pl.BlockSpec((ts, ts), lambda i: (0, 0))],
        out_specs=pl.BlockSpec((ts, c), lambda i: (i, 0)),
        out_shape=jax.ShapeDtypeStruct((s, c), F32),
        scratch_shapes=[pltpu.VMEM((1, c), F32)],
        compiler_params=_cp("arbitrary"),
        name="fox_logsig_cumsum",
    )(z, bias, tri)


def _head_rms(x, gain):
    return x * lax.rsqrt(jnp.mean(x * x, axis=-1, keepdims=True) + EPS) * gain


def _fox_prep_kernel(q_ref, k_ref, v_ref, cum_ref, qg_ref, kg_ref, qo_ref, ko_ref, vo_ref):
    bh0 = (pl.program_id(0) * (N_HEADS_A // 2) + pl.program_id(1)) * 2
    ts = q_ref.shape[0]
    cum = cum_ref[...]
    lane_c = lax.broadcasted_iota(I32, cum.shape, 1)
    lane = lax.broadcasted_iota(I32, (ts, LANES), 1)
    low = lane < HEAD_DIM
    ext = lane - HEAD_DIM

    def pair_rms(x, gain):
        sq = x * x
        ss = jnp.where(low, jnp.sum(jnp.where(low, sq, 0.0), axis=-1, keepdims=True),
                       jnp.sum(jnp.where(low, 0.0, sq), axis=-1, keepdims=True))
        return x * lax.rsqrt(ss * (1.0 / HEAD_DIM) + EPS) * gain

    qn = pair_rms(q_ref[...], qg_ref[...]) * QSCALE
    kn = pair_rms(k_ref[...], kg_ref[...])
    v = v_ref[...]
    for h2 in range(2):
        c = jnp.sum(jnp.where(lane_c == bh0 + h2, cum, 0.0), axis=-1, keepdims=True) * LOG2E
        hi, mid, lo = (p.astype(F32) for p in _split3(c))
        head = (lambda x: x) if h2 == 0 else (lambda x: pltpu.roll(x, HEAD_DIM, 1))
        ext_q = jnp.where(ext == 0, hi, jnp.where(ext == 1, mid, jnp.where(ext == 2, lo,
                          jnp.where(ext < 6, 1.0, 0.0))))
        ext_k = jnp.where(ext < 3, 1.0, jnp.where(ext == 3, -hi, jnp.where(ext == 4, -mid,
                          jnp.where(ext == 5, -lo, 0.0))))
        qo_ref[0, h2] = jnp.where(low, head(qn), ext_q).astype(BF16)
        ko_ref[0, h2] = jnp.where(low, head(kn), ext_k).T.astype(BF16)
        vo_ref[0, h2] = jnp.where(low, head(v), jnp.where(ext == 0, 1.0, 0.0)).astype(BF16)


def _fox_prep(proj, cum, q_gain, k_gain, batch, seq):
    ts = ROW_TILE
    nsb = seq // ts
    hp = N_HEADS_A // 2
    out = jax.ShapeDtypeStruct((batch, N_HEADS_A, seq, LANES), BF16)
    out_t = jax.ShapeDtypeStruct((batch, N_HEADS_A, LANES, seq), BF16)
    ospec = pl.BlockSpec((1, 2, ts, LANES), lambda b, p, i: (b, p, i, 0))
    ospec_t = pl.BlockSpec((1, 2, LANES, ts), lambda b, p, i: (b, p, 0, i))
    return pl.pallas_call(
        _fox_prep_kernel,
        grid=(batch, hp, nsb),
        in_specs=[pl.BlockSpec((ts, LANES), lambda b, p, i: (b * nsb + i, p)),
                  pl.BlockSpec((ts, LANES), lambda b, p, i: (b * nsb + i, hp + p)),
                  pl.BlockSpec((ts, LANES), lambda b, p, i: (b * nsb + i, 2 * hp + p)),
                  pl.BlockSpec((ts, cum.shape[1]), lambda b, p, i: (i, 0)),
                  pl.BlockSpec((1, LANES), lambda b, p, i: (0, 0)),
                  pl.BlockSpec((1, LANES), lambda b, p, i: (0, 0))],
        out_specs=[ospec, ospec_t, ospec],
        out_shape=[out, out_t, out],
        compiler_params=_cp("parallel", "parallel", "parallel"),
        name="fox_prep",
    )(proj, proj, proj, cum, jnp.tile(q_gain, LANES // HEAD_DIM).reshape(1, LANES),
      jnp.tile(k_gain, LANES // HEAD_DIM).reshape(1, LANES))


def _store_heads(o_ref, o, tq, grouped):
    if not grouped:
        o_ref[0, 0] = o.astype(o_ref.dtype)
        return
    for hh in range(o.shape[0] // tq):
        o_ref[:, hh * HEAD_DIM:(hh + 1) * HEAD_DIM] = o[hh * tq:(hh + 1) * tq].astype(o_ref.dtype)


def _flash_kernel(*refs, tq, tk, with_bias, grouped):
    if with_bias:
        q_ref, nb_ref, k_ref, v_ref, o_ref, m_scr, acc_scr = refs
    else:
        q_ref, k_ref, v_ref, o_ref, m_scr, acc_scr = refs
    i = pl.program_id(1)
    q = q_ref[0, 0]
    rows = q.shape[0]
    if with_bias:
        q = jnp.concatenate([jnp.tile(nb_ref[0], (rows // tq, 1)), q], axis=1)
    m_scr[...] = jnp.full(m_scr.shape, NEG, F32)
    acc_scr[...] = jnp.zeros(acc_scr.shape, F32)
    q_pos = i * tq + jnp.bitwise_and(lax.broadcasted_iota(I32, (rows, 1), 0), tq - 1)

    def step(j, masked, row_lo=0):
        ks = pl.multiple_of(j * tk, tk)
        s = _dot(q[row_lo:], k_ref[0, :, pl.ds(ks, tk)])
        if masked:
            k_pos = ks + lax.broadcasted_iota(I32, (1, tk), 1)
            s = jnp.where(k_pos <= q_pos[row_lo:], s, NEG)
        m_old = m_scr[row_lo:, :]
        m_new = jnp.maximum(m_old, jnp.max(s, axis=-1, keepdims=True))
        p = jnp.exp2(s - jnp.tile(m_new, (1, tk // LANES))).astype(BF16)
        acc_scr[row_lo:, :] = jnp.exp2(m_old - m_new) * acc_scr[row_lo:, :] + _dot(p, v_ref[0, pl.ds(ks, tk), :])
        m_scr[row_lo:, :] = m_new

    n_clear = (i * tq + 1) // tk
    n_diag = max(tq // tk, 1)

    def pair(jj, c):
        step(2 * jj, False)
        step(2 * jj + 1, False)
        return c

    lax.fori_loop(0, n_clear // 2, pair, 0)

    @pl.when(lax.rem(n_clear, 2) == 1)
    def _():
        step(n_clear - 1, False)

    for d in range(n_diag):
        step(n_clear + d, True, row_lo=d * tk if rows == tq else 0)

    acc = acc_scr[...]
    _store_heads(o_ref, acc[:, :HEAD_DIM] / acc[:, HEAD_DIM:HEAD_DIM + 1], tq, grouped)


def _flash(q, k, v, tq, tk, name, bias=None, grouped_out=None):
    bh, nq, rows, kd = q.shape
    s = k.shape[2]
    assert tq % tk == 0 or tk % tq == 0
    in_specs = [pl.BlockSpec((1, 1, rows, kd), lambda b, i: (b, i, 0, 0))]
    args = [q]
    if bias is not None:
        in_specs.append(pl.BlockSpec((1, tq, bias.shape[2]), lambda b, i: (b, i, 0)))
        args.append(bias)
    in_specs += [pl.BlockSpec((1, k.shape[1], s), lambda b, i: (b, 0, 0)),
                 pl.BlockSpec((1, s, LANES), lambda b, i: (b, 0, 0))]
    if grouped_out is None:
        out_spec = pl.BlockSpec((1, 1, rows, HEAD_DIM), lambda b, i: (b, i, 0, 0))
        out_shape = jax.ShapeDtypeStruct((bh, nq, rows, HEAD_DIM), BF16)
    else:
        out_spec, out_shape = _grouped_out(grouped_out, nq, tq, rows)
    return pl.pallas_call(
        functools.partial(_flash_kernel, tq=tq, tk=tk, with_bias=bias is not None, grouped=grouped_out is not None),
        grid=(bh, nq),
        in_specs=in_specs,
        out_specs=out_spec,
        out_shape=out_shape,
        scratch_shapes=[pltpu.VMEM((rows, LANES), F32), pltpu.VMEM((rows, LANES), F32)],
        compiler_params=_cp("parallel", "arbitrary"),
        name=name,
    )(*args, k, v)


def _grouped_out(batch_groups, nq, tq, rows):
    batch, groups = batch_groups
    width = rows // tq * HEAD_DIM
    spec = pl.BlockSpec((tq, width), lambda b, i: ((b // groups) * nq + i, lax.rem(b, groups)))
    return spec, jax.ShapeDtypeStruct((batch * nq * tq, groups * width), BF16)


def _window_kernel(q_ref, k_ref, v_ref, o_ref, *, tq, window):
    i = pl.program_id(1)
    q = q_ref[0, 0]
    rows = q.shape[0]
    span = window + tq
    ks = pl.multiple_of(jnp.maximum(i * tq - window, 0), tq)
    q_pos = i * tq + jnp.bitwise_and(lax.broadcasted_iota(I32, (rows, 1), 0), tq - 1)
    k_pos = ks + lax.broadcasted_iota(I32, (1, span), 1)
    ok = (q_pos - k_pos).astype(jnp.uint32) < window
    s = jnp.where(ok, _dot(q, k_ref[0, :, pl.ds(ks, span)]), NEG)
    p = jnp.exp2(s - jnp.max(s, axis=-1, keepdims=True)).astype(BF16)
    acc = _dot(p, v_ref[0, pl.ds(ks, span), :])
    _store_heads(o_ref, acc[:, :HEAD_DIM] / acc[:, HEAD_DIM:HEAD_DIM + 1], tq, True)


def _window_attention(q, k, v, tq, window, batch_groups):
    bh, nq, rows, kd = q.shape
    s = k.shape[2]
    assert window % tq == 0 and window + tq <= s
    out_spec, out_shape = _grouped_out(batch_groups, nq, tq, rows)
    return pl.pallas_call(
        functools.partial(_window_kernel, tq=tq, window=window),
        grid=(bh, nq),
        in_specs=[pl.BlockSpec((1, 1, rows, kd), lambda b, i: (b, i, 0, 0)),
                  pl.BlockSpec((1, kd, s), lambda b, i: (b, 0, 0)),
                  pl.BlockSpec((1, s, LANES), lambda b, i: (b, 0, 0))],
        out_specs=out_spec,
        out_shape=out_shape,
        compiler_params=_cp("parallel", "parallel"),
        name="nsa_window",
    )(q, k, v)


def _oproj_kernel(o_ref, w_ref, h_ref, g_ref, out_ref, *, tiles_per_batch):
    b = pl.program_id(0) // tiles_per_batch
    out_ref[...] = h_ref[...] + g_ref[pl.ds(b, 1), :] * _dot(o_ref[...], w_ref[...])


def _oproj_residual(o, w, h, ada, gate_blk, seq):
    t, d = h.shape
    tm = ROW_TILE
    return pl.pallas_call(
        functools.partial(_oproj_kernel, tiles_per_batch=seq // tm),
        grid=(t // tm,),
        in_specs=[pl.BlockSpec((tm, o.shape[1]), lambda i: (i, 0)),
                  pl.BlockSpec(w.shape, lambda i: (0, 0)),
                  pl.BlockSpec((tm, d), lambda i: (i, 0)),
                  pl.BlockSpec((SUBLANES, d), lambda i: (0, gate_blk))],
        out_specs=pl.BlockSpec((tm, d), lambda i: (i, 0)),
        out_shape=jax.ShapeDtypeStruct((t, d), F32),
        compiler_params=_cp("parallel"),
        name="oproj_residual",
    )(o, w, h, ada)


def _seg_rms(x, seg_ones, gain_tiled):
    ss = _dot((x * x).astype(BF16), seg_ones)
    return x * lax.rsqrt(ss * (1.0 / HEAD_DIM) + EPS) * gain_tiled


def _rope(x, cos_t, sin_t):
    width = x.shape[1]
    reps = width // LANES
    lane = jnp.bitwise_and(lax.broadcasted_iota(I32, x.shape, 1), HEAD_DIM - 1)
    half = ROPE_DIM // 2
    partner = jnp.where(lane < half, pltpu.roll(x, width - half, 1), pltpu.roll(x, half, 1))
    return x * jnp.tile(cos_t, (1, reps)) + partner * jnp.tile(sin_t, (1, reps))


def _nsa_q_kernel(p_ref, z_ref, cos_ref, sin_ref, gain_ref, seg_ref, bg_ref, qp_ref, qr_ref, g_ref):
    q = _seg_rms(p_ref[...], seg_ref[...], gain_ref[...]) * QSCALE
    qp_ref[...] = q.astype(BF16)
    qr_ref[...] = _rope(q, cos_ref[...], sin_ref[...]).astype(BF16)
    g_ref[...] = _sigmoid(z_ref[...] + bg_ref[...])


def _nsa_q_prep(proj, cos_t, sin_t, q_gain, b_gate_pad):
    t = proj.shape[0]
    hd = N_HEADS_B * HEAD_DIM
    tm = ROW_TILE
    seg = (jnp.arange(hd)[:, None] // HEAD_DIM == jnp.arange(hd)[None, :] // HEAD_DIM).astype(BF16)
    row = lambda i: (i, 0)
    fix = lambda i: (0, 0)
    return pl.pallas_call(
        _nsa_q_kernel,
        grid=(t // tm,),
        in_specs=[pl.BlockSpec((tm, hd), row),
                  pl.BlockSpec((tm, LANES), lambda i: (i, hd // LANES)),
                  pl.BlockSpec((tm, LANES), row), pl.BlockSpec((tm, LANES), row),
                  pl.BlockSpec((1, hd), fix), pl.BlockSpec((hd, hd), fix), pl.BlockSpec((1, LANES), fix)],
        out_specs=[pl.BlockSpec((tm, hd), row), pl.BlockSpec((tm, hd), row), pl.BlockSpec((tm, LANES), row)],
        out_shape=[jax.ShapeDtypeStruct((t, hd), BF16), jax.ShapeDtypeStruct((t, hd), BF16),
                   jax.ShapeDtypeStruct((t, LANES), F32)],
        compiler_params=_cp("parallel"),
        name="nsa_q_prep",
    )(proj, proj, cos_t, sin_t, jnp.tile(q_gain, N_HEADS_B).reshape(1, hd), seg, b_gate_pad)


def _nsa_kv_kernel(kv_ref, cos_ref, sin_ref, g1_ref, g2_ref, seg_ref, ks_ref, vs_ref, kw_ref, vw_ref):
    gw = N_KV_GROUPS_B * HEAD_DIM
    ks = _seg_rms(kv_ref[:, 2 * gw:3 * gw], seg_ref[...], g1_ref[...])
    kw = _seg_rms(kv_ref[:, 4 * gw:5 * gw], seg_ref[...], g2_ref[...])
    ks_ref[...] = _rope(ks, cos_ref[...], sin_ref[...]).astype(BF16)
    kw_ref[...] = _rope(kw, cos_ref[...], sin_ref[...]).astype(BF16)
    vs_ref[...] = kv_ref[:, 3 * gw:4 * gw].astype(BF16)
    vw_ref[...] = kv_ref[:, 5 * gw:6 * gw].astype(BF16)


def _nsa_kv_prep(kv, cos_t, sin_t, k_gains):
    t, n = kv.shape
    gw = N_KV_GROUPS_B * HEAD_DIM
    tm = ROW_TILE
    seg = (jnp.arange(gw)[:, None] // HEAD_DIM == jnp.arange(gw)[None, :] // HEAD_DIM).astype(BF16)
    row = lambda i: (i, 0)
    fix = lambda i: (0, 0)
    out = jax.ShapeDtypeStruct((t, gw), BF16)
    return pl.pallas_call(
        _nsa_kv_kernel,
        grid=(t // tm,),
        in_specs=[pl.BlockSpec((tm, n), row), pl.BlockSpec((tm, LANES), row), pl.BlockSpec((tm, LANES), row),
                  pl.BlockSpec((1, gw), fix), pl.BlockSpec((1, gw), fix), pl.BlockSpec((gw, gw), fix)],
        out_specs=[pl.BlockSpec((tm, gw), row)] * 4,
        out_shape=[out] * 4,
        compiler_params=_cp("parallel"),
        name="nsa_kv_prep",
    )(kv, cos_t, sin_t, jnp.tile(k_gains[1], N_KV_GROUPS_B).reshape(1, gw),
      jnp.tile(k_gains[2], N_KV_GROUPS_B).reshape(1, gw), seg)


def _compress_kernel(x_ref, pe_ref, w1_ref, w2_ref, g_ref, o_ref, *, norm):
    x = (x_ref[...] + pe_ref[...]).astype(BF16)
    y = _dot(_silu(_dot(x, w1_ref[...])).astype(BF16), w2_ref[...])
    if norm:
        y = _head_rms(y, g_ref[...])
    o_ref[...] = y.astype(o_ref.dtype)


def _compress(x, pe_flat, w1, w2, gain, norm, name):
    r, kdim = x.shape
    tm = min(ROW_TILE, r)
    fix = lambda i: (0, 0)
    return pl.pallas_call(
        functools.partial(_compress_kernel, norm=norm),
        grid=(r // tm,),
        in_specs=[pl.BlockSpec((tm, kdim), lambda i: (i, 0)), pl.BlockSpec((1, kdim), fix),
                  pl.BlockSpec(w1.shape, fix), pl.BlockSpec(w2.shape, fix), pl.BlockSpec((1, HEAD_DIM), fix)],
        out_specs=pl.BlockSpec((tm, HEAD_DIM), lambda i: (i, 0)),
        out_shape=jax.ShapeDtypeStruct((r, HEAD_DIM), BF16),
        compiler_params=_cp("parallel"),
        name=name,
    )(x, pe_flat, w1, w2, gain.reshape(1, HEAD_DIM))


def _cmp_select_kernel(q_ref, k_ref, v_ref, o_ref, nb_ref, *, tq, n_sel):
    i = pl.program_id(1)
    q = q_ref[0, 0]
    rows = q.shape[0]
    nblk = k_ref.shape[1]
    logits = _dot_nt(q, k_ref[0])
    t_pos = i * tq + jnp.bitwise_and(lax.broadcasted_iota(I32, (rows, 1), 0), tq - 1)
    blk = lax.broadcasted_iota(I32, (1, nblk), 1)
    vis = (blk + 1) * CMP_BLOCK - 1 <= t_pos
    lm = jnp.where(vis, logits, NEG)
    e = jnp.exp2(lm - jnp.max(lm, axis=-1, keepdims=True))
    p = jnp.where(vis, e / jnp.sum(e, axis=-1, keepdims=True), 0.0)
    _store_heads(o_ref, _dot(p.astype(BF16), v_ref[0])[:, :HEAD_DIM], tq, True)

    imp = p[0:tq]
    for hh in range(1, rows // tq):
        imp = imp + p[hh * tq:(hh + 1) * tq]
    cur = lax.shift_right_logical(t_pos[0:tq], CMP_BLOCK.bit_length() - 1)
    forced = jnp.logical_or(blk == 0, jnp.logical_or(blk == cur, blk == cur - 1))
    cand = jnp.where(jnp.logical_or(forced, blk > cur), -jnp.inf, imp)
    sel = jnp.where(forced, 1.0, 0.0)
    blk_f = blk.astype(F32)
    for _ in range(max(n_sel - 3, 0)):
        best = jnp.max(cand, axis=-1, keepdims=True)
        first = jnp.min(jnp.where(cand == best, blk_f, float(nblk)), axis=-1, keepdims=True)
        hit = blk_f == first
        sel = jnp.where(jnp.logical_and(hit, best > -jnp.inf), 1.0, sel)
        cand = jnp.where(hit, -jnp.inf, cand)
    nb_ref[0] = jnp.where(sel > 0.5, 0.0, NEG).astype(nb_ref.dtype)


def _cmp_select(q, k_cmp, v_cmp, seq, tq, batch_groups):
    bg, nq, rows, _ = q.shape
    nblk = k_cmp.shape[1]
    assert min(N_SELECT, nblk) >= 3
    out_spec, out_shape = _grouped_out(batch_groups, nq, tq, rows)
    return pl.pallas_call(
        functools.partial(_cmp_select_kernel, tq=tq, n_sel=min(N_SELECT, nblk)),
        grid=(bg, nq),
        in_specs=[pl.BlockSpec((1, 1, rows, HEAD_DIM), lambda b, i: (b, i, 0, 0)),
                  pl.BlockSpec((1, nblk, HEAD_DIM), lambda b, i: (b, 0, 0)),
                  pl.BlockSpec((1, nblk, LANES), lambda b, i: (b, 0, 0))],
        out_specs=[out_spec, pl.BlockSpec((1, tq, nblk), lambda b, i: (b, i, 0))],
        out_shape=[out_shape, jax.ShapeDtypeStruct((bg, seq, nblk), BF16)],
        compiler_params=_cp("parallel", "parallel"),
        name="nsa_cmp_select",
    )(q, k_cmp, v_cmp)


def _nsa_out_kernel(oc_ref, os_ref, ow_ref, g_ref, e_ref, w_ref, h_ref, ada_ref, out_ref, *, tiles_per_batch):
    b = pl.program_id(0) // tiles_per_batch
    g_hi, g_lo = _split2(g_ref[...])
    o = None
    for j, o_ref in enumerate((oc_ref, os_ref, ow_ref)):
        gate = _dot(g_hi, e_ref[j]) + _dot(g_lo, e_ref[j])
        term = gate * o_ref[...].astype(F32)
        o = term if o is None else o + term
    out_ref[...] = h_ref[...] + ada_ref[pl.ds(b, 1), :] * _dot(o.astype(BF16), w_ref[...])


def _nsa_out(o_cmp, o_sel, o_win, gates, w_o, h, ada, gate_blk, seq):
    t, d = h.shape
    hd = o_cmp.shape[1]
    tm = ROW_TILE
    col = jnp.arange(LANES)[:, None]
    lane = jnp.arange(hd)[None, :]
    expand = jnp.stack([(col == (lane // HEAD_DIM) * 3 + j) for j in range(3)]).astype(BF16)
    row = lambda i: (i, 0)
    return pl.pallas_call(
        functools.partial(_nsa_out_kernel, tiles_per_batch=seq // tm),
        grid=(t // tm,),
        in_specs=[pl.BlockSpec((tm, hd), row), pl.BlockSpec((tm, hd), row), pl.BlockSpec((tm, hd), row),
                  pl.BlockSpec((tm, LANES), row),
                  pl.BlockSpec((3, LANES, hd), lambda i: (0, 0, 0)),
                  pl.BlockSpec(w_o.shape, lambda i: (0, 0)),
                  pl.BlockSpec((tm, d), row),
                  pl.BlockSpec((SUBLANES, d), lambda i: (0, gate_blk))],
        out_specs=pl.BlockSpec((tm, d), row),
        out_shape=jax.ShapeDtypeStruct((t, d), F32),
        compiler_params=_cp("parallel"),
        name="nsa_out",
    )(o_cmp, o_sel, o_win, gates, expand, w_o, h, ada)


def _route_kernel(h_ref, gain_ref, shift_ref, scale_ref, wr_ref, br_ref, tri_ref, upper_ref,
                  u_ref, pos_ref, w_ref, nu_ref, off_ref, *, tiles_per_batch):
    b = (pl.program_id(0) * ROUTE_TILES) // tiles_per_batch
    shift, scale = shift_ref[pl.ds(b, 1), :], scale_ref[pl.ds(b, 1), :]
    for tile in range(ROUTE_TILES):
        rows = slice(tile * MOE_TOK_TILE, (tile + 1) * MOE_TOK_TILE)
        _route_tile(h_ref[rows, :], gain_ref[...], shift, scale, wr_ref, br_ref, tri_ref, upper_ref,
                    u_ref.at[rows, :], pos_ref.at[rows, :], w_ref.at[rows, :], nu_ref.at[tile], off_ref.at[tile])


def _route_tile(h, gain, shift, scale, wr_ref, br_ref, tri_ref, upper_ref, u_ref, pos_ref, w_ref, nu_ref, off_ref):
    u = _norm_mod(h, gain, shift, scale)
    u_ref[...] = u.astype(BF16)
    scores = _sigmoid(_dot_f32(u, wr_ref[...]))
    lane = lax.broadcasted_iota(I32, scores.shape, 1)
    lane_f = lane.astype(F32)
    biased = jnp.where(lane < N_EXPERTS, scores + br_ref[...], -jnp.inf)
    chosen = jnp.zeros(scores.shape, F32)
    hits, val = [], []
    upper = upper_ref[...]
    for _ in range(TOP_K):
        best = jnp.max(biased, axis=-1, keepdims=True)
        first = jnp.min(jnp.where(biased == best, lane_f, float(LANES)), axis=-1, keepdims=True)
        hit = lane_f == first
        hits.append(hit)
        val.append(jnp.sum(jnp.where(hit, scores, 0.0), axis=-1, keepdims=True))
        biased = jnp.where(hit, -jnp.inf, biased)
        chosen = jnp.where(hit, 1.0, chosen)
    top_s = jnp.concatenate(val, axis=1)
    w_ref[...] = top_s / jnp.sum(top_s, axis=-1, keepdims=True) * ROUTED_SCALE

    before = _dot(tri_ref[...], chosen.astype(BF16))
    count = jnp.sum(chosen, axis=0, keepdims=True)
    units = jnp.floor((count + (MOE_UNIT - 1)) * (1.0 / MOE_UNIT))
    padded = jnp.broadcast_to(units * MOE_UNIT, (SUBLANES, LANES))
    offset = _dot(padded.astype(BF16), upper)[0:1]
    row_of = before + offset
    pos = [jnp.sum(jnp.where(hit, row_of, 0.0), axis=-1, keepdims=True) for hit in hits]
    pos_ref[...] = jnp.concatenate(pos, axis=1).astype(I32)
    nu_ref[...] = units.astype(I32)
    off_ref[...] = offset.astype(I32)


def _route(h, gain, ada, shift_blk, scale_blk, w_router_pad, b_router_pad, seq):
    t, d = h.shape
    tt = MOE_TOK_TILE
    n_tiles = t // tt
    rt = ROUTE_TILES * tt
    assert n_tiles % ROUTE_TILES == 0 and seq % rt == 0
    row = lambda i: (i, 0)
    fix = lambda i: (0, 0)
    tri = (jnp.arange(tt)[:, None] > jnp.arange(tt)[None, :]).astype(BF16)
    upper = (jnp.arange(LANES)[:, None] < jnp.arange(LANES)[None, :]).astype(BF16)
    per_tile = jax.ShapeDtypeStruct((n_tiles, 1, LANES), I32)
    per_tile_spec = pl.BlockSpec((ROUTE_TILES, 1, LANES), lambda i: (i, 0, 0))
    return pl.pallas_call(
        functools.partial(_route_kernel, tiles_per_batch=seq // tt),
        grid=(n_tiles // ROUTE_TILES,),
        in_specs=[pl.BlockSpec((rt, d), row), pl.BlockSpec((1, d), fix),
                  pl.BlockSpec((SUBLANES, d), lambda i: (0, shift_blk)),
                  pl.BlockSpec((SUBLANES, d), lambda i: (0, scale_blk)),
                  pl.BlockSpec((d, LANES), fix), pl.BlockSpec((1, LANES), fix),
                  pl.BlockSpec((tt, tt), fix), pl.BlockSpec((LANES, LANES), fix)],
        out_specs=[pl.BlockSpec((rt, d), row), pl.BlockSpec((rt, TOP_K), row), pl.BlockSpec((rt, TOP_K), row),
                   per_tile_spec, per_tile_spec],
        out_shape=[jax.ShapeDtypeStruct((t, d), BF16), jax.ShapeDtypeStruct((t, TOP_K), I32),
                   jax.ShapeDtypeStruct((t, TOP_K), F32), per_tile, per_tile],
        compiler_params=_cp("parallel"),
        name="moe_route",
    )(h, gain.reshape(1, d), ada, ada, w_router_pad, b_router_pad, tri, upper)


PLAN_SLOTS = 256
PLAN_TILES = LANES


def _plan_kernel(nu_ref, off_ref, upper_ref, lower_ref, row_ref, exp_ref, nused_ref,
                 *, n_tiles, r_loc, upb, n_unit_slots, n_dump_units):
    nu = nu_ref[...]
    nu_t = nu.T
    off_t = off_ref[...].T
    upper = upper_ref[...]
    lower = lower_ref[...]
    lane = lax.broadcasted_iota(I32, (PLAN_SLOTS, LANES), 1)
    lane_f = lane.astype(F32)

    def exact_dot(a, b_bf16, left):
        parts = _split3(a)
        return sum(_dot(b_bf16, p) if left else _dot(p, b_bf16) for p in parts)

    units_before_t = _dot(nu_t.astype(BF16), upper)
    count_col = jnp.sum(nu_t, axis=1, keepdims=True)
    count_row = jnp.sum(nu, axis=0, keepdims=True)
    padded_col = jnp.floor((count_col + (upb - 1)) * (1.0 / upb)) * upb
    padded_row = jnp.floor((count_row + (upb - 1)) * (1.0 / upb)) * upb
    start_col = exact_dot(jnp.broadcast_to(padded_col, (LANES, LANES)), lower, True)
    real_col = exact_dot(jnp.broadcast_to(count_col, (LANES, LANES)), lower, True)
    start_row = exact_dot(jnp.broadcast_to(padded_row, (SUBLANES, LANES)), upper, False)[0:1]
    seg_start_t = start_col + units_before_t
    seg_real_t = real_col + units_before_t
    tile_id = lax.broadcasted_iota(I32, (LANES, LANES), 1).astype(F32)
    seg_row_t = tile_id * r_loc + off_t

    idx = pl.program_id(0) * PLAN_SLOTS + lax.broadcasted_iota(I32, (PLAN_SLOTS, 1), 0)
    slot = jnp.where(idx < n_unit_slots, idx, (idx - n_unit_slots) * upb).astype(F32)

    in_e = jnp.logical_and(start_row <= slot, lane < N_EXPERTS)
    expert = jnp.sum(jnp.where(in_e, 1.0, 0.0), axis=-1, keepdims=True) - 1.0
    pick_e = (lane_f == expert).astype(BF16)
    look = lambda tbl: exact_dot(tbl, pick_e, True)
    starts = look(seg_start_t)
    in_t = jnp.logical_and(starts <= slot, lane < n_tiles)
    tile = jnp.sum(jnp.where(in_t, 1.0, 0.0), axis=-1, keepdims=True) - 1.0
    at = lambda x: jnp.sum(jnp.where(lane_f == tile, x, 0.0), axis=-1, keepdims=True)
    m = slot - at(starts)
    n = at(look(nu_t))
    row = at(look(seg_row_t)) + m * MOE_UNIT
    pad_idx = slot - (at(look(seg_real_t)) + jnp.minimum(m, n))
    dump = n_tiles * r_loc + jnp.minimum(pad_idx, n_dump_units - 1.0) * MOE_UNIT
    row_ref[...] = jnp.where(m < n, row, dump).astype(I32)
    exp_ref[...] = expert.astype(I32)
    total = jnp.sum(padded_row, axis=-1, keepdims=True) * (1.0 / upb)
    nused_ref[...] = jnp.broadcast_to(total, (SUBLANES, LANES)).astype(I32)


def _plan(nu, off, n_tiles, r_loc, n_blocks, n_dump_units):
    assert n_tiles <= PLAN_TILES
    upb = MOE_BLOCK // MOE_UNIT
    n_unit_slots = -(-(n_blocks * upb) // PLAN_SLOTS) * PLAN_SLOTS
    n_steps = n_unit_slots // PLAN_SLOTS + -(-(n_blocks + 2) // PLAN_SLOTS)
    pad = lambda x: jnp.pad(x.astype(F32), ((0, PLAN_TILES - n_tiles), (0, 0)))
    upper = (jnp.arange(LANES)[:, None] < jnp.arange(LANES)[None, :]).astype(BF16)
    fix = lambda s: (0, 0)
    col = pl.BlockSpec((PLAN_SLOTS, 1), lambda s: (s, 0))
    rows, experts, n_used = pl.pallas_call(
        functools.partial(_plan_kernel, n_tiles=n_tiles, r_loc=r_loc, upb=upb, n_unit_slots=n_unit_slots,
                          n_dump_units=n_dump_units),
        grid=(n_steps,),
        in_specs=[pl.BlockSpec((PLAN_TILES, LANES), fix), pl.BlockSpec((PLAN_TILES, LANES), fix),
                  pl.BlockSpec((LANES, LANES), fix), pl.BlockSpec((LANES, LANES), fix)],
        out_specs=[col, col, pl.BlockSpec((SUBLANES, LANES), fix)],
        out_shape=[jax.ShapeDtypeStruct((n_steps * PLAN_SLOTS, 1), I32),
                   jax.ShapeDtypeStruct((n_steps * PLAN_SLOTS, 1), I32),
                   jax.ShapeDtypeStruct((SUBLANES, LANES), I32)],
        compiler_params=_cp("arbitrary"),
        name="moe_plan",
    )(pad(nu), pad(off), upper, upper.T)
    return (rows[:n_blocks * upb, 0], experts[n_unit_slots:n_unit_slots + n_blocks + 2, 0], n_used[0, :1])


def _one_hot_t(pos_ref, weight_ref, n_rows):
    tt = pos_ref.shape[0]
    col = lax.broadcasted_iota(I32, (tt, n_rows), 1)
    acc = jnp.zeros((tt, n_rows), F32)
    for k in range(TOP_K):
        acc = jnp.where(col == pos_ref[:, k:k + 1], 1.0 if weight_ref is None else weight_ref[:, k:k + 1], acc)
    return acc.astype(BF16)


def _dispatch_kernel(u_ref, pos_ref, x_ref):
    x_ref[0] = _dot_tn(_one_hot_t(pos_ref, None, x_ref.shape[1]), u_ref[...]).astype(x_ref.dtype)


def _dispatch(u, lpos, n_tiles_total, r_loc):
    t, d = u.shape
    tt = MOE_TOK_TILE
    n_real = t // tt
    return pl.pallas_call(
        _dispatch_kernel,
        grid=(n_tiles_total,),
        in_specs=[pl.BlockSpec((tt, d), lambda i: (jnp.minimum(i, n_real - 1), 0)),
                  pl.BlockSpec((tt, TOP_K), lambda i: (i, 0))],
        out_specs=pl.BlockSpec((1, r_loc, d), lambda i: (i, 0, 0)),
        out_shape=jax.ShapeDtypeStruct((n_tiles_total, r_loc, d), BF16),
        compiler_params=_cp("parallel"),
        name="moe_dispatch",
    )(u, lpos)


def _expert_kernel(tbl_ref, be_ref, nblk_ref, x_hbm, wg_ref, wu_ref, wd_ref, y_hbm,
                   xbuf, ybuf, wg_bf, wu_bf, wd_bf, gsem, ssem, *, upb):
    b = pl.program_id(0)
    nblk = nblk_ref[0]

    def gather(blk, sl):
        for m in range(upb):
            row = pl.multiple_of(tbl_ref[blk * upb + m], MOE_UNIT)
            yield pltpu.make_async_copy(x_hbm.at[pl.ds(row, MOE_UNIT), :],
                                        xbuf.at[sl, pl.ds(m * MOE_UNIT, MOE_UNIT), :], gsem.at[sl, m])

    def scatter(blk, sl):
        for m in range(upb):
            row = pl.multiple_of(tbl_ref[blk * upb + m], MOE_UNIT)
            yield pltpu.make_async_copy(ybuf.at[sl, pl.ds(m * MOE_UNIT, MOE_UNIT), :],
                                        y_hbm.at[pl.ds(row, MOE_UNIT), :], ssem.at[sl, m])

    @pl.when(jnp.logical_and(b == 0, nblk > 0))
    def _():
        for cp in gather(0, 0):
            cp.start()

    @pl.when(jnp.logical_or(b == 0, be_ref[b] != be_ref[jnp.maximum(b - 1, 0)]))
    def _():
        wg_bf[...] = wg_ref[0, 0].astype(BF16)
        wu_bf[...] = wu_ref[0, 0].astype(BF16)
        wd_bf[...] = wd_ref[0, 0].astype(BF16)

    slot = lax.rem(b, 2)

    @pl.when(b + 1 < nblk)
    def _():
        for cp in gather(b + 1, 1 - slot):
            cp.start()

    @pl.when(jnp.logical_and(b >= 2, b < nblk + 2))
    def _():
        for cp in scatter(b - 2, slot):
            cp.wait()

    @pl.when(b < nblk)
    def _():
        for cp in gather(b, slot):
            cp.wait()
        x = xbuf[slot]
        hid = _silu(_dot(x, wg_bf[...])) * _dot(x, wu_bf[...])
        ybuf[slot] = _dot(hid.astype(BF16), wd_bf[...]).astype(ybuf.dtype)
        for cp in scatter(b, slot):
            cp.start()


def _experts(x_loc, unit_rows, block_e, n_blocks_used, w_gate, w_up, w_down, layer, n_blocks):
    rows, d = x_loc.shape
    de = w_gate.shape[3]
    upb = MOE_BLOCK // MOE_UNIT
    grid_spec = pltpu.PrefetchScalarGridSpec(
        num_scalar_prefetch=3,
        grid=(n_blocks + 2,),
        in_specs=[pl.BlockSpec(memory_space=pl.ANY),
                  pl.BlockSpec((1, 1, d, de), lambda b, tbl, be, nb: (layer, be[b], 0, 0)),
                  pl.BlockSpec((1, 1, d, de), lambda b, tbl, be, nb: (layer, be[b], 0, 0)),
                  pl.BlockSpec((1, 1, de, d), lambda b, tbl, be, nb: (layer, be[b], 0, 0))],
        out_specs=pl.BlockSpec(memory_space=pl.ANY),
        scratch_shapes=[pltpu.VMEM((2, MOE_BLOCK, d), BF16), pltpu.VMEM((2, MOE_BLOCK, d), BF16),
                        pltpu.VMEM((d, de), BF16), pltpu.VMEM((d, de), BF16), pltpu.VMEM((de, d), BF16),
                        pltpu.SemaphoreType.DMA((2, upb)), pltpu.SemaphoreType.DMA((2, upb))],
    )
    return pl.pallas_call(
        functools.partial(_expert_kernel, upb=upb),
        grid_spec=grid_spec,
        out_shape=jax.ShapeDtypeStruct((rows, d), BF16),
        input_output_aliases={3: 0},
        compiler_params=_cp("arbitrary"),
        name="moe_experts",
    )(unit_rows, block_e, n_blocks_used, x_loc, w_gate, w_up, w_down)


def _combine_kernel(y_ref, pos_ref, tw_ref, u_ref, wsg_ref, wsu_ref, wsd_ref, h_ref, g_ref, out_ref, *, tiles_per_batch):
    b = pl.program_id(0) // tiles_per_batch
    routed = _dot(_one_hot_t(pos_ref, tw_ref, y_ref.shape[1]), y_ref[0])
    u = u_ref[...]
    hid = _silu(_dot(u, wsg_ref[...])) * _dot(u, wsu_ref[...])
    shared = _dot(hid.astype(BF16), wsd_ref[...])
    out_ref[...] = h_ref[...] + g_ref[pl.ds(b, 1), :] * (routed + shared)


def _combine(y_loc, lpos, top_w, u, w_sg, w_su, w_sd, h, ada, gate_blk, seq):
    t, d = h.shape
    tt = MOE_TOK_TILE
    r_loc = y_loc.shape[1]
    row = lambda i: (i, 0)
    fix = lambda i: (0, 0)
    return pl.pallas_call(
        functools.partial(_combine_kernel, tiles_per_batch=seq // tt),
        grid=(t // tt,),
        in_specs=[pl.BlockSpec((1, r_loc, d), lambda i: (i, 0, 0)),
                  pl.BlockSpec((tt, TOP_K), row), pl.BlockSpec((tt, TOP_K), row),
                  pl.BlockSpec((tt, d), row),
                  pl.BlockSpec(w_sg.shape, fix), pl.BlockSpec(w_su.shape, fix), pl.BlockSpec(w_sd.shape, fix),
                  pl.BlockSpec((tt, d), row),
                  pl.BlockSpec((SUBLANES, d), lambda i: (0, gate_blk))],
        out_specs=pl.BlockSpec((tt, d), row),
        out_shape=jax.ShapeDtypeStruct((t, d), F32),
        compiler_params=_cp("parallel"),
        name="moe_combine",
    )(y_loc, lpos, top_w, u, w_sg, w_su, w_sd, h, ada)


def _moe_ffn(h, gain, ada, blk0, w_router, b_router, w_eg, w_eu, w_ed, layer, w_sg, w_su, w_sd, seq):
    t, d = h.shape
    tt, unit, upb = MOE_TOK_TILE, MOE_UNIT, MOE_BLOCK // MOE_UNIT
    wr = jnp.pad(w_router, ((0, 0), (0, LANES - N_EXPERTS)))
    br = jnp.pad(b_router.astype(F32), (0, LANES - N_EXPERTS)).reshape(1, LANES)
    u, lpos, top_w, nu, off = _route(h, gain, ada, blk0, blk0 + 1, wr, br, seq)

    n_tok_tiles = t // tt
    r_loc = -(-(tt * TOP_K + N_EXPERTS * (unit - 1)) // LANES) * LANES
    n_dummy_tiles = -(-(N_EXPERTS * (upb - 1) * unit) // r_loc)
    max_units = n_tok_tiles * (tt * TOP_K // unit + N_EXPERTS)
    n_blocks = -(-max_units // upb) + N_EXPERTS
    unit_rows, block_e, n_used = _plan(nu[:, 0, :], off[:, 0, :], n_tok_tiles, r_loc, n_blocks,
                                       n_dummy_tiles * r_loc // unit)
    lpos = jnp.concatenate([lpos, jnp.full((n_dummy_tiles * tt, TOP_K), -1, I32)], axis=0)

    x_loc = _dispatch(u, lpos, n_tok_tiles + n_dummy_tiles, r_loc)
    y_loc = _experts(x_loc.reshape(-1, d), unit_rows, block_e, n_used, w_eg, w_eu, w_ed, layer, n_blocks)
    y_loc = y_loc.reshape(n_tok_tiles + n_dummy_tiles, r_loc, d)
    return _combine(y_loc, lpos, top_w, u, w_sg.astype(BF16), w_su.astype(BF16), w_sd.astype(BF16),
                    h, ada, blk0 + 2, seq)


def _fox_layer(h, ada, norm_gain, w_in, b_f, q_gain, k_gain, w_o, batch, seq):
    t, d = h.shape
    hd = N_HEADS_A * HEAD_DIM
    w = jnp.pad(w_in, ((0, 0), (0, MXU_COLS - N_HEADS_A))).astype(BF16)
    proj = _norm_mod_matmul(h, norm_gain, ada, 0, 1, w, seq, w.shape[1], "fox_in_proj")
    z = proj[:, 3 * hd:3 * hd + N_HEADS_A].reshape(batch, seq, N_HEADS_A)
    z = z.transpose(1, 0, 2).reshape(seq, batch * N_HEADS_A)
    cum = _logsig_cumsum(z, jnp.tile(b_f.astype(F32), batch).reshape(1, -1))
    q, k, v = _fox_prep(proj, cum, q_gain, k_gain, batch, seq)
    bh = batch * N_HEADS_A
    o = _flash(q.reshape(bh, seq // FOX_TQ, FOX_TQ, LANES), k.reshape(bh, LANES, seq),
               v.reshape(bh, seq, LANES), FOX_TQ, FOX_TK, "fox_attention")
    o = o.reshape(batch, N_HEADS_A, seq, HEAD_DIM).transpose(0, 2, 1, 3).reshape(t, hd)
    return _oproj_residual(o, w_o.astype(BF16), h, ada, 2, seq)


def _rope_tables(positions):
    inv_freq = ROPE_THETA ** (-jnp.arange(0, ROPE_DIM, 2, dtype=F32) / ROPE_DIM)
    ang = positions.astype(F32).reshape(-1, 1) * inv_freq
    half = ROPE_DIM // 2
    cos, sin = jnp.cos(ang), jnp.sin(ang)
    rest = HEAD_DIM - ROPE_DIM
    cos_h = jnp.concatenate([cos, cos, jnp.ones((ang.shape[0], rest), F32)], axis=1)
    sin_h = jnp.concatenate([-sin, sin, jnp.zeros((ang.shape[0], rest), F32)], axis=1)
    del half
    return jnp.tile(cos_h, (1, LANES // HEAD_DIM)), jnp.tile(sin_h, (1, LANES // HEAD_DIM))


def _group_rows(x, batch, seq, tq):
    g, hpg = N_KV_GROUPS_B, HEADS_PER_GROUP
    x = x.reshape(batch, seq // tq, tq, g, hpg, HEAD_DIM).transpose(0, 3, 1, 4, 2, 5)
    return x.reshape(batch * g, seq // tq, hpg * tq, HEAD_DIM)


def _per_group(x, batch, seq):
    g = N_KV_GROUPS_B
    return x.reshape(batch, seq, g, HEAD_DIM).transpose(0, 2, 1, 3).reshape(batch * g, seq, HEAD_DIM)


def _with_ones(v):
    pad = jnp.zeros(v.shape[:-1] + (LANES - HEAD_DIM,), v.dtype).at[..., 0].set(1)
    return jnp.concatenate([v, pad], axis=-1)


def _shared_kv(h, ada_kv, norm_kv, w_kv, pe_k, pe_v, w_ck1, w_ck2, w_cv1, w_cv2, k_gains, cos_t, sin_t, batch, seq):
    g = N_KV_GROUPS_B
    gw = g * HEAD_DIM
    nb = seq // CMP_BLOCK
    kv = _norm_mod_matmul(h, norm_kv, ada_kv, 0, 1, w_kv.astype(BF16), seq, w_kv.shape[1], "nsa_kv_proj")
    k_sel, v_sel, k_win, v_win = _nsa_kv_prep(kv, cos_t, sin_t, k_gains)

    def blocks(x):
        x = x.reshape(batch, nb, CMP_BLOCK, g, HEAD_DIM).transpose(0, 1, 3, 2, 4)
        return x.reshape(batch * nb * g, CMP_BLOCK * HEAD_DIM)

    k_cmp = _compress(blocks(kv[:, 0:gw]), pe_k.reshape(1, -1), w_ck1.astype(BF16), w_ck2.astype(BF16),
                      k_gains[0], True, "nsa_compress_k")
    v_cmp = _compress(blocks(kv[:, gw:2 * gw]), pe_v.reshape(1, -1), w_cv1.astype(BF16), w_cv2.astype(BF16),
                      k_gains[0], False, "nsa_compress_v")
    to_bg = lambda x: x.reshape(batch, nb, g, HEAD_DIM).transpose(0, 2, 1, 3).reshape(batch * g, nb, HEAD_DIM)
    k_cmp = to_bg(k_cmp)
    v_cmp = jnp.pad(to_bg(v_cmp), ((0, 0), (0, 0), (0, LANES - HEAD_DIM)))

    def per_group_t(x):
        return x.reshape(batch, seq, g, HEAD_DIM).transpose(0, 2, 3, 1).reshape(batch * g, HEAD_DIM, seq)

    blk_onehot = (jnp.arange(nb)[:, None] == jnp.arange(seq)[None, :] // CMP_BLOCK).astype(BF16)
    k_sel = jnp.concatenate([jnp.broadcast_to(blk_onehot, (batch * g, nb, seq)), per_group_t(k_sel)], axis=1)
    return (k_cmp, v_cmp, k_sel, _with_ones(_per_group(v_sel, batch, seq)),
            per_group_t(k_win), _with_ones(_per_group(v_win, batch, seq)))


def _nsa_layer(h, ada, norm_gain, kv, cos_t, sin_t, w_in, b_gate, q_gain, w_o, batch, seq):
    hd = N_HEADS_B * HEAD_DIM
    k_cmp, v_cmp, k_sel, v_sel, k_win, v_win = kv
    n_gate = 3 * N_HEADS_B
    w = jnp.pad(w_in, ((0, 0), (0, LANES - n_gate))).astype(BF16)
    proj = _norm_mod_matmul(h, norm_gain, ada, 0, 1, w, seq, hd + LANES, "nsa_in_proj")
    bg = jnp.pad(b_gate.astype(F32), (0, LANES - n_gate)).reshape(1, LANES)
    q_plain, q_rot, gates = _nsa_q_prep(proj, cos_t, sin_t, q_gain, bg)

    bgs = (batch, N_KV_GROUPS_B)
    o_cmp, negbias = _cmp_select(_group_rows(q_plain, batch, seq, CMP_TQ), k_cmp, v_cmp, seq, CMP_TQ, bgs)
    o_sel = _flash(_group_rows(q_rot, batch, seq, SEL_TQ), k_sel, v_sel, SEL_TQ, NSA_TK, "nsa_selected",
                   bias=negbias, grouped_out=bgs)
    o_win = _window_attention(_group_rows(q_rot, batch, seq, WIN_TQ), k_win, v_win, WIN_TQ, WINDOW, bgs)
    return _nsa_out(o_cmp, o_sel, o_win, gates, w_o.astype(BF16), h, ada, 2, seq)


def kernel(x, c, positions, w_ada, b_ada, norm_mix, norm_ffn, w_in_a, b_f_a, qnorm_a, knorm_a, w_o_a, norm_kv, w_ada_kv, b_ada_kv, w_kv_b, pe_cmp_k, pe_cmp_v, w_cmp_k1, w_cmp_k2, w_cmp_v1, w_cmp_v2, knorm_b, w_in_b, b_gate_b, qnorm_b, w_o_b, w_router, b_router, w_exp_gate, w_exp_up, w_exp_down, w_sh_gate, w_sh_up, w_sh_down):
    batch, seq, d = x.shape
    depth = w_ada.shape[0]
    n_a = w_in_a.shape[0]
    c_pad = jnp.pad(c, ((0, SUBLANES - batch), (0, 0)))
    cos_t, sin_t = _rope_tables(positions)
    h = x.reshape(batch * seq, d)
    kv = None
    for layer in range(depth):
        ada = _ada(c_pad, w_ada[layer], b_ada[layer])
        if layer < n_a:
            h = _fox_layer(h, ada, norm_mix[layer], w_in_a[layer], b_f_a[layer], qnorm_a[layer],
                           knorm_a[layer], w_o_a[layer], batch, seq)
        else:
            if layer == n_a:
                ada_kv = _ada(c_pad, w_ada_kv, b_ada_kv)
                kv = _shared_kv(h, ada_kv, norm_kv, w_kv_b, pe_cmp_k, pe_cmp_v, w_cmp_k1, w_cmp_k2,
                                w_cmp_v1, w_cmp_v2, knorm_b, cos_t, sin_t, batch, seq)
            j = layer - n_a
            h = _nsa_layer(h, ada, norm_mix[layer], kv, cos_t, sin_t, w_in_b[j], b_gate_b[j], qnorm_b[j],
                           w_o_b[j], batch, seq)
        h = _moe_ffn(h, norm_ffn[layer], ada, 3, w_router[layer], b_router[layer], w_exp_gate, w_exp_up,
                     w_exp_down, layer, w_sh_gate[layer], w_sh_up[layer], w_sh_down[layer], seq)
    return h.reshape(batch, seq, d)
```

```python
import functools

import jax
import jax.numpy as jnp
from jax import lax
from jax.experimental import pallas as pl
from jax.experimental.pallas import tpu as pltpu

F32 = jnp.float32
BF16 = jnp.bfloat16
I32 = jnp.int32

HEAD_DIM = 64
N_HEADS_A = 16
N_HEADS_B = 16
N_KV_GROUPS_B = 4
HEADS_PER_GROUP = N_HEADS_B // N_KV_GROUPS_B
ROPE_THETA = 500000.0
ROPE_DIM = HEAD_DIM // 4
CMP_BLOCK = 64
N_SELECT = 16
WINDOW = 512
N_EXPERTS = 64
TOP_K = 8
ROUTED_SCALE = 2.5
EPS = 1e-6
FORCE_SCORE = 1e4
NEG = -1e30
LOG2E = 1.4426950408889634
QSCALE = HEAD_DIM ** -0.5 * LOG2E

LANES = 128
SUBLANES = 8
BF16_ROWS = 16
MXU_COLS = 256

ROW_TILE = 512
FOX_TQ = 2048
FOX_TK = 512
SEL_TQ = 512
WIN_TQ = 256
NSA_TK = 512
CMP_TQ = 1024
MOE_TOK_TILE = 256
ROUTE_TILES = 2
MOE_UNIT = BF16_ROWS
MOE_BLOCK = 512
VMEM_LIMIT = 56 * 1024 * 1024


def _cp(*sem):
    return pltpu.CompilerParams(dimension_semantics=sem, vmem_limit_bytes=VMEM_LIMIT)


def _dot(a, b):
    return jnp.dot(a, b, preferred_element_type=F32)


def _dot_nt(a, b):
    return lax.dot_general(a, b, (((1,), (1,)), ((), ())), preferred_element_type=F32)


def _dot_tn(a, b):
    return lax.dot_general(a, b, (((0,), (0,)), ((), ())), preferred_element_type=F32)


def _split2(x):
    hi = x.astype(BF16)
    lo = (x - hi.astype(F32)).astype(BF16)
    return hi, lo


def _split3(x):
    hi = x.astype(BF16)
    r = x - hi.astype(F32)
    mid = r.astype(BF16)
    lo = (r - mid.astype(F32)).astype(BF16)
    return hi, mid, lo


def _dot_f32(a, b):
    a_hi, a_lo = _split2(a)
    b_hi, b_lo = _split2(b)
    return _dot(a_hi, b_hi) + _dot(a_lo, b_hi) + _dot(a_hi, b_lo)


def _sigmoid(x):
    return 1.0 / (1.0 + jnp.exp(-x))


def _silu(x):
    return x * _sigmoid(x)


def _ada_kernel(c_ref, w_ref, b_ref, o_ref):
    o_ref[...] = _dot_f32(_silu(c_ref[...]), w_ref[...]) + b_ref[...]


def _ada(c_pad, w, b):
    d, n = w.shape
    tn = 1024
    return pl.pallas_call(
        _ada_kernel,
        grid=(n // tn,),
        in_specs=[pl.BlockSpec((SUBLANES, d), lambda j: (0, 0)),
                  pl.BlockSpec((d, tn), lambda j: (0, j)),
                  pl.BlockSpec((1, tn), lambda j: (0, j))],
        out_specs=pl.BlockSpec((SUBLANES, tn), lambda j: (0, j)),
        out_shape=jax.ShapeDtypeStruct((SUBLANES, n), F32),
        compiler_params=_cp("parallel"),
        name="ada",
    )(c_pad, w, b.reshape(1, n))


def _norm_mod(x, gain, shift, scale):
    xn = x * lax.rsqrt(jnp.mean(x * x, axis=-1, keepdims=True) + EPS) * gain
    return xn * (1.0 + scale) + shift


def _nm_kernel(h_ref, gain_ref, shift_ref, scale_ref, w_ref, o_ref, u_scr, *, tiles_per_batch):
    i = pl.program_id(0)

    @pl.when(pl.program_id(1) == 0)
    def _():
        b = i // tiles_per_batch
        u = _norm_mod(h_ref[...], gain_ref[...], shift_ref[pl.ds(b, 1), :], scale_ref[pl.ds(b, 1), :])
        u_scr[...] = u.astype(BF16)

    o_ref[...] = _dot(u_scr[...], w_ref[...]).astype(o_ref.dtype)


def _norm_mod_matmul(h, gain, ada, shift_blk, scale_blk, w, seq, tn, name):
    t, d = h.shape
    n = w.shape[1]
    tm = ROW_TILE
    return pl.pallas_call(
        functools.partial(_nm_kernel, tiles_per_batch=seq // tm),
        grid=(t // tm, n // tn),
        in_specs=[pl.BlockSpec((tm, d), lambda i, j: (i, 0)),
                  pl.BlockSpec((1, d), lambda i, j: (0, 0)),
                  pl.BlockSpec((SUBLANES, d), lambda i, j: (0, shift_blk)),
                  pl.BlockSpec((SUBLANES, d), lambda i, j: (0, scale_blk)),
                  pl.BlockSpec((d, tn), lambda i, j: (0, j))],
        out_specs=pl.BlockSpec((tm, tn), lambda i, j: (i, j)),
        out_shape=jax.ShapeDtypeStruct((t, n), F32),
        scratch_shapes=[pltpu.VMEM((tm, d), BF16)],
        compiler_params=_cp("parallel", "arbitrary"),
        name=name,
    )(h, gain.reshape(1, d), ada, ada, w)


def _logsig_cumsum_kernel(z_ref, b_ref, tri_ref, o_ref, carry):
    @pl.when(pl.program_id(0) == 0)
    def _():
        carry[...] = jnp.zeros_like(carry)

    z = z_ref[...] + b_ref[...]
    lf = jnp.minimum(z, 0.0) - jnp.log1p(jnp.exp(-jnp.abs(z)))
    hi, mid, lo = _split3(lf)
    tri = tri_ref[...]
    c = _dot(tri, hi) + _dot(tri, mid) + _dot(tri, lo) + carry[...]
    o_ref[...] = c
    ts = c.shape[0]
    carry[...] = c[ts - 1:ts, :]


def _logsig_cumsum(z, bias):
    s, c = z.shape
    ts = 256
    tri = (jnp.arange(ts)[:, None] >= jnp.arange(ts)[None, :]).astype(BF16)
    return pl.pallas_call(
        _logsig_cumsum_kernel,
        grid=(s // ts,),
        in_specs=[pl.BlockSpec((ts, c), lambda i: (i, 0)),
                  pl.BlockSpec((1, c), lambda i: (0, 0)),
                  pl.BlockSpec((ts, ts), lambda i: (0, 0))],
        out_specs=pl.BlockSpec((ts, c), lambda i: (i, 0)),
        out_shape=jax.ShapeDtypeStruct((s, c), F32),
        scratch_shapes=[pltpu.VMEM((1, c), F32)],
        compiler_params=_cp("arbitrary"),
        name="fox_logsig_cumsum",
    )(z, bias, tri)


def _head_rms(x, gain):
    return x * lax.rsqrt(jnp.mean(x * x, axis=-1, keepdims=True) + EPS) * gain


def _fox_prep_kernel(q_ref, k_ref, v_ref, cum_ref, qg_ref, kg_ref, qo_ref, ko_ref, vo_ref):
    bh0 = (pl.program_id(0) * (N_HEADS_A // 2) + pl.program_id(1)) * 2
    ts = q_ref.shape[0]
    cum = cum_ref[...]
    lane_c = lax.broadcasted_iota(I32, cum.shape, 1)
    lane = lax.broadcasted_iota(I32, (ts, LANES), 1)
    low = lane < HEAD_DIM
    ext = lane - HEAD_DIM

    def pair_rms(x, gain):
        sq = x * x
        ss = jnp.where(low, jnp.sum(jnp.where(low, sq, 0.0), axis=-1, keepdims=True),
                       jnp.sum(jnp.where(low, 0.0, sq), axis=-1, keepdims=True))
        return x * lax.rsqrt(ss * (1.0 / HEAD_DIM) + EPS) * gain

    qn = pair_rms(q_ref[...], qg_ref[...]) * QSCALE
    kn = pair_rms(k_ref[...], kg_ref[...])
    v = v_ref[...]
    for h2 in range(2):
        c = jnp.sum(jnp.where(lane_c == bh0 + h2, cum, 0.0), axis=-1, keepdims=True) * LOG2E
        hi, mid, lo = (p.astype(F32) for p in _split3(c))
        head = (lambda x: x) if h2 == 0 else (lambda x: pltpu.roll(x, HEAD_DIM, 1))
        ext_q = jnp.where(ext == 0, hi, jnp.where(ext == 1, mid, jnp.where(ext == 2, lo,
                          jnp.where(ext < 6, 1.0, 0.0))))
        ext_k = jnp.where(ext < 3, 1.0, jnp.where(ext == 3, -hi, jnp.where(ext == 4, -mid,
                          jnp.where(ext == 5, -lo, 0.0))))
        qo_ref[0, h2] = jnp.where(low, head(qn), ext_q).astype(BF16)
        ko_ref[0, h2] = jnp.where(low, head(kn), ext_k).T.astype(BF16)
        vo_ref[0, h2] = jnp.where(low, head(v), jnp.where(ext == 0, 1.0, 0.0)).astype(BF16)


def _fox_prep(proj, cum, q_gain, k_gain, batch, seq):
    ts = ROW_TILE
    nsb = seq // ts
    hp = N_HEADS_A // 2
    out = jax.ShapeDtypeStruct((batch, N_HEADS_A, seq, LANES), BF16)
    out_t = jax.ShapeDtypeStruct((batch, N_HEADS_A, LANES, seq), BF16)
    ospec = pl.BlockSpec((1, 2, ts, LANES), lambda b, p, i: (b, p, i, 0))
    ospec_t = pl.BlockSpec((1, 2, LANES, ts), lambda b, p, i: (b, p, 0, i))
    return pl.pallas_call(
        _fox_prep_kernel,
        grid=(batch, hp, nsb),
        in_specs=[pl.BlockSpec((ts, LANES), lambda b, p, i: (b * nsb + i, p)),
                  pl.BlockSpec((ts, LANES), lambda b, p, i: (b * nsb + i, hp + p)),
                  pl.BlockSpec((ts, LANES), lambda b, p, i: (b * nsb + i, 2 * hp + p)),
                  pl.BlockSpec((ts, cum.shape[1]), lambda b, p, i: (i, 0)),
                  pl.BlockSpec((1, LANES), lambda b, p, i: (0, 0)),
                  pl.BlockSpec((1, LANES), lambda b, p, i: (0, 0))],
        out_specs=[ospec, ospec_t, ospec],
        out_shape=[out, out_t, out],
        compiler_params=_cp("parallel", "parallel", "parallel"),
        name="fox_prep",
    )(proj, proj, proj, cum, jnp.tile(q_gain, LANES // HEAD_DIM).reshape(1, LANES),
      jnp.tile(k_gain, LANES // HEAD_DIM).reshape(1, LANES))


def _store_heads(o_ref, o, tq, grouped):
    if not grouped:
        o_ref[0, 0] = o.astype(o_ref.dtype)
        return
    for hh in range(o.shape[0] // tq):
        o_ref[:, hh * HEAD_DIM:(hh + 1) * HEAD_DIM] = o[hh * tq:(hh + 1) * tq].astype(o_ref.dtype)


def _flash_kernel(*refs, tq, tk, with_bias, grouped):
    if with_bias:
        q_ref, nb_ref, k_ref, v_ref, o_ref, m_scr, acc_scr = refs
    else:
        q_ref, k_ref, v_ref, o_ref, m_scr, acc_scr = refs
    i = pl.program_id(1)
    q = q_ref[0, 0]
    rows = q.shape[0]
    if with_bias:
        q = jnp.concatenate([jnp.tile(nb_ref[0], (rows // tq, 1)), q], axis=1)
    m_scr[...] = jnp.full(m_scr.shape, NEG, F32)
    acc_scr[...] = jnp.zeros(acc_scr.shape, F32)
    q_pos = i * tq + jnp.bitwise_and(lax.broadcasted_iota(I32, (rows, 1), 0), tq - 1)

    def step(j, masked, row_lo=0):
        ks = pl.multiple_of(j * tk, tk)
        s = _dot(q[row_lo:], k_ref[0, :, pl.ds(ks, tk)])
        if masked:
            k_pos = ks + lax.broadcasted_iota(I32, (1, tk), 1)
            s = jnp.where(k_pos <= q_pos[row_lo:], s, NEG)
        m_old = m_scr[row_lo:, :]
        m_new = jnp.maximum(m_old, jnp.max(s, axis=-1, keepdims=True))
        p = jnp.exp2(s - jnp.tile(m_new, (1, tk // LANES))).astype(BF16)
        acc_scr[row_lo:, :] = jnp.exp2(m_old - m_new) * acc_scr[row_lo:, :] + _dot(p, v_ref[0, pl.ds(ks, tk), :])
        m_scr[row_lo:, :] = m_new

    n_clear = (i * tq + 1) // tk
    n_diag = max(tq // tk, 1)

    def pair(jj, c):
        step(2 * jj, False)
        step(2 * jj + 1, False)
        return c

    lax.fori_loop(0, n_clear // 2, pair, 0)

    @pl.when(lax.rem(n_clear, 2) == 1)
    def _():
        step(n_clear - 1, False)

    for d in range(n_diag):
        step(n_clear + d, True, row_lo=d * tk if rows == tq else 0)

    acc = acc_scr[...]
    _store_heads(o_ref, acc[:, :HEAD_DIM] / acc[:, HEAD_DIM:HEAD_DIM + 1], tq, grouped)


def _flash(q, k, v, tq, tk, name, bias=None, grouped_out=None):
    bh, nq, rows, kd = q.shape
    s = k.shape[2]
    assert tq % tk == 0 or tk % tq == 0
    in_specs = [pl.BlockSpec((1, 1, rows, kd), lambda b, i: (b, i, 0, 0))]
    args = [q]
    if bias is not None:
        in_specs.append(pl.BlockSpec((1, tq, bias.shape[2]), lambda b, i: (b, i, 0)))
        args.append(bias)
    in_specs += [pl.BlockSpec((1, k.shape[1], s), lambda b, i: (b, 0, 0)),
                 pl.BlockSpec((1, s, LANES), lambda b, i: (b, 0, 0))]
    if grouped_out is None:
        out_spec = pl.BlockSpec((1, 1, rows, HEAD_DIM), lambda b, i: (b, i, 0, 0))
        out_shape = jax.ShapeDtypeStruct((bh, nq, rows, HEAD_DIM), BF16)
    else:
        out_spec, out_shape = _grouped_out(grouped_out, nq, tq, rows)
    return pl.pallas_call(
        functools.partial(_flash_kernel, tq=tq, tk=tk, with_bias=bias is not None, grouped=grouped_out is not None),
        grid=(bh, nq),
        in_specs=in_specs,
        out_specs=out_spec,
        out_shape=out_shape,
        scratch_shapes=[pltpu.VMEM((rows, LANES), F32), pltpu.VMEM((rows, LANES), F32)],
        compiler_params=_cp("parallel", "arbitrary"),
        name=name,
    )(*args, k, v)


def _grouped_out(batch_groups, nq, tq, rows):
    batch, groups = batch_groups
    width = rows // tq * HEAD_DIM
    spec = pl.BlockSpec((tq, width), lambda b, i: ((b // groups) * nq + i, lax.rem(b, groups)))
    return spec, jax.ShapeDtypeStruct((batch * nq * tq, groups * width), BF16)


def _window_kernel(q_ref, k_ref, v_ref, o_ref, *, tq, window):
    i = pl.program_id(1)
    q = q_ref[0, 0]
    rows = q.shape[0]
    span = window + tq
    ks = pl.multiple_of(jnp.maximum(i * tq - window, 0), tq)
    q_pos = i * tq + jnp.bitwise_and(lax.broadcasted_iota(I32, (rows, 1), 0), tq - 1)
    k_pos = ks + lax.broadcasted_iota(I32, (1, span), 1)
    ok = (q_pos - k_pos).astype(jnp.uint32) < window
    s = jnp.where(ok, _dot(q, k_ref[0, :, pl.ds(ks, span)]), NEG)
    p = jnp.exp2(s - jnp.max(s, axis=-1, keepdims=True)).astype(BF16)
    acc = _dot(p, v_ref[0, pl.ds(ks, span), :])
    _store_heads(o_ref, acc[:, :HEAD_DIM] / acc[:, HEAD_DIM:HEAD_DIM + 1], tq, True)


def _window_attention(q, k, v, tq, window, batch_groups):
    bh, nq, rows, kd = q.shape
    s = k.shape[2]
    assert window % tq == 0 and window + tq <= s
    out_spec, out_shape = _grouped_out(batch_groups, nq, tq, rows)
    return pl.pallas_call(
        functools.partial(_window_kernel, tq=tq, window=window),
        grid=(bh, nq),
        in_specs=[pl.BlockSpec((1, 1, rows, kd), lambda b, i: (b, i, 0, 0)),
                  pl.BlockSpec((1, kd, s), lambda b, i: (b, 0, 0)),
                  pl.BlockSpec((1, s, LANES), lambda b, i: (b, 0, 0))],
        out_specs=out_spec,
        out_shape=out_shape,
        compiler_params=_cp("parallel", "parallel"),
        name="nsa_window",
    )(q, k, v)


def _oproj_kernel(o_ref, w_ref, h_ref, g_ref, out_ref, *, tiles_per_batch):
    b = pl.program_id(0) // tiles_per_batch
    out_ref[...] = h_ref[...] + g_ref[pl.ds(b, 1), :] * _dot(o_ref[...], w_ref[...])


def _oproj_residual(o, w, h, ada, gate_blk, seq):
    t, d = h.shape
    tm = ROW_TILE
    return pl.pallas_call(
        functools.partial(_oproj_kernel, tiles_per_batch=seq // tm),
        grid=(t // tm,),
        in_specs=[pl.BlockSpec((tm, o.shape[1]), lambda i: (i, 0)),
                  pl.BlockSpec(w.shape, lambda i: (0, 0)),
                  pl.BlockSpec((tm, d), lambda i: (i, 0)),
                  pl.BlockSpec((SUBLANES, d), lambda i: (0, gate_blk))],
        out_specs=pl.BlockSpec((tm, d), lambda i: (i, 0)),
        out_shape=jax.ShapeDtypeStruct((t, d), F32),
        compiler_params=_cp("parallel"),
        name="oproj_residual",
    )(o, w, h, ada)


def _seg_rms(x, seg_ones, gain_tiled):
    ss = _dot((x * x).astype(BF16), seg_ones)
    return x * lax.rsqrt(ss * (1.0 / HEAD_DIM) + EPS) * gain_tiled


def _rope(x, cos_t, sin_t):
    width = x.shape[1]
    reps = width // LANES
    lane = jnp.bitwise_and(lax.broadcasted_iota(I32, x.shape, 1), HEAD_DIM - 1)
    half = ROPE_DIM // 2
    partner = jnp.where(lane < half, pltpu.roll(x, width - half, 1), pltpu.roll(x, half, 1))
    return x * jnp.tile(cos_t, (1, reps)) + partner * jnp.tile(sin_t, (1, reps))


def _nsa_q_kernel(p_ref, z_ref, cos_ref, sin_ref, gain_ref, seg_ref, bg_ref, qp_ref, qr_ref, g_ref):
    q = _seg_rms(p_ref[...], seg_ref[...], gain_ref[...]) * QSCALE
    qp_ref[...] = q.astype(BF16)
    qr_ref[...] = _rope(q, cos_ref[...], sin_ref[...]).astype(BF16)
    g_ref[...] = _sigmoid(z_ref[...] + bg_ref[...])


def _nsa_q_prep(proj, cos_t, sin_t, q_gain, b_gate_pad):
    t = proj.shape[0]
    hd = N_HEADS_B * HEAD_DIM
    tm = ROW_TILE
    seg = (jnp.arange(hd)[:, None] // HEAD_DIM == jnp.arange(hd)[None, :] // HEAD_DIM).astype(BF16)
    row = lambda i: (i, 0)
    fix = lambda i: (0, 0)
    return pl.pallas_call(
        _nsa_q_kernel,
        grid=(t // tm,),
        in_specs=[pl.BlockSpec((tm, hd), row),
                  pl.BlockSpec((tm, LANES), lambda i: (i, hd // LANES)),
                  pl.BlockSpec((tm, LANES), row), pl.BlockSpec((tm, LANES), row),
                  pl.BlockSpec((1, hd), fix), pl.BlockSpec((hd, hd), fix), pl.BlockSpec((1, LANES), fix)],
        out_specs=[pl.BlockSpec((tm, hd), row), pl.BlockSpec((tm, hd), row), pl.BlockSpec((tm, LANES), row)],
        out_shape=[jax.ShapeDtypeStruct((t, hd), BF16), jax.ShapeDtypeStruct((t, hd), BF16),
                   jax.ShapeDtypeStruct((t, LANES), F32)],
        compiler_params=_cp("parallel"),
        name="nsa_q_prep",
    )(proj, proj, cos_t, sin_t, jnp.tile(q_gain, N_HEADS_B).reshape(1, hd), seg, b_gate_pad)


def _nsa_kv_kernel(kv_ref, cos_ref, sin_ref, g1_ref, g2_ref, seg_ref, ks_ref, vs_ref, kw_ref, vw_ref):
    gw = N_KV_GROUPS_B * HEAD_DIM
    ks = _seg_rms(kv_ref[:, 2 * gw:3 * gw], seg_ref[...], g1_ref[...])
    kw = _seg_rms(kv_ref[:, 4 * gw:5 * gw], seg_ref[...], g2_ref[...])
    ks_ref[...] = _rope(ks, cos_ref[...], sin_ref[...]).astype(BF16)
    kw_ref[...] = _rope(kw, cos_ref[...], sin_ref[...]).astype(BF16)
    vs_ref[...] = kv_ref[:, 3 * gw:4 * gw].astype(BF16)
    vw_ref[...] = kv_ref[:, 5 * gw:6 * gw].astype(BF16)


def _nsa_kv_prep(kv, cos_t, sin_t, k_gains):
    t, n = kv.shape
    gw = N_KV_GROUPS_B * HEAD_DIM
    tm = ROW_TILE
    seg = (jnp.arange(gw)[:, None] // HEAD_DIM == jnp.arange(gw)[None, :] // HEAD_DIM).astype(BF16)
    row = lambda i: (i, 0)
    fix = lambda i: (0, 0)
    out = jax.ShapeDtypeStruct((t, gw), BF16)
    return pl.pallas_call(
        _nsa_kv_kernel,
        grid=(t // tm,),
        in_specs=[pl.BlockSpec((tm, n), row), pl.BlockSpec((tm, LANES), row), pl.BlockSpec((tm, LANES), row),
                  pl.BlockSpec((1, gw), fix), pl.BlockSpec((1, gw), fix), pl.BlockSpec((gw, gw), fix)],
        out_specs=[pl.BlockSpec((tm, gw), row)] * 4,
        out_shape=[out] * 4,
        compiler_params=_cp("parallel"),
        name="nsa_kv_prep",
    )(kv, cos_t, sin_t, jnp.tile(k_gains[1], N_KV_GROUPS_B).reshape(1, gw),
      jnp.tile(k_gains[2], N_KV_GROUPS_B).reshape(1, gw), seg)


def _compress_kernel(x_ref, pe_ref, w1_ref, w2_ref, g_ref, o_ref, *, norm):
    x = (x_ref[...] + pe_ref[...]).astype(BF16)
    y = _dot(_silu(_dot(x, w1_ref[...])).astype(BF16), w2_ref[...])
    if norm:
        y = _head_rms(y, g_ref[...])
    o_ref[...] = y.astype(o_ref.dtype)


def _compress(x, pe_flat, w1, w2, gain, norm, name):
    r, kdim = x.shape
    tm = min(ROW_TILE, r)
    fix = lambda i: (0, 0)
    return pl.pallas_call(
        functools.partial(_compress_kernel, norm=norm),
        grid=(r // tm,),
        in_specs=[pl.BlockSpec((tm, kdim), lambda i: (i, 0)), pl.BlockSpec((1, kdim), fix),
                  pl.BlockSpec(w1.shape, fix), pl.BlockSpec(w2.shape, fix), pl.BlockSpec((1, HEAD_DIM), fix)],
        out_specs=pl.BlockSpec((tm, HEAD_DIM), lambda i: (i, 0)),
        out_shape=jax.ShapeDtypeStruct((r, HEAD_DIM), BF16),
        compiler_params=_cp("parallel"),
        name=name,
    )(x, pe_flat, w1, w2, gain.reshape(1, HEAD_DIM))


def _cmp_select_kernel(q_ref, k_ref, v_ref, o_ref, nb_ref, *, tq, n_sel):
    i = pl.program_id(1)
    q = q_ref[0, 0]
    rows = q.shape[0]
    nblk = k_ref.shape[1]
    logits = _dot_nt(q, k_ref[0])
    t_pos = i * tq + jnp.bitwise_and(lax.broadcasted_iota(I32, (rows, 1), 0), tq - 1)
    blk = lax.broadcasted_iota(I32, (1, nblk), 1)
    vis = (blk + 1) * CMP_BLOCK - 1 <= t_pos
    lm = jnp.where(vis, logits, NEG)
    e = jnp.exp2(lm - jnp.max(lm, axis=-1, keepdims=True))
    p = jnp.where(vis, e / jnp.sum(e, axis=-1, keepdims=True), 0.0)
    _store_heads(o_ref, _dot(p.astype(BF16), v_ref[0])[:, :HEAD_DIM], tq, True)

    imp = p[0:tq]
    for hh in range(1, rows // tq):
        imp = imp + p[hh * tq:(hh + 1) * tq]
    cur = lax.shift_right_logical(t_pos[0:tq], CMP_BLOCK.bit_length() - 1)
    forced = jnp.logical_or(blk == 0, jnp.logical_or(blk == cur, blk == cur - 1))
    cand = jnp.where(jnp.logical_or(forced, blk > cur), -jnp.inf, imp)
    sel = jnp.where(forced, 1.0, 0.0)
    blk_f = blk.astype(F32)
    for _ in range(max(n_sel - 3, 0)):
        best = jnp.max(cand, axis=-1, keepdims=True)
        first = jnp.min(jnp.where(cand == best, blk_f, float(nblk)), axis=-1, keepdims=True)
        hit = blk_f == first
        sel = jnp.where(jnp.logical_and(hit, best > -jnp.inf), 1.0, sel)
        cand = jnp.where(hit, -jnp.inf, cand)
    nb_ref[0] = jnp.where(sel > 0.5, 0.0, NEG).astype(nb_ref.dtype)


def _cmp_select(q, k_cmp, v_cmp, seq, tq, batch_groups):
    bg, nq, rows, _ = q.shape
    nblk = k_cmp.shape[1]
    assert min(N_SELECT, nblk) >= 3
    out_spec, out_shape = _grouped_out(batch_groups, nq, tq, rows)
    return pl.pallas_call(
        functools.partial(_cmp_select_kernel, tq=tq, n_sel=min(N_SELECT, nblk)),
        grid=(bg, nq),
        in_specs=[pl.BlockSpec((1, 1, rows, HEAD_DIM), lambda b, i: (b, i, 0, 0)),
                  pl.BlockSpec((1, nblk, HEAD_DIM), lambda b, i: (b, 0, 0)),
                  pl.BlockSpec((1, nblk, LANES), lambda b, i: (b, 0, 0))],
        out_specs=[out_spec, pl.BlockSpec((1, tq, nblk), lambda b, i: (b, i, 0))],
        out_shape=[out_shape, jax.ShapeDtypeStruct((bg, seq, nblk), BF16)],
        compiler_params=_cp("parallel", "parallel"),
        name="nsa_cmp_select",
    )(q, k_cmp, v_cmp)


def _nsa_out_kernel(oc_ref, os_ref, ow_ref, g_ref, e_ref, w_ref, h_ref, ada_ref, out_ref, *, tiles_per_batch):
    b = pl.program_id(0) // tiles_per_batch
    g_hi, g_lo = _split2(g_ref[...])
    o = None
    for j, o_ref in enumerate((oc_ref, os_ref, ow_ref)):
        gate = _dot(g_hi, e_ref[j]) + _dot(g_lo, e_ref[j])
        term = gate * o_ref[...].astype(F32)
        o = term if o is None else o + term
    out_ref[...] = h_ref[...] + ada_ref[pl.ds(b, 1), :] * _dot(o.astype(BF16), w_ref[...])


def _nsa_out(o_cmp, o_sel, o_win, gates, w_o, h, ada, gate_blk, seq):
    t, d = h.shape
    hd = o_cmp.shape[1]
    tm = ROW_TILE
    col = jnp.arange(LANES)[:, None]
    lane = jnp.arange(hd)[None, :]
    expand = jnp.stack([(col == (lane // HEAD_DIM) * 3 + j) for j in range(3)]).astype(BF16)
    row = lambda i: (i, 0)
    return pl.pallas_call(
        functools.partial(_nsa_out_kernel, tiles_per_batch=seq // tm),
        grid=(t // tm,),
        in_specs=[pl.BlockSpec((tm, hd), row), pl.BlockSpec((tm, hd), row), pl.BlockSpec((tm, hd), row),
                  pl.BlockSpec((tm, LANES), row),
                  pl.BlockSpec((3, LANES, hd), lambda i: (0, 0, 0)),
                  pl.BlockSpec(w_o.shape, lambda i: (0, 0)),
                  pl.BlockSpec((tm, d), row),
                  pl.BlockSpec((SUBLANES, d), lambda i: (0, gate_blk))],
        out_specs=pl.BlockSpec((tm, d), row),
        out_shape=jax.ShapeDtypeStruct((t, d), F32),
        compiler_params=_cp("parallel"),
        name="nsa_out",
    )(o_cmp, o_sel, o_win, gates, expand, w_o, h, ada)


def _route_kernel(h_ref, gain_ref, shift_ref, scale_ref, wr_ref, br_ref, tri_ref, upper_ref,
                  u_ref, pos_ref, w_ref, nu_ref, off_ref, *, tiles_per_batch):
    b = (pl.program_id(0) * ROUTE_TILES) // tiles_per_batch
    shift, scale = shift_ref[pl.ds(b, 1), :], scale_ref[pl.ds(b, 1), :]
    for tile in range(ROUTE_TILES):
        rows = slice(tile * MOE_TOK_TILE, (tile + 1) * MOE_TOK_TILE)
        _route_tile(h_ref[rows, :], gain_ref[...], shift, scale, wr_ref, br_ref, tri_ref, upper_ref,
                    u_ref.at[rows, :], pos_ref.at[rows, :], w_ref.at[rows, :], nu_ref.at[tile], off_ref.at[tile])


def _route_tile(h, gain, shift, scale, wr_ref, br_ref, tri_ref, upper_ref, u_ref, pos_ref, w_ref, nu_ref, off_ref):
    u = _norm_mod(h, gain, shift, scale)
    u_ref[...] = u.astype(BF16)
    scores = _sigmoid(_dot_f32(u, wr_ref[...]))
    lane = lax.broadcasted_iota(I32, scores.shape, 1)
    lane_f = lane.astype(F32)
    biased = jnp.where(lane < N_EXPERTS, scores + br_ref[...], -jnp.inf)
    chosen = jnp.zeros(scores.shape, F32)
    hits, val = [], []
    upper = upper_ref[...]
    for _ in range(TOP_K):
        best = jnp.max(biased, axis=-1, keepdims=True)
        first = jnp.min(jnp.where(biased == best, lane_f, float(LANES)), axis=-1, keepdims=True)
        hit = lane_f == first
        hits.append(hit)
        val.append(jnp.sum(jnp.where(hit, scores, 0.0), axis=-1, keepdims=True))
        biased = jnp.where(hit, -jnp.inf, biased)
        chosen = jnp.where(hit, 1.0, chosen)
    top_s = jnp.concatenate(val, axis=1)
    w_ref[...] = top_s / jnp.sum(top_s, axis=-1, keepdims=True) * ROUTED_SCALE

    before = _dot(tri_ref[...], chosen.astype(BF16))
    count = jnp.sum(chosen, axis=0, keepdims=True)
    units = jnp.floor((count + (MOE_UNIT - 1)) * (1.0 / MOE_UNIT))
    padded = jnp.broadcast_to(units * MOE_UNIT, (SUBLANES, LANES))
    offset = _dot(padded.astype(BF16), upper)[0:1]
    row_of = before + offset
    pos = [jnp.sum(jnp.where(hit, row_of, 0.0), axis=-1, keepdims=True) for hit in hits]
    pos_ref[...] = jnp.concatenate(pos, axis=1).astype(I32)
    nu_ref[...] = units.astype(I32)
    off_ref[...] = offset.astype(I32)


def _route(h, gain, ada, shift_blk, scale_blk, w_router_pad, b_router_pad, seq):
    t, d = h.shape
    tt = MOE_TOK_TILE
    n_tiles = t // tt
    rt = ROUTE_TILES * tt
    assert n_tiles % ROUTE_TILES == 0 and seq % rt == 0
    row = lambda i: (i, 0)
    fix = lambda i: (0, 0)
    tri = (jnp.arange(tt)[:, None] > jnp.arange(tt)[None, :]).astype(BF16)
    upper = (jnp.arange(LANES)[:, None] < jnp.arange(LANES)[None, :]).astype(BF16)
    per_tile = jax.ShapeDtypeStruct((n_tiles, 1, LANES), I32)
    per_tile_spec = pl.BlockSpec((ROUTE_TILES, 1, LANES), lambda i: (i, 0, 0))
    return pl.pallas_call(
        functools.partial(_route_kernel, tiles_per_batch=seq // tt),
        grid=(n_tiles // ROUTE_TILES,),
        in_specs=[pl.BlockSpec((rt, d), row), pl.BlockSpec((1, d), fix),
                  pl.BlockSpec((SUBLANES, d), lambda i: (0, shift_blk)),
                  pl.BlockSpec((SUBLANES, d), lambda i: (0, scale_blk)),
                  pl.BlockSpec((d, LANES), fix), pl.BlockSpec((1, LANES), fix),
                  pl.BlockSpec((tt, tt), fix), pl.BlockSpec((LANES, LANES), fix)],
        out_specs=[pl.BlockSpec((rt, d), row), pl.BlockSpec((rt, TOP_K), row), pl.BlockSpec((rt, TOP_K), row),
                   per_tile_spec, per_tile_spec],
        out_shape=[jax.ShapeDtypeStruct((t, d), BF16), jax.ShapeDtypeStruct((t, TOP_K), I32),
                   jax.ShapeDtypeStruct((t, TOP_K), F32), per_tile, per_tile],
        compiler_params=_cp("parallel"),
        name="moe_route",
    )(h, gain.reshape(1, d), ada, ada, w_router_pad, b_router_pad, tri, upper)


PLAN_SLOTS = 256
PLAN_TILES = LANES


def _plan_kernel(nu_ref, off_ref, upper_ref, lower_ref, row_ref, exp_ref, nused_ref,
                 *, n_tiles, r_loc, upb, n_unit_slots, n_dump_units):
    nu = nu_ref[...]
    nu_t = nu.T
    off_t = off_ref[...].T
    upper = upper_ref[...]
    lower = lower_ref[...]
    lane = lax.broadcasted_iota(I32, (PLAN_SLOTS, LANES), 1)
    lane_f = lane.astype(F32)

    def exact_dot(a, b_bf16, left):
        parts = _split3(a)
        return sum(_dot(b_bf16, p) if left else _dot(p, b_bf16) for p in parts)

    units_before_t = _dot(nu_t.astype(BF16), upper)
    count_col = jnp.sum(nu_t, axis=1, keepdims=True)
    count_row = jnp.sum(nu, axis=0, keepdims=True)
    padded_col = jnp.floor((count_col + (upb - 1)) * (1.0 / upb)) * upb
    padded_row = jnp.floor((count_row + (upb - 1)) * (1.0 / upb)) * upb
    start_col = exact_dot(jnp.broadcast_to(padded_col, (LANES, LANES)), lower, True)
    real_col = exact_dot(jnp.broadcast_to(count_col, (LANES, LANES)), lower, True)
    start_row = exact_dot(jnp.broadcast_to(padded_row, (SUBLANES, LANES)), upper, False)[0:1]
    seg_start_t = start_col + units_before_t
    seg_real_t = real_col + units_before_t
    tile_id = lax.broadcasted_iota(I32, (LANES, LANES), 1).astype(F32)
    seg_row_t = tile_id * r_loc + off_t

    idx = pl.program_id(0) * PLAN_SLOTS + lax.broadcasted_iota(I32, (PLAN_SLOTS, 1), 0)
    slot = jnp.where(idx < n_unit_slots, idx, (idx - n_unit_slots) * upb).astype(F32)

    in_e = jnp.logical_and(start_row <= slot, lane < N_EXPERTS)
    expert = jnp.sum(jnp.where(in_e, 1.0, 0.0), axis=-1, keepdims=True) - 1.0
    pick_e = (lane_f == expert).astype(BF16)
    look = lambda tbl: exact_dot(tbl, pick_e, True)
    starts = look(seg_start_t)
    in_t = jnp.logical_and(starts <= slot, lane < n_tiles)
    tile = jnp.sum(jnp.where(in_t, 1.0, 0.0), axis=-1, keepdims=True) - 1.0
    at = lambda x: jnp.sum(jnp.where(lane_f == tile, x, 0.0), axis=-1, keepdims=True)
    m = slot - at(starts)
    n = at(look(nu_t))
    row = at(look(seg_row_t)) + m * MOE_UNIT
    pad_idx = slot - (at(look(seg_real_t)) + jnp.minimum(m, n))
    dump = n_tiles * r_loc + jnp.minimum(pad_idx, n_dump_units - 1.0) * MOE_UNIT
    row_ref[...] = jnp.where(m < n, row, dump).astype(I32)
    exp_ref[...] = expert.astype(I32)
    total = jnp.sum(padded_row, axis=-1, keepdims=True) * (1.0 / upb)
    nused_ref[...] = jnp.broadcast_to(total, (SUBLANES, LANES)).astype(I32)


def _plan(nu, off, n_tiles, r_loc, n_blocks, n_dump_units):
    assert n_tiles <= PLAN_TILES
    upb = MOE_BLOCK // MOE_UNIT
    n_unit_slots = -(-(n_blocks * upb) // PLAN_SLOTS) * PLAN_SLOTS
    n_steps = n_unit_slots // PLAN_SLOTS + -(-(n_blocks + 2) // PLAN_SLOTS)
    pad = lambda x: jnp.pad(x.astype(F32), ((0, PLAN_TILES - n_tiles), (0, 0)))
    upper = (jnp.arange(LANES)[:, None] < jnp.arange(LANES)[None, :]).astype(BF16)
    fix = lambda s: (0, 0)
    col = pl.BlockSpec((PLAN_SLOTS, 1), lambda s: (s, 0))
    rows, experts, n_used = pl.pallas_call(
        functools.partial(_plan_kernel, n_tiles=n_tiles, r_loc=r_loc, upb=upb, n_unit_slots=n_unit_slots,
                          n_dump_units=n_dump_units),
        grid=(n_steps,),
        in_specs=[pl.BlockSpec((PLAN_TILES, LANES), fix), pl.BlockSpec((PLAN_TILES, LANES), fix),
                  pl.BlockSpec((LANES, LANES), fix), pl.BlockSpec((LANES, LANES), fix)],
        out_specs=[col, col, pl.BlockSpec((SUBLANES, LANES), fix)],
        out_shape=[jax.ShapeDtypeStruct((n_steps * PLAN_SLOTS, 1), I32),
                   jax.ShapeDtypeStruct((n_steps * PLAN_SLOTS, 1), I32),
                   jax.ShapeDtypeStruct((SUBLANES, LANES), I32)],
        compiler_params=_cp("arbitrary"),
        name="moe_plan",
    )(pad(nu), pad(off), upper, upper.T)
    return (rows[:n_blocks * upb, 0], experts[n_unit_slots:n_unit_slots + n_blocks + 2, 0], n_used[0, :1])


def _one_hot_t(pos_ref, weight_ref, n_rows):
    tt = pos_ref.shape[0]
    col = lax.broadcasted_iota(I32, (tt, n_rows), 1)
    acc = jnp.zeros((tt, n_rows), F32)
    for k in range(TOP_K):
        acc = jnp.where(col == pos_ref[:, k:k + 1], 1.0 if weight_ref is None else weight_ref[:, k:k + 1], acc)
    return acc.astype(BF16)


def _dispatch_kernel(u_ref, pos_ref, x_ref):
    x_ref[0] = _dot_tn(_one_hot_t(pos_ref, None, x_ref.shape[1]), u_ref[...]).astype(x_ref.dtype)


def _dispatch(u, lpos, n_tiles_total, r_loc):
    t, d = u.shape
    tt = MOE_TOK_TILE
    n_real = t // tt
    return pl.pallas_call(
        _dispatch_kernel,
        grid=(n_tiles_total,),
        in_specs=[pl.BlockSpec((tt, d), lambda i: (jnp.minimum(i, n_real - 1), 0)),
                  pl.BlockSpec((tt, TOP_K), lambda i: (i, 0))],
        out_specs=pl.BlockSpec((1, r_loc, d), lambda i: (i, 0, 0)),
        out_shape=jax.ShapeDtypeStruct((n_tiles_total, r_loc, d), BF16),
        compiler_params=_cp("parallel"),
        name="moe_dispatch",
    )(u, lpos)


def _expert_kernel(tbl_ref, be_ref, nblk_ref, x_hbm, wg_ref, wu_ref, wd_ref, y_hbm,
                   xbuf, ybuf, wg_bf, wu_bf, wd_bf, gsem, ssem, *, upb):
    b = pl.program_id(0)
    nblk = nblk_ref[0]

    def gather(blk, sl):
        for m in range(upb):
            row = pl.multiple_of(tbl_ref[blk * upb + m], MOE_UNIT)
            yield pltpu.make_async_copy(x_hbm.at[pl.ds(row, MOE_UNIT), :],
                                        xbuf.at[sl, pl.ds(m * MOE_UNIT, MOE_UNIT), :], gsem.at[sl, m])

    def scatter(blk, sl):
        for m in range(upb):
            row = pl.multiple_of(tbl_ref[blk * upb + m], MOE_UNIT)
            yield pltpu.make_async_copy(ybuf.at[sl, pl.ds(m * MOE_UNIT, MOE_UNIT), :],
                                        y_hbm.at[pl.ds(row, MOE_UNIT), :], ssem.at[sl, m])

    @pl.when(jnp.logical_and(b == 0, nblk > 0))
    def _():
        for cp in gather(0, 0):
            cp.start()

    @pl.when(jnp.logical_or(b == 0, be_ref[b] != be_ref[jnp.maximum(b - 1, 0)]))
    def _():
        wg_bf[...] = wg_ref[0, 0].astype(BF16)
        wu_bf[...] = wu_ref[0, 0].astype(BF16)
        wd_bf[...] = wd_ref[0, 0].astype(BF16)

    slot = lax.rem(b, 2)

    @pl.when(b + 1 < nblk)
    def _():
        for m, cp in enumerate(gather(b + 1, 1 - slot)):
            cp.start(priority=m % 2)

    @pl.when(jnp.logical_and(b >= 2, b < nblk + 2))
    def _():
        for cp in scatter(b - 2, slot):
            cp.wait()

    @pl.when(b < nblk)
    def _():
        for cp in gather(b, slot):
            cp.wait()
        x = xbuf[slot]
        hid = _silu(_dot(x, wg_bf[...])) * _dot(x, wu_bf[...])
        ybuf[slot] = _dot(hid.astype(BF16), wd_bf[...]).astype(ybuf.dtype)
        for m, cp in enumerate(scatter(b, slot)):
            cp.start(priority=m % 2)


def _experts(x_loc, unit_rows, block_e, n_blocks_used, w_gate, w_up, w_down, layer, n_blocks):
    rows, d = x_loc.shape
    de = w_gate.shape[3]
    upb = MOE_BLOCK // MOE_UNIT
    grid_spec = pltpu.PrefetchScalarGridSpec(
        num_scalar_prefetch=3,
        grid=(n_blocks + 2,),
        in_specs=[pl.BlockSpec(memory_space=pl.ANY),
                  pl.BlockSpec((1, 1, d, de), lambda b, tbl, be, nb: (layer, be[b], 0, 0)),
                  pl.BlockSpec((1, 1, d, de), lambda b, tbl, be, nb: (layer, be[b], 0, 0)),
                  pl.BlockSpec((1, 1, de, d), lambda b, tbl, be, nb: (layer, be[b], 0, 0))],
        out_specs=pl.BlockSpec(memory_space=pl.ANY),
        scratch_shapes=[pltpu.VMEM((2, MOE_BLOCK, d), BF16), pltpu.VMEM((2, MOE_BLOCK, d), BF16),
                        pltpu.VMEM((d, de), BF16), pltpu.VMEM((d, de), BF16), pltpu.VMEM((de, d), BF16),
                        pltpu.SemaphoreType.DMA((2, upb)), pltpu.SemaphoreType.DMA((2, upb))],
    )
    return pl.pallas_call(
        functools.partial(_expert_kernel, upb=upb),
        grid_spec=grid_spec,
        out_shape=jax.ShapeDtypeStruct((rows, d), BF16),
        input_output_aliases={3: 0},
        compiler_params=_cp("arbitrary"),
        name="moe_experts",
    )(unit_rows, block_e, n_blocks_used, x_loc, w_gate, w_up, w_down)


def _combine_kernel(y_ref, pos_ref, tw_ref, u_ref, wsg_ref, wsu_ref, wsd_ref, h_ref, g_ref, out_ref, *, tiles_per_batch):
    b = pl.program_id(0) // tiles_per_batch
    routed = _dot(_one_hot_t(pos_ref, tw_ref, y_ref.shape[1]), y_ref[0])
    u = u_ref[...]
    hid = _silu(_dot(u, wsg_ref[...])) * _dot(u, wsu_ref[...])
    shared = _dot(hid.astype(BF16), wsd_ref[...])
    out_ref[...] = h_ref[...] + g_ref[pl.ds(b, 1), :] * (routed + shared)


def _combine(y_loc, lpos, top_w, u, w_sg, w_su, w_sd, h, ada, gate_blk, seq):
    t, d = h.shape
    tt = MOE_TOK_TILE
    r_loc = y_loc.shape[1]
    row = lambda i: (i, 0)
    fix = lambda i: (0, 0)
    return pl.pallas_call(
        functools.partial(_combine_kernel, tiles_per_batch=seq // tt),
        grid=(t // tt,),
        in_specs=[pl.BlockSpec((1, r_loc, d), lambda i: (i, 0, 0)),
                  pl.BlockSpec((tt, TOP_K), row), pl.BlockSpec((tt, TOP_K), row),
                  pl.BlockSpec((tt, d), row),
                  pl.BlockSpec(w_sg.shape, fix), pl.BlockSpec(w_su.shape, fix), pl.BlockSpec(w_sd.shape, fix),
                  pl.BlockSpec((tt, d), row),
                  pl.BlockSpec((SUBLANES, d), lambda i: (0, gate_blk))],
        out_specs=pl.BlockSpec((tt, d), row),
        out_shape=jax.ShapeDtypeStruct((t, d), F32),
        compiler_params=_cp("parallel"),
        name="moe_combine",
    )(y_loc, lpos, top_w, u, w_sg, w_su, w_sd, h, ada)


def _moe_ffn(h, gain, ada, blk0, w_router, b_router, w_eg, w_eu, w_ed, layer, w_sg, w_su, w_sd, seq):
    t, d = h.shape
    tt, unit, upb = MOE_TOK_TILE, MOE_UNIT, MOE_BLOCK // MOE_UNIT
    wr = jnp.pad(w_router, ((0, 0), (0, LANES - N_EXPERTS)))
    br = jnp.pad(b_router.astype(F32), (0, LANES - N_EXPERTS)).reshape(1, LANES)
    u, lpos, top_w, nu, off = _route(h, gain, ada, blk0, blk0 + 1, wr, br, seq)

    n_tok_tiles = t // tt
    r_loc = -(-(tt * TOP_K + N_EXPERTS * (unit - 1)) // LANES) * LANES
    n_dummy_tiles = -(-(N_EXPERTS * (upb - 1) * unit) // r_loc)
    max_units = n_tok_tiles * (tt * TOP_K // unit + N_EXPERTS)
    n_blocks = -(-max_units // upb) + N_EXPERTS
    unit_rows, block_e, n_used = _plan(nu[:, 0, :], off[:, 0, :], n_tok_tiles, r_loc, n_blocks,
                                       n_dummy_tiles * r_loc // unit)
    lpos = jnp.concatenate([lpos, jnp.full((n_dummy_tiles * tt, TOP_K), -1, I32)], axis=0)

    x_loc = _dispatch(u, lpos, n_tok_tiles + n_dummy_tiles, r_loc)
    y_loc = _experts(x_loc.reshape(-1, d), unit_rows, block_e, n_used, w_eg, w_eu, w_ed, layer, n_blocks)
    y_loc = y_loc.reshape(n_tok_tiles + n_dummy_tiles, r_loc, d)
    return _combine(y_loc, lpos, top_w, u, w_sg.astype(BF16), w_su.astype(BF16), w_sd.astype(BF16),
                    h, ada, blk0 + 2, seq)


def _fox_layer(h, ada, norm_gain, w_in, b_f, q_gain, k_gain, w_o, batch, seq):
    t, d = h.shape
    hd = N_HEADS_A * HEAD_DIM
    w = jnp.pad(w_in, ((0, 0), (0, MXU_COLS - N_HEADS_A))).astype(BF16)
    proj = _norm_mod_matmul(h, norm_gain, ada, 0, 1, w, seq, w.shape[1], "fox_in_proj")
    z = proj[:, 3 * hd:3 * hd + N_HEADS_A].reshape(batch, seq, N_HEADS_A)
    z = z.transpose(1, 0, 2).reshape(seq, batch * N_HEADS_A)
    cum = _logsig_cumsum(z, jnp.tile(b_f.astype(F32), batch).reshape(1, -1))
    q, k, v = _fox_prep(proj, cum, q_gain, k_gain, batch, seq)
    bh = batch * N_HEADS_A
    o = _flash(q.reshape(bh, seq // FOX_TQ, FOX_TQ, LANES), k.reshape(bh, LANES, seq),
               v.reshape(bh, seq, LANES), FOX_TQ, FOX_TK, "fox_attention")
    o = o.reshape(batch, N_HEADS_A, seq, HEAD_DIM).transpose(0, 2, 1, 3).reshape(t, hd)
    return _oproj_residual(o, w_o.astype(BF16), h, ada, 2, seq)


def _rope_tables(positions):
    inv_freq = ROPE_THETA ** (-jnp.arange(0, ROPE_DIM, 2, dtype=F32) / ROPE_DIM)
    ang = positions.astype(F32).reshape(-1, 1) * inv_freq
    half = ROPE_DIM // 2
    cos, sin = jnp.cos(ang), jnp.sin(ang)
    rest = HEAD_DIM - ROPE_DIM
    cos_h = jnp.concatenate([cos, cos, jnp.ones((ang.shape[0], rest), F32)], axis=1)
    sin_h = jnp.concatenate([-sin, sin, jnp.zeros((ang.shape[0], rest), F32)], axis=1)
    del half
    return jnp.tile(cos_h, (1, LANES // HEAD_DIM)), jnp.tile(sin_h, (1, LANES // HEAD_DIM))


def _group_rows(x, batch, seq, tq):
    g, hpg = N_KV_GROUPS_B, HEADS_PER_GROUP
    x = x.reshape(batch, seq // tq, tq, g, hpg, HEAD_DIM).transpose(0, 3, 1, 4, 2, 5)
    return x.reshape(batch * g, seq // tq, hpg * tq, HEAD_DIM)


def _per_group(x, batch, seq):
    g = N_KV_GROUPS_B
    return x.reshape(batch, seq, g, HEAD_DIM).transpose(0, 2, 1, 3).reshape(batch * g, seq, HEAD_DIM)


def _with_ones(v):
    pad = jnp.zeros(v.shape[:-1] + (LANES - HEAD_DIM,), v.dtype).at[..., 0].set(1)
    return jnp.concatenate([v, pad], axis=-1)


def _shared_kv(h, ada_kv, norm_kv, w_kv, pe_k, pe_v, w_ck1, w_ck2, w_cv1, w_cv2, k_gains, cos_t, sin_t, batch, seq):
    g = N_KV_GROUPS_B
    gw = g * HEAD_DIM
    nb = seq // CMP_BLOCK
    kv = _norm_mod_matmul(h, norm_kv, ada_kv, 0, 1, w_kv.astype(BF16), seq, w_kv.shape[1], "nsa_kv_proj")
    k_sel, v_sel, k_win, v_win = _nsa_kv_prep(kv, cos_t, sin_t, k_gains)

    def blocks(x):
        x = x.reshape(batch, nb, CMP_BLOCK, g, HEAD_DIM).transpose(0, 1, 3, 2, 4)
        return x.reshape(batch * nb * g, CMP_BLOCK * HEAD_DIM)

    k_cmp = _compress(blocks(kv[:, 0:gw]), pe_k.reshape(1, -1), w_ck1.astype(BF16), w_ck2.astype(BF16),
                      k_gains[0], True, "nsa_compress_k")
    v_cmp = _compress(blocks(kv[:, gw:2 * gw]), pe_v.reshape(1, -1), w_cv1.astype(BF16), w_cv2.astype(BF16),
                      k_gains[0], False, "nsa_compress_v")
    to_bg = lambda x: x.reshape(batch, nb, g, HEAD_DIM).transpose(0, 2, 1, 3).reshape(batch * g, nb, HEAD_DIM)
    k_cmp = to_bg(k_cmp)
    v_cmp = jnp.pad(to_bg(v_cmp), ((0, 0), (0, 0), (0, LANES - HEAD_DIM)))

    def per_group_t(x):
        return x.reshape(batch, seq, g, HEAD_DIM).transpose(0, 2, 3, 1).reshape(batch * g, HEAD_DIM, seq)

    blk_onehot = (jnp.arange(nb)[:, None] == jnp.arange(seq)[None, :] // CMP_BLOCK).astype(BF16)
    k_sel = jnp.concatenate([jnp.broadcast_to(blk_onehot, (batch * g, nb, seq)), per_group_t(k_sel)], axis=1)
    return (k_cmp, v_cmp, k_sel, _with_ones(_per_group(v_sel, batch, seq)),
            per_group_t(k_win), _with_ones(_per_group(v_win, batch, seq)))


def _nsa_layer(h, ada, norm_gain, kv, cos_t, sin_t, w_in, b_gate, q_gain, w_o, batch, seq):
    hd = N_HEADS_B * HEAD_DIM
    k_cmp, v_cmp, k_sel, v_sel, k_win, v_win = kv
    n_gate = 3 * N_HEADS_B
    w = jnp.pad(w_in, ((0, 0), (0, LANES - n_gate))).astype(BF16)
    proj = _norm_mod_matmul(h, norm_gain, ada, 0, 1, w, seq, hd + LANES, "nsa_in_proj")
    bg = jnp.pad(b_gate.astype(F32), (0, LANES - n_gate)).reshape(1, LANES)
    q_plain, q_rot, gates = _nsa_q_prep(proj, cos_t, sin_t, q_gain, bg)

    bgs = (batch, N_KV_GROUPS_B)
    o_cmp, negbias = _cmp_select(_group_rows(q_plain, batch, seq, CMP_TQ), k_cmp, v_cmp, seq, CMP_TQ, bgs)
    o_sel = _flash(_group_rows(q_rot, batch, seq, SEL_TQ), k_sel, v_sel, SEL_TQ, NSA_TK, "nsa_selected",
                   bias=negbias, grouped_out=bgs)
    o_win = _window_attention(_group_rows(q_rot, batch, seq, WIN_TQ), k_win, v_win, WIN_TQ, WINDOW, bgs)
    return _nsa_out(o_cmp, o_sel, o_win, gates, w_o.astype(BF16), h, ada, 2, seq)


def kernel(x, c, positions, w_ada, b_ada, norm_mix, norm_ffn, w_in_a, b_f_a, qnorm_a, knorm_a, w_o_a, norm_kv, w_ada_kv, b_ada_kv, w_kv_b, pe_cmp_k, pe_cmp_v, w_cmp_k1, w_cmp_k2, w_cmp_v1, w_cmp_v2, knorm_b, w_in_b, b_gate_b, qnorm_b, w_o_b, w_router, b_router, w_exp_gate, w_exp_up, w_exp_down, w_sh_gate, w_sh_up, w_sh_down):
    batch, seq, d = x.shape
    depth = w_ada.shape[0]
    n_a = w_in_a.shape[0]
    c_pad = jnp.pad(c, ((0, SUBLANES - batch), (0, 0)))
    cos_t, sin_t = _rope_tables(positions)
    h = x.reshape(batch * seq, d)
    kv = None
    for layer in range(depth):
        ada = _ada(c_pad, w_ada[layer], b_ada[layer])
        if layer < n_a:
            h = _fox_layer(h, ada, norm_mix[layer], w_in_a[layer], b_f_a[layer], qnorm_a[layer],
                           knorm_a[layer], w_o_a[layer], batch, seq)
        else:
            if layer == n_a:
                ada_kv = _ada(c_pad, w_ada_kv, b_ada_kv)
                kv = _shared_kv(h, ada_kv, norm_kv, w_kv_b, pe_cmp_k, pe_cmp_v, w_cmp_k1, w_cmp_k2,
                                w_cmp_v1, w_cmp_v2, knorm_b, cos_t, sin_t, batch, seq)
            j = layer - n_a
            h = _nsa_layer(h, ada, norm_mix[layer], kv, cos_t, sin_t, w_in_b[j], b_gate_b[j], qnorm_b[j],
                           w_o_b[j], batch, seq)
        h = _moe_ffn(h, norm_ffn[layer], ada, 3, w_router[layer], b_router[layer], w_exp_gate, w_exp_up,
                     w_exp_down, layer, w_sh_gate[layer], w_sh_up[layer], w_sh_down[layer], seq)
    return h.reshape(batch, seq, d)
```
